```python
import math
import jax
import jax.numpy as jnp
from jax import lax
import numpy as np

D_MODEL = 1024
BATCH = 4
SEQ = 8192
DEPTH = 4

GRID_W = 64
CTX_LEN = 256
N_MIXERS = 4
Q_BLOCK = 128
ROPE_BASE = 10000.0
EPS = 1e-6
NEG_INF = -1e30

GQA_HEADS = 16
GQA_KV_HEADS = 4
GQA_HEAD_DIM = 64
MLA_HEADS = 16
MLA_NOPE_DIM = 64
MLA_ROPE_DIM = 32
MLA_V_DIM = 64
MLA_Q_LORA = 384
MLA_KV_LORA = 256
DIFF_HEADS = 8
DIFF_HEAD_DIM = 64
NA_HEADS = 16
NA_HEAD_DIM = 64
NA_ROWS = 8
NA_COLS = 16
NA_QROWS = Q_BLOCK // GRID_W
FFN_DIM = 2816

kernel_name = 'hybrid_prefix_dit_trunk'


def rms_norm(x, g):
    xf = x.astype(jnp.float32)
    y = xf * lax.rsqrt(jnp.mean(xf * xf, axis=-1, keepdims=True) + EPS)
    return (y * g.astype(jnp.float32)).astype(x.dtype)


def norm_modulate(h, g, shift, scale):
    return rms_norm(h, g) * (1 + scale) + shift


def softmax_f32(s):
    return jax.nn.softmax(s.astype(jnp.float32), axis=-1)


def axial_rope(n_tokens, rot_dim, dtype):
    t = jnp.arange(n_tokens)
    row = (t // GRID_W).astype(jnp.float32)
    col = (t % GRID_W).astype(jnp.float32)
    half = rot_dim // 2
    inv = ROPE_BASE ** (-jnp.arange(0, half, 2, dtype=jnp.float32) / half)
    ar, ac = row[:, None] * inv, col[:, None] * inv
    ang = jnp.concatenate([ar, ar, ac, ac], axis=-1)
    return jnp.cos(ang).astype(dtype), jnp.sin(ang).astype(dtype)


def apply_rope(x, cos, sin):
    x1, x2, x3, x4 = jnp.split(x, 4, axis=-1)
    rot = jnp.concatenate([-x2, x1, -x4, x3], axis=-1)
    shape = (cos.shape[0],) + (1,) * (x.ndim - 3) + (cos.shape[1],)
    return x * cos.reshape(shape) + rot * sin.reshape(shape)


def sweep_query_blocks(fn, *qs):
    b, t = qs[0].shape[:2]
    nb = t // Q_BLOCK
    blocks = tuple(jnp.moveaxis(q.reshape(b, nb, Q_BLOCK, *q.shape[2:]), 1, 0) for q in qs)
    out = lax.map(lambda qb: fn(*qb), blocks)
    return jnp.moveaxis(out, 0, 1).reshape(b, t, *out.shape[3:])


def gqa_mixer(hx, hc, w_in, q_g, k_g, w_out, ctx_out):
    b, t, d = hx.shape
    grp = GQA_HEADS // GQA_KV_HEADS
    splits = [GQA_HEADS * GQA_HEAD_DIM, (GQA_HEADS + GQA_KV_HEADS) * GQA_HEAD_DIM]

    def project(h):
        n = h.shape[1]
        q, k, v = jnp.split(h @ w_in, splits, axis=-1)
        q = rms_norm(q.reshape(b, n, GQA_KV_HEADS, grp, GQA_HEAD_DIM), q_g)
        k = rms_norm(k.reshape(b, n, GQA_KV_HEADS, GQA_HEAD_DIM), k_g)
        return q, k, v.reshape(b, n, GQA_KV_HEADS, GQA_HEAD_DIM)

    qx, kx, vx = project(hx)
    qc, kc, vc = project(hc)
    cos, sin = axial_rope(t, GQA_HEAD_DIM, hx.dtype)
    qx, kx = apply_rope(qx, cos, sin), apply_rope(kx, cos, sin)
    k_all = jnp.concatenate([kc, kx], axis=1)
    v_all = jnp.concatenate([vc, vx], axis=1)
    scale = GQA_HEAD_DIM ** -0.5

    def attend(q, k, v):
        s = jnp.einsum('bqhgd,bkhd->bhgqk', q, k) * scale
        p = softmax_f32(s).astype(v.dtype)
        return jnp.einsum('bhgqk,bkhd->bqhgd', p, v)

    out_x = sweep_query_blocks(lambda qb: attend(qb, k_all, v_all), qx).reshape(b, t, d) @ w_out
    out_c = attend(qc, kc, vc).reshape(b, hc.shape[1], d) @ w_out if ctx_out else None
    return out_x, out_c


def mla_mixer(hx, hc, w_in, q_norm_g, kv_norm_g, w_uq, w_ukv, w_out, ctx_out):
    b, t, d = hx.shape
    qk_dim = MLA_NOPE_DIM + MLA_ROPE_DIM

    def project(h):
        n = h.shape[1]
        cq, ckv, k_rope = jnp.split(h @ w_in, [MLA_Q_LORA, MLA_Q_LORA + MLA_KV_LORA], axis=-1)
        q = (rms_norm(cq, q_norm_g) @ w_uq).reshape(b, n, MLA_HEADS, qk_dim)
        kv = (rms_norm(ckv, kv_norm_g) @ w_ukv).reshape(b, n, MLA_HEADS, MLA_NOPE_DIM + MLA_V_DIM)
        q_nope, q_rope = jnp.split(q, [MLA_NOPE_DIM], axis=-1)
        k_nope, v = jnp.split(kv, [MLA_NOPE_DIM], axis=-1)
        return q_nope, q_rope, k_nope, k_rope, v

    qnx, qrx, knx, krx, vx = project(hx)
    qnc, qrc, knc, krc, vc = project(hc)
    cos, sin = axial_rope(t, MLA_ROPE_DIM, hx.dtype)
    qrx, krx = apply_rope(qrx, cos, sin), apply_rope(krx, cos, sin)
    kn_all = jnp.concatenate([knc, knx], axis=1)
    kr_all = jnp.concatenate([krc, krx], axis=1)
    v_all = jnp.concatenate([vc, vx], axis=1)
    scale = qk_dim ** -0.5

    def attend(qn, qr, kn, kr, v):
        s = (jnp.einsum('bqhd,bkhd->bhqk', qn, kn) + jnp.einsum('bqhd,bkd->bhqk', qr, kr)) * scale
        p = softmax_f32(s).astype(v.dtype)
        return jnp.einsum('bhqk,bkhd->bqhd', p, v)

    ox = sweep_query_blocks(lambda qn, qr: attend(qn, qr, kn_all, kr_all, v_all), qnx, qrx)
    out_x = ox.reshape(b, t, MLA_HEADS * MLA_V_DIM) @ w_out
    out_c = attend(qnc, qrc, knc, krc, vc).reshape(b, hc.shape[1], MLA_HEADS * MLA_V_DIM) @ w_out if ctx_out else None
    return out_x, out_c


def diff_mixer(hx, hc, w_in, lam, subln_g, w_out, layer_idx, ctx_out):
    b, t, d = hx.shape
    lambda_init = 0.8 - 0.6 * math.exp(-0.3 * layer_idx)
    lf = lam.astype(jnp.float32)
    lam_full = jnp.exp(jnp.sum(lf[0] * lf[1])) - jnp.exp(jnp.sum(lf[2] * lf[3])) + lambda_init

    def project(h):
        n = h.shape[1]
        q, k, v = jnp.split(h @ w_in, 3, axis=-1)
        q = q.reshape(b, n, DIFF_HEADS, 2, DIFF_HEAD_DIM)
        k = k.reshape(b, n, DIFF_HEADS, 2, DIFF_HEAD_DIM)
        return q, k, v.reshape(b, n, DIFF_HEADS, 2 * DIFF_HEAD_DIM)

    qx, kx, vx = project(hx)
    qc, kc, vc = project(hc)
    cos, sin = axial_rope(t, DIFF_HEAD_DIM, hx.dtype)
    qx, kx = apply_rope(qx, cos, sin), apply_rope(kx, cos, sin)
    k_all = jnp.concatenate([kc, kx], axis=1)
    v_all = jnp.concatenate([vc, vx], axis=1)
    scale = DIFF_HEAD_DIM ** -0.5

    def attend(q, k, v):
        s = jnp.einsum('bqhjd,bkhjd->bhjqk', q, k) * scale
        p = softmax_f32(s)
        p = p[:, :, 0] - lam_full * p[:, :, 1]
        o = jnp.einsum('bhqk,bkhe->bqhe', p.astype(v.dtype), v)
        return rms_norm(o, subln_g) * (1 - lambda_init)

    out_x = sweep_query_blocks(lambda qb: attend(qb, k_all, v_all), qx).reshape(b, t, d) @ w_out
    out_c = attend(qc, kc, vc).reshape(b, hc.shape[1], d) @ w_out if ctx_out else None
    return out_x, out_c


def na_mixer(hx, hc, w_in, rpb, w_out, ctx_out):
    b, t, d = hx.shape
    rows = t // GRID_W
    win_rows = min(NA_ROWS, rows)
    band_rows = min(win_rows + 1, rows)
    band = band_rows * GRID_W
    n_ctx = hc.shape[1]

    def project(h):
        n = h.shape[1]
        q, k, v = jnp.split(h @ w_in, 3, axis=-1)
        return tuple(a.reshape(b, n, NA_HEADS, NA_HEAD_DIM) for a in (q, k, v))

    qx, kx, vx = project(hx)
    qc, kc, vc = project(hc)
    scale = NA_HEAD_DIM ** -0.5
    k_grid = kx.reshape(b, rows, GRID_W, NA_HEADS, NA_HEAD_DIM)
    v_grid = vx.reshape(b, rows, GRID_W, NA_HEADS, NA_HEAD_DIM)
    nb = t // Q_BLOCK
    q_blocks = jnp.moveaxis(qx.reshape(b, nb, Q_BLOCK, NA_HEADS, NA_HEAD_DIM), 1, 0)
    q_off, k_off = jnp.arange(Q_BLOCK), jnp.arange(band)
    q_dr, q_col = q_off // GRID_W, q_off % GRID_W
    k_dr, k_col = k_off // GRID_W, k_off % GRID_W
    col_start = jnp.clip(q_col - NA_COLS // 2, 0, GRID_W - NA_COLS)
    col_in = (k_col[None, :] >= col_start[:, None]) & (k_col[None, :] < col_start[:, None] + NA_COLS)
    dc_idx = jnp.clip(k_col[None, :] - q_col[:, None] + NA_COLS - 1, 0, 2 * NA_COLS - 2)

    def block(args):
        j, qb = args
        r = j * NA_QROWS + q_dr
        row_start = jnp.clip(r - win_rows // 2, 0, rows - win_rows)
        b0 = jnp.minimum(row_start[0], rows - band_rows)
        k_band = lax.dynamic_slice_in_dim(k_grid, b0, band_rows, axis=1).reshape(b, band, NA_HEADS, NA_HEAD_DIM)
        v_band = lax.dynamic_slice_in_dim(v_grid, b0, band_rows, axis=1).reshape(b, band, NA_HEADS, NA_HEAD_DIM)
        k_row = b0 + k_dr
        in_win = col_in & (k_row[None, :] >= row_start[:, None]) & (k_row[None, :] < row_start[:, None] + win_rows)
        dr_idx = jnp.clip(k_row[None, :] - r[:, None] + NA_ROWS - 1, 0, 2 * NA_ROWS - 2)
        bias = rpb[:, dr_idx, dc_idx].astype(jnp.float32)
        s_lat = jnp.einsum('bqhd,bkhd->bhqk', qb, k_band).astype(jnp.float32) * scale + bias
        s_lat = jnp.where(in_win, s_lat, NEG_INF)
        s_ctx = jnp.einsum('bqhd,bkhd->bhqk', qb, kc).astype(jnp.float32) * scale
        p = softmax_f32(jnp.concatenate([s_ctx, s_lat], axis=-1)).astype(vx.dtype)
        return (jnp.einsum('bhqk,bkhd->bqhd', p[..., :n_ctx], vc)
                + jnp.einsum('bhqk,bkhd->bqhd', p[..., n_ctx:], v_band))

    ox = lax.map(block, (jnp.arange(nb), q_blocks))
    out_x = jnp.moveaxis(ox, 0, 1).reshape(b, t, d) @ w_out
    if ctx_out:
        s = jnp.einsum('bqhd,bkhd->bhqk', qc, kc) * scale
        p = softmax_f32(s).astype(vc.dtype)
        out_c = jnp.einsum('bhqk,bkhd->bqhd', p, vc).reshape(b, n_ctx, d) @ w_out
    else:
        out_c = None
    return out_x, out_c


def dwconv_centred(u, w, bias):
    p = jnp.pad(u, ((0, 0), (1, 1), (0, 0)))
    return p[:, :-2] * w[0] + p[:, 1:-1] * w[1] + p[:, 2:] * w[2] + bias


def conv_ffn(h, w_up, conv_w, conv_b, w_down):
    u = dwconv_centred(h @ w_up, conv_w, conv_b)
    val, gate = jnp.split(u, 2, axis=-1)
    return (jax.nn.silu(gate) * val) @ w_down


def setup_inputs(seed: int = 0) -> dict:
    key = jax.random.key(seed)
    ks = iter(jax.random.split(key, 32))
    D = D_MODEL

    def nrm(shape, scale=1.0):
        return jax.random.normal(next(ks), shape, jnp.float32) * scale

    def gain(shape):
        return 1.0 + nrm(shape, 0.02)

    la, lb, lc, ld = (len(range(m, DEPTH, N_MIXERS)) for m in range(N_MIXERS))
    gqa_in = (GQA_HEADS + 2 * GQA_KV_HEADS) * GQA_HEAD_DIM
    mla_in = MLA_Q_LORA + MLA_KV_LORA + MLA_ROPE_DIM
    return {
        'x': nrm((BATCH, SEQ, D)),
        'c': nrm((BATCH, D)),
        'ctx': nrm((BATCH, CTX_LEN, D)),
        'c_ctx': nrm((D,)),
        'ada_w': nrm((DEPTH, D, 6 * D), 0.5 * D ** -0.5),
        'ada_b': nrm((DEPTH, 6 * D), 0.02),
        'norm1_g': gain((DEPTH, D)),
        'norm2_g': gain((DEPTH, D)),
        'ffn_w_up': nrm((DEPTH, D, 2 * FFN_DIM), D ** -0.5),
        'ffn_conv_w': nrm((DEPTH, 3, 2 * FFN_DIM), 3 ** -0.5),
        'ffn_conv_b': nrm((DEPTH, 2 * FFN_DIM), 0.02),
        'ffn_w_down': nrm((DEPTH, FFN_DIM, D), FFN_DIM ** -0.5),
        'gqa_w_in': nrm((la, D, gqa_in), D ** -0.5),
        'gqa_q_norm_g': gain((la, GQA_HEAD_DIM)),
        'gqa_k_norm_g': gain((la, GQA_HEAD_DIM)),
        'gqa_w_out': nrm((la, GQA_HEADS * GQA_HEAD_DIM, D), (GQA_HEADS * GQA_HEAD_DIM) ** -0.5),
        'mla_w_in': nrm((lb, D, mla_in), D ** -0.5),
        'mla_q_norm_g': gain((lb, MLA_Q_LORA)),
        'mla_kv_norm_g': gain((lb, MLA_KV_LORA)),
        'mla_w_uq': nrm((lb, MLA_Q_LORA, MLA_HEADS * (MLA_NOPE_DIM + MLA_ROPE_DIM)), MLA_Q_LORA ** -0.5),
        'mla_w_ukv': nrm((lb, MLA_KV_LORA, MLA_HEADS * (MLA_NOPE_DIM + MLA_V_DIM)), MLA_KV_LORA ** -0.5),
        'mla_w_out': nrm((lb, MLA_HEADS * MLA_V_DIM, D), (MLA_HEADS * MLA_V_DIM) ** -0.5),
        'diff_w_in': nrm((lc, D, 3 * D), D ** -0.5),
        'diff_lambda': nrm((lc, 4, DIFF_HEAD_DIM), 0.1),
        'diff_subln_g': gain((lc, 2 * DIFF_HEAD_DIM)),
        'diff_w_out': nrm((lc, D, D), D ** -0.5),
        'na_w_in': nrm((ld, D, 3 * D), D ** -0.5),
        'na_rpb': nrm((ld, NA_HEADS, 2 * NA_ROWS - 1, 2 * NA_COLS - 1), 0.05),
        'na_w_out': nrm((ld, D, D), D ** -0.5),
        'final_norm_g': gain((D,)),
    }


def reference(x, c, ctx, c_ctx, ada_w, ada_b, norm1_g, norm2_g, ffn_w_up, ffn_conv_w, ffn_conv_b, ffn_w_down,
              gqa_w_in, gqa_q_norm_g, gqa_k_norm_g, gqa_w_out,
              mla_w_in, mla_q_norm_g, mla_kv_norm_g, mla_w_uq, mla_w_ukv, mla_w_out,
              diff_w_in, diff_lambda, diff_subln_g, diff_w_out,
              na_w_in, na_rpb, na_w_out, final_norm_g):
    silu_c = jax.nn.silu(c)
    silu_cc = jax.nn.silu(c_ctx)
    for i in range(DEPTH):
        ctx_out = i < DEPTH - 1
        m, j = i % N_MIXERS, i // N_MIXERS
        sh1, sc1, g1, sh2, sc2, g2 = (a[:, None, :] for a in jnp.split(silu_c @ ada_w[i] + ada_b[i], 6, axis=-1))
        csh1, csc1, cg1, csh2, csc2, cg2 = jnp.split(silu_cc @ ada_w[i] + ada_b[i], 6, axis=-1)
        hx = norm_modulate(x, norm1_g[i], sh1, sc1)
        hc = norm_modulate(ctx, norm1_g[i], csh1, csc1)
        if m == 0:
            ox, oc = gqa_mixer(hx, hc, gqa_w_in[j], gqa_q_norm_g[j], gqa_k_norm_g[j], gqa_w_out[j], ctx_out)
        elif m == 1:
            ox, oc = mla_mixer(hx, hc, mla_w_in[j], mla_q_norm_g[j], mla_kv_norm_g[j], mla_w_uq[j],
                               mla_w_ukv[j], mla_w_out[j], ctx_out)
        elif m == 2:
            ox, oc = diff_mixer(hx, hc, diff_w_in[j], diff_lambda[j], diff_subln_g[j], diff_w_out[j], i, ctx_out)
        else:
            ox, oc = na_mixer(hx, hc, na_w_in[j], na_rpb[j], na_w_out[j], ctx_out)
        x = x + g1 * ox
        x = x + g2 * conv_ffn(norm_modulate(x, norm2_g[i], sh2, sc2),
                              ffn_w_up[i], ffn_conv_w[i], ffn_conv_b[i], ffn_w_down[i])
        if ctx_out:
            ctx = ctx + cg1 * oc
            ctx = ctx + cg2 * conv_ffn(norm_modulate(ctx, norm2_g[i], csh2, csc2),
                                       ffn_w_up[i], ffn_conv_w[i], ffn_conv_b[i], ffn_w_down[i])
    return rms_norm(x, final_norm_g)
```

```python
import functools
import math

import numpy as np
import jax
import jax.numpy as jnp
from jax import lax
from jax.experimental import pallas as pl
from jax.experimental.pallas import tpu as pltpu

F32 = jnp.float32
BF16 = jnp.bfloat16

LANES = 128
TOKEN_TILE = 256
HALO = 8
KV_CHUNK = 768
VMEM_LIMIT = 56 * 1024 * 1024

GRID_W = 64
ROPE_BASE = 10000.0
EPS = 1e-6
NEG_INF = -1e30
LOG2E = 1.4426950408889634

GQA_HEADS, GQA_KV_HEADS, GQA_HEAD_DIM = 16, 4, 64
MLA_HEADS, MLA_NOPE_DIM, MLA_ROPE_DIM, MLA_V_DIM = 16, 64, 32, 64
MLA_Q_LORA, MLA_KV_LORA = 384, 256
DIFF_HEADS, DIFF_HEAD_DIM = 8, 64
NA_HEADS, NA_HEAD_DIM, NA_ROWS, NA_COLS = 16, 64, 8, 16
NA_QBLOCK = 128
NA_WIN_ROWS = 10
N_MIXERS = 4


def _cparams(n_axes):
    return pltpu.CompilerParams(dimension_semantics=("arbitrary",) * n_axes, vmem_limit_bytes=VMEM_LIMIT)


def _resident(shape):
    nd = len(shape)
    return pl.BlockSpec(shape, lambda *_: (0,) * nd)


def _ada_kernel(c_ref, w_ref, b_ref, o_ref):
    c = c_ref[...]
    s = c * (1.0 / (1.0 + jnp.exp(-c)))
    o_ref[0] = jnp.dot(s.astype(BF16), w_ref[0].astype(BF16), preferred_element_type=F32) + b_ref[0]


def _ada_all(cin, ada_w, ada_b):
    depth, d, n = ada_w.shape
    tn = n // 4
    return pl.pallas_call(
        _ada_kernel,
        grid=(depth, n // tn),
        in_specs=[
            pl.BlockSpec((8, d), lambda l, j: (0, 0)),
            pl.BlockSpec((1, d, tn), lambda l, j: (l, 0, j)),
            pl.BlockSpec((1, 1, tn), lambda l, j: (l, 0, j)),
        ],
        out_specs=pl.BlockSpec((1, 8, tn), lambda l, j: (l, 0, j)),
        out_shape=jax.ShapeDtypeStruct((depth, 8, n), F32),
        compiler_params=_cparams(2),
        name="ada",
    )(cin, ada_w, ada_b.reshape(depth, 1, n))


def _norm_mod(x, g, shift, scale):
    ms = jnp.mean(x * x, axis=-1, keepdims=True)
    return (x * lax.rsqrt(ms + EPS) * g) * (1.0 + scale) + shift


def _rope_slot(y, cos, sin_signed, lo_mask, quarter):
    fwd = pltpu.roll(y, LANES - quarter, axis=1)
    bwd = pltpu.roll(y, quarter, axis=1)
    return y * cos + jnp.where(lo_mask, fwd, bwd) * sin_signed


def _lo_mask(shape, quarter):
    lane = lax.broadcasted_iota(jnp.int32, shape, 1)
    return (lane % (2 * quarter)) < quarter


def _proj_kernel(x_ref, mod_ref, g_ref, wq_ref, wk_ref, wv_ref, qg_ref, kg_ref, cos_ref, sin_ref,
                 q_ref, k_ref, v_ref, *, qk_norm, rope, head_dim):
    h = _norm_mod(x_ref[0], g_ref[...], mod_ref[0, 0:1, :], mod_ref[0, 1:2, :]).astype(BF16)
    tm = h.shape[0]
    if rope:
        cos, sin = cos_ref[...], sin_ref[...]
        lo = _lo_mask((tm, LANES), head_dim // 4)

    def finish(w_ref, o_ref, gain_ref):
        n = w_ref.shape[1]
        for c0 in range(0, n, 2 * LANES):
            c1 = min(c0 + 2 * LANES, n)
            y2 = jnp.dot(h, w_ref[:, c0:c1], preferred_element_type=F32)
            for s0 in range(0, c1 - c0, LANES):
                y = y2[:, s0:s0 + LANES]
                if qk_norm:
                    ms = jnp.sum(y * y, axis=-1, keepdims=True) * (1.0 / head_dim)
                    y = y * lax.rsqrt(ms + EPS) * gain_ref[...]
                if rope:
                    y = _rope_slot(y, cos, sin, lo, head_dim // 4)
                o_ref[0, :, c0 + s0:c0 + s0 + LANES] = y.astype(BF16)

    finish(wq_ref, q_ref, qg_ref)
    finish(wk_ref, k_ref, kg_ref)
    v_ref[0] = jnp.dot(h, wv_ref[...], preferred_element_type=F32).astype(BF16)


def _project(xs, mods, g, wq, wk, wv, qg, kg, cos, sin, *, qk_norm, rope, head_dim, n_ctx_tiles, name):
    b, s, d = xs.shape
    tm = TOKEN_TILE
    nq, nk, nv = wq.shape[1], wk.shape[1], wv.shape[1]
    tok = lambda n: pl.BlockSpec((1, tm, n), lambda bi, i: (bi, i, 0))
    tab = pl.BlockSpec((tm, LANES), lambda bi, i: (i, 0))
    return pl.pallas_call(
        functools.partial(_proj_kernel, qk_norm=qk_norm, rope=rope, head_dim=head_dim),
        grid=(b, s // tm),
        in_specs=[
            tok(d),
            pl.BlockSpec((1, 6, d), lambda bi, i: (jnp.where(i < n_ctx_tiles, b, bi), 0, 0)),
            _resident((1, d)),
            _resident(wq.shape), _resident(wk.shape), _resident(wv.shape),
            _resident((1, LANES)), _resident((1, LANES)),
            tab, tab,
        ],
        out_specs=[tok(nq), tok(nk), tok(nv)],
        out_shape=[jax.ShapeDtypeStruct((b, s, n), BF16) for n in (nq, nk, nv)],
        compiler_params=_cparams(2),
        name=name,
    )(xs, mods, g, wq, wk, wv, qg, kg, cos, sin)


def _mla_proj_kernel(x_ref, mod_ref, g_ref, win_ref, qn_ref, kvn_ref, wuq_ref, wuk_ref, wuv_ref, cos_ref, sin_ref,
                     q_ref, k_ref, v_ref):
    h = _norm_mod(x_ref[0], g_ref[...], mod_ref[0, 0:1, :], mod_ref[0, 1:2, :]).astype(BF16)
    tm = h.shape[0]
    t1 = jnp.dot(h, win_ref[...], preferred_element_type=F32)
    cq = t1[:, :MLA_Q_LORA]
    ckv = t1[:, MLA_Q_LORA:MLA_Q_LORA + MLA_KV_LORA]
    kr = t1[:, MLA_Q_LORA + MLA_KV_LORA:]
    cqn = (cq * lax.rsqrt(jnp.mean(cq * cq, axis=-1, keepdims=True) + EPS) * qn_ref[...]).astype(BF16)
    ckvn = (ckv * lax.rsqrt(jnp.mean(ckv * ckv, axis=-1, keepdims=True) + EPS) * kvn_ref[...]).astype(BF16)
    cos, sin = cos_ref[...], sin_ref[...]
    quarter = MLA_ROPE_DIM // 4
    lo = _lo_mask((tm, LANES), quarter)
    kr = _rope_slot(kr, cos, sin, lo, quarter)
    n = wuq_ref.shape[1]
    for c0 in range(0, n, 2 * LANES):
        q2 = jnp.dot(cqn, wuq_ref[:, c0:c0 + 2 * LANES], preferred_element_type=F32)
        k2 = jnp.dot(ckvn, wuk_ref[:, c0:c0 + 2 * LANES], preferred_element_type=F32)
        for s0 in (0, LANES):
            q_ref[0, :, c0 + s0:c0 + s0 + LANES] = _rope_slot(q2[:, s0:s0 + LANES], cos, sin, lo, quarter).astype(BF16)
            k_ref[0, :, c0 + s0:c0 + s0 + LANES] = (k2[:, s0:s0 + LANES] + kr).astype(BF16)
    v_ref[0] = jnp.dot(ckvn, wuv_ref[...], preferred_element_type=F32).astype(BF16)


def _project_mla(xs, mods, g, win, qn, kvn, wuq, wuk, wuv, cos, sin, *, n_ctx_tiles):
    b, s, d = xs.shape
    tm = TOKEN_TILE
    nq, nk, nv = wuq.shape[1], wuk.shape[1], wuv.shape[1]
    tok = lambda n: pl.BlockSpec((1, tm, n), lambda bi, i: (bi, i, 0))
    tab = pl.BlockSpec((tm, LANES), lambda bi, i: (i, 0))
    return pl.pallas_call(
        _mla_proj_kernel,
        grid=(b, s // tm),
        in_specs=[
            tok(d),
            pl.BlockSpec((1, 6, d), lambda bi, i: (jnp.where(i < n_ctx_tiles, b, bi), 0, 0)),
            _resident((1, d)),
            _resident(win.shape), _resident(qn.shape), _resident(kvn.shape),
            _resident(wuq.shape), _resident(wuk.shape), _resident(wuv.shape),
            tab, tab,
        ],
        out_specs=[tok(nq), tok(nk), tok(nv)],
        out_shape=[jax.ShapeDtypeStruct((b, s, n), BF16) for n in (nq, nk, nv)],
        compiler_params=_cparams(2),
        name="proj_mla",
    )(xs, mods, g, win, qn, kvn, wuq, wuk, wuv, cos, sin)


def _softmax_step(q, k, v, m, l, acc, c):
    s = lax.dot_general(q, k, (((1,), (1,)), ((), ())), preferred_element_type=F32)
    m_new = jnp.maximum(m, jnp.max(s, axis=-1, keepdims=True))
    alpha = jnp.exp2((m - m_new) * c)
    p = jnp.exp2((s - m_new) * c)
    l = alpha * l + jnp.sum(p, axis=-1, keepdims=True)
    acc = alpha * acc + jnp.dot(p.astype(BF16), v, preferred_element_type=F32)
    return m_new, l, acc


def _attend(q, k_ref, v_ref, k_cols, n_keys, chunk, c):
    tq = q.shape[0]
    m0 = jnp.full((tq, 1), -jnp.inf, F32)
    l0 = jnp.zeros((tq, 1), F32)
    a0 = jnp.zeros((tq, LANES), F32)
    if n_keys <= chunk:
        _, l, acc = _softmax_step(q, k_ref[0, 0:n_keys, k_cols[0]:k_cols[1]], v_ref[0, 0:n_keys, :], m0, l0, a0, c)
        return acc, l

    def body(i, carry):
        start = pl.multiple_of(i * chunk, chunk)
        kc = k_ref[0, pl.ds(start, chunk), k_cols[0]:k_cols[1]]
        vc = v_ref[0, pl.ds(start, chunk), :]
        return _softmax_step(q, kc, vc, *carry, c)

    _, l, acc = lax.fori_loop(0, n_keys // chunk, body, (m0, l0, a0))
    return acc, l


def _halves(lo_src, hi_src):
    lane = lax.broadcasted_iota(jnp.int32, lo_src.shape, 1)
    return jnp.where(lane < LANES // 2, lo_src, hi_src)


def _flash_kernel(q_ref, k_ref, v_ref, lam_ref, subg_ref, o_ref, *, mode, k_cols, n_ctx, n_ctx_tiles, c, diff_scale):
    qi = pl.program_id(2)
    n_all = k_ref.shape[1]

    def run(n_keys):
        res = [_attend(q_ref[0, :, a * LANES:(a + 1) * LANES], k_ref, v_ref, k_cols[a], n_keys, KV_CHUNK, c)
               for a in (0, 1)]
        (acc_a, l_a), (acc_b, l_b) = res
        oa, ob = acc_a / l_a, acc_b / l_b
        if mode == "slot_lo":
            out = _halves(oa, pltpu.roll(ob, LANES // 2, axis=1))
        elif mode == "pair":
            out = _halves(oa, ob)
        else:
            lam = lam_ref[...]
            lam_full = (jnp.exp(jnp.sum(lam[0:1] * lam[1:2], axis=-1, keepdims=True))
                        - jnp.exp(jnp.sum(lam[2:3] * lam[3:4], axis=-1, keepdims=True))
                        + (1.0 - diff_scale))
            o = oa - lam_full * ob
            out = o * lax.rsqrt(jnp.mean(o * o, axis=-1, keepdims=True) + EPS) * subg_ref[...] * diff_scale
        o_ref[0] = out.astype(BF16)

    @pl.when(qi < n_ctx_tiles)
    def _():
        run(n_ctx)

    @pl.when(qi >= n_ctx_tiles)
    def _():
        run(n_all)


def _flash(q, k, v, lam, subg, *, mode, kw, k_cols, k_index, v_index, n_ctx, scale, diff_scale, name):
    b, s, nq = q.shape
    tq = TOKEN_TILE
    n_pairs = nq // (2 * LANES)
    assert s % KV_CHUNK == 0 and n_ctx <= KV_CHUNK and n_ctx % tq == 0
    return pl.pallas_call(
        functools.partial(_flash_kernel, mode=mode, k_cols=k_cols, n_ctx=n_ctx, n_ctx_tiles=n_ctx // tq,
                          c=scale * LOG2E, diff_scale=diff_scale),
        grid=(b, n_pairs, s // tq),
        in_specs=[
            pl.BlockSpec((1, tq, 2 * LANES), lambda bi, p, i: (bi, i, p)),
            pl.BlockSpec((1, s, kw), lambda bi, p, i: (bi, 0, k_index(p))),
            pl.BlockSpec((1, s, LANES), lambda bi, p, i: (bi, 0, v_index(p))),
            _resident(lam.shape), _resident(subg.shape),
        ],
        out_specs=pl.BlockSpec((1, tq, LANES), lambda bi, p, i: (bi, i, p)),
        out_shape=jax.ShapeDtypeStruct((b, s, n_pairs * LANES), BF16),
        compiler_params=_cparams(3),
        name=name,
    )(q, k, v, lam, subg)


def _na_tables(t):
    rows = t // GRID_W
    win_rows = min(NA_ROWS, rows)
    band_rows = min(win_rows + 1, rows)
    assert rows >= NA_WIN_ROWS and NA_QBLOCK % GRID_W == 0
    qrows = NA_QBLOCK // GRID_W
    nb = t // NA_QBLOCK
    nwin = NA_WIN_ROWS * GRID_W
    q_off, k_off = np.arange(NA_QBLOCK), np.arange(nwin)
    q_dr, q_col = q_off // GRID_W, q_off % GRID_W
    k_dr, k_col = k_off // GRID_W, k_off % GRID_W
    col_start = np.clip(q_col - NA_COLS // 2, 0, GRID_W - NA_COLS)
    col_in = (k_col[None, :] >= col_start[:, None]) & (k_col[None, :] < col_start[:, None] + NA_COLS)
    dc_idx = np.clip(k_col[None, :] - q_col[:, None] + NA_COLS - 1, 0, 2 * NA_COLS - 2)
    patterns, var, wstart = {}, [], []
    for j in range(nb):
        r = j * qrows + q_dr
        row_start = np.clip(r - win_rows // 2, 0, rows - win_rows)
        b0 = min(int(row_start[0]), rows - band_rows)
        w0 = min(b0 - b0 % 2, rows - NA_WIN_ROWS)
        assert w0 % qrows == 0 and w0 <= b0 and b0 + band_rows <= w0 + NA_WIN_ROWS
        k_row = w0 + k_dr
        in_band = (k_row >= b0) & (k_row < b0 + band_rows)
        in_win = (col_in & in_band[None, :] & (k_row[None, :] >= row_start[:, None])
                  & (k_row[None, :] < row_start[:, None] + win_rows))
        dr_idx = np.clip(k_row[None, :] - r[:, None] + NA_ROWS - 1, 0, 2 * NA_ROWS - 2)
        idx = np.where(in_win, dr_idx * (2 * NA_COLS - 1) + dc_idx, -1).astype(np.int32)
        key = idx.tobytes()
        if key not in patterns:
            patterns[key] = (len(patterns), idx)
        var.append(patterns[key][0])
        wstart.append(w0 * GRID_W)
    idx_all = np.stack([p[1] for p in sorted(patterns.values(), key=lambda p: p[0])])
    return idx_all.reshape(len(patterns), 1, -1), np.asarray(var, np.int32), np.asarray(wstart, np.int32)


def _na_bias_kernel(idx_ref, rpb_ref, o_ref):
    idx = idx_ref[0]
    r = rpb_ref[...]
    onehot = (lax.broadcasted_iota(jnp.int32, (r.shape[1], idx.shape[1]), 0) == idx).astype(BF16)
    b1 = r.astype(BF16)
    r2 = r - b1.astype(F32)
    b2 = r2.astype(BF16)
    b3 = (r2 - b2.astype(F32)).astype(BF16)
    out = (jnp.dot(b1, onehot, preferred_element_type=F32) + jnp.dot(b2, onehot, preferred_element_type=F32)
           + jnp.dot(b3, onehot, preferred_element_type=F32))
    o_ref[0] = jnp.where(idx >= 0, out, NEG_INF)


def _na_bias(idx_all, rpb):
    nvar, _, npos = idx_all.shape
    heads = rpb.shape[0]
    nr = (2 * NA_ROWS - 1) * (2 * NA_COLS - 1)
    nr_pad = pl.cdiv(nr, LANES) * LANES
    rflat = jnp.pad(rpb.reshape(heads, nr), ((0, 0), (0, nr_pad - nr)))
    tn = 2048
    assert npos % tn == 0
    out = pl.pallas_call(
        _na_bias_kernel,
        grid=(nvar, npos // tn),
        in_specs=[pl.BlockSpec((1, 1, tn), lambda v, j: (v, 0, j)), _resident(rflat.shape)],
        out_specs=pl.BlockSpec((1, heads, tn), lambda v, j: (v, 0, j)),
        out_shape=jax.ShapeDtypeStruct((nvar, heads, npos), F32),
        compiler_params=_cparams(2),
        name="na_bias",
    )(jnp.asarray(idx_all), rflat)
    return out.reshape(nvar, heads, NA_QBLOCK, npos // NA_QBLOCK)


def _na_kernel(var_ref, ws_ref, q_ref, k_ref, v_ref, b0_ref, b1_ref, o_ref, *, n_ctx, n_ctx_tiles, scale):
    qi = pl.program_id(2)
    nwin = NA_WIN_ROWS * GRID_W
    nt = (((1,), (1,)), ((), ()))

    @pl.when(qi < n_ctx_tiles)
    def _():
        outs = []
        for a in (0, 1):
            acc, l = _attend(q_ref[0, :, a * LANES:(a + 1) * LANES], k_ref, v_ref, (0, LANES), n_ctx, n_ctx,
                             scale * LOG2E)
            outs.append(acc / l)
        o_ref[0] = _halves(outs[0], outs[1]).astype(BF16)

    @pl.when(qi >= n_ctx_tiles)
    def _():
        kc, vc = k_ref[0, 0:n_ctx, :], v_ref[0, 0:n_ctx, :]
        for sub, bias_ref in ((0, b0_ref), (1, b1_ref)):
            j = (qi - n_ctx_tiles) * (TOKEN_TILE // NA_QBLOCK) + sub
            start = pl.multiple_of(n_ctx + ws_ref[j], NA_QBLOCK)
            kw, vw = k_ref[0, pl.ds(start, nwin), :], v_ref[0, pl.ds(start, nwin), :]
            outs = []
            for a in (0, 1):
                q = q_ref[0, sub * NA_QBLOCK:(sub + 1) * NA_QBLOCK, a * LANES:(a + 1) * LANES]
                s_c = lax.dot_general(q, kc, nt, preferred_element_type=F32) * scale
                s_w = lax.dot_general(q, kw, nt, preferred_element_type=F32) * scale + bias_ref[0, a]
                m = jnp.maximum(jnp.max(s_c, axis=-1, keepdims=True), jnp.max(s_w, axis=-1, keepdims=True))
                p_c, p_w = jnp.exp(s_c - m), jnp.exp(s_w - m)
                l = jnp.sum(p_c, axis=-1, keepdims=True) + jnp.sum(p_w, axis=-1, keepdims=True)
                acc = (jnp.dot(p_c.astype(BF16), vc, preferred_element_type=F32)
                       + jnp.dot(p_w.astype(BF16), vw, preferred_element_type=F32))
                outs.append(acc / l)
            o_ref[0, sub * NA_QBLOCK:(sub + 1) * NA_QBLOCK, :] = _halves(outs[0], outs[1]).astype(BF16)


def _na_attention(q, k, v, bias, var, wstart, *, n_ctx, scale):
    b, s, nq = q.shape
    tq = TOKEN_TILE
    n_pairs = nq // (2 * LANES)
    n_ctx_tiles = n_ctx // tq
    per_tile = tq // NA_QBLOCK
    nwin = NA_WIN_ROWS * GRID_W

    def bias_spec(sub):
        def index(bi, p, i, var_ref, ws_ref):
            j = jnp.maximum(i - n_ctx_tiles, 0) * per_tile + sub
            return (var_ref[j], p, 0, 0)
        return pl.BlockSpec((1, 2, NA_QBLOCK, nwin), index)

    grid_spec = pltpu.PrefetchScalarGridSpec(
        num_scalar_prefetch=2,
        grid=(b, n_pairs, s // tq),
        in_specs=[
            pl.BlockSpec((1, tq, 2 * LANES), lambda bi, p, i, *_: (bi, i, p)),
            pl.BlockSpec((1, s, LANES), lambda bi, p, i, *_: (bi, 0, p)),
            pl.BlockSpec((1, s, LANES), lambda bi, p, i, *_: (bi, 0, p)),
            bias_spec(0), bias_spec(1),
        ],
        out_specs=pl.BlockSpec((1, tq, LANES), lambda bi, p, i, *_: (bi, i, p)),
    )
    return pl.pallas_call(
        functools.partial(_na_kernel, n_ctx=n_ctx, n_ctx_tiles=n_ctx_tiles, scale=scale),
        grid_spec=grid_spec,
        out_shape=jax.ShapeDtypeStruct((b, s, n_pairs * LANES), BF16),
        compiler_params=_cparams(3),
        name="na_attn",
    )(var, wstart, q, k, v, bias, bias)


def _outproj_kernel(x_ref, o_ref, w_ref, mod_ref, y_ref):
    y = jnp.dot(o_ref[0], w_ref[...], preferred_element_type=F32)
    y_ref[0] = x_ref[0] + mod_ref[0, 2:3, :] * y


def _outproj(xs, o, w, mods, *, n_ctx_tiles):
    b, s, d = xs.shape
    tm = TOKEN_TILE
    return pl.pallas_call(
        _outproj_kernel,
        grid=(b, s // tm),
        in_specs=[
            pl.BlockSpec((1, tm, d), lambda bi, i: (bi, i, 0)),
            pl.BlockSpec((1, tm, o.shape[2]), lambda bi, i: (bi, i, 0)),
            _resident(w.shape),
            pl.BlockSpec((1, 6, d), lambda bi, i: (jnp.where(i < n_ctx_tiles, b, bi), 0, 0)),
        ],
        out_specs=pl.BlockSpec((1, tm, d), lambda bi, i: (bi, i, 0)),
        out_shape=jax.ShapeDtypeStruct((b, s, d), F32),
        compiler_params=_cparams(2),
        name="outproj",
    )(xs, o, w, mods)


def _ffn_kernel(xp_ref, x_ref, xn_ref, mod_ref, g_ref, wup_ref, cw_ref, cb_ref, wdn_ref, y_ref, u_scr, a_scr,
                *, n_ctx_tiles, n_tiles, chunk):
    i = pl.program_id(1)
    tm = x_ref.shape[1]
    has_prev = jnp.logical_and(i != 0, i != n_ctx_tiles).astype(F32)
    has_next = jnp.logical_and(i != n_ctx_tiles - 1, i != n_tiles - 1).astype(F32)
    g, shift, scale = g_ref[...], mod_ref[0, 3:4, :], mod_ref[0, 4:5, :]
    x = x_ref[0]
    h = jnp.concatenate([_norm_mod(xp_ref[0], g, shift, scale) * has_prev,
                         _norm_mod(x, g, shift, scale),
                         _norm_mod(xn_ref[0], g, shift, scale) * has_next], axis=0).astype(BF16)
    n_chunks = wup_ref.shape[1] // (2 * chunk)
    for j in range(n_chunks):
        cols = slice(j * 2 * chunk, (j + 1) * 2 * chunk)
        u_scr[...] = jnp.dot(h, wup_ref[:, cols], preferred_element_type=F32)
        cw = cw_ref[:, cols]
        u = (u_scr[HALO - 1:HALO - 1 + tm, :] * cw[0:1] + u_scr[HALO:HALO + tm, :] * cw[1:2]
             + u_scr[HALO + 1:HALO + 1 + tm, :] * cw[2:3] + cb_ref[:, cols])
        val, gate = u[:, :chunk], u[:, chunk:]
        a_scr[:, j * chunk:(j + 1) * chunk] = (gate * (1.0 / (1.0 + jnp.exp(-gate))) * val).astype(BF16)
    y = jnp.dot(a_scr[...], wdn_ref[...], preferred_element_type=F32)
    y_ref[0] = x + mod_ref[0, 5:6, :] * y


def _ffn(xs, mods, g, wup, cw, cb, wdn, *, n_ctx_tiles, chunk):
    b, s, d = xs.shape
    tm = TOKEN_TILE
    n_tiles = s // tm
    per = tm // HALO
    f = wdn.shape[0]
    return pl.pallas_call(
        functools.partial(_ffn_kernel, n_ctx_tiles=n_ctx_tiles, n_tiles=n_tiles, chunk=chunk),
        grid=(b, n_tiles),
        in_specs=[
            pl.BlockSpec((1, HALO, d), lambda bi, i: (bi, jnp.maximum(i * per - 1, 0), 0)),
            pl.BlockSpec((1, tm, d), lambda bi, i: (bi, i, 0)),
            pl.BlockSpec((1, HALO, d), lambda bi, i: (bi, jnp.minimum((i + 1) * per, s // HALO - 1), 0)),
            pl.BlockSpec((1, 6, d), lambda bi, i: (jnp.where(i < n_ctx_tiles, b, bi), 0, 0)),
            _resident((1, d)),
            pl.BlockSpec(wup.shape, lambda bi, i: (0, 0), pipeline_mode=pl.Buffered(1)),
            _resident(cw.shape), _resident(cb.shape),
            pl.BlockSpec(wdn.shape, lambda bi, i: (0, 0), pipeline_mode=pl.Buffered(1)),
        ],
        out_specs=pl.BlockSpec((1, tm, d), lambda bi, i: (bi, i, 0)),
        out_shape=jax.ShapeDtypeStruct((b, s, d), F32),
        scratch_shapes=[pltpu.VMEM((tm + 2 * HALO, 2 * chunk), F32), pltpu.VMEM((tm, f), BF16)],
        compiler_params=_cparams(2),
        name="ffn",
    )(xs, xs, xs, mods, g, wup, cw, cb, wdn)


def _final_kernel(x_ref, g_ref, o_ref):
    x = x_ref[0]
    o_ref[0] = x * lax.rsqrt(jnp.mean(x * x, axis=-1, keepdims=True) + EPS) * g_ref[...]


def _final_norm(xs, g, *, n_ctx_tiles):
    b, s, d = xs.shape
    tm = TOKEN_TILE
    nt = s // tm - n_ctx_tiles
    return pl.pallas_call(
        _final_kernel,
        grid=(b, nt),
        in_specs=[pl.BlockSpec((1, tm, d), lambda bi, i: (bi, i + n_ctx_tiles, 0)), _resident((1, d))],
        out_specs=pl.BlockSpec((1, tm, d), lambda bi, i: (bi, i, 0)),
        out_shape=jax.ShapeDtypeStruct((b, nt * tm, d), F32),
        compiler_params=_cparams(2),
        name="final_norm",
    )(xs, g)


def _rope_table(t, n_ctx, rot_dim, lane_lo):
    tt = jnp.arange(t)
    row = (tt // GRID_W).astype(F32)
    col = (tt % GRID_W).astype(F32)
    half = rot_dim // 2
    inv = ROPE_BASE ** (-jnp.arange(0, half, 2, dtype=F32) / half)
    ar, ac = row[:, None] * inv, col[:, None] * inv
    ang = jnp.concatenate([ar, ar, ac, ac], axis=-1)
    quarter = rot_dim // 4
    sign = jnp.where((jnp.arange(rot_dim) % (2 * quarter)) < quarter, -1.0, 1.0).astype(F32)
    cos, sin = jnp.cos(ang), jnp.sin(ang) * sign
    reps = (LANES - lane_lo) // rot_dim if lane_lo == 0 else 1
    cos_l = jnp.ones((t, LANES), F32).at[:, lane_lo:lane_lo + reps * rot_dim].set(jnp.tile(cos, (1, reps)))
    sin_l = jnp.zeros((t, LANES), F32).at[:, lane_lo:lane_lo + reps * rot_dim].set(jnp.tile(sin, (1, reps)))
    cos_s = jnp.concatenate([jnp.ones((n_ctx, LANES), F32), cos_l], axis=0)
    sin_s = jnp.concatenate([jnp.zeros((n_ctx, LANES), F32), sin_l], axis=0)
    return cos_s, sin_s


def _to_slots(w, n_heads, width, halves):
    d = w.shape[0]
    wh = w.reshape(d, n_heads, width)
    out = jnp.zeros((d, n_heads, LANES), w.dtype)
    for h in range(n_heads):
        out = out.at[:, h, halves[h] * (LANES // 2):halves[h] * (LANES // 2) + width].set(wh[:, h])
    return out.reshape(d, n_heads * LANES)


def _dup(g):
    return jnp.concatenate([g, g]).reshape(1, LANES).astype(F32)


def kernel(x, c, ctx, c_ctx, ada_w, ada_b, norm1_g, norm2_g, ffn_w_up, ffn_conv_w, ffn_conv_b, ffn_w_down,
           gqa_w_in, gqa_q_norm_g, gqa_k_norm_g, gqa_w_out,
           mla_w_in, mla_q_norm_g, mla_kv_norm_g, mla_w_uq, mla_w_ukv, mla_w_out,
           diff_w_in, diff_lambda, diff_subln_g, diff_w_out,
           na_w_in, na_rpb, na_w_out, final_norm_g):
    b, t, d = x.shape
    n_ctx = ctx.shape[1]
    depth = ada_w.shape[0]
    assert n_ctx % TOKEN_TILE == 0 and t % TOKEN_TILE == 0 and b + 1 <= 8
    nct = n_ctx // TOKEN_TILE
    f = ffn_w_down.shape[1]
    chunk = 2 * LANES
    assert f % chunk == 0

    xs = jnp.concatenate([ctx, x], axis=1)
    cin = jnp.zeros((8, d), F32).at[:b].set(c).at[b].set(c_ctx)
    mods_all = _ada_all(cin, ada_w, ada_b).reshape(depth, 8, 6, d)

    ones_l = jnp.ones((1, LANES), F32)
    dummy_tab = jnp.zeros((xs.shape[1], LANES), F32)
    cos64, sin64 = _rope_table(t, n_ctx, GQA_HEAD_DIM, 0)
    cos_mla, sin_mla = _rope_table(t, n_ctx, MLA_ROPE_DIM, MLA_NOPE_DIM)
    no_lam, no_g = jnp.zeros((4, DIFF_HEAD_DIM), F32), jnp.ones((1, LANES), F32)

    for i in range(depth):
        m, j = i % N_MIXERS, i // N_MIXERS
        mods = mods_all[i]
        g1 = norm1_g[i].reshape(1, d)
        if m == 0:
            w = gqa_w_in[j].astype(BF16)
            nqc, nkc = GQA_HEADS * GQA_HEAD_DIM, GQA_KV_HEADS * GQA_HEAD_DIM
            wq = _to_slots(w[:, :nqc], GQA_HEADS, GQA_HEAD_DIM, [0] * GQA_HEADS)
            wk = _to_slots(w[:, nqc:nqc + nkc], GQA_KV_HEADS, GQA_HEAD_DIM, [0] * GQA_KV_HEADS)
            wv = _to_slots(w[:, nqc + nkc:], GQA_KV_HEADS, GQA_HEAD_DIM, [0] * GQA_KV_HEADS)
            q, k, v = _project(xs, mods, g1, wq, wk, wv, _dup(gqa_q_norm_g[j]), _dup(gqa_k_norm_g[j]), cos64, sin64,
                               qk_norm=True, rope=True, head_dim=GQA_HEAD_DIM, n_ctx_tiles=nct, name="proj_gqa")
            o = _flash(q, k, v, no_lam, no_g, mode="slot_lo", kw=LANES, k_cols=((0, LANES), (0, LANES)),
                       k_index=lambda p: p // 2, v_index=lambda p: p // 2, n_ctx=n_ctx,
                       scale=GQA_HEAD_DIM ** -0.5, diff_scale=1.0, name="attn_gqa")
            w_out = gqa_w_out[j]
        elif m == 1:
            w = mla_w_in[j].astype(BF16)
            nlat = MLA_Q_LORA + MLA_KV_LORA
            kr_slot = jnp.zeros((d, LANES), BF16).at[:, MLA_NOPE_DIM:MLA_NOPE_DIM + MLA_ROPE_DIM].set(w[:, nlat:])
            win = jnp.concatenate([w[:, :nlat], kr_slot], axis=1)
            qk_dim = MLA_NOPE_DIM + MLA_ROPE_DIM
            wuq = _to_slots(mla_w_uq[j].astype(BF16), MLA_HEADS, qk_dim, [0] * MLA_HEADS)
            wukv = mla_w_ukv[j].astype(BF16).reshape(MLA_KV_LORA, MLA_HEADS, MLA_NOPE_DIM + MLA_V_DIM)
            wuk = _to_slots(wukv[:, :, :MLA_NOPE_DIM].reshape(MLA_KV_LORA, -1), MLA_HEADS, MLA_NOPE_DIM,
                            [0] * MLA_HEADS)
            wuv = wukv[:, :, MLA_NOPE_DIM:].reshape(MLA_KV_LORA, MLA_HEADS * MLA_V_DIM)
            q, k, v = _project_mla(xs, mods, g1, win, mla_q_norm_g[j].reshape(1, -1), mla_kv_norm_g[j].reshape(1, -1),
                                   wuq, wuk, wuv, cos_mla, sin_mla, n_ctx_tiles=nct)
            o = _flash(q, k, v, no_lam, no_g, mode="pair", kw=2 * LANES, k_cols=((0, LANES), (LANES, 2 * LANES)),
                       k_index=lambda p: p, v_index=lambda p: p, n_ctx=n_ctx,
                       scale=qk_dim ** -0.5, diff_scale=1.0, name="attn_mla")
            w_out = mla_w_out[j]
        elif m == 2:
            w = diff_w_in[j].astype(BF16)
            wq = _to_slots(w[:, :d], 2 * DIFF_HEADS, DIFF_HEAD_DIM, [0, 1] * DIFF_HEADS)
            lambda_init = 0.8 - 0.6 * math.exp(-0.3 * i)
            q, k, v = _project(xs, mods, g1, wq, w[:, d:2 * d], w[:, 2 * d:], ones_l, ones_l, cos64, sin64,
                               qk_norm=False, rope=True, head_dim=DIFF_HEAD_DIM, n_ctx_tiles=nct, name="proj_diff")
            o = _flash(q, k, v, diff_lambda[j], diff_subln_g[j].reshape(1, LANES), mode="diff", kw=LANES,
                       k_cols=((0, LANES), (0, LANES)), k_index=lambda p: p, v_index=lambda p: p, n_ctx=n_ctx,
                       scale=DIFF_HEAD_DIM ** -0.5, diff_scale=1.0 - lambda_init, name="attn_diff")
            w_out = diff_w_out[j]
        else:
            w = na_w_in[j].astype(BF16)
            wq = _to_slots(w[:, :d], NA_HEADS, NA_HEAD_DIM, [0, 1] * (NA_HEADS // 2))
            q, k, v = _project(xs, mods, g1, wq, w[:, d:2 * d], w[:, 2 * d:], ones_l, ones_l, dummy_tab, dummy_tab,
                               qk_norm=False, rope=False, head_dim=NA_HEAD_DIM, n_ctx_tiles=nct, name="proj_na")
            idx_all, var, wstart = _na_tables(t)
            bias = _na_bias(idx_all, na_rpb[j])
            o = _na_attention(q, k, v, bias, jnp.asarray(var), jnp.asarray(wstart), n_ctx=n_ctx,
                              scale=NA_HEAD_DIM ** -0.5)
            w_out = na_w_out[j]
        xs = _outproj(xs, o, w_out.astype(BF16), mods, n_ctx_tiles=nct)

        def regroup(a):
            lead = a.shape[:-1]
            return jnp.stack([a[..., :f].reshape(*lead, f // chunk, chunk),
                              a[..., f:].reshape(*lead, f // chunk, chunk)], axis=-2).reshape(*lead, 2 * f)
        xs = _ffn(xs, mods, norm2_g[i].reshape(1, d), regroup(ffn_w_up[i]).astype(BF16), regroup(ffn_conv_w[i]),
                  regroup(ffn_conv_b[i]).reshape(1, 2 * f), ffn_w_down[i].astype(BF16), n_ctx_tiles=nct, chunk=chunk)

    return _final_norm(xs, final_norm_g.reshape(1, d), n_ctx_tiles=nct)
```

```python
import functools
import math

import numpy as np
import jax
import jax.numpy as jnp
from jax import lax
from jax.experimental import pallas as pl
from jax.experimental.pallas import tpu as pltpu

F32 = jnp.float32
BF16 = jnp.bfloat16

LANES = 128
TOKEN_TILE = 256
HALO = 8
KV_CHUNK = 768
VMEM_LIMIT = 56 * 1024 * 1024

GRID_W = 64
ROPE_BASE = 10000.0
EPS = 1e-6
NEG_INF = -1e30
LOG2E = 1.4426950408889634

GQA_HEADS, GQA_KV_HEADS, GQA_HEAD_DIM = 16, 4, 64
MLA_HEADS, MLA_NOPE_DIM, MLA_ROPE_DIM, MLA_V_DIM = 16, 64, 32, 64
MLA_Q_LORA, MLA_KV_LORA = 384, 256
DIFF_HEADS, DIFF_HEAD_DIM = 8, 64
NA_HEADS, NA_HEAD_DIM, NA_ROWS, NA_COLS = 16, 64, 8, 16
NA_QBLOCK = 128
NA_WIN_ROWS = 10
N_MIXERS = 4

NT_DIMS = (((1,), (1,)), ((), ()))


def _cparams(n_axes):
    return pltpu.CompilerParams(dimension_semantics=("arbitrary",) * n_axes, vmem_limit_bytes=VMEM_LIMIT)


def _resident(shape):
    nd = len(shape)
    return pl.BlockSpec(shape, lambda *_: (0,) * nd)


def _ada_kernel(c_ref, w_ref, b_ref, o_ref):
    c = c_ref[...]
    s = c * (1.0 / (1.0 + jnp.exp(-c)))
    o_ref[0] = jnp.dot(s.astype(BF16), w_ref[0].astype(BF16), preferred_element_type=F32) + b_ref[0]


def _ada_all(cin, ada_w, ada_b):
    depth, d, n = ada_w.shape
    tn = n // 4
    return pl.pallas_call(
        _ada_kernel,
        grid=(depth, n // tn),
        in_specs=[
            pl.BlockSpec((8, d), lambda l, j: (0, 0)),
            pl.BlockSpec((1, d, tn), lambda l, j: (l, 0, j)),
            pl.BlockSpec((1, 1, tn), lambda l, j: (l, 0, j)),
        ],
        out_specs=pl.BlockSpec((1, 8, tn), lambda l, j: (l, 0, j)),
        out_shape=jax.ShapeDtypeStruct((depth, 8, n), F32),
        compiler_params=_cparams(2),
        name="ada",
    )(cin, ada_w, ada_b.reshape(depth, 1, n))


def _norm_mod(x, g, shift, scale):
    ms = jnp.mean(x * x, axis=-1, keepdims=True)
    return (x * lax.rsqrt(ms + EPS) * g) * (1.0 + scale) + shift


def _rope_slot(y, cos, sin_signed, lo_mask, quarter):
    fwd = pltpu.roll(y, LANES - quarter, axis=1)
    bwd = pltpu.roll(y, quarter, axis=1)
    return y * cos + jnp.where(lo_mask, fwd, bwd) * sin_signed


def _lo_mask(shape, quarter):
    lane = lax.broadcasted_iota(jnp.int32, shape, 1)
    return (lane % (2 * quarter)) < quarter


def _vt_spec(rows, tm):
    per = KV_CHUNK // tm
    return pl.BlockSpec((1, 1, rows, tm), lambda bi, i: (bi, i // per, 0, i % per))


def _proj_kernel(x_ref, mod_ref, g_ref, wq_ref, wk_ref, wv_ref, qg_ref, kg_ref, cosq_ref, sinq_ref, cosk_ref, sink_ref,
                 vones_ref, q_ref, k_ref, v_ref, *, qk_norm, rope, head_dim, v_transposed):
    h = _norm_mod(x_ref[0], g_ref[...], mod_ref[0, 0:1, :], mod_ref[0, 1:2, :]).astype(BF16)
    tm = h.shape[0]
    lo = _lo_mask((tm, LANES), head_dim // 4) if rope else None

    def finish(w_ref, o_ref, gain_ref, cos_ref, sin_ref):
        n = w_ref.shape[1]
        for c0 in range(0, n, 2 * LANES):
            c1 = min(c0 + 2 * LANES, n)
            y2 = jnp.dot(h, w_ref[:, c0:c1], preferred_element_type=F32)
            for s0 in range(0, c1 - c0, LANES):
                y = y2[:, s0:s0 + LANES]
                if qk_norm:
                    ms = jnp.sum(y * y, axis=-1, keepdims=True) * (1.0 / head_dim)
                    y = y * lax.rsqrt(ms + EPS) * gain_ref[...]
                if rope:
                    y = _rope_slot(y, cos_ref[...], sin_ref[...], lo, head_dim // 4)
                o_ref[0, :, c0 + s0:c0 + s0 + LANES] = y.astype(BF16)

    finish(wq_ref, q_ref, qg_ref, cosq_ref, sinq_ref)
    finish(wk_ref, k_ref, kg_ref, cosk_ref, sink_ref)
    if v_transposed:
        vt = lax.dot_general(wv_ref[...], h, NT_DIMS, preferred_element_type=F32) + vones_ref[...]
        v_ref[0, 0] = vt.astype(BF16)
    else:
        v_ref[0] = jnp.dot(h, wv_ref[...], preferred_element_type=F32).astype(BF16)


def _project(xs, mods, g, wq, wk, wv, qg, kg, cosq, sinq, cosk, sink, vones, *, qk_norm, rope, head_dim, v_transposed,
             n_ctx_tiles, name):
    b, s, d = xs.shape
    tm = TOKEN_TILE
    nq, nk = wq.shape[1], wk.shape[1]
    tok = lambda n: pl.BlockSpec((1, tm, n), lambda bi, i: (bi, i, 0))
    tab = pl.BlockSpec((tm, LANES), lambda bi, i: (i, 0))
    if v_transposed:
        v_spec = _vt_spec(wv.shape[0], tm)
        v_shape = jax.ShapeDtypeStruct((b, s // KV_CHUNK, wv.shape[0], KV_CHUNK), BF16)
    else:
        v_spec, v_shape = tok(wv.shape[1]), jax.ShapeDtypeStruct((b, s, wv.shape[1]), BF16)
    return pl.pallas_call(
        functools.partial(_proj_kernel, qk_norm=qk_norm, rope=rope, head_dim=head_dim, v_transposed=v_transposed),
        grid=(b, s // tm),
        in_specs=[
            tok(d),
            pl.BlockSpec((1, 6, d), lambda bi, i: (jnp.where(i < n_ctx_tiles, b, bi), 0, 0)),
            _resident((1, d)),
            _resident(wq.shape), _resident(wk.shape), _resident(wv.shape),
            _resident((1, LANES)), _resident((1, LANES)),
            tab, tab, tab, tab,
            _resident(vones.shape),
        ],
        out_specs=[tok(nq), tok(nk), v_spec],
        out_shape=[jax.ShapeDtypeStruct((b, s, nq), BF16), jax.ShapeDtypeStruct((b, s, nk), BF16), v_shape],
        compiler_params=_cparams(2),
        name=name,
    )(xs, mods, g, wq, wk, wv, qg, kg, cosq, sinq, cosk, sink, vones)


def _mla_proj_kernel(x_ref, mod_ref, g_ref, win_ref, qn_ref, kvn_ref, wuq_ref, wuk_ref, wuvt_ref,
                     cosq_ref, sinq_ref, cosk_ref, sink_ref, vones_ref, q_ref, k_ref, v_ref):
    h = _norm_mod(x_ref[0], g_ref[...], mod_ref[0, 0:1, :], mod_ref[0, 1:2, :]).astype(BF16)
    tm = h.shape[0]
    t1 = jnp.dot(h, win_ref[...], preferred_element_type=F32)
    cq = t1[:, :MLA_Q_LORA]
    ckv = t1[:, MLA_Q_LORA:MLA_Q_LORA + MLA_KV_LORA]
    kr = t1[:, MLA_Q_LORA + MLA_KV_LORA:]
    cqn = (cq * lax.rsqrt(jnp.mean(cq * cq, axis=-1, keepdims=True) + EPS) * qn_ref[...]).astype(BF16)
    ckvn = (ckv * lax.rsqrt(jnp.mean(ckv * ckv, axis=-1, keepdims=True) + EPS) * kvn_ref[...]).astype(BF16)
    quarter = MLA_ROPE_DIM // 4
    lo = _lo_mask((tm, LANES), quarter)
    kr = _rope_slot(kr, cosk_ref[...], sink_ref[...], lo, quarter)
    n = wuq_ref.shape[1]
    for c0 in range(0, n, 2 * LANES):
        q2 = jnp.dot(cqn, wuq_ref[:, c0:c0 + 2 * LANES], preferred_element_type=F32)
        k2 = jnp.dot(ckvn, wuk_ref[:, c0:c0 + 2 * LANES], preferred_element_type=F32)
        for s0 in (0, LANES):
            q_ref[0, :, c0 + s0:c0 + s0 + LANES] = _rope_slot(
                q2[:, s0:s0 + LANES], cosq_ref[...], sinq_ref[...], lo, quarter).astype(BF16)
            k_ref[0, :, c0 + s0:c0 + s0 + LANES] = (k2[:, s0:s0 + LANES] + kr).astype(BF16)
    vt = lax.dot_general(wuvt_ref[...], ckvn, NT_DIMS, preferred_element_type=F32) + vones_ref[...]
    v_ref[0, 0] = vt.astype(BF16)


def _project_mla(xs, mods, g, win, qn, kvn, wuq, wuk, wuvt, cosq, sinq, cosk, sink, vones, *, n_ctx_tiles):
    b, s, d = xs.shape
    tm = TOKEN_TILE
    nq, nk = wuq.shape[1], wuk.shape[1]
    tok = lambda n: pl.BlockSpec((1, tm, n), lambda bi, i: (bi, i, 0))
    tab = pl.BlockSpec((tm, LANES), lambda bi, i: (i, 0))
    return pl.pallas_call(
        _mla_proj_kernel,
        grid=(b, s // tm),
        in_specs=[
            tok(d),
            pl.BlockSpec((1, 6, d), lambda bi, i: (jnp.where(i < n_ctx_tiles, b, bi), 0, 0)),
            _resident((1, d)),
            _resident(win.shape), _resident(qn.shape), _resident(kvn.shape),
            _resident(wuq.shape), _resident(wuk.shape), _resident(wuvt.shape),
            tab, tab, tab, tab,
            _resident(vones.shape),
        ],
        out_specs=[tok(nq), tok(nk), _vt_spec(wuvt.shape[0], tm)],
        out_shape=[jax.ShapeDtypeStruct((b, s, nq), BF16), jax.ShapeDtypeStruct((b, s, nk), BF16),
                   jax.ShapeDtypeStruct((b, s // KV_CHUNK, wuvt.shape[0], KV_CHUNK), BF16)],
        compiler_params=_cparams(2),
        name="proj_mla",
    )(xs, mods, g, win, qn, kvn, wuq, wuk, wuvt, cosq, sinq, cosk, sink, vones)


def _flash_kernel(q_ref, k_ref, vt_ref, lam_ref, subg_ref, o_ref, st_scr, *, mode, k_cols, v_rows, vw, n_ctx,
                  n_ctx_tiles, diff_scale):
    qi = pl.program_id(2)
    tq = q_ref.shape[1]
    n_chunks = vt_ref.shape[1]
    qs = [q_ref[0, :, a * LANES:(a + 1) * LANES] for a in (0, 1)]
    n_rows = v_rows[0][1] - v_rows[0][0]

    def keys(key_rows):
        k0 = k_ref[0, key_rows, k_cols[0][0]:k_cols[0][1]]
        k1 = k0 if k_cols[1] == k_cols[0] else k_ref[0, key_rows, k_cols[1][0]:k_cols[1][1]]
        return k0, k1

    def values(chunk_idx, key_lanes):
        v0 = vt_ref[0, chunk_idx, v_rows[0][0]:v_rows[0][1], key_lanes]
        v1 = v0 if v_rows[1] == v_rows[0] else vt_ref[0, chunk_idx, v_rows[1][0]:v_rows[1][1], key_lanes]
        return v0, v1

    def update(a, st, cmax, vt_a, carry_a):
        m, acc = carry_a
        m_new = jnp.maximum(m, cmax)
        p = jnp.exp2(st - m_new).astype(BF16)
        return m_new, jnp.exp2(m - m_new) * acc + jnp.dot(vt_a, p, preferred_element_type=F32)

    def scores(slot, chunk_idx):
        start = chunk_idx * KV_CHUNK
        ks = keys(pl.ds(start if isinstance(start, int) else pl.multiple_of(start, KV_CHUNK), KV_CHUNK))
        cmax = []
        for a in (0, 1):
            st = lax.dot_general(ks[a], qs[a], NT_DIMS, preferred_element_type=F32)
            st_scr[slot, a] = st
            cmax.append(jnp.max(st, axis=0, keepdims=True))
        return tuple(cmax)

    def consume(slot, chunk_idx, cmax, carry):
        vts = values(chunk_idx, slice(None))
        return tuple(update(a, st_scr[slot, a], cmax[a], vts[a], carry[a]) for a in (0, 1))

    init = tuple((jnp.full((1, tq), -jnp.inf, F32), jnp.zeros((n_rows, tq), F32)) for _ in (0, 1))

    def finish(carry):
        (_, acc_a), (_, acc_b) = carry
        oa = acc_a[0:vw] / acc_a[vw:vw + 1]
        ob = acc_b[0:vw] / acc_b[vw:vw + 1]
        if mode == "pair":
            out = jnp.concatenate([oa, ob], axis=0).T
        else:
            lam = lam_ref[...]
            lam_full = (jnp.exp(jnp.sum(lam[0:1] * lam[1:2], axis=-1, keepdims=True))
                        - jnp.exp(jnp.sum(lam[2:3] * lam[3:4], axis=-1, keepdims=True))
                        + (1.0 - diff_scale))
            o = (oa - lam_full * ob).T
            out = o * lax.rsqrt(jnp.mean(o * o, axis=-1, keepdims=True) + EPS) * subg_ref[...] * diff_scale
        o_ref[0] = out.astype(BF16)

    @pl.when(qi < n_ctx_tiles)
    def _():
        ks, vts = keys(slice(0, n_ctx)), values(0, slice(0, n_ctx))
        out = []
        for a in (0, 1):
            st = lax.dot_general(ks[a], qs[a], NT_DIMS, preferred_element_type=F32)
            out.append(update(a, st, jnp.max(st, axis=0, keepdims=True), vts[a], init[a]))
        finish(tuple(out))

    @pl.when(qi >= n_ctx_tiles)
    def _():
        def body(i, state):
            cmax_even, carry = state
            cmax_odd = scores(1, 2 * i + 1)
            carry = consume(0, 2 * i, cmax_even, carry)
            cmax_even = scores(0, 2 * i + 2)
            carry = consume(1, 2 * i + 1, cmax_odd, carry)
            return cmax_even, carry

        n_loop = (n_chunks - 1) // 2
        cmax_even, carry = lax.fori_loop(0, n_loop, body, (scores(0, 0), init))
        if n_chunks % 2 == 0:
            cmax_odd = scores(1, n_chunks - 1)
            carry = consume(0, n_chunks - 2, cmax_even, carry)
            carry = consume(1, n_chunks - 1, cmax_odd, carry)
        else:
            carry = consume(0, n_chunks - 1, cmax_even, carry)
        finish(carry)


def _flash(q, k, vt, lam, subg, *, mode, kw, k_cols, v_block_rows, v_rows, vw, k_index, v_index, n_ctx, diff_scale,
           name):
    b, s, nq = q.shape
    tq = TOKEN_TILE
    n_pairs = nq // (2 * LANES)
    n_chunks = s // KV_CHUNK
    assert s % KV_CHUNK == 0 and n_ctx <= KV_CHUNK and n_ctx % tq == 0 and vt.shape[1] == n_chunks
    return pl.pallas_call(
        functools.partial(_flash_kernel, mode=mode, k_cols=k_cols, v_rows=v_rows, vw=vw, n_ctx=n_ctx,
                          n_ctx_tiles=n_ctx // tq, diff_scale=diff_scale),
        grid=(b, n_pairs, s // tq),
        in_specs=[
            pl.BlockSpec((1, tq, 2 * LANES), lambda bi, p, i: (bi, i, p)),
            pl.BlockSpec((1, s, kw), lambda bi, p, i: (bi, 0, k_index(p))),
            pl.BlockSpec((1, n_chunks, v_block_rows, KV_CHUNK), lambda bi, p, i: (bi, 0, v_index(p), 0)),
            _resident(lam.shape), _resident(subg.shape),
        ],
        out_specs=pl.BlockSpec((1, tq, LANES), lambda bi, p, i: (bi, i, p)),
        out_shape=jax.ShapeDtypeStruct((b, s, n_pairs * LANES), BF16),
        scratch_shapes=[pltpu.VMEM((2, 2, KV_CHUNK, tq), F32)],
        compiler_params=_cparams(3),
        name=name,
    )(q, k, vt, lam, subg)


def _na_tables(t):
    rows = t // GRID_W
    win_rows = min(NA_ROWS, rows)
    band_rows = min(win_rows + 1, rows)
    assert rows >= NA_WIN_ROWS and NA_QBLOCK % GRID_W == 0
    qrows = NA_QBLOCK // GRID_W
    nb = t // NA_QBLOCK
    nwin = NA_WIN_ROWS * GRID_W
    q_off, k_off = np.arange(NA_QBLOCK), np.arange(nwin)
    q_dr, q_col = q_off // GRID_W, q_off % GRID_W
    k_dr, k_col = k_off // GRID_W, k_off % GRID_W
    col_start = np.clip(q_col - NA_COLS // 2, 0, GRID_W - NA_COLS)
    col_in = (k_col[None, :] >= col_start[:, None]) & (k_col[None, :] < col_start[:, None] + NA_COLS)
    dc_idx = np.clip(k_col[None, :] - q_col[:, None] + NA_COLS - 1, 0, 2 * NA_COLS - 2)
    patterns, var, wstart = {}, [], []
    for j in range(nb):
        r = j * qrows + q_dr
        row_start = np.clip(r - win_rows // 2, 0, rows - win_rows)
        b0 = min(int(row_start[0]), rows - band_rows)
        w0 = min(b0 - b0 % 2, rows - NA_WIN_ROWS)
        assert w0 % qrows == 0 and w0 <= b0 and b0 + band_rows <= w0 + NA_WIN_ROWS
        k_row = w0 + k_dr
        in_band = (k_row >= b0) & (k_row < b0 + band_rows)
        in_win = (col_in & in_band[None, :] & (k_row[None, :] >= row_start[:, None])
                  & (k_row[None, :] < row_start[:, None] + win_rows))
        dr_idx = np.clip(k_row[None, :] - r[:, None] + NA_ROWS - 1, 0, 2 * NA_ROWS - 2)
        idx = np.where(in_win, dr_idx * (2 * NA_COLS - 1) + dc_idx, -1).astype(np.int32)
        key = idx.tobytes()
        if key not in patterns:
            patterns[key] = (len(patterns), idx)
        var.append(patterns[key][0])
        wstart.append(w0 * GRID_W)
    idx_all = np.stack([p[1] for p in sorted(patterns.values(), key=lambda p: p[0])])
    return idx_all.reshape(len(patterns), 1, -1), np.asarray(var, np.int32), np.asarray(wstart, np.int32)


def _na_bias_kernel(idx_ref, rpb_ref, o_ref):
    idx = idx_ref[0]
    r = rpb_ref[...]
    onehot = (lax.broadcasted_iota(jnp.int32, (r.shape[1], idx.shape[1]), 0) == idx).astype(BF16)
    b1 = r.astype(BF16)
    r2 = r - b1.astype(F32)
    b2 = r2.astype(BF16)
    b3 = (r2 - b2.astype(F32)).astype(BF16)
    out = (jnp.dot(b1, onehot, preferred_element_type=F32) + jnp.dot(b2, onehot, preferred_element_type=F32)
           + jnp.dot(b3, onehot, preferred_element_type=F32))
    o_ref[0] = jnp.where(idx >= 0, out, NEG_INF)


def _na_bias(idx_all, rpb):
    nvar, _, npos = idx_all.shape
    heads = rpb.shape[0]
    nr = (2 * NA_ROWS - 1) * (2 * NA_COLS - 1)
    nr_pad = pl.cdiv(nr, LANES) * LANES
    rflat = jnp.pad(rpb.reshape(heads, nr), ((0, 0), (0, nr_pad - nr)))
    tn = 2048
    assert npos % tn == 0
    out = pl.pallas_call(
        _na_bias_kernel,
        grid=(nvar, npos // tn),
        in_specs=[pl.BlockSpec((1, 1, tn), lambda v, j: (v, 0, j)), _resident(rflat.shape)],
        out_specs=pl.BlockSpec((1, heads, tn), lambda v, j: (v, 0, j)),
        out_shape=jax.ShapeDtypeStruct((nvar, heads, npos), F32),
        compiler_params=_cparams(2),
        name="na_bias",
    )(jnp.asarray(idx_all), rflat)
    return out.reshape(nvar, heads, NA_QBLOCK, npos // NA_QBLOCK)


def _halves(lo_src, hi_src):
    lane = lax.broadcasted_iota(jnp.int32, lo_src.shape, 1)
    return jnp.where(lane < LANES // 2, lo_src, hi_src)


def _na_kernel(var_ref, ws_ref, q_ref, k_ref, v_ref, b0_ref, b1_ref, o_ref, *, n_ctx, n_ctx_tiles, scale):
    qi = pl.program_id(2)
    nwin = NA_WIN_ROWS * GRID_W

    @pl.when(qi < n_ctx_tiles)
    def _():
        kc, vc = k_ref[0, 0:n_ctx, :], v_ref[0, 0:n_ctx, :]
        outs = []
        for a in (0, 1):
            s_c = lax.dot_general(q_ref[0, :, a * LANES:(a + 1) * LANES], kc, NT_DIMS,
                                  preferred_element_type=F32) * scale
            p_c = jnp.exp(s_c - jnp.max(s_c, axis=-1, keepdims=True))
            acc = jnp.dot(p_c.astype(BF16), vc, preferred_element_type=F32)
            outs.append(acc / jnp.sum(p_c, axis=-1, keepdims=True))
        o_ref[0] = _halves(outs[0], outs[1]).astype(BF16)

    @pl.when(qi >= n_ctx_tiles)
    def _():
        kc, vc = k_ref[0, 0:n_ctx, :], v_ref[0, 0:n_ctx, :]
        for sub, bias_ref in ((0, b0_ref), (1, b1_ref)):
            j = (qi - n_ctx_tiles) * (TOKEN_TILE // NA_QBLOCK) + sub
            start = pl.multiple_of(n_ctx + ws_ref[j], NA_QBLOCK)
            kw, vw = k_ref[0, pl.ds(start, nwin), :], v_ref[0, pl.ds(start, nwin), :]
            outs = []
            for a in (0, 1):
                q = q_ref[0, sub * NA_QBLOCK:(sub + 1) * NA_QBLOCK, a * LANES:(a + 1) * LANES]
                s_c = lax.dot_general(q, kc, NT_DIMS, preferred_element_type=F32) * scale
                s_w = lax.dot_general(q, kw, NT_DIMS, preferred_element_type=F32) * scale + bias_ref[0, a]
                m = jnp.maximum(jnp.max(s_c, axis=-1, keepdims=True), jnp.max(s_w, axis=-1, keepdims=True))
                p_c, p_w = jnp.exp(s_c - m), jnp.exp(s_w - m)
                l = jnp.sum(p_c, axis=-1, keepdims=True) + jnp.sum(p_w, axis=-1, keepdims=True)
                acc = (jnp.dot(p_c.astype(BF16), vc, preferred_element_type=F32)
                       + jnp.dot(p_w.astype(BF16), vw, preferred_element_type=F32))
                outs.append(acc / l)
            o_ref[0, sub * NA_QBLOCK:(sub + 1) * NA_QBLOCK, :] = _halves(outs[0], outs[1]).astype(BF16)


def _na_attention(q, k, v, bias, var, wstart, *, n_ctx, scale):
    b, s, nq = q.shape
    tq = TOKEN_TILE
    n_pairs = nq // (2 * LANES)
    n_ctx_tiles = n_ctx // tq
    per_tile = tq // NA_QBLOCK
    nwin = NA_WIN_ROWS * GRID_W

    def bias_spec(sub):
        def index(bi, p, i, var_ref, ws_ref):
            j = jnp.maximum(i - n_ctx_tiles, 0) * per_tile + sub
            return (var_ref[j], p, 0, 0)
        return pl.BlockSpec((1, 2, NA_QBLOCK, nwin), index)

    grid_spec = pltpu.PrefetchScalarGridSpec(
        num_scalar_prefetch=2,
        grid=(b, n_pairs, s // tq),
        in_specs=[
            pl.BlockSpec((1, tq, 2 * LANES), lambda bi, p, i, *_: (bi, i, p)),
            pl.BlockSpec((1, s, LANES), lambda bi, p, i, *_: (bi, 0, p)),
            pl.BlockSpec((1, s, LANES), lambda bi, p, i, *_: (bi, 0, p)),
            bias_spec(0), bias_spec(1),
        ],
        out_specs=pl.BlockSpec((1, tq, LANES), lambda bi, p, i, *_: (bi, i, p)),
    )
    return pl.pallas_call(
        functools.partial(_na_kernel, n_ctx=n_ctx, n_ctx_tiles=n_ctx_tiles, scale=scale),
        grid_spec=grid_spec,
        out_shape=jax.ShapeDtypeStruct((b, s, n_pairs * LANES), BF16),
        compiler_params=_cparams(3),
        name="na_attn",
    )(var, wstart, q, k, v, bias, bias)


def _outproj_kernel(x_ref, o_ref, w_ref, mod_ref, y_ref):
    y = jnp.dot(o_ref[0], w_ref[...], preferred_element_type=F32)
    y_ref[0] = x_ref[0] + mod_ref[0, 2:3, :] * y


def _outproj(xs, o, w, mods, *, n_ctx_tiles):
    b, s, d = xs.shape
    tm = TOKEN_TILE
    return pl.pallas_call(
        _outproj_kernel,
        grid=(b, s // tm),
        in_specs=[
            pl.BlockSpec((1, tm, d), lambda bi, i: (bi, i, 0)),
            pl.BlockSpec((1, tm, o.shape[2]), lambda bi, i: (bi, i, 0)),
            _resident(w.shape),
            pl.BlockSpec((1, 6, d), lambda bi, i: (jnp.where(i < n_ctx_tiles, b, bi), 0, 0)),
        ],
        out_specs=pl.BlockSpec((1, tm, d), lambda bi, i: (bi, i, 0)),
        out_shape=jax.ShapeDtypeStruct((b, s, d), F32),
        compiler_params=_cparams(2),
        name="outproj",
    )(xs, o, w, mods)


def _ffn_kernel(xp_ref, x_ref, xn_ref, mod_ref, g_ref, wup_ref, cw_ref, cb_ref, wdn_ref, y_ref, u_scr, a_scr,
                *, n_ctx_tiles, n_tiles, chunk):
    i = pl.program_id(1)
    tm = x_ref.shape[1]
    has_prev = jnp.logical_and(i != 0, i != n_ctx_tiles).astype(F32)
    has_next = jnp.logical_and(i != n_ctx_tiles - 1, i != n_tiles - 1).astype(F32)
    g, shift, scale = g_ref[...], mod_ref[0, 3:4, :], mod_ref[0, 4:5, :]
    x = x_ref[0]
    h = jnp.concatenate([_norm_mod(xp_ref[0], g, shift, scale) * has_prev,
                         _norm_mod(x, g, shift, scale),
                         _norm_mod(xn_ref[0], g, shift, scale) * has_next], axis=0).astype(BF16)
    n_chunks = wup_ref.shape[1] // (2 * chunk)
    for j in range(n_chunks):
        cols = slice(j * 2 * chunk, (j + 1) * 2 * chunk)
        u_scr[...] = jnp.dot(h, wup_ref[:, cols], preferred_element_type=F32)
        cw = cw_ref[:, cols]
        u = (u_scr[HALO - 1:HALO - 1 + tm, :] * cw[0:1] + u_scr[HALO:HALO + tm, :] * cw[1:2]
             + u_scr[HALO + 1:HALO + 1 + tm, :] * cw[2:3] + cb_ref[:, cols])
        val, gate = u[:, :chunk], u[:, chunk:]
        a_scr[:, j * chunk:(j + 1) * chunk] = (gate * (1.0 / (1.0 + jnp.exp(-gate))) * val).astype(BF16)
    y = jnp.dot(a_scr[...], wdn_ref[...], preferred_element_type=F32)
    y_ref[0] = x + mod_ref[0, 5:6, :] * y


def _ffn(xs, mods, g, wup, cw, cb, wdn, *, n_ctx_tiles, chunk):
    b, s, d = xs.shape
    tm = TOKEN_TILE
    n_tiles = s // tm
    per = tm // HALO
    f = wdn.shape[0]
    return pl.pallas_call(
        functools.partial(_ffn_kernel, n_ctx_tiles=n_ctx_tiles, n_tiles=n_tiles, chunk=chunk),
        grid=(b, n_tiles),
        in_specs=[
            pl.BlockSpec((1, HALO, d), lambda bi, i: (bi, jnp.maximum(i * per - 1, 0), 0)),
            pl.BlockSpec((1, tm, d), lambda bi, i: (bi, i, 0)),
            pl.BlockSpec((1, HALO, d), lambda bi, i: (bi, jnp.minimum((i + 1) * per, s // HALO - 1), 0)),
            pl.BlockSpec((1, 6, d), lambda bi, i: (jnp.where(i < n_ctx_tiles, b, bi), 0, 0)),
            _resident((1, d)),
            pl.BlockSpec(wup.shape, lambda bi, i: (0, 0), pipeline_mode=pl.Buffered(1)),
            _resident(cw.shape), _resident(cb.shape),
            pl.BlockSpec(wdn.shape, lambda bi, i: (0, 0), pipeline_mode=pl.Buffered(1)),
        ],
        out_specs=pl.BlockSpec((1, tm, d), lambda bi, i: (bi, i, 0)),
        out_shape=jax.ShapeDtypeStruct((b, s, d), F32),
        scratch_shapes=[pltpu.VMEM((tm + 2 * HALO, 2 * chunk), F32), pltpu.VMEM((tm, f), BF16)],
        compiler_params=_cparams(2),
        name="ffn",
    )(xs, xs, xs, mods, g, wup, cw, cb, wdn)


def _final_kernel(x_ref, g_ref, o_ref):
    x = x_ref[0]
    o_ref[0] = x * lax.rsqrt(jnp.mean(x * x, axis=-1, keepdims=True) + EPS) * g_ref[...]


def _final_norm(xs, g, *, n_ctx_tiles):
    b, s, d = xs.shape
    tm = TOKEN_TILE
    nt = s // tm - n_ctx_tiles
    return pl.pallas_call(
        _final_kernel,
        grid=(b, nt),
        in_specs=[pl.BlockSpec((1, tm, d), lambda bi, i: (bi, i + n_ctx_tiles, 0)), _resident((1, d))],
        out_specs=pl.BlockSpec((1, tm, d), lambda bi, i: (bi, i, 0)),
        out_shape=jax.ShapeDtypeStruct((b, nt * tm, d), F32),
        compiler_params=_cparams(2),
        name="final_norm",
    )(xs, g)


def _rope_table(t, n_ctx, rot_dim, lane_lo):
    tt = jnp.arange(t)
    row = (tt // GRID_W).astype(F32)
    col = (tt % GRID_W).astype(F32)
    half = rot_dim // 2
    inv = ROPE_BASE ** (-jnp.arange(0, half, 2, dtype=F32) / half)
    ar, ac = row[:, None] * inv, col[:, None] * inv
    ang = jnp.concatenate([ar, ar, ac, ac], axis=-1)
    quarter = rot_dim // 4
    sign = jnp.where((jnp.arange(rot_dim) % (2 * quarter)) < quarter, -1.0, 1.0).astype(F32)
    cos, sin = jnp.cos(ang), jnp.sin(ang) * sign
    reps = (LANES - lane_lo) // rot_dim if lane_lo == 0 else 1
    cos_l = jnp.ones((t, LANES), F32).at[:, lane_lo:lane_lo + reps * rot_dim].set(jnp.tile(cos, (1, reps)))
    sin_l = jnp.zeros((t, LANES), F32).at[:, lane_lo:lane_lo + reps * rot_dim].set(jnp.tile(sin, (1, reps)))
    cos_s = jnp.concatenate([jnp.ones((n_ctx, LANES), F32), cos_l], axis=0)
    sin_s = jnp.concatenate([jnp.zeros((n_ctx, LANES), F32), sin_l], axis=0)
    return cos_s, sin_s


def _to_slots(w, n_heads, width, halves):
    d = w.shape[0]
    wh = w.reshape(d, n_heads, width)
    out = jnp.zeros((d, n_heads, LANES), w.dtype)
    for h in range(n_heads):
        out = out.at[:, h, halves[h] * (LANES // 2):halves[h] * (LANES // 2) + width].set(wh[:, h])
    return out.reshape(d, n_heads * LANES)


def _vt_weights(wv, groups, width, tm):
    d = wv.shape[0]
    rows = width + 16
    wt = jnp.zeros((groups, rows, d), wv.dtype).at[:, :width, :].set(wv.T.reshape(groups, width, d))
    ones = jnp.zeros((groups, rows, tm), F32).at[:, width, :].set(1.0)
    return wt.reshape(groups * rows, d), ones.reshape(groups * rows, tm)


def _dup(g):
    return jnp.concatenate([g, g]).reshape(1, LANES).astype(F32)


def kernel(x, c, ctx, c_ctx, ada_w, ada_b, norm1_g, norm2_g, ffn_w_up, ffn_conv_w, ffn_conv_b, ffn_w_down,
           gqa_w_in, gqa_q_norm_g, gqa_k_norm_g, gqa_w_out,
           mla_w_in, mla_q_norm_g, mla_kv_norm_g, mla_w_uq, mla_w_ukv, mla_w_out,
           diff_w_in, diff_lambda, diff_subln_g, diff_w_out,
           na_w_in, na_rpb, na_w_out, final_norm_g):
    b, t, d = x.shape
    n_ctx = ctx.shape[1]
    depth = ada_w.shape[0]
    assert n_ctx % TOKEN_TILE == 0 and t % TOKEN_TILE == 0 and b + 1 <= 8
    nct = n_ctx // TOKEN_TILE
    f = ffn_w_down.shape[1]
    chunk = 2 * LANES
    assert f % chunk == 0

    xs = jnp.concatenate([ctx, x], axis=1)
    cin = jnp.zeros((8, d), F32).at[:b].set(c).at[b].set(c_ctx)
    mods_all = _ada_all(cin, ada_w, ada_b).reshape(depth, 8, 6, d)

    tm = TOKEN_TILE
    ones_l = jnp.ones((1, LANES), F32)
    dummy_tab = jnp.zeros((xs.shape[1], LANES), F32)
    dummy_ones = jnp.zeros((8, LANES), F32)
    cos64, sin64 = _rope_table(t, n_ctx, GQA_HEAD_DIM, 0)
    cos_mla, sin_mla = _rope_table(t, n_ctx, MLA_ROPE_DIM, MLA_NOPE_DIM)
    no_lam, no_g = jnp.zeros((4, DIFF_HEAD_DIM), F32), jnp.ones((1, LANES), F32)
    pair_rows = MLA_V_DIM + 16
    diff_rows = 2 * DIFF_HEAD_DIM + 16

    for i in range(depth):
        m, j = i % N_MIXERS, i // N_MIXERS
        mods = mods_all[i]
        g1 = norm1_g[i].reshape(1, d)
        if m == 0:
            w = gqa_w_in[j].astype(BF16)
            nqc, nkc = GQA_HEADS * GQA_HEAD_DIM, GQA_KV_HEADS * GQA_HEAD_DIM
            wq = _to_slots(w[:, :nqc], GQA_HEADS, GQA_HEAD_DIM, [0] * GQA_HEADS)
            wk = _to_slots(w[:, nqc:nqc + nkc], GQA_KV_HEADS, GQA_HEAD_DIM, [0] * GQA_KV_HEADS)
            wvt, vones = _vt_weights(w[:, nqc + nkc:], GQA_KV_HEADS, GQA_HEAD_DIM, tm)
            qc = GQA_HEAD_DIM ** -0.5 * LOG2E
            q, k, vt = _project(xs, mods, g1, wq, wk, wvt, _dup(gqa_q_norm_g[j]), _dup(gqa_k_norm_g[j]),
                                cos64 * qc, sin64 * qc, cos64, sin64, vones, qk_norm=True, rope=True,
                                head_dim=GQA_HEAD_DIM, v_transposed=True, n_ctx_tiles=nct, name="proj_gqa")
            o = _flash(q, k, vt, no_lam, no_g, mode="pair", kw=LANES, k_cols=((0, LANES), (0, LANES)),
                       v_block_rows=pair_rows, v_rows=((0, pair_rows), (0, pair_rows)), vw=GQA_HEAD_DIM,
                       k_index=lambda p: p // 2, v_index=lambda p: p // 2, n_ctx=n_ctx, diff_scale=1.0,
                       name="attn_gqa")
            w_out = gqa_w_out[j]
        elif m == 1:
            w = mla_w_in[j].astype(BF16)
            nlat = MLA_Q_LORA + MLA_KV_LORA
            kr_slot = jnp.zeros((d, LANES), BF16).at[:, MLA_NOPE_DIM:MLA_NOPE_DIM + MLA_ROPE_DIM].set(w[:, nlat:])
            win = jnp.concatenate([w[:, :nlat], kr_slot], axis=1)
            qk_dim = MLA_NOPE_DIM + MLA_ROPE_DIM
            wuq = _to_slots(mla_w_uq[j].astype(BF16), MLA_HEADS, qk_dim, [0] * MLA_HEADS)
            wukv = mla_w_ukv[j].astype(BF16).reshape(MLA_KV_LORA, MLA_HEADS, MLA_NOPE_DIM + MLA_V_DIM)
            wuk = _to_slots(wukv[:, :, :MLA_NOPE_DIM].reshape(MLA_KV_LORA, -1), MLA_HEADS, MLA_NOPE_DIM,
                            [0] * MLA_HEADS)
            wuvt, vones = _vt_weights(wukv[:, :, MLA_NOPE_DIM:].reshape(MLA_KV_LORA, MLA_HEADS * MLA_V_DIM),
                                      MLA_HEADS, MLA_V_DIM, tm)
            qc = qk_dim ** -0.5 * LOG2E
            q, k, vt = _project_mla(xs, mods, g1, win, mla_q_norm_g[j].reshape(1, -1),
                                    mla_kv_norm_g[j].reshape(1, -1), wuq, wuk, wuvt,
                                    cos_mla * qc, sin_mla * qc, cos_mla, sin_mla, vones, n_ctx_tiles=nct)
            o = _flash(q, k, vt, no_lam, no_g, mode="pair", kw=2 * LANES, k_cols=((0, LANES), (LANES, 2 * LANES)),
                       v_block_rows=2 * pair_rows, v_rows=((0, pair_rows), (pair_rows, 2 * pair_rows)),
                       vw=MLA_V_DIM, k_index=lambda p: p, v_index=lambda p: p, n_ctx=n_ctx, diff_scale=1.0,
                       name="attn_mla")
            w_out = mla_w_out[j]
        elif m == 2:
            w = diff_w_in[j].astype(BF16)
            wq = _to_slots(w[:, :d], 2 * DIFF_HEADS, DIFF_HEAD_DIM, [0, 1] * DIFF_HEADS)
            wvt, vones = _vt_weights(w[:, 2 * d:], DIFF_HEADS, 2 * DIFF_HEAD_DIM, tm)
            lambda_init = 0.8 - 0.6 * math.exp(-0.3 * i)
            qc = DIFF_HEAD_DIM ** -0.5 * LOG2E
            q, k, vt = _project(xs, mods, g1, wq, w[:, d:2 * d], wvt, ones_l, ones_l,
                                cos64 * qc, sin64 * qc, cos64, sin64, vones, qk_norm=False, rope=True,
                                head_dim=DIFF_HEAD_DIM, v_transposed=True, n_ctx_tiles=nct, name="proj_diff")
            o = _flash(q, k, vt, diff_lambda[j], diff_subln_g[j].reshape(1, LANES), mode="diff", kw=LANES,
                       k_cols=((0, LANES), (0, LANES)), v_block_rows=diff_rows,
                       v_rows=((0, diff_rows), (0, diff_rows)), vw=2 * DIFF_HEAD_DIM,
                       k_index=lambda p: p, v_index=lambda p: p, n_ctx=n_ctx, diff_scale=1.0 - lambda_init,
                       name="attn_diff")
            w_out = diff_w_out[j]
        else:
            w = na_w_in[j].astype(BF16)
            wq = _to_slots(w[:, :d], NA_HEADS, NA_HEAD_DIM, [0, 1] * (NA_HEADS // 2))
            q, k, v = _project(xs, mods, g1, wq, w[:, d:2 * d], w[:, 2 * d:], ones_l, ones_l,
                               dummy_tab, dummy_tab, dummy_tab, dummy_tab, dummy_ones, qk_norm=False, rope=False,
                               head_dim=NA_HEAD_DIM, v_transposed=False, n_ctx_tiles=nct, name="proj_na")
            idx_all, var, wstart = _na_tables(t)
            bias = _na_bias(idx_all, na_rpb[j])
            o = _na_attention(q, k, v, bias, jnp.asarray(var), jnp.asarray(wstart), n_ctx=n_ctx,
                              scale=NA_HEAD_DIM ** -0.5)
            w_out = na_w_out[j]
        xs = _outproj(xs, o, w_out.astype(BF16), mods, n_ctx_tiles=nct)

        def regroup(a):
            lead = a.shape[:-1]
            return jnp.stack([a[..., :f].reshape(*lead, f // chunk, chunk),
                              a[..., f:].reshape(*lead, f // chunk, chunk)], axis=-2).reshape(*lead, 2 * f)
        xs = _ffn(xs, mods, norm2_g[i].reshape(1, d), regroup(ffn_w_up[i]).astype(BF16), regroup(ffn_conv_w[i]),
                  regroup(ffn_conv_b[i]).reshape(1, 2 * f), ffn_w_down[i].astype(BF16), n_ctx_tiles=nct, chunk=chunk)

    return _final_norm(xs, final_norm_g.reshape(1, d), n_ctx_tiles=nct)
```

```python
import functools
import math

import numpy as np
import jax
import jax.numpy as jnp
from jax import lax
from jax.experimental import pallas as pl
from jax.experimental.pallas import tpu as pltpu

F32 = jnp.float32
BF16 = jnp.bfloat16

LANES = 128
TOKEN_TILE = 256
HALO = 8
KV_CHUNK = 768
FLASH_TQ = 512
LOOP_CHUNKS = 4
VMEM_LIMIT = 56 * 1024 * 1024

GRID_W = 64
ROPE_BASE = 10000.0
EPS = 1e-6
NEG_INF = -1e30
LOG2E = 1.4426950408889634

GQA_HEADS, GQA_KV_HEADS, GQA_HEAD_DIM = 16, 4, 64
MLA_HEADS, MLA_NOPE_DIM, MLA_ROPE_DIM, MLA_V_DIM = 16, 64, 32, 64
MLA_Q_LORA, MLA_KV_LORA = 384, 256
DIFF_HEADS, DIFF_HEAD_DIM = 8, 64
NA_HEADS, NA_HEAD_DIM, NA_ROWS, NA_COLS = 16, 64, 8, 16
NA_QBLOCK = 128
NA_WIN_ROWS = 10
N_MIXERS = 4

NT_DIMS = (((1,), (1,)), ((), ()))


def _cparams(n_axes):
    return pltpu.CompilerParams(dimension_semantics=("arbitrary",) * n_axes, vmem_limit_bytes=VMEM_LIMIT)


def _resident(shape):
    nd = len(shape)
    return pl.BlockSpec(shape, lambda *_: (0,) * nd)


def _ada_kernel(c_ref, w_ref, b_ref, o_ref):
    c = c_ref[...]
    s = c * (1.0 / (1.0 + jnp.exp(-c)))
    o_ref[0] = jnp.dot(s.astype(BF16), w_ref[0].astype(BF16), preferred_element_type=F32) + b_ref[0]


def _ada_all(cin, ada_w, ada_b):
    depth, d, n = ada_w.shape
    tn = n // 4
    return pl.pallas_call(
        _ada_kernel,
        grid=(depth, n // tn),
        in_specs=[
            pl.BlockSpec((8, d), lambda l, j: (0, 0)),
            pl.BlockSpec((1, d, tn), lambda l, j: (l, 0, j)),
            pl.BlockSpec((1, 1, tn), lambda l, j: (l, 0, j)),
        ],
        out_specs=pl.BlockSpec((1, 8, tn), lambda l, j: (l, 0, j)),
        out_shape=jax.ShapeDtypeStruct((depth, 8, n), F32),
        compiler_params=_cparams(2),
        name="ada",
    )(cin, ada_w, ada_b.reshape(depth, 1, n))


def _norm_mod(x, g, shift, scale):
    ms = jnp.mean(x * x, axis=-1, keepdims=True)
    return (x * lax.rsqrt(ms + EPS) * g) * (1.0 + scale) + shift


def _rope_slot(y, cos, sin_signed, lo_mask, quarter):
    fwd = pltpu.roll(y, LANES - quarter, axis=1)
    bwd = pltpu.roll(y, quarter, axis=1)
    return y * cos + jnp.where(lo_mask, fwd, bwd) * sin_signed


def _lo_mask(shape, quarter):
    lane = lax.broadcasted_iota(jnp.int32, shape, 1)
    return (lane % (2 * quarter)) < quarter


def _q_spec(n, tm, n_tiles, n_ctx_tiles, rotate):
    if not rotate:
        return pl.BlockSpec((1, tm, n), lambda bi, i: (bi, i, 0))
    n_lat = n_tiles - n_ctx_tiles
    return pl.BlockSpec((1, tm, n), lambda bi, i: (bi, jnp.where(i < n_ctx_tiles, i + n_lat, i - n_ctx_tiles), 0))


def _vt_spec(rows, tm):
    per = KV_CHUNK // tm
    return pl.BlockSpec((1, 1, rows, tm), lambda bi, i: (bi, i // per, 0, i % per))


def _proj_kernel(x_ref, mod_ref, g_ref, wq_ref, wk_ref, wv_ref, qg_ref, kg_ref, cosq_ref, sinq_ref, cosk_ref, sink_ref,
                 vones_ref, q_ref, k_ref, v_ref, *, qk_norm, rope, head_dim, v_transposed):
    h = _norm_mod(x_ref[0], g_ref[...], mod_ref[0, 0:1, :], mod_ref[0, 1:2, :]).astype(BF16)
    tm = h.shape[0]
    lo = _lo_mask((tm, LANES), head_dim // 4) if rope else None

    def finish(w_ref, o_ref, gain_ref, cos_ref, sin_ref):
        n = w_ref.shape[1]
        for c0 in range(0, n, 2 * LANES):
            c1 = min(c0 + 2 * LANES, n)
            y2 = jnp.dot(h, w_ref[:, c0:c1], preferred_element_type=F32)
            for s0 in range(0, c1 - c0, LANES):
                y = y2[:, s0:s0 + LANES]
                if qk_norm:
                    ms = jnp.sum(y * y, axis=-1, keepdims=True) * (1.0 / head_dim)
                    y = y * lax.rsqrt(ms + EPS) * gain_ref[...]
                if rope:
                    y = _rope_slot(y, cos_ref[...], sin_ref[...], lo, head_dim // 4)
                o_ref[0, :, c0 + s0:c0 + s0 + LANES] = y.astype(BF16)

    finish(wq_ref, q_ref, qg_ref, cosq_ref, sinq_ref)
    finish(wk_ref, k_ref, kg_ref, cosk_ref, sink_ref)
    if v_transposed:
        vt = lax.dot_general(wv_ref[...], h, NT_DIMS, preferred_element_type=F32) + vones_ref[...]
        v_ref[0, 0] = vt.astype(BF16)
    else:
        v_ref[0] = jnp.dot(h, wv_ref[...], preferred_element_type=F32).astype(BF16)


def _project(xs, mods, g, wq, wk, wv, qg, kg, cosq, sinq, cosk, sink, vones, *, qk_norm, rope, head_dim, v_transposed,
             q_rotated, n_ctx_tiles, name):
    b, s, d = xs.shape
    tm = TOKEN_TILE
    nq, nk = wq.shape[1], wk.shape[1]
    tok = lambda n: pl.BlockSpec((1, tm, n), lambda bi, i: (bi, i, 0))
    tab = pl.BlockSpec((tm, LANES), lambda bi, i: (i, 0))
    if v_transposed:
        v_spec = _vt_spec(wv.shape[0], tm)
        v_shape = jax.ShapeDtypeStruct((b, s // KV_CHUNK, wv.shape[0], KV_CHUNK), BF16)
    else:
        v_spec, v_shape = tok(wv.shape[1]), jax.ShapeDtypeStruct((b, s, wv.shape[1]), BF16)
    return pl.pallas_call(
        functools.partial(_proj_kernel, qk_norm=qk_norm, rope=rope, head_dim=head_dim, v_transposed=v_transposed),
        grid=(b, s // tm),
        in_specs=[
            tok(d),
            pl.BlockSpec((1, 6, d), lambda bi, i: (jnp.where(i < n_ctx_tiles, b, bi), 0, 0)),
            _resident((1, d)),
            _resident(wq.shape), _resident(wk.shape), _resident(wv.shape),
            _resident((1, LANES)), _resident((1, LANES)),
            tab, tab, tab, tab,
            _resident(vones.shape),
        ],
        out_specs=[_q_spec(nq, tm, s // tm, n_ctx_tiles, q_rotated), tok(nk), v_spec],
        out_shape=[jax.ShapeDtypeStruct((b, s, nq), BF16), jax.ShapeDtypeStruct((b, s, nk), BF16), v_shape],
        compiler_params=_cparams(2),
        name=name,
    )(xs, mods, g, wq, wk, wv, qg, kg, cosq, sinq, cosk, sink, vones)


def _mla_proj_kernel(x_ref, mod_ref, g_ref, win_ref, qn_ref, kvn_ref, wuq_ref, wuk_ref, wuvt_ref,
                     cosq_ref, sinq_ref, cosk_ref, sink_ref, vones_ref, q_ref, k_ref, v_ref):
    h = _norm_mod(x_ref[0], g_ref[...], mod_ref[0, 0:1, :], mod_ref[0, 1:2, :]).astype(BF16)
    tm = h.shape[0]
    t1 = jnp.dot(h, win_ref[...], preferred_element_type=F32)
    cq = t1[:, :MLA_Q_LORA]
    ckv = t1[:, MLA_Q_LORA:MLA_Q_LORA + MLA_KV_LORA]
    kr = t1[:, MLA_Q_LORA + MLA_KV_LORA:]
    cqn = (cq * lax.rsqrt(jnp.mean(cq * cq, axis=-1, keepdims=True) + EPS) * qn_ref[...]).astype(BF16)
    ckvn = (ckv * lax.rsqrt(jnp.mean(ckv * ckv, axis=-1, keepdims=True) + EPS) * kvn_ref[...]).astype(BF16)
    quarter = MLA_ROPE_DIM // 4
    lo = _lo_mask((tm, LANES), quarter)
    kr = _rope_slot(kr, cosk_ref[...], sink_ref[...], lo, quarter)
    n = wuq_ref.shape[1]
    for c0 in range(0, n, 2 * LANES):
        q2 = jnp.dot(cqn, wuq_ref[:, c0:c0 + 2 * LANES], preferred_element_type=F32)
        k2 = jnp.dot(ckvn, wuk_ref[:, c0:c0 + 2 * LANES], preferred_element_type=F32)
        for s0 in (0, LANES):
            q_ref[0, :, c0 + s0:c0 + s0 + LANES] = _rope_slot(
                q2[:, s0:s0 + LANES], cosq_ref[...], sinq_ref[...], lo, quarter).astype(BF16)
            k_ref[0, :, c0 + s0:c0 + s0 + LANES] = (k2[:, s0:s0 + LANES] + kr).astype(BF16)
    vt = lax.dot_general(wuvt_ref[...], ckvn, NT_DIMS, preferred_element_type=F32) + vones_ref[...]
    v_ref[0, 0] = vt.astype(BF16)


def _project_mla(xs, mods, g, win, qn, kvn, wuq, wuk, wuvt, cosq, sinq, cosk, sink, vones, *, n_ctx_tiles):
    b, s, d = xs.shape
    tm = TOKEN_TILE
    nq, nk = wuq.shape[1], wuk.shape[1]
    tok = lambda n: pl.BlockSpec((1, tm, n), lambda bi, i: (bi, i, 0))
    tab = pl.BlockSpec((tm, LANES), lambda bi, i: (i, 0))
    return pl.pallas_call(
        _mla_proj_kernel,
        grid=(b, s // tm),
        in_specs=[
            tok(d),
            pl.BlockSpec((1, 6, d), lambda bi, i: (jnp.where(i < n_ctx_tiles, b, bi), 0, 0)),
            _resident((1, d)),
            _resident(win.shape), _resident(qn.shape), _resident(kvn.shape),
            _resident(wuq.shape), _resident(wuk.shape), _resident(wuvt.shape),
            tab, tab, tab, tab,
            _resident(vones.shape),
        ],
        out_specs=[_q_spec(nq, tm, s // tm, n_ctx_tiles, True), tok(nk), _vt_spec(wuvt.shape[0], tm)],
        out_shape=[jax.ShapeDtypeStruct((b, s, nq), BF16), jax.ShapeDtypeStruct((b, s, nk), BF16),
                   jax.ShapeDtypeStruct((b, s // KV_CHUNK, wuvt.shape[0], KV_CHUNK), BF16)],
        compiler_params=_cparams(2),
        name="proj_mla",
    )(xs, mods, g, win, qn, kvn, wuq, wuk, wuvt, cosq, sinq, cosk, sink, vones)


def _flash_kernel(q_ref, k_ref, vt_ref, lam_ref, subg_ref, o_ref, *scratch, mode, k_cols, v_rows, vw, diff_scale,
                  ctx_keys):
    tq = q_ref.shape[1]
    n_chunks = vt_ref.shape[1]
    qs = [q_ref[0, :, a * LANES:(a + 1) * LANES] for a in (0, 1)]
    n_rows = v_rows[0][1] - v_rows[0][0]

    def keys(key_rows):
        k0 = k_ref[0, key_rows, k_cols[0][0]:k_cols[0][1]]
        k1 = k0 if k_cols[1] == k_cols[0] else k_ref[0, key_rows, k_cols[1][0]:k_cols[1][1]]
        return k0, k1

    def values(chunk_idx, key_lanes):
        v0 = vt_ref[0, chunk_idx, v_rows[0][0]:v_rows[0][1], key_lanes]
        v1 = v0 if v_rows[1] == v_rows[0] else vt_ref[0, chunk_idx, v_rows[1][0]:v_rows[1][1], key_lanes]
        return v0, v1

    def update(st, cmax, vt_a, carry_a):
        m, acc = carry_a
        m_new = jnp.maximum(m, cmax)
        p = jnp.exp2(st - m_new).astype(BF16)
        return m_new, jnp.exp2(m - m_new) * acc + jnp.dot(vt_a, p, preferred_element_type=F32)

    init = tuple((jnp.full((1, tq), -jnp.inf, F32), jnp.zeros((n_rows, tq), F32)) for _ in (0, 1))

    def finish(carry):
        (_, acc_a), (_, acc_b) = carry
        oa = acc_a[0:vw] / acc_a[vw:vw + 1]
        ob = acc_b[0:vw] / acc_b[vw:vw + 1]
        if mode == "pair":
            out = jnp.concatenate([oa, ob], axis=0).T
        else:
            lam = lam_ref[...]
            lam_full = (jnp.exp(jnp.sum(lam[0:1] * lam[1:2], axis=-1, keepdims=True))
                        - jnp.exp(jnp.sum(lam[2:3] * lam[3:4], axis=-1, keepdims=True))
                        + (1.0 - diff_scale))
            o = (oa - lam_full * ob).T
            out = o * lax.rsqrt(jnp.mean(o * o, axis=-1, keepdims=True) + EPS) * subg_ref[...] * diff_scale
        o_ref[0] = out.astype(BF16)

    if ctx_keys:
        ks, vts = keys(slice(0, ctx_keys)), values(0, slice(0, ctx_keys))
        out = []
        for a in (0, 1):
            st = lax.dot_general(ks[a], qs[a], NT_DIMS, preferred_element_type=F32)
            out.append(update(st, jnp.max(st, axis=0, keepdims=True), vts[a], init[a]))
        finish(tuple(out))
        return

    (st_scr,) = scratch

    def scores(slot, chunk_idx):
        start = chunk_idx * KV_CHUNK
        ks = keys(pl.ds(start if isinstance(start, int) else pl.multiple_of(start, KV_CHUNK), KV_CHUNK))
        cmax = []
        for a in (0, 1):
            st = lax.dot_general(ks[a], qs[a], NT_DIMS, preferred_element_type=F32)
            st_scr[slot, a] = st
            cmax.append(jnp.max(st, axis=0, keepdims=True))
        return tuple(cmax)

    def consume(slot, chunk_idx, cmax, carry):
        vts = values(chunk_idx, slice(None))
        return tuple(update(st_scr[slot, a], cmax[a], vts[a], carry[a]) for a in (0, 1))

    def run(first, count, cmax, carry):
        for u in range(count):
            nxt = first + u + 1
            cmax_next = scores((u + 1) % 2, nxt) if isinstance(nxt, jax.Array) or nxt < n_chunks else None
            carry = consume(u % 2, first + u, cmax, carry)
            cmax = cmax_next
        return cmax, carry

    n_loop = (n_chunks - 1) // LOOP_CHUNKS
    cmax, carry = lax.fori_loop(0, n_loop, lambda i, st: run(i * LOOP_CHUNKS, LOOP_CHUNKS, *st),
                                (scores(0, 0), init))
    _, carry = run(n_loop * LOOP_CHUNKS, n_chunks - n_loop * LOOP_CHUNKS, cmax, carry)
    finish(carry)


def _flash(q, k, vt, lam, subg, *, mode, kw, k_cols, v_block_rows, v_rows, vw, k_index, v_index, n_ctx, diff_scale,
           name):
    b, s, nq = q.shape
    t = s - n_ctx
    n_pairs = nq // (2 * LANES)
    n_chunks = s // KV_CHUNK
    assert s % KV_CHUNK == 0 and n_ctx <= KV_CHUNK and t % FLASH_TQ == 0 and t % n_ctx == 0
    assert vt.shape[1] == n_chunks
    common = dict(mode=mode, k_cols=k_cols, v_rows=v_rows, vw=vw, diff_scale=diff_scale)
    o_x = pl.pallas_call(
        functools.partial(_flash_kernel, ctx_keys=0, **common),
        grid=(b, n_pairs, t // FLASH_TQ),
        in_specs=[
            pl.BlockSpec((1, FLASH_TQ, 2 * LANES), lambda bi, p, i: (bi, i, p)),
            pl.BlockSpec((1, s, kw), lambda bi, p, i: (bi, 0, k_index(p))),
            pl.BlockSpec((1, n_chunks, v_block_rows, KV_CHUNK), lambda bi, p, i: (bi, 0, v_index(p), 0)),
            _resident(lam.shape), _resident(subg.shape),
        ],
        out_specs=pl.BlockSpec((1, FLASH_TQ, LANES), lambda bi, p, i: (bi, i, p)),
        out_shape=jax.ShapeDtypeStruct((b, t, n_pairs * LANES), BF16),
        scratch_shapes=[pltpu.VMEM((2, 2, KV_CHUNK, FLASH_TQ), F32)],
        compiler_params=_cparams(3),
        name=name,
    )(q, k, vt, lam, subg)
    o_c = pl.pallas_call(
        functools.partial(_flash_kernel, ctx_keys=n_ctx, **common),
        grid=(b, n_pairs),
        in_specs=[
            pl.BlockSpec((1, n_ctx, 2 * LANES), lambda bi, p: (bi, t // n_ctx, p)),
            pl.BlockSpec((1, n_ctx, kw), lambda bi, p: (bi, 0, k_index(p))),
            pl.BlockSpec((1, 1, v_block_rows, KV_CHUNK), lambda bi, p: (bi, 0, v_index(p), 0)),
            _resident(lam.shape), _resident(subg.shape),
        ],
        out_specs=pl.BlockSpec((1, n_ctx, LANES), lambda bi, p: (bi, 0, p)),
        out_shape=jax.ShapeDtypeStruct((b, n_ctx, n_pairs * LANES), BF16),
        compiler_params=_cparams(2),
        name=name + "_ctx",
    )(q, k, vt, lam, subg)
    return o_x, o_c


def _na_tables(t):
    rows = t // GRID_W
    win_rows = min(NA_ROWS, rows)
    band_rows = min(win_rows + 1, rows)
    assert rows >= NA_WIN_ROWS and NA_QBLOCK % GRID_W == 0
    qrows = NA_QBLOCK // GRID_W
    nb = t // NA_QBLOCK
    nwin = NA_WIN_ROWS * GRID_W
    q_off, k_off = np.arange(NA_QBLOCK), np.arange(nwin)
    q_dr, q_col = q_off // GRID_W, q_off % GRID_W
    k_dr, k_col = k_off // GRID_W, k_off % GRID_W
    col_start = np.clip(q_col - NA_COLS // 2, 0, GRID_W - NA_COLS)
    col_in = (k_col[None, :] >= col_start[:, None]) & (k_col[None, :] < col_start[:, None] + NA_COLS)
    dc_idx = np.clip(k_col[None, :] - q_col[:, None] + NA_COLS - 1, 0, 2 * NA_COLS - 2)
    patterns, var, wstart = {}, [], []
    for j in range(nb):
        r = j * qrows + q_dr
        row_start = np.clip(r - win_rows // 2, 0, rows - win_rows)
        b0 = min(int(row_start[0]), rows - band_rows)
        w0 = min(b0 - b0 % 2, rows - NA_WIN_ROWS)
        assert w0 % qrows == 0 and w0 <= b0 and b0 + band_rows <= w0 + NA_WIN_ROWS
        k_row = w0 + k_dr
        in_band = (k_row >= b0) & (k_row < b0 + band_rows)
        in_win = (col_in & in_band[None, :] & (k_row[None, :] >= row_start[:, None])
                  & (k_row[None, :] < row_start[:, None] + win_rows))
        dr_idx = np.clip(k_row[None, :] - r[:, None] + NA_ROWS - 1, 0, 2 * NA_ROWS - 2)
        idx = np.where(in_win, dr_idx * (2 * NA_COLS - 1) + dc_idx, -1).astype(np.int32)
        key = idx.tobytes()
        if key not in patterns:
            patterns[key] = (len(patterns), idx)
        var.append(patterns[key][0])
        wstart.append(w0 * GRID_W)
    idx_all = np.stack([p[1] for p in sorted(patterns.values(), key=lambda p: p[0])])
    return idx_all.reshape(len(patterns), 1, -1), np.asarray(var, np.int32), np.asarray(wstart, np.int32)


def _na_bias_kernel(idx_ref, rpb_ref, o_ref):
    idx = idx_ref[0]
    r = rpb_ref[...]
    onehot = (lax.broadcasted_iota(jnp.int32, (r.shape[1], idx.shape[1]), 0) == idx).astype(BF16)
    b1 = r.astype(BF16)
    r2 = r - b1.astype(F32)
    b2 = r2.astype(BF16)
    b3 = (r2 - b2.astype(F32)).astype(BF16)
    out = (jnp.dot(b1, onehot, preferred_element_type=F32) + jnp.dot(b2, onehot, preferred_element_type=F32)
           + jnp.dot(b3, onehot, preferred_element_type=F32))
    o_ref[0] = jnp.where(idx >= 0, out, NEG_INF)


def _na_bias(idx_all, rpb):
    nvar, _, npos = idx_all.shape
    heads = rpb.shape[0]
    nr = (2 * NA_ROWS - 1) * (2 * NA_COLS - 1)
    nr_pad = pl.cdiv(nr, LANES) * LANES
    rflat = jnp.pad(rpb.reshape(heads, nr), ((0, 0), (0, nr_pad - nr)))
    tn = 2048
    assert npos % tn == 0
    out = pl.pallas_call(
        _na_bias_kernel,
        grid=(nvar, npos // tn),
        in_specs=[pl.BlockSpec((1, 1, tn), lambda v, j: (v, 0, j)), _resident(rflat.shape)],
        out_specs=pl.BlockSpec((1, heads, tn), lambda v, j: (v, 0, j)),
        out_shape=jax.ShapeDtypeStruct((nvar, heads, npos), F32),
        compiler_params=_cparams(2),
        name="na_bias",
    )(jnp.asarray(idx_all), rflat)
    return out.reshape(nvar, heads, NA_QBLOCK, npos // NA_QBLOCK)


def _halves(lo_src, hi_src):
    lane = lax.broadcasted_iota(jnp.int32, lo_src.shape, 1)
    return jnp.where(lane < LANES // 2, lo_src, hi_src)


def _na_kernel(var_ref, ws_ref, q_ref, k_ref, v_ref, b0_ref, b1_ref, o_ref, *, n_ctx, n_ctx_tiles, scale):
    qi = pl.program_id(2)
    nwin = NA_WIN_ROWS * GRID_W

    @pl.when(qi < n_ctx_tiles)
    def _():
        kc, vc = k_ref[0, 0:n_ctx, :], v_ref[0, 0:n_ctx, :]
        outs = []
        for a in (0, 1):
            s_c = lax.dot_general(q_ref[0, :, a * LANES:(a + 1) * LANES], kc, NT_DIMS,
                                  preferred_element_type=F32) * scale
            p_c = jnp.exp(s_c - jnp.max(s_c, axis=-1, keepdims=True))
            acc = jnp.dot(p_c.astype(BF16), vc, preferred_element_type=F32)
            outs.append(acc / jnp.sum(p_c, axis=-1, keepdims=True))
        o_ref[0] = _halves(outs[0], outs[1]).astype(BF16)

    @pl.when(qi >= n_ctx_tiles)
    def _():
        kc, vc = k_ref[0, 0:n_ctx, :], v_ref[0, 0:n_ctx, :]
        work = []
        for sub, bias_ref in ((0, b0_ref), (1, b1_ref)):
            j = (qi - n_ctx_tiles) * (TOKEN_TILE // NA_QBLOCK) + sub
            start = pl.multiple_of(n_ctx + ws_ref[j], NA_QBLOCK)
            kw, vw = k_ref[0, pl.ds(start, nwin), :], v_ref[0, pl.ds(start, nwin), :]
            for a in (0, 1):
                q = q_ref[0, sub * NA_QBLOCK:(sub + 1) * NA_QBLOCK, a * LANES:(a + 1) * LANES]
                s_c = lax.dot_general(q, kc, NT_DIMS, preferred_element_type=F32) * scale
                s_w = lax.dot_general(q, kw, NT_DIMS, preferred_element_type=F32) * scale + bias_ref[0, a]
                work.append((s_c, s_w, vw))
        outs = []
        for s_c, s_w, vw in work:
            m = jnp.maximum(jnp.max(s_c, axis=-1, keepdims=True), jnp.max(s_w, axis=-1, keepdims=True))
            p_c, p_w = jnp.exp(s_c - m), jnp.exp(s_w - m)
            l = jnp.sum(p_c, axis=-1, keepdims=True) + jnp.sum(p_w, axis=-1, keepdims=True)
            acc = (jnp.dot(p_c.astype(BF16), vc, preferred_element_type=F32)
                   + jnp.dot(p_w.astype(BF16), vw, preferred_element_type=F32))
            outs.append(acc / l)
        for sub in (0, 1):
            o_ref[0, sub * NA_QBLOCK:(sub + 1) * NA_QBLOCK, :] = _halves(outs[2 * sub], outs[2 * sub + 1]).astype(BF16)


def _na_attention(q, k, v, bias, var, wstart, *, n_ctx, scale):
    b, s, nq = q.shape
    tq = TOKEN_TILE
    n_pairs = nq // (2 * LANES)
    n_ctx_tiles = n_ctx // tq
    per_tile = tq // NA_QBLOCK
    nwin = NA_WIN_ROWS * GRID_W

    def bias_spec(sub):
        def index(bi, p, i, var_ref, ws_ref):
            j = jnp.maximum(i - n_ctx_tiles, 0) * per_tile + sub
            return (var_ref[j], p, 0, 0)
        return pl.BlockSpec((1, 2, NA_QBLOCK, nwin), index)

    grid_spec = pltpu.PrefetchScalarGridSpec(
        num_scalar_prefetch=2,
        grid=(b, n_pairs, s // tq),
        in_specs=[
            pl.BlockSpec((1, tq, 2 * LANES), lambda bi, p, i, *_: (bi, i, p)),
            pl.BlockSpec((1, s, LANES), lambda bi, p, i, *_: (bi, 0, p)),
            pl.BlockSpec((1, s, LANES), lambda bi, p, i, *_: (bi, 0, p)),
            bias_spec(0), bias_spec(1),
        ],
        out_specs=pl.BlockSpec((1, tq, LANES), lambda bi, p, i, *_: (bi, i, p)),
    )
    return pl.pallas_call(
        functools.partial(_na_kernel, n_ctx=n_ctx, n_ctx_tiles=n_ctx_tiles, scale=scale),
        grid_spec=grid_spec,
        out_shape=jax.ShapeDtypeStruct((b, s, n_pairs * LANES), BF16),
        compiler_params=_cparams(3),
        name="na_attn",
    )(var, wstart, q, k, v, bias, bias)


def _outproj_kernel(x_ref, ox_ref, oc_ref, w_ref, mod_ref, y_ref, *, n_ctx_tiles):
    o = jnp.where(pl.program_id(1) < n_ctx_tiles, oc_ref[0], ox_ref[0])
    y = jnp.dot(o, w_ref[...], preferred_element_type=F32)
    y_ref[0] = x_ref[0] + mod_ref[0, 2:3, :] * y


def _outproj(xs, o_x, o_c, w, mods, *, n_ctx_tiles, x_offset):
    b, s, d = xs.shape
    tm = TOKEN_TILE
    n = o_x.shape[2]
    return pl.pallas_call(
        functools.partial(_outproj_kernel, n_ctx_tiles=n_ctx_tiles),
        grid=(b, s // tm),
        in_specs=[
            pl.BlockSpec((1, tm, d), lambda bi, i: (bi, i, 0)),
            pl.BlockSpec((1, tm, n), lambda bi, i: (bi, jnp.maximum(i, n_ctx_tiles) - x_offset, 0)),
            pl.BlockSpec((1, tm, n), lambda bi, i: (bi, jnp.minimum(i, n_ctx_tiles - 1), 0)),
            _resident(w.shape),
            pl.BlockSpec((1, 6, d), lambda bi, i: (jnp.where(i < n_ctx_tiles, b, bi), 0, 0)),
        ],
        out_specs=pl.BlockSpec((1, tm, d), lambda bi, i: (bi, i, 0)),
        out_shape=jax.ShapeDtypeStruct((b, s, d), F32),
        compiler_params=_cparams(2),
        name="outproj",
    )(xs, o_x, o_c, w, mods)


def _ffn_kernel(xp_ref, x_ref, xn_ref, mod_ref, g_ref, wup_ref, cw_ref, cb_ref, wdn_ref, y_ref, u_scr, a_scr,
                *, n_ctx_tiles, n_tiles, chunk):
    i = pl.program_id(1)
    tm = x_ref.shape[1]
    has_prev = jnp.logical_and(i != 0, i != n_ctx_tiles).astype(F32)
    has_next = jnp.logical_and(i != n_ctx_tiles - 1, i != n_tiles - 1).astype(F32)
    g, shift, scale = g_ref[...], mod_ref[0, 3:4, :], mod_ref[0, 4:5, :]
    x = x_ref[0]
    h = jnp.concatenate([_norm_mod(xp_ref[0], g, shift, scale) * has_prev,
                         _norm_mod(x, g, shift, scale),
                         _norm_mod(xn_ref[0], g, shift, scale) * has_next], axis=0).astype(BF16)
    n_chunks = wup_ref.shape[1] // (2 * chunk)
    for j in range(n_chunks):
        cols = slice(j * 2 * chunk, (j + 1) * 2 * chunk)
        u_scr[...] = jnp.dot(h, wup_ref[:, cols], preferred_element_type=F32)
        cw = cw_ref[:, cols]
        u = (u_scr[HALO - 1:HALO - 1 + tm, :] * cw[0:1] + u_scr[HALO:HALO + tm, :] * cw[1:2]
             + u_scr[HALO + 1:HALO + 1 + tm, :] * cw[2:3] + cb_ref[:, cols])
        val, gate = u[:, :chunk], u[:, chunk:]
        a_scr[:, j * chunk:(j + 1) * chunk] = (gate * (1.0 / (1.0 + jnp.exp(-gate))) * val).astype(BF16)
    y = jnp.dot(a_scr[...], wdn_ref[...], preferred_element_type=F32)
    y_ref[0] = x + mod_ref[0, 5:6, :] * y


def _ffn(xs, mods, g, wup, cw, cb, wdn, *, n_ctx_tiles, chunk):
    b, s, d = xs.shape
    tm = TOKEN_TILE
    n_tiles = s // tm
    per = tm // HALO
    f = wdn.shape[0]
    return pl.pallas_call(
        functools.partial(_ffn_kernel, n_ctx_tiles=n_ctx_tiles, n_tiles=n_tiles, chunk=chunk),
        grid=(b, n_tiles),
        in_specs=[
            pl.BlockSpec((1, HALO, d), lambda bi, i: (bi, jnp.maximum(i * per - 1, 0), 0)),
            pl.BlockSpec((1, tm, d), lambda bi, i: (bi, i, 0)),
            pl.BlockSpec((1, HALO, d), lambda bi, i: (bi, jnp.minimum((i + 1) * per, s // HALO - 1), 0)),
            pl.BlockSpec((1, 6, d), lambda bi, i: (jnp.where(i < n_ctx_tiles, b, bi), 0, 0)),
            _resident((1, d)),
            pl.BlockSpec(wup.shape, lambda bi, i: (0, 0), pipeline_mode=pl.Buffered(1)),
            _resident(cw.shape), _resident(cb.shape),
            pl.BlockSpec(wdn.shape, lambda bi, i: (0, 0), pipeline_mode=pl.Buffered(1)),
        ],
        out_specs=pl.BlockSpec((1, tm, d), lambda bi, i: (bi, i, 0)),
        out_shape=jax.ShapeDtypeStruct((b, s, d), F32),
        scratch_shapes=[pltpu.VMEM((tm + 2 * HALO, 2 * chunk), F32), pltpu.VMEM((tm, f), BF16)],
        compiler_params=_cparams(2),
        name="ffn",
    )(xs, xs, xs, mods, g, wup, cw, cb, wdn)


def _final_kernel(x_ref, g_ref, o_ref):
    x = x_ref[0]
    o_ref[0] = x * lax.rsqrt(jnp.mean(x * x, axis=-1, keepdims=True) + EPS) * g_ref[...]


def _final_norm(xs, g, *, n_ctx_tiles):
    b, s, d = xs.shape
    tm = TOKEN_TILE
    nt = s // tm - n_ctx_tiles
    return pl.pallas_call(
        _final_kernel,
        grid=(b, nt),
        in_specs=[pl.BlockSpec((1, tm, d), lambda bi, i: (bi, i + n_ctx_tiles, 0)), _resident((1, d))],
        out_specs=pl.BlockSpec((1, tm, d), lambda bi, i: (bi, i, 0)),
        out_shape=jax.ShapeDtypeStruct((b, nt * tm, d), F32),
        compiler_params=_cparams(2),
        name="final_norm",
    )(xs, g)


def _rope_table(t, n_ctx, rot_dim, lane_lo):
    tt = jnp.arange(t)
    row = (tt // GRID_W).astype(F32)
    col = (tt % GRID_W).astype(F32)
    half = rot_dim // 2
    inv = ROPE_BASE ** (-jnp.arange(0, half, 2, dtype=F32) / half)
    ar, ac = row[:, None] * inv, col[:, None] * inv
    ang = jnp.concatenate([ar, ar, ac, ac], axis=-1)
    quarter = rot_dim // 4
    sign = jnp.where((jnp.arange(rot_dim) % (2 * quarter)) < quarter, -1.0, 1.0).astype(F32)
    cos, sin = jnp.cos(ang), jnp.sin(ang) * sign
    reps = (LANES - lane_lo) // rot_dim if lane_lo == 0 else 1
    cos_l = jnp.ones((t, LANES), F32).at[:, lane_lo:lane_lo + reps * rot_dim].set(jnp.tile(cos, (1, reps)))
    sin_l = jnp.zeros((t, LANES), F32).at[:, lane_lo:lane_lo + reps * rot_dim].set(jnp.tile(sin, (1, reps)))
    cos_s = jnp.concatenate([jnp.ones((n_ctx, LANES), F32), cos_l], axis=0)
    sin_s = jnp.concatenate([jnp.zeros((n_ctx, LANES), F32), sin_l], axis=0)
    return cos_s, sin_s


def _to_slots(w, n_heads, width, halves):
    d = w.shape[0]
    wh = w.reshape(d, n_heads, width)
    out = jnp.zeros((d, n_heads, LANES), w.dtype)
    for h in range(n_heads):
        out = out.at[:, h, halves[h] * (LANES // 2):halves[h] * (LANES // 2) + width].set(wh[:, h])
    return out.reshape(d, n_heads * LANES)


def _vt_weights(wv, groups, width, tm):
    d = wv.shape[0]
    rows = width + 16
    wt = jnp.zeros((groups, rows, d), wv.dtype).at[:, :width, :].set(wv.T.reshape(groups, width, d))
    ones = jnp.zeros((groups, rows, tm), F32).at[:, width, :].set(1.0)
    return wt.reshape(groups * rows, d), ones.reshape(groups * rows, tm)


def _dup(g):
    return jnp.concatenate([g, g]).reshape(1, LANES).astype(F32)


def kernel(x, c, ctx, c_ctx, ada_w, ada_b, norm1_g, norm2_g, ffn_w_up, ffn_conv_w, ffn_conv_b, ffn_w_down,
           gqa_w_in, gqa_q_norm_g, gqa_k_norm_g, gqa_w_out,
           mla_w_in, mla_q_norm_g, mla_kv_norm_g, mla_w_uq, mla_w_ukv, mla_w_out,
           diff_w_in, diff_lambda, diff_subln_g, diff_w_out,
           na_w_in, na_rpb, na_w_out, final_norm_g):
    b, t, d = x.shape
    n_ctx = ctx.shape[1]
    depth = ada_w.shape[0]
    assert n_ctx % TOKEN_TILE == 0 and t % TOKEN_TILE == 0 and b + 1 <= 8
    nct = n_ctx // TOKEN_TILE
    f = ffn_w_down.shape[1]
    chunk = 2 * LANES
    assert f % chunk == 0

    xs = jnp.concatenate([ctx, x], axis=1)
    cin = jnp.zeros((8, d), F32).at[:b].set(c).at[b].set(c_ctx)
    mods_all = _ada_all(cin, ada_w, ada_b).reshape(depth, 8, 6, d)

    tm = TOKEN_TILE
    ones_l = jnp.ones((1, LANES), F32)
    dummy_tab = jnp.zeros((xs.shape[1], LANES), F32)
    dummy_ones = jnp.zeros((8, LANES), F32)
    cos64, sin64 = _rope_table(t, n_ctx, GQA_HEAD_DIM, 0)
    cos_mla, sin_mla = _rope_table(t, n_ctx, MLA_ROPE_DIM, MLA_NOPE_DIM)
    no_lam, no_g = jnp.zeros((4, DIFF_HEAD_DIM), F32), jnp.ones((1, LANES), F32)
    pair_rows = MLA_V_DIM + 16
    diff_rows = 2 * DIFF_HEAD_DIM + 16

    for i in range(depth):
        m, j = i % N_MIXERS, i // N_MIXERS
        mods = mods_all[i]
        g1 = norm1_g[i].reshape(1, d)
        if m == 0:
            w = gqa_w_in[j].astype(BF16)
            nqc, nkc = GQA_HEADS * GQA_HEAD_DIM, GQA_KV_HEADS * GQA_HEAD_DIM
            wq = _to_slots(w[:, :nqc], GQA_HEADS, GQA_HEAD_DIM, [0] * GQA_HEADS)
            wk = _to_slots(w[:, nqc:nqc + nkc], GQA_KV_HEADS, GQA_HEAD_DIM, [0] * GQA_KV_HEADS)
            wvt, vones = _vt_weights(w[:, nqc + nkc:], GQA_KV_HEADS, GQA_HEAD_DIM, tm)
            qc = GQA_HEAD_DIM ** -0.5 * LOG2E
            q, k, vt = _project(xs, mods, g1, wq, wk, wvt, _dup(gqa_q_norm_g[j]), _dup(gqa_k_norm_g[j]),
                                cos64 * qc, sin64 * qc, cos64, sin64, vones, qk_norm=True, rope=True,
                                head_dim=GQA_HEAD_DIM, v_transposed=True, q_rotated=True, n_ctx_tiles=nct,
                                name="proj_gqa")
            o_x, o_c = _flash(q, k, vt, no_lam, no_g, mode="pair", kw=LANES, k_cols=((0, LANES), (0, LANES)),
                       v_block_rows=pair_rows, v_rows=((0, pair_rows), (0, pair_rows)), vw=GQA_HEAD_DIM,
                       k_index=lambda p: p // 2, v_index=lambda p: p // 2, n_ctx=n_ctx, diff_scale=1.0,
                       name="attn_gqa")
            w_out = gqa_w_out[j]
        elif m == 1:
            w = mla_w_in[j].astype(BF16)
            nlat = MLA_Q_LORA + MLA_KV_LORA
            kr_slot = jnp.zeros((d, LANES), BF16).at[:, MLA_NOPE_DIM:MLA_NOPE_DIM + MLA_ROPE_DIM].set(w[:, nlat:])
            win = jnp.concatenate([w[:, :nlat], kr_slot], axis=1)
            qk_dim = MLA_NOPE_DIM + MLA_ROPE_DIM
            wuq = _to_slots(mla_w_uq[j].astype(BF16), MLA_HEADS, qk_dim, [0] * MLA_HEADS)
            wukv = mla_w_ukv[j].astype(BF16).reshape(MLA_KV_LORA, MLA_HEADS, MLA_NOPE_DIM + MLA_V_DIM)
            wuk = _to_slots(wukv[:, :, :MLA_NOPE_DIM].reshape(MLA_KV_LORA, -1), MLA_HEADS, MLA_NOPE_DIM,
                            [0] * MLA_HEADS)
            wuvt, vones = _vt_weights(wukv[:, :, MLA_NOPE_DIM:].reshape(MLA_KV_LORA, MLA_HEADS * MLA_V_DIM),
                                      MLA_HEADS, MLA_V_DIM, tm)
            qc = qk_dim ** -0.5 * LOG2E
            q, k, vt = _project_mla(xs, mods, g1, win, mla_q_norm_g[j].reshape(1, -1),
                                    mla_kv_norm_g[j].reshape(1, -1), wuq, wuk, wuvt,
                                    cos_mla * qc, sin_mla * qc, cos_mla, sin_mla, vones, n_ctx_tiles=nct)
            o_x, o_c = _flash(q, k, vt, no_lam, no_g, mode="pair", kw=2 * LANES, k_cols=((0, LANES), (LANES, 2 * LANES)),
                       v_block_rows=2 * pair_rows, v_rows=((0, pair_rows), (pair_rows, 2 * pair_rows)),
                       vw=MLA_V_DIM, k_index=lambda p: p, v_index=lambda p: p, n_ctx=n_ctx, diff_scale=1.0,
                       name="attn_mla")
            w_out = mla_w_out[j]
        elif m == 2:
            w = diff_w_in[j].astype(BF16)
            wq = _to_slots(w[:, :d], 2 * DIFF_HEADS, DIFF_HEAD_DIM, [0, 1] * DIFF_HEADS)
            wvt, vones = _vt_weights(w[:, 2 * d:], DIFF_HEADS, 2 * DIFF_HEAD_DIM, tm)
            lambda_init = 0.8 - 0.6 * math.exp(-0.3 * i)
            qc = DIFF_HEAD_DIM ** -0.5 * LOG2E
            q, k, vt = _project(xs, mods, g1, wq, w[:, d:2 * d], wvt, ones_l, ones_l,
                                cos64 * qc, sin64 * qc, cos64, sin64, vones, qk_norm=False, rope=True,
                                head_dim=DIFF_HEAD_DIM, v_transposed=True, q_rotated=True, n_ctx_tiles=nct,
                                name="proj_diff")
            o_x, o_c = _flash(q, k, vt, diff_lambda[j], diff_subln_g[j].reshape(1, LANES), mode="diff", kw=LANES,
                       k_cols=((0, LANES), (0, LANES)), v_block_rows=diff_rows,
                       v_rows=((0, diff_rows), (0, diff_rows)), vw=2 * DIFF_HEAD_DIM,
                       k_index=lambda p: p, v_index=lambda p: p, n_ctx=n_ctx, diff_scale=1.0 - lambda_init,
                       name="attn_diff")
            w_out = diff_w_out[j]
        else:
            w = na_w_in[j].astype(BF16)
            wq = _to_slots(w[:, :d], NA_HEADS, NA_HEAD_DIM, [0, 1] * (NA_HEADS // 2))
            q, k, v = _project(xs, mods, g1, wq, w[:, d:2 * d], w[:, 2 * d:], ones_l, ones_l,
                               dummy_tab, dummy_tab, dummy_tab, dummy_tab, dummy_ones, qk_norm=False, rope=False,
                               head_dim=NA_HEAD_DIM, v_transposed=False, q_rotated=False, n_ctx_tiles=nct,
                               name="proj_na")
            idx_all, var, wstart = _na_tables(t)
            bias = _na_bias(idx_all, na_rpb[j])
            o_x = o_c = _na_attention(q, k, v, bias, jnp.asarray(var), jnp.asarray(wstart), n_ctx=n_ctx,
                                      scale=NA_HEAD_DIM ** -0.5)
            w_out = na_w_out[j]
        xs = _outproj(xs, o_x, o_c, w_out.astype(BF16), mods, n_ctx_tiles=nct, x_offset=0 if m == 3 else nct)

        def regroup(a):
            lead = a.shape[:-1]
            return jnp.stack([a[..., :f].reshape(*lead, f // chunk, chunk),
                              a[..., f:].reshape(*lead, f // chunk, chunk)], axis=-2).reshape(*lead, 2 * f)
        xs = _ffn(xs, mods, norm2_g[i].reshape(1, d), regroup(ffn_w_up[i]).astype(BF16), regroup(ffn_conv_w[i]),
                  regroup(ffn_conv_b[i]).reshape(1, 2 * f), ffn_w_down[i].astype(BF16), n_ctx_tiles=nct, chunk=chunk)

    return _final_norm(xs, final_norm_g.reshape(1, d), n_ctx_tiles=nct)
```

```python
import functools
import math

import numpy as np
import jax
import jax.numpy as jnp
from jax import lax
from jax.experimental import pallas as pl
from jax.experimental.pallas import tpu as pltpu

F32 = jnp.float32
BF16 = jnp.bfloat16

LANES = 128
TOKEN_TILE = 256
HALO = 8
KV_CHUNK = 768
SUB_KEYS = 256
FLASH_TQ = 512
LOOP_CHUNKS = 4
VMEM_LIMIT = 56 * 1024 * 1024

GRID_W = 64
ROPE_BASE = 10000.0
EPS = 1e-6
NEG_INF = -1e30
LOG2E = 1.4426950408889634

GQA_HEADS, GQA_KV_HEADS, GQA_HEAD_DIM = 16, 4, 64
MLA_HEADS, MLA_NOPE_DIM, MLA_ROPE_DIM, MLA_V_DIM = 16, 64, 32, 64
MLA_Q_LORA, MLA_KV_LORA = 384, 256
DIFF_HEADS, DIFF_HEAD_DIM = 8, 64
NA_HEADS, NA_HEAD_DIM, NA_ROWS, NA_COLS = 16, 64, 8, 16
NA_QBLOCK = 128
NA_WIN_ROWS = 10
N_MIXERS = 4

NT_DIMS = (((1,), (1,)), ((), ()))


def _cparams(n_axes):
    return pltpu.CompilerParams(dimension_semantics=("arbitrary",) * n_axes, vmem_limit_bytes=VMEM_LIMIT)


def _resident(shape):
    nd = len(shape)
    return pl.BlockSpec(shape, lambda *_: (0,) * nd)


def _ada_kernel(c_ref, w_ref, b_ref, o_ref):
    c = c_ref[...]
    s = c * (1.0 / (1.0 + jnp.exp(-c)))
    o_ref[0] = jnp.dot(s.astype(BF16), w_ref[0].astype(BF16), preferred_element_type=F32) + b_ref[0]


def _ada_all(cin, ada_w, ada_b):
    depth, d, n = ada_w.shape
    tn = n // 4
    return pl.pallas_call(
        _ada_kernel,
        grid=(depth, n // tn),
        in_specs=[
            pl.BlockSpec((8, d), lambda l, j: (0, 0)),
            pl.BlockSpec((1, d, tn), lambda l, j: (l, 0, j)),
            pl.BlockSpec((1, 1, tn), lambda l, j: (l, 0, j)),
        ],
        out_specs=pl.BlockSpec((1, 8, tn), lambda l, j: (l, 0, j)),
        out_shape=jax.ShapeDtypeStruct((depth, 8, n), F32),
        compiler_params=_cparams(2),
        name="ada",
    )(cin, ada_w, ada_b.reshape(depth, 1, n))


def _norm_mod(x, g, shift, scale):
    ms = jnp.mean(x * x, axis=-1, keepdims=True)
    return (x * lax.rsqrt(ms + EPS) * g) * (1.0 + scale) + shift


def _rope_slot(y, cos, sin_signed, lo_mask, quarter):
    fwd = pltpu.roll(y, LANES - quarter, axis=1)
    bwd = pltpu.roll(y, quarter, axis=1)
    return y * cos + jnp.where(lo_mask, fwd, bwd) * sin_signed


def _lo_mask(shape, quarter):
    lane = lax.broadcasted_iota(jnp.int32, shape, 1)
    return (lane % (2 * quarter)) < quarter


def _q_spec(n, tm, n_tiles, n_ctx_tiles, rotate):
    if not rotate:
        return pl.BlockSpec((1, tm, n), lambda bi, i: (bi, i, 0))
    n_lat = n_tiles - n_ctx_tiles
    return pl.BlockSpec((1, tm, n), lambda bi, i: (bi, jnp.where(i < n_ctx_tiles, i + n_lat, i - n_ctx_tiles), 0))


def _vt_spec(rows, tm):
    per = KV_CHUNK // tm
    return pl.BlockSpec((1, 1, rows, tm), lambda bi, i: (bi, i // per, 0, i % per))


def _proj_kernel(x_ref, mod_ref, g_ref, wq_ref, wk_ref, wv_ref, qg_ref, kg_ref, cosq_ref, sinq_ref, cosk_ref, sink_ref,
                 vones_ref, q_ref, k_ref, v_ref, *, qk_norm, rope, head_dim, v_transposed):
    h = _norm_mod(x_ref[0], g_ref[...], mod_ref[0, 0:1, :], mod_ref[0, 1:2, :]).astype(BF16)
    tm = h.shape[0]
    lo = _lo_mask((tm, LANES), head_dim // 4) if rope else None

    def finish(w_ref, o_ref, gain_ref, cos_ref, sin_ref):
        n = w_ref.shape[1]
        for c0 in range(0, n, 2 * LANES):
            c1 = min(c0 + 2 * LANES, n)
            y2 = jnp.dot(h, w_ref[:, c0:c1], preferred_element_type=F32)
            for s0 in range(0, c1 - c0, LANES):
                y = y2[:, s0:s0 + LANES]
                if qk_norm:
                    ms = jnp.sum(y * y, axis=-1, keepdims=True) * (1.0 / head_dim)
                    y = y * lax.rsqrt(ms + EPS) * gain_ref[...]
                if rope:
                    y = _rope_slot(y, cos_ref[...], sin_ref[...], lo, head_dim // 4)
                o_ref[0, :, c0 + s0:c0 + s0 + LANES] = y.astype(BF16)

    finish(wq_ref, q_ref, qg_ref, cosq_ref, sinq_ref)
    finish(wk_ref, k_ref, kg_ref, cosk_ref, sink_ref)
    if v_transposed:
        vt = lax.dot_general(wv_ref[...], h, NT_DIMS, preferred_element_type=F32) + vones_ref[...]
        v_ref[0, 0] = vt.astype(BF16)
    else:
        v_ref[0] = jnp.dot(h, wv_ref[...], preferred_element_type=F32).astype(BF16)


def _project(xs, mods, g, wq, wk, wv, qg, kg, cosq, sinq, cosk, sink, vones, *, qk_norm, rope, head_dim, v_transposed,
             q_rotated, n_ctx_tiles, name):
    b, s, d = xs.shape
    tm = TOKEN_TILE
    nq, nk = wq.shape[1], wk.shape[1]
    tok = lambda n: pl.BlockSpec((1, tm, n), lambda bi, i: (bi, i, 0))
    tab = pl.BlockSpec((tm, LANES), lambda bi, i: (i, 0))
    if v_transposed:
        v_spec = _vt_spec(wv.shape[0], tm)
        v_shape = jax.ShapeDtypeStruct((b, s // KV_CHUNK, wv.shape[0], KV_CHUNK), BF16)
    else:
        v_spec, v_shape = tok(wv.shape[1]), jax.ShapeDtypeStruct((b, s, wv.shape[1]), BF16)
    return pl.pallas_call(
        functools.partial(_proj_kernel, qk_norm=qk_norm, rope=rope, head_dim=head_dim, v_transposed=v_transposed),
        grid=(b, s // tm),
        in_specs=[
            tok(d),
            pl.BlockSpec((1, 6, d), lambda bi, i: (jnp.where(i < n_ctx_tiles, b, bi), 0, 0)),
            _resident((1, d)),
            _resident(wq.shape), _resident(wk.shape), _resident(wv.shape),
            _resident((1, LANES)), _resident((1, LANES)),
            tab, tab, tab, tab,
            _resident(vones.shape),
        ],
        out_specs=[_q_spec(nq, tm, s // tm, n_ctx_tiles, q_rotated), tok(nk), v_spec],
        out_shape=[jax.ShapeDtypeStruct((b, s, nq), BF16), jax.ShapeDtypeStruct((b, s, nk), BF16), v_shape],
        compiler_params=_cparams(2),
        name=name,
    )(xs, mods, g, wq, wk, wv, qg, kg, cosq, sinq, cosk, sink, vones)


def _mla_proj_kernel(x_ref, mod_ref, g_ref, win_ref, qn_ref, kvn_ref, wuq_ref, wuk_ref, wuvt_ref,
                     cosq_ref, sinq_ref, cosk_ref, sink_ref, vones_ref, q_ref, k_ref, v_ref):
    h = _norm_mod(x_ref[0], g_ref[...], mod_ref[0, 0:1, :], mod_ref[0, 1:2, :]).astype(BF16)
    tm = h.shape[0]
    t1 = jnp.dot(h, win_ref[...], preferred_element_type=F32)
    cq = t1[:, :MLA_Q_LORA]
    ckv = t1[:, MLA_Q_LORA:MLA_Q_LORA + MLA_KV_LORA]
    kr = t1[:, MLA_Q_LORA + MLA_KV_LORA:]
    cqn = (cq * lax.rsqrt(jnp.mean(cq * cq, axis=-1, keepdims=True) + EPS) * qn_ref[...]).astype(BF16)
    ckvn = (ckv * lax.rsqrt(jnp.mean(ckv * ckv, axis=-1, keepdims=True) + EPS) * kvn_ref[...]).astype(BF16)
    quarter = MLA_ROPE_DIM // 4
    lo = _lo_mask((tm, LANES), quarter)
    kr = _rope_slot(kr, cosk_ref[...], sink_ref[...], lo, quarter)
    n = wuq_ref.shape[1]
    for c0 in range(0, n, 2 * LANES):
        q2 = jnp.dot(cqn, wuq_ref[:, c0:c0 + 2 * LANES], preferred_element_type=F32)
        k2 = jnp.dot(ckvn, wuk_ref[:, c0:c0 + 2 * LANES], preferred_element_type=F32)
        for s0 in (0, LANES):
            q_ref[0, :, c0 + s0:c0 + s0 + LANES] = _rope_slot(
                q2[:, s0:s0 + LANES], cosq_ref[...], sinq_ref[...], lo, quarter).astype(BF16)
            k_ref[0, :, c0 + s0:c0 + s0 + LANES] = (k2[:, s0:s0 + LANES] + kr).astype(BF16)
    vt = lax.dot_general(wuvt_ref[...], ckvn, NT_DIMS, preferred_element_type=F32) + vones_ref[...]
    v_ref[0, 0] = vt.astype(BF16)


def _project_mla(xs, mods, g, win, qn, kvn, wuq, wuk, wuvt, cosq, sinq, cosk, sink, vones, *, n_ctx_tiles):
    b, s, d = xs.shape
    tm = TOKEN_TILE
    nq, nk = wuq.shape[1], wuk.shape[1]
    tok = lambda n: pl.BlockSpec((1, tm, n), lambda bi, i: (bi, i, 0))
    tab = pl.BlockSpec((tm, LANES), lambda bi, i: (i, 0))
    return pl.pallas_call(
        _mla_proj_kernel,
        grid=(b, s // tm),
        in_specs=[
            tok(d),
            pl.BlockSpec((1, 6, d), lambda bi, i: (jnp.where(i < n_ctx_tiles, b, bi), 0, 0)),
            _resident((1, d)),
            _resident(win.shape), _resident(qn.shape), _resident(kvn.shape),
            _resident(wuq.shape), _resident(wuk.shape), _resident(wuvt.shape),
            tab, tab, tab, tab,
            _resident(vones.shape),
        ],
        out_specs=[_q_spec(nq, tm, s // tm, n_ctx_tiles, True), tok(nk), _vt_spec(wuvt.shape[0], tm)],
        out_shape=[jax.ShapeDtypeStruct((b, s, nq), BF16), jax.ShapeDtypeStruct((b, s, nk), BF16),
                   jax.ShapeDtypeStruct((b, s // KV_CHUNK, wuvt.shape[0], KV_CHUNK), BF16)],
        compiler_params=_cparams(2),
        name="proj_mla",
    )(xs, mods, g, win, qn, kvn, wuq, wuk, wuvt, cosq, sinq, cosk, sink, vones)


def _flash_kernel(q_ref, k_ref, vt_ref, lam_ref, subg_ref, o_ref, *scratch, mode, k_cols, v_rows, vw, diff_scale,
                  ctx_keys):
    tq = q_ref.shape[1]
    n_chunks = vt_ref.shape[1]
    qs = [q_ref[0, :, a * LANES:(a + 1) * LANES] for a in (0, 1)]
    n_rows = v_rows[0][1] - v_rows[0][0]

    def keys(key_rows):
        k0 = k_ref[0, key_rows, k_cols[0][0]:k_cols[0][1]]
        k1 = k0 if k_cols[1] == k_cols[0] else k_ref[0, key_rows, k_cols[1][0]:k_cols[1][1]]
        return k0, k1

    def values(chunk_idx, key_lanes):
        v0 = vt_ref[0, chunk_idx, v_rows[0][0]:v_rows[0][1], key_lanes]
        v1 = v0 if v_rows[1] == v_rows[0] else vt_ref[0, chunk_idx, v_rows[1][0]:v_rows[1][1], key_lanes]
        return v0, v1

    def update(st, cmax, vt_a, carry_a):
        m, acc = carry_a
        m_new = jnp.maximum(m, cmax)
        p = jnp.exp2(st - m_new).astype(BF16)
        return m_new, jnp.exp2(m - m_new) * acc + jnp.dot(vt_a, p, preferred_element_type=F32)

    init = tuple((jnp.full((1, tq), -jnp.inf, F32), jnp.zeros((n_rows, tq), F32)) for _ in (0, 1))

    def finish(carry):
        (_, acc_a), (_, acc_b) = carry
        oa = acc_a[0:vw] / acc_a[vw:vw + 1]
        ob = acc_b[0:vw] / acc_b[vw:vw + 1]
        if mode == "pair":
            out = jnp.concatenate([oa, ob], axis=0).T
        else:
            lam = lam_ref[...]
            lam_full = (jnp.exp(jnp.sum(lam[0:1] * lam[1:2], axis=-1, keepdims=True))
                        - jnp.exp(jnp.sum(lam[2:3] * lam[3:4], axis=-1, keepdims=True))
                        + (1.0 - diff_scale))
            o = (oa - lam_full * ob).T
            out = o * lax.rsqrt(jnp.mean(o * o, axis=-1, keepdims=True) + EPS) * subg_ref[...] * diff_scale
        o_ref[0] = out.astype(BF16)

    if ctx_keys:
        ks, vts = keys(slice(0, ctx_keys)), values(0, slice(0, ctx_keys))
        out = []
        for a in (0, 1):
            st = lax.dot_general(ks[a], qs[a], NT_DIMS, preferred_element_type=F32)
            out.append(update(st, jnp.max(st, axis=0, keepdims=True), vts[a], init[a]))
        finish(tuple(out))
        return

    (st_scr,) = scratch

    def chunk_keys(chunk_idx):
        start = chunk_idx * KV_CHUNK
        return keys(pl.ds(start if isinstance(start, int) else pl.multiple_of(start, KV_CHUNK), KV_CHUNK))

    def score(slot, a, ks):
        st = lax.dot_general(ks[a], qs[a], NT_DIMS, preferred_element_type=F32)
        st_scr[slot, a] = st
        return jnp.max(st, axis=0, keepdims=True)

    def scores(slot, chunk_idx):
        ks = chunk_keys(chunk_idx)
        return tuple(score(slot, a, ks) for a in (0, 1))

    def run(first, count, cmax, carry):
        for u in range(count):
            nxt, slot = first + u + 1, u % 2
            has_next = isinstance(nxt, jax.Array) or nxt < n_chunks
            m_new = [jnp.maximum(carry[a][0], cmax[a]) for a in (0, 1)]
            acc = [jnp.exp2(carry[a][0] - m_new[a]) * carry[a][1] for a in (0, 1)]
            cmax_next = [None, None]
            for j in range(KV_CHUNK // SUB_KEYS):
                rows = slice(j * SUB_KEYS, (j + 1) * SUB_KEYS)
                if has_next:
                    start = nxt * KV_CHUNK + j * SUB_KEYS
                    ks = keys(pl.ds(start if isinstance(start, int) else pl.multiple_of(start, SUB_KEYS), SUB_KEYS))
                    for a in (0, 1):
                        st = lax.dot_general(ks[a], qs[a], NT_DIMS, preferred_element_type=F32)
                        st_scr[1 - slot, a, rows, :] = st
                        mx = jnp.max(st, axis=0, keepdims=True)
                        cmax_next[a] = mx if cmax_next[a] is None else jnp.maximum(cmax_next[a], mx)
                vts = values(first + u, rows)
                for a in (0, 1):
                    p = jnp.exp2(st_scr[slot, a, rows, :] - m_new[a]).astype(BF16)
                    acc[a] = acc[a] + jnp.dot(vts[a], p, preferred_element_type=F32)
            carry = tuple((m_new[a], acc[a]) for a in (0, 1))
            cmax = tuple(cmax_next)
        return cmax, carry

    n_loop = (n_chunks - 1) // LOOP_CHUNKS
    cmax, carry = lax.fori_loop(0, n_loop, lambda i, st: run(i * LOOP_CHUNKS, LOOP_CHUNKS, *st),
                                (scores(0, 0), init))
    _, carry = run(n_loop * LOOP_CHUNKS, n_chunks - n_loop * LOOP_CHUNKS, cmax, carry)
    finish(carry)


def _flash(q, k, vt, lam, subg, *, mode, kw, k_cols, v_block_rows, v_rows, vw, k_index, v_index, n_ctx, diff_scale,
           name):
    b, s, nq = q.shape
    t = s - n_ctx
    n_pairs = nq // (2 * LANES)
    n_chunks = s // KV_CHUNK
    assert s % KV_CHUNK == 0 and n_ctx <= KV_CHUNK and t % FLASH_TQ == 0 and t % n_ctx == 0
    assert vt.shape[1] == n_chunks
    common = dict(mode=mode, k_cols=k_cols, v_rows=v_rows, vw=vw, diff_scale=diff_scale)
    o_x = pl.pallas_call(
        functools.partial(_flash_kernel, ctx_keys=0, **common),
        grid=(b, n_pairs, t // FLASH_TQ),
        in_specs=[
            pl.BlockSpec((1, FLASH_TQ, 2 * LANES), lambda bi, p, i: (bi, i, p)),
            pl.BlockSpec((1, s, kw), lambda bi, p, i: (bi, 0, k_index(p))),
            pl.BlockSpec((1, n_chunks, v_block_rows, KV_CHUNK), lambda bi, p, i: (bi, 0, v_index(p), 0)),
            _resident(lam.shape), _resident(subg.shape),
        ],
        out_specs=pl.BlockSpec((1, FLASH_TQ, LANES), lambda bi, p, i: (bi, i, p)),
        out_shape=jax.ShapeDtypeStruct((b, t, n_pairs * LANES), BF16),
        scratch_shapes=[pltpu.VMEM((2, 2, KV_CHUNK, FLASH_TQ), F32)],
        compiler_params=_cparams(3),
        name=name,
    )(q, k, vt, lam, subg)
    o_c = pl.pallas_call(
        functools.partial(_flash_kernel, ctx_keys=n_ctx, **common),
        grid=(b, n_pairs),
        in_specs=[
            pl.BlockSpec((1, n_ctx, 2 * LANES), lambda bi, p: (bi, t // n_ctx, p)),
            pl.BlockSpec((1, n_ctx, kw), lambda bi, p: (bi, 0, k_index(p))),
            pl.BlockSpec((1, 1, v_block_rows, KV_CHUNK), lambda bi, p: (bi, 0, v_index(p), 0)),
            _resident(lam.shape), _resident(subg.shape),
        ],
        out_specs=pl.BlockSpec((1, n_ctx, LANES), lambda bi, p: (bi, 0, p)),
        out_shape=jax.ShapeDtypeStruct((b, n_ctx, n_pairs * LANES), BF16),
        compiler_params=_cparams(2),
        name=name + "_ctx",
    )(q, k, vt, lam, subg)
    return o_x, o_c


def _na_tables(t):
    rows = t // GRID_W
    win_rows = min(NA_ROWS, rows)
    band_rows = min(win_rows + 1, rows)
    assert rows >= NA_WIN_ROWS and NA_QBLOCK % GRID_W == 0
    qrows = NA_QBLOCK // GRID_W
    nb = t // NA_QBLOCK
    nwin = NA_WIN_ROWS * GRID_W
    q_off, k_off = np.arange(NA_QBLOCK), np.arange(nwin)
    q_dr, q_col = q_off // GRID_W, q_off % GRID_W
    k_dr, k_col = k_off // GRID_W, k_off % GRID_W
    col_start = np.clip(q_col - NA_COLS // 2, 0, GRID_W - NA_COLS)
    col_in = (k_col[None, :] >= col_start[:, None]) & (k_col[None, :] < col_start[:, None] + NA_COLS)
    dc_idx = np.clip(k_col[None, :] - q_col[:, None] + NA_COLS - 1, 0, 2 * NA_COLS - 2)
    patterns, var, wstart = {}, [], []
    for j in range(nb):
        r = j * qrows + q_dr
        row_start = np.clip(r - win_rows // 2, 0, rows - win_rows)
        b0 = min(int(row_start[0]), rows - band_rows)
        w0 = min(b0 - b0 % 2, rows - NA_WIN_ROWS)
        assert w0 % qrows == 0 and w0 <= b0 and b0 + band_rows <= w0 + NA_WIN_ROWS
        k_row = w0 + k_dr
        in_band = (k_row >= b0) & (k_row < b0 + band_rows)
        in_win = (col_in & in_band[None, :] & (k_row[None, :] >= row_start[:, None])
                  & (k_row[None, :] < row_start[:, None] + win_rows))
        dr_idx = np.clip(k_row[None, :] - r[:, None] + NA_ROWS - 1, 0, 2 * NA_ROWS - 2)
        idx = np.where(in_win, dr_idx * (2 * NA_COLS - 1) + dc_idx, -1).astype(np.int32)
        key = idx.tobytes()
        if key not in patterns:
            patterns[key] = (len(patterns), idx)
        var.append(patterns[key][0])
        wstart.append(w0 * GRID_W)
    idx_all = np.stack([p[1] for p in sorted(patterns.values(), key=lambda p: p[0])])
    return idx_all.reshape(len(patterns), 1, -1), np.asarray(var, np.int32), np.asarray(wstart, np.int32)


def _na_bias_kernel(idx_ref, rpb_ref, o_ref):
    idx = idx_ref[0]
    r = rpb_ref[...]
    onehot = (lax.broadcasted_iota(jnp.int32, (r.shape[1], idx.shape[1]), 0) == idx).astype(BF16)
    b1 = r.astype(BF16)
    r2 = r - b1.astype(F32)
    b2 = r2.astype(BF16)
    b3 = (r2 - b2.astype(F32)).astype(BF16)
    out = (jnp.dot(b1, onehot, preferred_element_type=F32) + jnp.dot(b2, onehot, preferred_element_type=F32)
           + jnp.dot(b3, onehot, preferred_element_type=F32))
    o_ref[0] = jnp.where(idx >= 0, out, NEG_INF)


def _na_bias(idx_all, rpb):
    nvar, _, npos = idx_all.shape
    heads = rpb.shape[0]
    nr = (2 * NA_ROWS - 1) * (2 * NA_COLS - 1)
    nr_pad = pl.cdiv(nr, LANES) * LANES
    rflat = jnp.pad(rpb.reshape(heads, nr), ((0, 0), (0, nr_pad - nr)))
    tn = 2048
    assert npos % tn == 0
    out = pl.pallas_call(
        _na_bias_kernel,
        grid=(nvar, npos // tn),
        in_specs=[pl.BlockSpec((1, 1, tn), lambda v, j: (v, 0, j)), _resident(rflat.shape)],
        out_specs=pl.BlockSpec((1, heads, tn), lambda v, j: (v, 0, j)),
        out_shape=jax.ShapeDtypeStruct((nvar, heads, npos), F32),
        compiler_params=_cparams(2),
        name="na_bias",
    )(jnp.asarray(idx_all), rflat)
    return out.reshape(nvar, heads, NA_QBLOCK, npos // NA_QBLOCK)


def _halves(lo_src, hi_src):
    lane = lax.broadcasted_iota(jnp.int32, lo_src.shape, 1)
    return jnp.where(lane < LANES // 2, lo_src, hi_src)


def _na_kernel(var_ref, ws_ref, q_ref, k_ref, v_ref, b0_ref, b1_ref, o_ref, *, n_ctx, n_ctx_tiles, scale):
    qi = pl.program_id(2)
    nwin = NA_WIN_ROWS * GRID_W

    @pl.when(qi < n_ctx_tiles)
    def _():
        kc, vc = k_ref[0, 0:n_ctx, :], v_ref[0, 0:n_ctx, :]
        outs = []
        for a in (0, 1):
            s_c = lax.dot_general(q_ref[0, :, a * LANES:(a + 1) * LANES], kc, NT_DIMS,
                                  preferred_element_type=F32) * scale
            p_c = jnp.exp(s_c - jnp.max(s_c, axis=-1, keepdims=True))
            acc = jnp.dot(p_c.astype(BF16), vc, preferred_element_type=F32)
            outs.append(acc / jnp.sum(p_c, axis=-1, keepdims=True))
        o_ref[0] = _halves(outs[0], outs[1]).astype(BF16)

    @pl.when(qi >= n_ctx_tiles)
    def _():
        kc, vc = k_ref[0, 0:n_ctx, :], v_ref[0, 0:n_ctx, :]
        work = []
        for sub, bias_ref in ((0, b0_ref), (1, b1_ref)):
            j = (qi - n_ctx_tiles) * (TOKEN_TILE // NA_QBLOCK) + sub
            start = pl.multiple_of(n_ctx + ws_ref[j], NA_QBLOCK)
            kw, vw = k_ref[0, pl.ds(start, nwin), :], v_ref[0, pl.ds(start, nwin), :]
            for a in (0, 1):
                q = q_ref[0, sub * NA_QBLOCK:(sub + 1) * NA_QBLOCK, a * LANES:(a + 1) * LANES]
                s_c = lax.dot_general(q, kc, NT_DIMS, preferred_element_type=F32) * scale
                s_w = lax.dot_general(q, kw, NT_DIMS, preferred_element_type=F32) * scale + bias_ref[0, a]
                work.append((s_c, s_w, vw))
        outs = []
        for s_c, s_w, vw in work:
            m = jnp.maximum(jnp.max(s_c, axis=-1, keepdims=True), jnp.max(s_w, axis=-1, keepdims=True))
            p_c, p_w = jnp.exp(s_c - m), jnp.exp(s_w - m)
            l = jnp.sum(p_c, axis=-1, keepdims=True) + jnp.sum(p_w, axis=-1, keepdims=True)
            acc = (jnp.dot(p_c.astype(BF16), vc, preferred_element_type=F32)
                   + jnp.dot(p_w.astype(BF16), vw, preferred_element_type=F32))
            outs.append(acc / l)
        for sub in (0, 1):
            o_ref[0, sub * NA_QBLOCK:(sub + 1) * NA_QBLOCK, :] = _halves(outs[2 * sub], outs[2 * sub + 1]).astype(BF16)


def _na_attention(q, k, v, bias, var, wstart, *, n_ctx, scale):
    b, s, nq = q.shape
    tq = TOKEN_TILE
    n_pairs = nq // (2 * LANES)
    n_ctx_tiles = n_ctx // tq
    per_tile = tq // NA_QBLOCK
    nwin = NA_WIN_ROWS * GRID_W

    def bias_spec(sub):
        def index(bi, p, i, var_ref, ws_ref):
            j = jnp.maximum(i - n_ctx_tiles, 0) * per_tile + sub
            return (var_ref[j], p, 0, 0)
        return pl.BlockSpec((1, 2, NA_QBLOCK, nwin), index)

    grid_spec = pltpu.PrefetchScalarGridSpec(
        num_scalar_prefetch=2,
        grid=(b, n_pairs, s // tq),
        in_specs=[
            pl.BlockSpec((1, tq, 2 * LANES), lambda bi, p, i, *_: (bi, i, p)),
            pl.BlockSpec((1, s, LANES), lambda bi, p, i, *_: (bi, 0, p)),
            pl.BlockSpec((1, s, LANES), lambda bi, p, i, *_: (bi, 0, p)),
            bias_spec(0), bias_spec(1),
        ],
        out_specs=pl.BlockSpec((1, tq, LANES), lambda bi, p, i, *_: (bi, i, p)),
    )
    return pl.pallas_call(
        functools.partial(_na_kernel, n_ctx=n_ctx, n_ctx_tiles=n_ctx_tiles, scale=scale),
        grid_spec=grid_spec,
        out_shape=jax.ShapeDtypeStruct((b, s, n_pairs * LANES), BF16),
        compiler_params=_cparams(3),
        name="na_attn",
    )(var, wstart, q, k, v, bias, bias)


def _outproj_kernel(x_ref, ox_ref, oc_ref, w_ref, mod_ref, y_ref, *, n_ctx_tiles):
    o = jnp.where(pl.program_id(1) < n_ctx_tiles, oc_ref[0], ox_ref[0])
    y = jnp.dot(o, w_ref[...], preferred_element_type=F32)
    y_ref[0] = x_ref[0] + mod_ref[0, 2:3, :] * y


def _outproj(xs, o_x, o_c, w, mods, *, n_ctx_tiles, x_offset):
    b, s, d = xs.shape
    tm = TOKEN_TILE
    n = o_x.shape[2]
    return pl.pallas_call(
        functools.partial(_outproj_kernel, n_ctx_tiles=n_ctx_tiles),
        grid=(b, s // tm),
        in_specs=[
            pl.BlockSpec((1, tm, d), lambda bi, i: (bi, i, 0)),
            pl.BlockSpec((1, tm, n), lambda bi, i: (bi, jnp.maximum(i, n_ctx_tiles) - x_offset, 0)),
            pl.BlockSpec((1, tm, n), lambda bi, i: (bi, jnp.minimum(i, n_ctx_tiles - 1), 0)),
            _resident(w.shape),
            pl.BlockSpec((1, 6, d), lambda bi, i: (jnp.where(i < n_ctx_tiles, b, bi), 0, 0)),
        ],
        out_specs=pl.BlockSpec((1, tm, d), lambda bi, i: (bi, i, 0)),
        out_shape=jax.ShapeDtypeStruct((b, s, d), F32),
        compiler_params=_cparams(2),
        name="outproj",
    )(xs, o_x, o_c, w, mods)


def _ffn_kernel(xp_ref, x_ref, xn_ref, mod_ref, g_ref, wup_ref, cw_ref, cb_ref, wdn_ref, y_ref, u_scr, a_scr,
                *, n_ctx_tiles, n_tiles, chunk):
    i = pl.program_id(1)
    tm = x_ref.shape[1]
    has_prev = jnp.logical_and(i != 0, i != n_ctx_tiles).astype(F32)
    has_next = jnp.logical_and(i != n_ctx_tiles - 1, i != n_tiles - 1).astype(F32)
    g, shift, scale = g_ref[...], mod_ref[0, 3:4, :], mod_ref[0, 4:5, :]
    x = x_ref[0]
    h = jnp.concatenate([_norm_mod(xp_ref[0], g, shift, scale) * has_prev,
                         _norm_mod(x, g, shift, scale),
                         _norm_mod(xn_ref[0], g, shift, scale) * has_next], axis=0).astype(BF16)
    n_chunks = wup_ref.shape[1] // (2 * chunk)
    for j in range(n_chunks):
        cols = slice(j * 2 * chunk, (j + 1) * 2 * chunk)
        u_scr[...] = jnp.dot(h, wup_ref[:, cols], preferred_element_type=F32)
        cw = cw_ref[:, cols]
        u = (u_scr[HALO - 1:HALO - 1 + tm, :] * cw[0:1] + u_scr[HALO:HALO + tm, :] * cw[1:2]
             + u_scr[HALO + 1:HALO + 1 + tm, :] * cw[2:3] + cb_ref[:, cols])
        val, gate = u[:, :chunk], u[:, chunk:]
        a_scr[:, j * chunk:(j + 1) * chunk] = (gate * (1.0 / (1.0 + jnp.exp(-gate))) * val).astype(BF16)
    y = jnp.dot(a_scr[...], wdn_ref[...], preferred_element_type=F32)
    y_ref[0] = x + mod_ref[0, 5:6, :] * y


def _ffn(xs, mods, g, wup, cw, cb, wdn, *, n_ctx_tiles, chunk):
    b, s, d = xs.shape
    tm = TOKEN_TILE
    n_tiles = s // tm
    per = tm // HALO
    f = wdn.shape[0]
    return pl.pallas_call(
        functools.partial(_ffn_kernel, n_ctx_tiles=n_ctx_tiles, n_tiles=n_tiles, chunk=chunk),
        grid=(b, n_tiles),
        in_specs=[
            pl.BlockSpec((1, HALO, d), lambda bi, i: (bi, jnp.maximum(i * per - 1, 0), 0)),
            pl.BlockSpec((1, tm, d), lambda bi, i: (bi, i, 0)),
            pl.BlockSpec((1, HALO, d), lambda bi, i: (bi, jnp.minimum((i + 1) * per, s // HALO - 1), 0)),
            pl.BlockSpec((1, 6, d), lambda bi, i: (jnp.where(i < n_ctx_tiles, b, bi), 0, 0)),
            _resident((1, d)),
            pl.BlockSpec(wup.shape, lambda bi, i: (0, 0), pipeline_mode=pl.Buffered(1)),
            _resident(cw.shape), _resident(cb.shape),
            pl.BlockSpec(wdn.shape, lambda bi, i: (0, 0), pipeline_mode=pl.Buffered(1)),
        ],
        out_specs=pl.BlockSpec((1, tm, d), lambda bi, i: (bi, i, 0)),
        out_shape=jax.ShapeDtypeStruct((b, s, d), F32),
        scratch_shapes=[pltpu.VMEM((tm + 2 * HALO, 2 * chunk), F32), pltpu.VMEM((tm, f), BF16)],
        compiler_params=_cparams(2),
        name="ffn",
    )(xs, xs, xs, mods, g, wup, cw, cb, wdn)


def _final_kernel(x_ref, g_ref, o_ref):
    x = x_ref[0]
    o_ref[0] = x * lax.rsqrt(jnp.mean(x * x, axis=-1, keepdims=True) + EPS) * g_ref[...]


def _final_norm(xs, g, *, n_ctx_tiles):
    b, s, d = xs.shape
    tm = TOKEN_TILE
    nt = s // tm - n_ctx_tiles
    return pl.pallas_call(
        _final_kernel,
        grid=(b, nt),
        in_specs=[pl.BlockSpec((1, tm, d), lambda bi, i: (bi, i + n_ctx_tiles, 0)), _resident((1, d))],
        out_specs=pl.BlockSpec((1, tm, d), lambda bi, i: (bi, i, 0)),
        out_shape=jax.ShapeDtypeStruct((b, nt * tm, d), F32),
        compiler_params=_cparams(2),
        name="final_norm",
    )(xs, g)


def _rope_table(t, n_ctx, rot_dim, lane_lo):
    tt = jnp.arange(t)
    row = (tt // GRID_W).astype(F32)
    col = (tt % GRID_W).astype(F32)
    half = rot_dim // 2
    inv = ROPE_BASE ** (-jnp.arange(0, half, 2, dtype=F32) / half)
    ar, ac = row[:, None] * inv, col[:, None] * inv
    ang = jnp.concatenate([ar, ar, ac, ac], axis=-1)
    quarter = rot_dim // 4
    sign = jnp.where((jnp.arange(rot_dim) % (2 * quarter)) < quarter, -1.0, 1.0).astype(F32)
    cos, sin = jnp.cos(ang), jnp.sin(ang) * sign
    reps = (LANES - lane_lo) // rot_dim if lane_lo == 0 else 1
    cos_l = jnp.ones((t, LANES), F32).at[:, lane_lo:lane_lo + reps * rot_dim].set(jnp.tile(cos, (1, reps)))
    sin_l = jnp.zeros((t, LANES), F32).at[:, lane_lo:lane_lo + reps * rot_dim].set(jnp.tile(sin, (1, reps)))
    cos_s = jnp.concatenate([jnp.ones((n_ctx, LANES), F32), cos_l], axis=0)
    sin_s = jnp.concatenate([jnp.zeros((n_ctx, LANES), F32), sin_l], axis=0)
    return cos_s, sin_s


def _to_slots(w, n_heads, width, halves):
    d = w.shape[0]
    wh = w.reshape(d, n_heads, width)
    lo = jnp.pad(wh, ((0, 0), (0, 0), (0, LANES - width)))
    if not any(halves):
        return lo.reshape(d, n_heads * LANES)
    hi = jnp.pad(wh, ((0, 0), (0, 0), (LANES // 2, LANES // 2 - width)))
    in_hi = jnp.asarray(halves, jnp.int32).reshape(1, n_heads, 1) == 1
    return jnp.where(in_hi, hi, lo).reshape(d, n_heads * LANES)


def _vt_weights(wv, groups, width, tm):
    d = wv.shape[0]
    rows = width + 16
    wt = jnp.zeros((groups, rows, d), wv.dtype).at[:, :width, :].set(wv.T.reshape(groups, width, d))
    ones = jnp.zeros((groups, rows, tm), F32).at[:, width, :].set(1.0)
    return wt.reshape(groups * rows, d), ones.reshape(groups * rows, tm)


def _dup(g):
    return jnp.concatenate([g, g]).reshape(1, LANES).astype(F32)


def kernel(x, c, ctx, c_ctx, ada_w, ada_b, norm1_g, norm2_g, ffn_w_up, ffn_conv_w, ffn_conv_b, ffn_w_down,
           gqa_w_in, gqa_q_norm_g, gqa_k_norm_g, gqa_w_out,
           mla_w_in, mla_q_norm_g, mla_kv_norm_g, mla_w_uq, mla_w_ukv, mla_w_out,
           diff_w_in, diff_lambda, diff_subln_g, diff_w_out,
           na_w_in, na_rpb, na_w_out, final_norm_g):
    b, t, d = x.shape
    n_ctx = ctx.shape[1]
    depth = ada_w.shape[0]
    assert n_ctx % TOKEN_TILE == 0 and t % TOKEN_TILE == 0 and b + 1 <= 8
    nct = n_ctx // TOKEN_TILE
    f = ffn_w_down.shape[1]
    chunk = 2 * LANES
    assert f % chunk == 0

    xs = jnp.concatenate([ctx, x], axis=1)
    cin = jnp.zeros((8, d), F32).at[:b].set(c).at[b].set(c_ctx)
    mods_all = _ada_all(cin, ada_w, ada_b).reshape(depth, 8, 6, d)

    tm = TOKEN_TILE
    ones_l = jnp.ones((1, LANES), F32)
    dummy_tab = jnp.zeros((xs.shape[1], LANES), F32)
    dummy_ones = jnp.zeros((8, LANES), F32)
    cos64, sin64 = _rope_table(t, n_ctx, GQA_HEAD_DIM, 0)
    cos_mla, sin_mla = _rope_table(t, n_ctx, MLA_ROPE_DIM, MLA_NOPE_DIM)
    no_lam, no_g = jnp.zeros((4, DIFF_HEAD_DIM), F32), jnp.ones((1, LANES), F32)
    pair_rows = MLA_V_DIM + 16
    diff_rows = 2 * DIFF_HEAD_DIM + 16

    for i in range(depth):
        m, j = i % N_MIXERS, i // N_MIXERS
        mods = mods_all[i]
        g1 = norm1_g[i].reshape(1, d)
        if m == 0:
            w = gqa_w_in[j].astype(BF16)
            nqc, nkc = GQA_HEADS * GQA_HEAD_DIM, GQA_KV_HEADS * GQA_HEAD_DIM
            wq = _to_slots(w[:, :nqc], GQA_HEADS, GQA_HEAD_DIM, [0] * GQA_HEADS)
            wk = _to_slots(w[:, nqc:nqc + nkc], GQA_KV_HEADS, GQA_HEAD_DIM, [0] * GQA_KV_HEADS)
            wvt, vones = _vt_weights(w[:, nqc + nkc:], GQA_KV_HEADS, GQA_HEAD_DIM, tm)
            qc = GQA_HEAD_DIM ** -0.5 * LOG2E
            q, k, vt = _project(xs, mods, g1, wq, wk, wvt, _dup(gqa_q_norm_g[j]), _dup(gqa_k_norm_g[j]),
                                cos64 * qc, sin64 * qc, cos64, sin64, vones, qk_norm=True, rope=True,
                                head_dim=GQA_HEAD_DIM, v_transposed=True, q_rotated=True, n_ctx_tiles=nct,
                                name="proj_gqa")
            o_x, o_c = _flash(q, k, vt, no_lam, no_g, mode="pair", kw=LANES, k_cols=((0, LANES), (0, LANES)),
                       v_block_rows=pair_rows, v_rows=((0, pair_rows), (0, pair_rows)), vw=GQA_HEAD_DIM,
                       k_index=lambda p: p // 2, v_index=lambda p: p // 2, n_ctx=n_ctx, diff_scale=1.0,
                       name="attn_gqa")
            w_out = gqa_w_out[j]
        elif m == 1:
            w = mla_w_in[j].astype(BF16)
            nlat = MLA_Q_LORA + MLA_KV_LORA
            kr_slot = jnp.zeros((d, LANES), BF16).at[:, MLA_NOPE_DIM:MLA_NOPE_DIM + MLA_ROPE_DIM].set(w[:, nlat:])
            win = jnp.concatenate([w[:, :nlat], kr_slot], axis=1)
            qk_dim = MLA_NOPE_DIM + MLA_ROPE_DIM
            wuq = _to_slots(mla_w_uq[j].astype(BF16), MLA_HEADS, qk_dim, [0] * MLA_HEADS)
            wukv = mla_w_ukv[j].astype(BF16).reshape(MLA_KV_LORA, MLA_HEADS, MLA_NOPE_DIM + MLA_V_DIM)
            wuk = _to_slots(wukv[:, :, :MLA_NOPE_DIM].reshape(MLA_KV_LORA, -1), MLA_HEADS, MLA_NOPE_DIM,
                            [0] * MLA_HEADS)
            wuvt, vones = _vt_weights(wukv[:, :, MLA_NOPE_DIM:].reshape(MLA_KV_LORA, MLA_HEADS * MLA_V_DIM),
                                      MLA_HEADS, MLA_V_DIM, tm)
            qc = qk_dim ** -0.5 * LOG2E
            q, k, vt = _project_mla(xs, mods, g1, win, mla_q_norm_g[j].reshape(1, -1),
                                    mla_kv_norm_g[j].reshape(1, -1), wuq, wuk, wuvt,
                                    cos_mla * qc, sin_mla * qc, cos_mla, sin_mla, vones, n_ctx_tiles=nct)
            o_x, o_c = _flash(q, k, vt, no_lam, no_g, mode="pair", kw=2 * LANES, k_cols=((0, LANES), (LANES, 2 * LANES)),
                       v_block_rows=2 * pair_rows, v_rows=((0, pair_rows), (pair_rows, 2 * pair_rows)),
                       vw=MLA_V_DIM, k_index=lambda p: p, v_index=lambda p: p, n_ctx=n_ctx, diff_scale=1.0,
                       name="attn_mla")
            w_out = mla_w_out[j]
        elif m == 2:
            w = diff_w_in[j].astype(BF16)
            wq = _to_slots(w[:, :d], 2 * DIFF_HEADS, DIFF_HEAD_DIM, [0, 1] * DIFF_HEADS)
            wvt, vones = _vt_weights(w[:, 2 * d:], DIFF_HEADS, 2 * DIFF_HEAD_DIM, tm)
            lambda_init = 0.8 - 0.6 * math.exp(-0.3 * i)
            qc = DIFF_HEAD_DIM ** -0.5 * LOG2E
            q, k, vt = _project(xs, mods, g1, wq, w[:, d:2 * d], wvt, ones_l, ones_l,
                                cos64 * qc, sin64 * qc, cos64, sin64, vones, qk_norm=False, rope=True,
                                head_dim=DIFF_HEAD_DIM, v_transposed=True, q_rotated=True, n_ctx_tiles=nct,
                                name="proj_diff")
            o_x, o_c = _flash(q, k, vt, diff_lambda[j], diff_subln_g[j].reshape(1, LANES), mode="diff", kw=LANES,
                       k_cols=((0, LANES), (0, LANES)), v_block_rows=diff_rows,
                       v_rows=((0, diff_rows), (0, diff_rows)), vw=2 * DIFF_HEAD_DIM,
                       k_index=lambda p: p, v_index=lambda p: p, n_ctx=n_ctx, diff_scale=1.0 - lambda_init,
                       name="attn_diff")
            w_out = diff_w_out[j]
        else:
            w = na_w_in[j].astype(BF16)
            wq = _to_slots(w[:, :d], NA_HEADS, NA_HEAD_DIM, [0, 1] * (NA_HEADS // 2))
            q, k, v = _project(xs, mods, g1, wq, w[:, d:2 * d], w[:, 2 * d:], ones_l, ones_l,
                               dummy_tab, dummy_tab, dummy_tab, dummy_tab, dummy_ones, qk_norm=False, rope=False,
                               head_dim=NA_HEAD_DIM, v_transposed=False, q_rotated=False, n_ctx_tiles=nct,
                               name="proj_na")
            idx_all, var, wstart = _na_tables(t)
            bias = _na_bias(idx_all, na_rpb[j])
            o_x = o_c = _na_attention(q, k, v, bias, jnp.asarray(var), jnp.asarray(wstart), n_ctx=n_ctx,
                                      scale=NA_HEAD_DIM ** -0.5)
            w_out = na_w_out[j]
        xs = _outproj(xs, o_x, o_c, w_out.astype(BF16), mods, n_ctx_tiles=nct, x_offset=0 if m == 3 else nct)

        def regroup(a):
            lead = a.shape[:-1]
            return jnp.stack([a[..., :f].reshape(*lead, f // chunk, chunk),
                              a[..., f:].reshape(*lead, f // chunk, chunk)], axis=-2).reshape(*lead, 2 * f)
        xs = _ffn(xs, mods, norm2_g[i].reshape(1, d), regroup(ffn_w_up[i].astype(BF16)), regroup(ffn_conv_w[i]),
                  regroup(ffn_conv_b[i]).reshape(1, 2 * f), ffn_w_down[i].astype(BF16), n_ctx_tiles=nct, chunk=chunk)

    return _final_norm(xs, final_norm_g.reshape(1, d), n_ctx_tiles=nct)
```

```python
import functools
import math

import numpy as np
import jax
import jax.numpy as jnp
from jax import lax
from jax.experimental import pallas as pl
from jax.experimental.pallas import tpu as pltpu

F32 = jnp.float32
BF16 = jnp.bfloat16

LANES = 128
TOKEN_TILE = 256
HALO = 8
KV_CHUNK = 768
SUB_KEYS = 256
FLASH_TQ = 512
LOOP_CHUNKS = 4
VMEM_LIMIT = 56 * 1024 * 1024

GRID_W = 64
ROPE_BASE = 10000.0
EPS = 1e-6
NEG_INF = -1e30
LOG2E = 1.4426950408889634

GQA_HEADS, GQA_KV_HEADS, GQA_HEAD_DIM = 16, 4, 64
MLA_HEADS, MLA_NOPE_DIM, MLA_ROPE_DIM, MLA_V_DIM = 16, 64, 32, 64
MLA_Q_LORA, MLA_KV_LORA = 384, 256
DIFF_HEADS, DIFF_HEAD_DIM = 8, 64
NA_HEADS, NA_HEAD_DIM, NA_ROWS, NA_COLS = 16, 64, 8, 16
NA_QBLOCK = 128
NA_WIN_ROWS = 10
N_MIXERS = 4

NT_DIMS = (((1,), (1,)), ((), ()))


def _cparams(n_axes):
    return pltpu.CompilerParams(dimension_semantics=("arbitrary",) * n_axes, vmem_limit_bytes=VMEM_LIMIT)


def _resident(shape):
    nd = len(shape)
    return pl.BlockSpec(shape, lambda *_: (0,) * nd)


def _ada_kernel(c_ref, w_ref, b_ref, o_ref):
    c = c_ref[...]
    s = c * (1.0 / (1.0 + jnp.exp(-c)))
    o_ref[0] = jnp.dot(s.astype(BF16), w_ref[0].astype(BF16), preferred_element_type=F32) + b_ref[0]


def _ada_all(cin, ada_w, ada_b):
    depth, d, n = ada_w.shape
    tn = n // 4
    return pl.pallas_call(
        _ada_kernel,
        grid=(depth, n // tn),
        in_specs=[
            pl.BlockSpec((8, d), lambda l, j: (0, 0)),
            pl.BlockSpec((1, d, tn), lambda l, j: (l, 0, j)),
            pl.BlockSpec((1, 1, tn), lambda l, j: (l, 0, j)),
        ],
        out_specs=pl.BlockSpec((1, 8, tn), lambda l, j: (l, 0, j)),
        out_shape=jax.ShapeDtypeStruct((depth, 8, n), F32),
        compiler_params=_cparams(2),
        name="ada",
    )(cin, ada_w, ada_b.reshape(depth, 1, n))


def _norm_mod(x, g, shift, scale):
    ms = jnp.mean(x * x, axis=-1, keepdims=True)
    return (x * lax.rsqrt(ms + EPS) * g) * (1.0 + scale) + shift


def _rope_slot(y, cos, sin_signed, lo_mask, quarter):
    fwd = pltpu.roll(y, LANES - quarter, axis=1)
    bwd = pltpu.roll(y, quarter, axis=1)
    return y * cos + jnp.where(lo_mask, fwd, bwd) * sin_signed


def _lo_mask(shape, quarter):
    lane = lax.broadcasted_iota(jnp.int32, shape, 1)
    return (lane % (2 * quarter)) < quarter


def _q_spec(n, tm, n_tiles, n_ctx_tiles, rotate):
    if not rotate:
        return pl.BlockSpec((1, tm, n), lambda bi, i: (bi, i, 0))
    n_lat = n_tiles - n_ctx_tiles
    return pl.BlockSpec((1, tm, n), lambda bi, i: (bi, jnp.where(i < n_ctx_tiles, i + n_lat, i - n_ctx_tiles), 0))


def _vt_spec(rows, tm):
    per = KV_CHUNK // tm
    return pl.BlockSpec((1, 1, rows, tm), lambda bi, i: (bi, i // per, 0, i % per))


def _proj_kernel(x_ref, mod_ref, g_ref, wq_ref, wk_ref, wv_ref, qg_ref, kg_ref, cosq_ref, sinq_ref, cosk_ref, sink_ref,
                 vones_ref, q_ref, k_ref, v_ref, *, qk_norm, rope, head_dim, v_transposed):
    h = _norm_mod(x_ref[0], g_ref[...], mod_ref[0, 0:1, :], mod_ref[0, 1:2, :]).astype(BF16)
    tm = h.shape[0]
    lo = _lo_mask((tm, LANES), head_dim // 4) if rope else None

    def finish(w_ref, o_ref, gain_ref, cos_ref, sin_ref):
        n = w_ref.shape[1]
        for c0 in range(0, n, 2 * LANES):
            c1 = min(c0 + 2 * LANES, n)
            y2 = jnp.dot(h, w_ref[:, c0:c1], preferred_element_type=F32)
            for s0 in range(0, c1 - c0, LANES):
                y = y2[:, s0:s0 + LANES]
                if qk_norm:
                    ms = jnp.sum(y * y, axis=-1, keepdims=True) * (1.0 / head_dim)
                    y = y * lax.rsqrt(ms + EPS) * gain_ref[...]
                if rope:
                    y = _rope_slot(y, cos_ref[...], sin_ref[...], lo, head_dim // 4)
                o_ref[0, :, c0 + s0:c0 + s0 + LANES] = y.astype(BF16)

    finish(wq_ref, q_ref, qg_ref, cosq_ref, sinq_ref)
    finish(wk_ref, k_ref, kg_ref, cosk_ref, sink_ref)
    if v_transposed:
        vt = lax.dot_general(wv_ref[...], h, NT_DIMS, preferred_element_type=F32) + vones_ref[...]
        v_ref[0, 0] = vt.astype(BF16)
    else:
        v_ref[0] = jnp.dot(h, wv_ref[...], preferred_element_type=F32).astype(BF16)


def _project(xs, mods, g, wq, wk, wv, qg, kg, cosq, sinq, cosk, sink, vones, *, qk_norm, rope, head_dim, v_transposed,
             q_rotated, n_ctx_tiles, name):
    b, s, d = xs.shape
    tm = TOKEN_TILE
    nq, nk = wq.shape[1], wk.shape[1]
    tok = lambda n: pl.BlockSpec((1, tm, n), lambda bi, i: (bi, i, 0))
    tab = pl.BlockSpec((tm, LANES), lambda bi, i: (i, 0))
    if v_transposed:
        v_spec = _vt_spec(wv.shape[0], tm)
        v_shape = jax.ShapeDtypeStruct((b, s // KV_CHUNK, wv.shape[0], KV_CHUNK), BF16)
    else:
        v_spec, v_shape = tok(wv.shape[1]), jax.ShapeDtypeStruct((b, s, wv.shape[1]), BF16)
    return pl.pallas_call(
        functools.partial(_proj_kernel, qk_norm=qk_norm, rope=rope, head_dim=head_dim, v_transposed=v_transposed),
        grid=(b, s // tm),
        in_specs=[
            tok(d),
            pl.BlockSpec((1, 6, d), lambda bi, i: (jnp.where(i < n_ctx_tiles, b, bi), 0, 0)),
            _resident((1, d)),
            _resident(wq.shape), _resident(wk.shape), _resident(wv.shape),
            _resident((1, LANES)), _resident((1, LANES)),
            tab, tab, tab, tab,
            _resident(vones.shape),
        ],
        out_specs=[_q_spec(nq, tm, s // tm, n_ctx_tiles, q_rotated), tok(nk), v_spec],
        out_shape=[jax.ShapeDtypeStruct((b, s, nq), BF16), jax.ShapeDtypeStruct((b, s, nk), BF16), v_shape],
        compiler_params=_cparams(2),
        name=name,
    )(xs, mods, g, wq, wk, wv, qg, kg, cosq, sinq, cosk, sink, vones)


def _mla_proj_kernel(x_ref, mod_ref, g_ref, win_ref, qn_ref, kvn_ref, wuq_ref, wuk_ref, wuvt_ref,
                     cosq_ref, sinq_ref, cosk_ref, sink_ref, vones_ref, q_ref, k_ref, v_ref):
    h = _norm_mod(x_ref[0], g_ref[...], mod_ref[0, 0:1, :], mod_ref[0, 1:2, :]).astype(BF16)
    tm = h.shape[0]
    t1 = jnp.dot(h, win_ref[...], preferred_element_type=F32)
    cq = t1[:, :MLA_Q_LORA]
    ckv = t1[:, MLA_Q_LORA:MLA_Q_LORA + MLA_KV_LORA]
    kr = t1[:, MLA_Q_LORA + MLA_KV_LORA:]
    cqn = (cq * lax.rsqrt(jnp.mean(cq * cq, axis=-1, keepdims=True) + EPS) * qn_ref[...]).astype(BF16)
    ckvn = (ckv * lax.rsqrt(jnp.mean(ckv * ckv, axis=-1, keepdims=True) + EPS) * kvn_ref[...]).astype(BF16)
    quarter = MLA_ROPE_DIM // 4
    lo = _lo_mask((tm, LANES), quarter)
    kr = _rope_slot(kr, cosk_ref[...], sink_ref[...], lo, quarter)
    n = wuq_ref.shape[1]
    for c0 in range(0, n, 2 * LANES):
        q2 = jnp.dot(cqn, wuq_ref[:, c0:c0 + 2 * LANES], preferred_element_type=F32)
        k2 = jnp.dot(ckvn, wuk_ref[:, c0:c0 + 2 * LANES], preferred_element_type=F32)
        for s0 in (0, LANES):
            q_ref[0, :, c0 + s0:c0 + s0 + LANES] = _rope_slot(
                q2[:, s0:s0 + LANES], cosq_ref[...], sinq_ref[...], lo, quarter).astype(BF16)
            k_ref[0, :, c0 + s0:c0 + s0 + LANES] = (k2[:, s0:s0 + LANES] + kr).astype(BF16)
    vt = lax.dot_general(wuvt_ref[...], ckvn, NT_DIMS, preferred_element_type=F32) + vones_ref[...]
    v_ref[0, 0] = vt.astype(BF16)


def _project_mla(xs, mods, g, win, qn, kvn, wuq, wuk, wuvt, cosq, sinq, cosk, sink, vones, *, n_ctx_tiles):
    b, s, d = xs.shape
    tm = TOKEN_TILE
    nq, nk = wuq.shape[1], wuk.shape[1]
    tok = lambda n: pl.BlockSpec((1, tm, n), lambda bi, i: (bi, i, 0))
    tab = pl.BlockSpec((tm, LANES), lambda bi, i: (i, 0))
    return pl.pallas_call(
        _mla_proj_kernel,
        grid=(b, s // tm),
        in_specs=[
            tok(d),
            pl.BlockSpec((1, 6, d), lambda bi, i: (jnp.where(i < n_ctx_tiles, b, bi), 0, 0)),
            _resident((1, d)),
            _resident(win.shape), _resident(qn.shape), _resident(kvn.shape),
            _resident(wuq.shape), _resident(wuk.shape), _resident(wuvt.shape),
            tab, tab, tab, tab,
            _resident(vones.shape),
        ],
        out_specs=[_q_spec(nq, tm, s // tm, n_ctx_tiles, True), tok(nk), _vt_spec(wuvt.shape[0], tm)],
        out_shape=[jax.ShapeDtypeStruct((b, s, nq), BF16), jax.ShapeDtypeStruct((b, s, nk), BF16),
                   jax.ShapeDtypeStruct((b, s // KV_CHUNK, wuvt.shape[0], KV_CHUNK), BF16)],
        compiler_params=_cparams(2),
        name="proj_mla",
    )(xs, mods, g, win, qn, kvn, wuq, wuk, wuvt, cosq, sinq, cosk, sink, vones)


def _flash_kernel(q_ref, k_ref, vt_ref, lam_ref, subg_ref, o_ref, *scratch, mode, k_cols, v_rows, vw, diff_scale,
                  ctx_keys):
    tq = q_ref.shape[1]
    n_chunks = vt_ref.shape[1]
    qs = [q_ref[0, :, a * LANES:(a + 1) * LANES] for a in (0, 1)]
    n_rows = v_rows[0][1] - v_rows[0][0]

    def keys(key_rows):
        k0 = k_ref[0, key_rows, k_cols[0][0]:k_cols[0][1]]
        k1 = k0 if k_cols[1] == k_cols[0] else k_ref[0, key_rows, k_cols[1][0]:k_cols[1][1]]
        return k0, k1

    def values(chunk_idx, key_lanes):
        v0 = vt_ref[0, chunk_idx, v_rows[0][0]:v_rows[0][1], key_lanes]
        v1 = v0 if v_rows[1] == v_rows[0] else vt_ref[0, chunk_idx, v_rows[1][0]:v_rows[1][1], key_lanes]
        return v0, v1

    def update(st, cmax, vt_a, carry_a):
        m, acc = carry_a
        m_new = jnp.maximum(m, cmax)
        p = jnp.exp2(st - m_new).astype(BF16)
        return m_new, jnp.exp2(m - m_new) * acc + jnp.dot(vt_a, p, preferred_element_type=F32)

    init = tuple((jnp.full((1, tq), -jnp.inf, F32), jnp.zeros((n_rows, tq), F32)) for _ in (0, 1))

    def finish(carry):
        (_, acc_a), (_, acc_b) = carry
        oa = acc_a[0:vw] / acc_a[vw:vw + 1]
        ob = acc_b[0:vw] / acc_b[vw:vw + 1]
        if mode == "pair":
            out = jnp.concatenate([oa, ob], axis=0).T
        else:
            lam = lam_ref[...]
            lam_full = (jnp.exp(jnp.sum(lam[0:1] * lam[1:2], axis=-1, keepdims=True))
                        - jnp.exp(jnp.sum(lam[2:3] * lam[3:4], axis=-1, keepdims=True))
                        + (1.0 - diff_scale))
            o = (oa - lam_full * ob).T
            out = o * lax.rsqrt(jnp.mean(o * o, axis=-1, keepdims=True) + EPS) * subg_ref[...] * diff_scale
        o_ref[0] = out.astype(BF16)

    if ctx_keys:
        ks, vts = keys(slice(0, ctx_keys)), values(0, slice(0, ctx_keys))
        out = []
        for a in (0, 1):
            st = lax.dot_general(ks[a], qs[a], NT_DIMS, preferred_element_type=F32)
            out.append(update(st, jnp.max(st, axis=0, keepdims=True), vts[a], init[a]))
        finish(tuple(out))
        return

    (st_scr,) = scratch

    def chunk_keys(chunk_idx):
        start = chunk_idx * KV_CHUNK
        return keys(pl.ds(start if isinstance(start, int) else pl.multiple_of(start, KV_CHUNK), KV_CHUNK))

    def score(slot, a, ks):
        st = lax.dot_general(ks[a], qs[a], NT_DIMS, preferred_element_type=F32)
        st_scr[slot, a] = st
        return jnp.max(st, axis=0, keepdims=True)

    def scores(slot, chunk_idx):
        ks = chunk_keys(chunk_idx)
        return tuple(score(slot, a, ks) for a in (0, 1))

    def run(first, count, cmax, carry):
        for u in range(count):
            nxt, slot = first + u + 1, u % 2
            has_next = isinstance(nxt, jax.Array) or nxt < n_chunks
            m_new = [jnp.maximum(carry[a][0], cmax[a]) for a in (0, 1)]
            acc = [jnp.exp2(carry[a][0] - m_new[a]) * carry[a][1] for a in (0, 1)]
            cmax_next = [None, None]
            for j in range(KV_CHUNK // SUB_KEYS):
                rows = slice(j * SUB_KEYS, (j + 1) * SUB_KEYS)
                if has_next:
                    start = nxt * KV_CHUNK + j * SUB_KEYS
                    ks = keys(pl.ds(start if isinstance(start, int) else pl.multiple_of(start, SUB_KEYS), SUB_KEYS))
                    for a in (0, 1):
                        st = lax.dot_general(ks[a], qs[a], NT_DIMS, preferred_element_type=F32)
                        st_scr[1 - slot, a, rows, :] = st
                        mx = jnp.max(st, axis=0, keepdims=True)
                        cmax_next[a] = mx if cmax_next[a] is None else jnp.maximum(cmax_next[a], mx)
                vts = values(first + u, rows)
                for a in (0, 1):
                    p = jnp.exp2(st_scr[slot, a, rows, :] - m_new[a]).astype(BF16)
                    acc[a] = acc[a] + jnp.dot(vts[a], p, preferred_element_type=F32)
            carry = tuple((m_new[a], acc[a]) for a in (0, 1))
            cmax = tuple(cmax_next)
        return cmax, carry

    n_loop = (n_chunks - 1) // LOOP_CHUNKS
    cmax, carry = lax.fori_loop(0, n_loop, lambda i, st: run(i * LOOP_CHUNKS, LOOP_CHUNKS, *st),
                                (scores(0, 0), init))
    _, carry = run(n_loop * LOOP_CHUNKS, n_chunks - n_loop * LOOP_CHUNKS, cmax, carry)
    finish(carry)


def _flash(q, k, vt, lam, subg, *, mode, kw, k_cols, v_block_rows, v_rows, vw, k_index, v_index, n_ctx, diff_scale,
           name):
    b, s, nq = q.shape
    t = s - n_ctx
    n_pairs = nq // (2 * LANES)
    n_chunks = s // KV_CHUNK
    assert s % KV_CHUNK == 0 and n_ctx <= KV_CHUNK and t % FLASH_TQ == 0 and t % n_ctx == 0
    assert vt.shape[1] == n_chunks
    common = dict(mode=mode, k_cols=k_cols, v_rows=v_rows, vw=vw, diff_scale=diff_scale)
    o_x = pl.pallas_call(
        functools.partial(_flash_kernel, ctx_keys=0, **common),
        grid=(b, n_pairs, t // FLASH_TQ),
        in_specs=[
            pl.BlockSpec((1, FLASH_TQ, 2 * LANES), lambda bi, p, i: (bi, i, p)),
            pl.BlockSpec((1, s, kw), lambda bi, p, i: (bi, 0, k_index(p))),
            pl.BlockSpec((1, n_chunks, v_block_rows, KV_CHUNK), lambda bi, p, i: (bi, 0, v_index(p), 0)),
            _resident(lam.shape), _resident(subg.shape),
        ],
        out_specs=pl.BlockSpec((1, FLASH_TQ, LANES), lambda bi, p, i: (bi, i, p)),
        out_shape=jax.ShapeDtypeStruct((b, t, n_pairs * LANES), BF16),
        scratch_shapes=[pltpu.VMEM((2, 2, KV_CHUNK, FLASH_TQ), F32)],
        compiler_params=_cparams(3),
        name=name,
    )(q, k, vt, lam, subg)
    o_c = pl.pallas_call(
        functools.partial(_flash_kernel, ctx_keys=n_ctx, **common),
        grid=(b, n_pairs),
        in_specs=[
            pl.BlockSpec((1, n_ctx, 2 * LANES), lambda bi, p: (bi, t // n_ctx, p)),
            pl.BlockSpec((1, n_ctx, kw), lambda bi, p: (bi, 0, k_index(p))),
            pl.BlockSpec((1, 1, v_block_rows, KV_CHUNK), lambda bi, p: (bi, 0, v_index(p), 0)),
            _resident(lam.shape), _resident(subg.shape),
        ],
        out_specs=pl.BlockSpec((1, n_ctx, LANES), lambda bi, p: (bi, 0, p)),
        out_shape=jax.ShapeDtypeStruct((b, n_ctx, n_pairs * LANES), BF16),
        compiler_params=_cparams(2),
        name=name + "_ctx",
    )(q, k, vt, lam, subg)
    return o_x, o_c


def _na_tables(t):
    rows = t // GRID_W
    win_rows = min(NA_ROWS, rows)
    band_rows = min(win_rows + 1, rows)
    assert rows >= NA_WIN_ROWS and NA_QBLOCK % GRID_W == 0
    qrows = NA_QBLOCK // GRID_W
    nb = t // NA_QBLOCK
    nwin = NA_WIN_ROWS * GRID_W
    q_off, k_off = np.arange(NA_QBLOCK), np.arange(nwin)
    q_dr, q_col = q_off // GRID_W, q_off % GRID_W
    k_dr, k_col = k_off // GRID_W, k_off % GRID_W
    col_start = np.clip(q_col - NA_COLS // 2, 0, GRID_W - NA_COLS)
    col_in = (k_col[None, :] >= col_start[:, None]) & (k_col[None, :] < col_start[:, None] + NA_COLS)
    dc_idx = np.clip(k_col[None, :] - q_col[:, None] + NA_COLS - 1, 0, 2 * NA_COLS - 2)
    patterns, var, wstart = {}, [], []
    for j in range(nb):
        r = j * qrows + q_dr
        row_start = np.clip(r - win_rows // 2, 0, rows - win_rows)
        b0 = min(int(row_start[0]), rows - band_rows)
        w0 = min(b0 - b0 % 2, rows - NA_WIN_ROWS)
        assert w0 % qrows == 0 and w0 <= b0 and b0 + band_rows <= w0 + NA_WIN_ROWS
        k_row = w0 + k_dr
        in_band = (k_row >= b0) & (k_row < b0 + band_rows)
        in_win = (col_in & in_band[None, :] & (k_row[None, :] >= row_start[:, None])
                  & (k_row[None, :] < row_start[:, None] + win_rows))
        dr_idx = np.clip(k_row[None, :] - r[:, None] + NA_ROWS - 1, 0, 2 * NA_ROWS - 2)
        idx = np.where(in_win, dr_idx * (2 * NA_COLS - 1) + dc_idx, -1).astype(np.int32)
        key = idx.tobytes()
        if key not in patterns:
            patterns[key] = (len(patterns), idx)
        var.append(patterns[key][0])
        wstart.append(w0 * GRID_W)
    idx_all = np.stack([p[1] for p in sorted(patterns.values(), key=lambda p: p[0])])
    return idx_all.reshape(len(patterns), 1, -1), np.asarray(var, np.int32), np.asarray(wstart, np.int32)


def _na_bias_kernel(idx_ref, rpb_ref, o_ref):
    idx = idx_ref[0]
    r = rpb_ref[...]
    onehot = (lax.broadcasted_iota(jnp.int32, (r.shape[1], idx.shape[1]), 0) == idx).astype(BF16)
    b1 = r.astype(BF16)
    r2 = r - b1.astype(F32)
    b2 = r2.astype(BF16)
    b3 = (r2 - b2.astype(F32)).astype(BF16)
    out = (jnp.dot(b1, onehot, preferred_element_type=F32) + jnp.dot(b2, onehot, preferred_element_type=F32)
           + jnp.dot(b3, onehot, preferred_element_type=F32))
    o_ref[0] = jnp.where(idx >= 0, out, NEG_INF)


def _na_bias(idx_all, rpb):
    nvar, _, npos = idx_all.shape
    heads = rpb.shape[0]
    nr = (2 * NA_ROWS - 1) * (2 * NA_COLS - 1)
    nr_pad = pl.cdiv(nr, LANES) * LANES
    rflat = jnp.pad(rpb.reshape(heads, nr), ((0, 0), (0, nr_pad - nr)))
    tn = 2048
    assert npos % tn == 0
    out = pl.pallas_call(
        _na_bias_kernel,
        grid=(nvar, npos // tn),
        in_specs=[pl.BlockSpec((1, 1, tn), lambda v, j: (v, 0, j)), _resident(rflat.shape)],
        out_specs=pl.BlockSpec((1, heads, tn), lambda v, j: (v, 0, j)),
        out_shape=jax.ShapeDtypeStruct((nvar, heads, npos), F32),
        compiler_params=_cparams(2),
        name="na_bias",
    )(jnp.asarray(idx_all), rflat)
    return out.reshape(nvar, heads, NA_QBLOCK, npos // NA_QBLOCK)


def _halves(lo_src, hi_src):
    lane = lax.broadcasted_iota(jnp.int32, lo_src.shape, 1)
    return jnp.where(lane < LANES // 2, lo_src, hi_src)


def _na_kernel(var_ref, ws_ref, q_ref, k_ref, v_ref, b0_ref, b1_ref, o_ref, *, n_ctx, n_ctx_tiles):
    qi = pl.program_id(2)
    nwin = NA_WIN_ROWS * GRID_W

    @pl.when(qi < n_ctx_tiles)
    def _():
        kc, vc = k_ref[0, 0:n_ctx, :], v_ref[0, 0:n_ctx, :]
        outs = []
        for a in (0, 1):
            s_c = lax.dot_general(q_ref[0, :, a * LANES:(a + 1) * LANES], kc, NT_DIMS,
                                  preferred_element_type=F32)
            p_c = jnp.exp(s_c - jnp.max(s_c, axis=-1, keepdims=True))
            acc = jnp.dot(p_c.astype(BF16), vc, preferred_element_type=F32)
            outs.append(acc / jnp.sum(p_c, axis=-1, keepdims=True))
        o_ref[0] = _halves(outs[0], outs[1]).astype(BF16)

    @pl.when(qi >= n_ctx_tiles)
    def _():
        kc, vc = k_ref[0, 0:n_ctx, :], v_ref[0, 0:n_ctx, :]
        work = []
        for sub, bias_ref in ((0, b0_ref), (1, b1_ref)):
            j = (qi - n_ctx_tiles) * (TOKEN_TILE // NA_QBLOCK) + sub
            start = pl.multiple_of(n_ctx + ws_ref[j], NA_QBLOCK)
            kw, vw = k_ref[0, pl.ds(start, nwin), :], v_ref[0, pl.ds(start, nwin), :]
            for a in (0, 1):
                q = q_ref[0, sub * NA_QBLOCK:(sub + 1) * NA_QBLOCK, a * LANES:(a + 1) * LANES]
                s_c = lax.dot_general(q, kc, NT_DIMS, preferred_element_type=F32)
                s_w = lax.dot_general(q, kw, NT_DIMS, preferred_element_type=F32) + bias_ref[0, a]
                work.append((s_c, s_w, vw))
        outs = []
        for s_c, s_w, vw in work:
            m = jnp.maximum(jnp.max(s_c, axis=-1, keepdims=True), jnp.max(s_w, axis=-1, keepdims=True))
            p_c, p_w = jnp.exp(s_c - m), jnp.exp(s_w - m)
            l = jnp.sum(p_c, axis=-1, keepdims=True) + jnp.sum(p_w, axis=-1, keepdims=True)
            acc = (jnp.dot(p_c.astype(BF16), vc, preferred_element_type=F32)
                   + jnp.dot(p_w.astype(BF16), vw, preferred_element_type=F32))
            outs.append(acc / l)
        for sub in (0, 1):
            o_ref[0, sub * NA_QBLOCK:(sub + 1) * NA_QBLOCK, :] = _halves(outs[2 * sub], outs[2 * sub + 1]).astype(BF16)


def _na_attention(q, k, v, bias, var, wstart, *, n_ctx):
    b, s, nq = q.shape
    tq = TOKEN_TILE
    n_pairs = nq // (2 * LANES)
    n_ctx_tiles = n_ctx // tq
    per_tile = tq // NA_QBLOCK
    nwin = NA_WIN_ROWS * GRID_W

    def bias_spec(sub):
        def index(bi, p, i, var_ref, ws_ref):
            j = jnp.maximum(i - n_ctx_tiles, 0) * per_tile + sub
            return (var_ref[j], p, 0, 0)
        return pl.BlockSpec((1, 2, NA_QBLOCK, nwin), index)

    grid_spec = pltpu.PrefetchScalarGridSpec(
        num_scalar_prefetch=2,
        grid=(b, n_pairs, s // tq),
        in_specs=[
            pl.BlockSpec((1, tq, 2 * LANES), lambda bi, p, i, *_: (bi, i, p)),
            pl.BlockSpec((1, s, LANES), lambda bi, p, i, *_: (bi, 0, p)),
            pl.BlockSpec((1, s, LANES), lambda bi, p, i, *_: (bi, 0, p)),
            bias_spec(0), bias_spec(1),
        ],
        out_specs=pl.BlockSpec((1, tq, LANES), lambda bi, p, i, *_: (bi, i, p)),
    )
    return pl.pallas_call(
        functools.partial(_na_kernel, n_ctx=n_ctx, n_ctx_tiles=n_ctx_tiles),
        grid_spec=grid_spec,
        out_shape=jax.ShapeDtypeStruct((b, s, n_pairs * LANES), BF16),
        compiler_params=_cparams(3),
        name="na_attn",
    )(var, wstart, q, k, v, bias, bias)


def _outproj_kernel(x_ref, ox_ref, oc_ref, w_ref, mod_ref, y_ref, *, n_ctx_tiles, first_tile):
    o = jnp.where(pl.program_id(1) + first_tile < n_ctx_tiles, oc_ref[0], ox_ref[0])
    y = jnp.dot(o, w_ref[...], preferred_element_type=F32)
    y_ref[0] = x_ref[0] + mod_ref[0, 2:3, :] * y


def _outproj(xs, o_x, o_c, w, mods, *, n_ctx_tiles, x_offset, latent_only):
    b, s, d = xs.shape
    tm = TOKEN_TILE
    n = o_x.shape[2]
    first = n_ctx_tiles if latent_only else 0
    return pl.pallas_call(
        functools.partial(_outproj_kernel, n_ctx_tiles=n_ctx_tiles, first_tile=first),
        grid=(b, s // tm - first),
        in_specs=[
            pl.BlockSpec((1, tm, d), lambda bi, i: (bi, i + first, 0)),
            pl.BlockSpec((1, tm, n), lambda bi, i: (bi, jnp.maximum(i + first, n_ctx_tiles) - x_offset, 0)),
            pl.BlockSpec((1, tm, n), lambda bi, i: (bi, jnp.minimum(i + first, n_ctx_tiles - 1), 0)),
            _resident(w.shape),
            pl.BlockSpec((1, 6, d), lambda bi, i: (jnp.where(i + first < n_ctx_tiles, b, bi), 0, 0)),
        ],
        out_specs=pl.BlockSpec((1, tm, d), lambda bi, i: (bi, i, 0)),
        out_shape=jax.ShapeDtypeStruct((b, s - first * tm, d), F32),
        compiler_params=_cparams(2),
        name="outproj",
    )(xs, o_x, o_c, w, mods)


def _ffn_kernel(xp_ref, x_ref, xn_ref, mod_ref, g_ref, wup_ref, cw_ref, cb_ref, wdn_ref, fg_ref, y_ref, u_scr, a_scr,
                *, n_ctx_tiles, n_tiles, chunk, final_norm):
    i = pl.program_id(1)
    tm = x_ref.shape[1]
    has_prev = jnp.logical_and(i != 0, i != n_ctx_tiles).astype(F32)
    has_next = jnp.logical_and(i != n_ctx_tiles - 1, i != n_tiles - 1).astype(F32)
    g, shift, scale = g_ref[...], mod_ref[0, 3:4, :], mod_ref[0, 4:5, :]
    x = x_ref[0]
    h = jnp.concatenate([_norm_mod(xp_ref[0], g, shift, scale) * has_prev,
                         _norm_mod(x, g, shift, scale),
                         _norm_mod(xn_ref[0], g, shift, scale) * has_next], axis=0).astype(BF16)
    n_chunks = wup_ref.shape[1] // (2 * chunk)
    for j in range(n_chunks):
        cols = slice(j * 2 * chunk, (j + 1) * 2 * chunk)
        u_scr[...] = jnp.dot(h, wup_ref[:, cols], preferred_element_type=F32)
        cw = cw_ref[:, cols]
        u = (u_scr[HALO - 1:HALO - 1 + tm, :] * cw[0:1] + u_scr[HALO:HALO + tm, :] * cw[1:2]
             + u_scr[HALO + 1:HALO + 1 + tm, :] * cw[2:3] + cb_ref[:, cols])
        val, gate = u[:, :chunk], u[:, chunk:]
        a_scr[:, j * chunk:(j + 1) * chunk] = (gate * (1.0 / (1.0 + jnp.exp(-gate))) * val).astype(BF16)
    y = x + mod_ref[0, 5:6, :] * jnp.dot(a_scr[...], wdn_ref[...], preferred_element_type=F32)
    if final_norm:
        y = y * lax.rsqrt(jnp.mean(y * y, axis=-1, keepdims=True) + EPS) * fg_ref[...]
    y_ref[0] = y


def _ffn(xs, mods, g, wup, cw, cb, wdn, final_g, *, n_ctx_tiles, chunk, final_norm):
    b, s, d = xs.shape
    tm = TOKEN_TILE
    n_tiles = s // tm
    per = tm // HALO
    f = wdn.shape[0]
    return pl.pallas_call(
        functools.partial(_ffn_kernel, n_ctx_tiles=n_ctx_tiles, n_tiles=n_tiles, chunk=chunk,
                          final_norm=final_norm),
        grid=(b, n_tiles),
        in_specs=[
            pl.BlockSpec((1, HALO, d), lambda bi, i: (bi, jnp.maximum(i * per - 1, 0), 0)),
            pl.BlockSpec((1, tm, d), lambda bi, i: (bi, i, 0)),
            pl.BlockSpec((1, HALO, d), lambda bi, i: (bi, jnp.minimum((i + 1) * per, s // HALO - 1), 0)),
            pl.BlockSpec((1, 6, d), lambda bi, i: (jnp.where(i < n_ctx_tiles, b, bi), 0, 0)),
            _resident((1, d)),
            pl.BlockSpec(wup.shape, lambda bi, i: (0, 0), pipeline_mode=pl.Buffered(1)),
            _resident(cw.shape), _resident(cb.shape),
            pl.BlockSpec(wdn.shape, lambda bi, i: (0, 0), pipeline_mode=pl.Buffered(1)),
            _resident((1, d)),
        ],
        out_specs=pl.BlockSpec((1, tm, d), lambda bi, i: (bi, i, 0)),
        out_shape=jax.ShapeDtypeStruct((b, s, d), F32),
        scratch_shapes=[pltpu.VMEM((tm + 2 * HALO, 2 * chunk), F32), pltpu.VMEM((tm, f), BF16)],
        compiler_params=_cparams(2),
        name="ffn",
    )(xs, xs, xs, mods, g, wup, cw, cb, wdn, final_g)


def _rope_table(t, n_ctx, rot_dim, lane_lo):
    tt = jnp.arange(t)
    row = (tt // GRID_W).astype(F32)
    col = (tt % GRID_W).astype(F32)
    half = rot_dim // 2
    inv = ROPE_BASE ** (-jnp.arange(0, half, 2, dtype=F32) / half)
    ar, ac = row[:, None] * inv, col[:, None] * inv
    ang = jnp.concatenate([ar, ar, ac, ac], axis=-1)
    quarter = rot_dim // 4
    sign = jnp.where((jnp.arange(rot_dim) % (2 * quarter)) < quarter, -1.0, 1.0).astype(F32)
    cos, sin = jnp.cos(ang), jnp.sin(ang) * sign
    reps = (LANES - lane_lo) // rot_dim if lane_lo == 0 else 1
    cos_l = jnp.ones((t, LANES), F32).at[:, lane_lo:lane_lo + reps * rot_dim].set(jnp.tile(cos, (1, reps)))
    sin_l = jnp.zeros((t, LANES), F32).at[:, lane_lo:lane_lo + reps * rot_dim].set(jnp.tile(sin, (1, reps)))
    cos_s = jnp.concatenate([jnp.ones((n_ctx, LANES), F32), cos_l], axis=0)
    sin_s = jnp.concatenate([jnp.zeros((n_ctx, LANES), F32), sin_l], axis=0)
    return cos_s, sin_s


def _to_slots(w, n_heads, width, halves):
    d = w.shape[0]
    wh = w.reshape(d, n_heads, width)
    lo = jnp.pad(wh, ((0, 0), (0, 0), (0, LANES - width)))
    if not any(halves):
        return lo.reshape(d, n_heads * LANES)
    hi = jnp.pad(wh, ((0, 0), (0, 0), (LANES // 2, LANES // 2 - width)))
    in_hi = jnp.asarray(halves, jnp.int32).reshape(1, n_heads, 1) == 1
    return jnp.where(in_hi, hi, lo).reshape(d, n_heads * LANES)


def _vt_weights(wv, groups, width, tm):
    d = wv.shape[0]
    rows = width + 16
    wt = jnp.zeros((groups, rows, d), wv.dtype).at[:, :width, :].set(wv.T.reshape(groups, width, d))
    ones = jnp.zeros((groups, rows, tm), F32).at[:, width, :].set(1.0)
    return wt.reshape(groups * rows, d), ones.reshape(groups * rows, tm)


def _dup(g):
    return jnp.concatenate([g, g]).reshape(1, LANES).astype(F32)


def kernel(x, c, ctx, c_ctx, ada_w, ada_b, norm1_g, norm2_g, ffn_w_up, ffn_conv_w, ffn_conv_b, ffn_w_down,
           gqa_w_in, gqa_q_norm_g, gqa_k_norm_g, gqa_w_out,
           mla_w_in, mla_q_norm_g, mla_kv_norm_g, mla_w_uq, mla_w_ukv, mla_w_out,
           diff_w_in, diff_lambda, diff_subln_g, diff_w_out,
           na_w_in, na_rpb, na_w_out, final_norm_g):
    b, t, d = x.shape
    n_ctx = ctx.shape[1]
    depth = ada_w.shape[0]
    assert n_ctx % TOKEN_TILE == 0 and t % TOKEN_TILE == 0 and b + 1 <= 8
    nct = n_ctx // TOKEN_TILE
    f = ffn_w_down.shape[1]
    chunk = 2 * LANES
    assert f % chunk == 0

    xs = jnp.concatenate([ctx, x], axis=1)
    cin = jnp.zeros((8, d), F32).at[:b].set(c).at[b].set(c_ctx)
    mods_all = _ada_all(cin, ada_w, ada_b).reshape(depth, 8, 6, d)

    tm = TOKEN_TILE
    ones_l = jnp.ones((1, LANES), F32)
    dummy_tab = jnp.zeros((xs.shape[1], LANES), F32)
    dummy_ones = jnp.zeros((8, LANES), F32)
    cos64, sin64 = _rope_table(t, n_ctx, GQA_HEAD_DIM, 0)
    cos_mla, sin_mla = _rope_table(t, n_ctx, MLA_ROPE_DIM, MLA_NOPE_DIM)
    no_lam, no_g = jnp.zeros((4, DIFF_HEAD_DIM), F32), jnp.ones((1, LANES), F32)
    pair_rows = MLA_V_DIM + 16
    diff_rows = 2 * DIFF_HEAD_DIM + 16

    for i in range(depth):
        m, j = i % N_MIXERS, i // N_MIXERS
        mods = mods_all[i]
        g1 = norm1_g[i].reshape(1, d)
        if m == 0:
            w = gqa_w_in[j].astype(BF16)
            nqc, nkc = GQA_HEADS * GQA_HEAD_DIM, GQA_KV_HEADS * GQA_HEAD_DIM
            wq = _to_slots(w[:, :nqc], GQA_HEADS, GQA_HEAD_DIM, [0] * GQA_HEADS)
            wk = _to_slots(w[:, nqc:nqc + nkc], GQA_KV_HEADS, GQA_HEAD_DIM, [0] * GQA_KV_HEADS)
            wvt, vones = _vt_weights(w[:, nqc + nkc:], GQA_KV_HEADS, GQA_HEAD_DIM, tm)
            qc = GQA_HEAD_DIM ** -0.5 * LOG2E
            q, k, vt = _project(xs, mods, g1, wq, wk, wvt, _dup(gqa_q_norm_g[j]), _dup(gqa_k_norm_g[j]),
                                cos64 * qc, sin64 * qc, cos64, sin64, vones, qk_norm=True, rope=True,
                                head_dim=GQA_HEAD_DIM, v_transposed=True, q_rotated=True, n_ctx_tiles=nct,
                                name="proj_gqa")
            o_x, o_c = _flash(q, k, vt, no_lam, no_g, mode="pair", kw=LANES, k_cols=((0, LANES), (0, LANES)),
                       v_block_rows=pair_rows, v_rows=((0, pair_rows), (0, pair_rows)), vw=GQA_HEAD_DIM,
                       k_index=lambda p: p // 2, v_index=lambda p: p // 2, n_ctx=n_ctx, diff_scale=1.0,
                       name="attn_gqa")
            w_out = gqa_w_out[j]
        elif m == 1:
            w = mla_w_in[j].astype(BF16)
            nlat = MLA_Q_LORA + MLA_KV_LORA
            kr_slot = jnp.zeros((d, LANES), BF16).at[:, MLA_NOPE_DIM:MLA_NOPE_DIM + MLA_ROPE_DIM].set(w[:, nlat:])
            win = jnp.concatenate([w[:, :nlat], kr_slot], axis=1)
            qk_dim = MLA_NOPE_DIM + MLA_ROPE_DIM
            wuq = _to_slots(mla_w_uq[j].astype(BF16), MLA_HEADS, qk_dim, [0] * MLA_HEADS)
            wukv = mla_w_ukv[j].astype(BF16).reshape(MLA_KV_LORA, MLA_HEADS, MLA_NOPE_DIM + MLA_V_DIM)
            wuk = _to_slots(wukv[:, :, :MLA_NOPE_DIM].reshape(MLA_KV_LORA, -1), MLA_HEADS, MLA_NOPE_DIM,
                            [0] * MLA_HEADS)
            wuvt, vones = _vt_weights(wukv[:, :, MLA_NOPE_DIM:].reshape(MLA_KV_LORA, MLA_HEADS * MLA_V_DIM),
                                      MLA_HEADS, MLA_V_DIM, tm)
            qc = qk_dim ** -0.5 * LOG2E
            q, k, vt = _project_mla(xs, mods, g1, win, mla_q_norm_g[j].reshape(1, -1),
                                    mla_kv_norm_g[j].reshape(1, -1), wuq, wuk, wuvt,
                                    cos_mla * qc, sin_mla * qc, cos_mla, sin_mla, vones, n_ctx_tiles=nct)
            o_x, o_c = _flash(q, k, vt, no_lam, no_g, mode="pair", kw=2 * LANES, k_cols=((0, LANES), (LANES, 2 * LANES)),
                       v_block_rows=2 * pair_rows, v_rows=((0, pair_rows), (pair_rows, 2 * pair_rows)),
                       vw=MLA_V_DIM, k_index=lambda p: p, v_index=lambda p: p, n_ctx=n_ctx, diff_scale=1.0,
                       name="attn_mla")
            w_out = mla_w_out[j]
        elif m == 2:
            w = diff_w_in[j].astype(BF16)
            wq = _to_slots(w[:, :d], 2 * DIFF_HEADS, DIFF_HEAD_DIM, [0, 1] * DIFF_HEADS)
            wvt, vones = _vt_weights(w[:, 2 * d:], DIFF_HEADS, 2 * DIFF_HEAD_DIM, tm)
            lambda_init = 0.8 - 0.6 * math.exp(-0.3 * i)
            qc = DIFF_HEAD_DIM ** -0.5 * LOG2E
            q, k, vt = _project(xs, mods, g1, wq, w[:, d:2 * d], wvt, ones_l, ones_l,
                                cos64 * qc, sin64 * qc, cos64, sin64, vones, qk_norm=False, rope=True,
                                head_dim=DIFF_HEAD_DIM, v_transposed=True, q_rotated=True, n_ctx_tiles=nct,
                                name="proj_diff")
            o_x, o_c = _flash(q, k, vt, diff_lambda[j], diff_subln_g[j].reshape(1, LANES), mode="diff", kw=LANES,
                       k_cols=((0, LANES), (0, LANES)), v_block_rows=diff_rows,
                       v_rows=((0, diff_rows), (0, diff_rows)), vw=2 * DIFF_HEAD_DIM,
                       k_index=lambda p: p, v_index=lambda p: p, n_ctx=n_ctx, diff_scale=1.0 - lambda_init,
                       name="attn_diff")
            w_out = diff_w_out[j]
        else:
            w = na_w_in[j].astype(BF16)
            na_scale = NA_HEAD_DIM ** -0.5
            assert math.frexp(na_scale)[0] == 0.5
            wq = _to_slots(w[:, :d] * jnp.asarray(na_scale, BF16), NA_HEADS, NA_HEAD_DIM, [0, 1] * (NA_HEADS // 2))
            q, k, v = _project(xs, mods, g1, wq, w[:, d:2 * d], w[:, 2 * d:], ones_l, ones_l,
                               dummy_tab, dummy_tab, dummy_tab, dummy_tab, dummy_ones, qk_norm=False, rope=False,
                               head_dim=NA_HEAD_DIM, v_transposed=False, q_rotated=False, n_ctx_tiles=nct,
                               name="proj_na")
            idx_all, var, wstart = _na_tables(t)
            bias = _na_bias(idx_all, na_rpb[j])
            o_x = o_c = _na_attention(q, k, v, bias, jnp.asarray(var), jnp.asarray(wstart), n_ctx=n_ctx)
            w_out = na_w_out[j]
        last = i == depth - 1
        xs = _outproj(xs, o_x, o_c, w_out.astype(BF16), mods, n_ctx_tiles=nct, x_offset=0 if m == 3 else nct,
                      latent_only=last)

        def regroup(a):
            lead = a.shape[:-1]
            return jnp.stack([a[..., :f].reshape(*lead, f // chunk, chunk),
                              a[..., f:].reshape(*lead, f // chunk, chunk)], axis=-2).reshape(*lead, 2 * f)
        xs = _ffn(xs, mods, norm2_g[i].reshape(1, d), regroup(ffn_w_up[i].astype(BF16)), regroup(ffn_conv_w[i]),
                  regroup(ffn_conv_b[i]).reshape(1, 2 * f), ffn_w_down[i].astype(BF16), final_norm_g.reshape(1, d),
                  n_ctx_tiles=0 if last else nct, chunk=chunk, final_norm=last)
    return xs
```

```python
import functools
import math

import numpy as np
import jax
import jax.numpy as jnp
from jax import lax
from jax.experimental import pallas as pl
from jax.experimental.pallas import tpu as pltpu

F32 = jnp.float32
BF16 = jnp.bfloat16

LANES = 128
TOKEN_TILE = 256
HALO = 8
KV_CHUNK = 768
SUB_KEYS = 256
FLASH_TQ = 1024
LOOP_CHUNKS = 4
VMEM_LIMIT = 56 * 1024 * 1024

GRID_W = 64
ROPE_BASE = 10000.0
EPS = 1e-6
NEG_INF = -1e30
LOG2E = 1.4426950408889634

GQA_HEADS, GQA_KV_HEADS, GQA_HEAD_DIM = 16, 4, 64
MLA_HEADS, MLA_NOPE_DIM, MLA_ROPE_DIM, MLA_V_DIM = 16, 64, 32, 64
MLA_Q_LORA, MLA_KV_LORA = 384, 256
DIFF_HEADS, DIFF_HEAD_DIM = 8, 64
NA_HEADS, NA_HEAD_DIM, NA_ROWS, NA_COLS = 16, 64, 8, 16
NA_QBLOCK = 128
NA_WIN_ROWS = 10
N_MIXERS = 4

NT_DIMS = (((1,), (1,)), ((), ()))


def _cparams(n_axes):
    return pltpu.CompilerParams(dimension_semantics=("arbitrary",) * n_axes, vmem_limit_bytes=VMEM_LIMIT)


def _resident(shape):
    nd = len(shape)
    return pl.BlockSpec(shape, lambda *_: (0,) * nd)


def _ada_kernel(c_ref, w_ref, b_ref, o_ref):
    c = c_ref[...]
    s = c * (1.0 / (1.0 + jnp.exp(-c)))
    o_ref[0] = jnp.dot(s.astype(BF16), w_ref[0].astype(BF16), preferred_element_type=F32) + b_ref[0]


def _ada_all(cin, ada_w, ada_b):
    depth, d, n = ada_w.shape
    tn = n // 4
    return pl.pallas_call(
        _ada_kernel,
        grid=(depth, n // tn),
        in_specs=[
            pl.BlockSpec((8, d), lambda l, j: (0, 0)),
            pl.BlockSpec((1, d, tn), lambda l, j: (l, 0, j)),
            pl.BlockSpec((1, 1, tn), lambda l, j: (l, 0, j)),
        ],
        out_specs=pl.BlockSpec((1, 8, tn), lambda l, j: (l, 0, j)),
        out_shape=jax.ShapeDtypeStruct((depth, 8, n), F32),
        compiler_params=_cparams(2),
        name="ada",
    )(cin, ada_w, ada_b.reshape(depth, 1, n))


def _norm_mod(x, g, shift, scale):
    ms = jnp.mean(x * x, axis=-1, keepdims=True)
    return (x * lax.rsqrt(ms + EPS) * g) * (1.0 + scale) + shift


def _rope_slot(y, cos, sin_signed, lo_mask, quarter):
    fwd = pltpu.roll(y, LANES - quarter, axis=1)
    bwd = pltpu.roll(y, quarter, axis=1)
    return y * cos + jnp.where(lo_mask, fwd, bwd) * sin_signed


def _lo_mask(shape, quarter):
    lane = lax.broadcasted_iota(jnp.int32, shape, 1)
    return (lane % (2 * quarter)) < quarter


def _q_spec(n, tm, n_tiles, n_ctx_tiles, rotate):
    if not rotate:
        return pl.BlockSpec((1, tm, n), lambda bi, i: (bi, i, 0))
    n_lat = n_tiles - n_ctx_tiles
    return pl.BlockSpec((1, tm, n), lambda bi, i: (bi, jnp.where(i < n_ctx_tiles, i + n_lat, i - n_ctx_tiles), 0))


def _vt_spec(rows, tm):
    per = KV_CHUNK // tm
    return pl.BlockSpec((1, 1, rows, tm), lambda bi, i: (bi, i // per, 0, i % per))


def _proj_kernel(x_ref, mod_ref, g_ref, wq_ref, wk_ref, wv_ref, qg_ref, kg_ref, cosq_ref, sinq_ref, cosk_ref, sink_ref,
                 vones_ref, q_ref, k_ref, v_ref, *, qk_norm, rope, head_dim, v_transposed):
    h = _norm_mod(x_ref[0], g_ref[...], mod_ref[0, 0:1, :], mod_ref[0, 1:2, :]).astype(BF16)
    tm = h.shape[0]
    lo = _lo_mask((tm, LANES), head_dim // 4) if rope else None

    def finish(w_ref, o_ref, gain_ref, cos_ref, sin_ref):
        n = w_ref.shape[1]
        for c0 in range(0, n, 2 * LANES):
            c1 = min(c0 + 2 * LANES, n)
            y2 = jnp.dot(h, w_ref[:, c0:c1], preferred_element_type=F32)
            for s0 in range(0, c1 - c0, LANES):
                y = y2[:, s0:s0 + LANES]
                if qk_norm:
                    ms = jnp.sum(y * y, axis=-1, keepdims=True) * (1.0 / head_dim)
                    y = y * lax.rsqrt(ms + EPS) * gain_ref[...]
                if rope:
                    y = _rope_slot(y, cos_ref[...], sin_ref[...], lo, head_dim // 4)
                o_ref[0, :, c0 + s0:c0 + s0 + LANES] = y.astype(BF16)

    finish(wq_ref, q_ref, qg_ref, cosq_ref, sinq_ref)
    finish(wk_ref, k_ref, kg_ref, cosk_ref, sink_ref)
    if v_transposed:
        vt = lax.dot_general(wv_ref[...], h, NT_DIMS, preferred_element_type=F32) + vones_ref[...]
        v_ref[0, 0] = vt.astype(BF16)
    else:
        v_ref[0] = jnp.dot(h, wv_ref[...], preferred_element_type=F32).astype(BF16)


def _project(xs, mods, g, wq, wk, wv, qg, kg, cosq, sinq, cosk, sink, vones, *, qk_norm, rope, head_dim, v_transposed,
             q_rotated, n_ctx_tiles, name):
    b, s, d = xs.shape
    tm = TOKEN_TILE
    nq, nk = wq.shape[1], wk.shape[1]
    tok = lambda n: pl.BlockSpec((1, tm, n), lambda bi, i: (bi, i, 0))
    tab = pl.BlockSpec((tm, LANES), lambda bi, i: (i, 0))
    if v_transposed:
        v_spec = _vt_spec(wv.shape[0], tm)
        v_shape = jax.ShapeDtypeStruct((b, s // KV_CHUNK, wv.shape[0], KV_CHUNK), BF16)
    else:
        v_spec, v_shape = tok(wv.shape[1]), jax.ShapeDtypeStruct((b, s, wv.shape[1]), BF16)
    return pl.pallas_call(
        functools.partial(_proj_kernel, qk_norm=qk_norm, rope=rope, head_dim=head_dim, v_transposed=v_transposed),
        grid=(b, s // tm),
        in_specs=[
            tok(d),
            pl.BlockSpec((1, 6, d), lambda bi, i: (jnp.where(i < n_ctx_tiles, b, bi), 0, 0)),
            _resident((1, d)),
            _resident(wq.shape), _resident(wk.shape), _resident(wv.shape),
            _resident((1, LANES)), _resident((1, LANES)),
            tab, tab, tab, tab,
            _resident(vones.shape),
        ],
        out_specs=[_q_spec(nq, tm, s // tm, n_ctx_tiles, q_rotated), tok(nk), v_spec],
        out_shape=[jax.ShapeDtypeStruct((b, s, nq), BF16), jax.ShapeDtypeStruct((b, s, nk), BF16), v_shape],
        compiler_params=_cparams(2),
        name=name,
    )(xs, mods, g, wq, wk, wv, qg, kg, cosq, sinq, cosk, sink, vones)


def _mla_proj_kernel(x_ref, mod_ref, g_ref, win_ref, qn_ref, kvn_ref, wuq_ref, wuk_ref, wuvt_ref,
                     cosq_ref, sinq_ref, cosk_ref, sink_ref, vones_ref, q_ref, k_ref, v_ref):
    h = _norm_mod(x_ref[0], g_ref[...], mod_ref[0, 0:1, :], mod_ref[0, 1:2, :]).astype(BF16)
    tm = h.shape[0]
    t1 = jnp.dot(h, win_ref[...], preferred_element_type=F32)
    cq = t1[:, :MLA_Q_LORA]
    ckv = t1[:, MLA_Q_LORA:MLA_Q_LORA + MLA_KV_LORA]
    kr = t1[:, MLA_Q_LORA + MLA_KV_LORA:]
    cqn = (cq * lax.rsqrt(jnp.mean(cq * cq, axis=-1, keepdims=True) + EPS) * qn_ref[...]).astype(BF16)
    ckvn = (ckv * lax.rsqrt(jnp.mean(ckv * ckv, axis=-1, keepdims=True) + EPS) * kvn_ref[...]).astype(BF16)
    quarter = MLA_ROPE_DIM // 4
    lo = _lo_mask((tm, LANES), quarter)
    kr = _rope_slot(kr, cosk_ref[...], sink_ref[...], lo, quarter)
    n = wuq_ref.shape[1]
    for c0 in range(0, n, 2 * LANES):
        q2 = jnp.dot(cqn, wuq_ref[:, c0:c0 + 2 * LANES], preferred_element_type=F32)
        k2 = jnp.dot(ckvn, wuk_ref[:, c0:c0 + 2 * LANES], preferred_element_type=F32)
        for s0 in (0, LANES):
            q_ref[0, :, c0 + s0:c0 + s0 + LANES] = _rope_slot(
                q2[:, s0:s0 + LANES], cosq_ref[...], sinq_ref[...], lo, quarter).astype(BF16)
            k_ref[0, :, c0 + s0:c0 + s0 + LANES] = (k2[:, s0:s0 + LANES] + kr).astype(BF16)
    vt = lax.dot_general(wuvt_ref[...], ckvn, NT_DIMS, preferred_element_type=F32) + vones_ref[...]
    v_ref[0, 0] = vt.astype(BF16)


def _project_mla(xs, mods, g, win, qn, kvn, wuq, wuk, wuvt, cosq, sinq, cosk, sink, vones, *, n_ctx_tiles):
    b, s, d = xs.shape
    tm = TOKEN_TILE
    nq, nk = wuq.shape[1], wuk.shape[1]
    tok = lambda n: pl.BlockSpec((1, tm, n), lambda bi, i: (bi, i, 0))
    tab = pl.BlockSpec((tm, LANES), lambda bi, i: (i, 0))
    return pl.pallas_call(
        _mla_proj_kernel,
        grid=(b, s // tm),
        in_specs=[
            tok(d),
            pl.BlockSpec((1, 6, d), lambda bi, i: (jnp.where(i < n_ctx_tiles, b, bi), 0, 0)),
            _resident((1, d)),
            _resident(win.shape), _resident(qn.shape), _resident(kvn.shape),
            _resident(wuq.shape), _resident(wuk.shape), _resident(wuvt.shape),
            tab, tab, tab, tab,
            _resident(vones.shape),
        ],
        out_specs=[_q_spec(nq, tm, s // tm, n_ctx_tiles, True), tok(nk), _vt_spec(wuvt.shape[0], tm)],
        out_shape=[jax.ShapeDtypeStruct((b, s, nq), BF16), jax.ShapeDtypeStruct((b, s, nk), BF16),
                   jax.ShapeDtypeStruct((b, s // KV_CHUNK, wuvt.shape[0], KV_CHUNK), BF16)],
        compiler_params=_cparams(2),
        name="proj_mla",
    )(xs, mods, g, win, qn, kvn, wuq, wuk, wuvt, cosq, sinq, cosk, sink, vones)


def _flash_kernel(q_ref, k_ref, vt_ref, lam_ref, subg_ref, o_ref, *scratch, mode, k_cols, v_rows, vw, diff_scale,
                  ctx_keys):
    tq = q_ref.shape[1]
    n_chunks = vt_ref.shape[1]
    qs = [q_ref[0, :, a * LANES:(a + 1) * LANES] for a in (0, 1)]
    n_rows = v_rows[0][1] - v_rows[0][0]

    def keys(key_rows):
        k0 = k_ref[0, key_rows, k_cols[0][0]:k_cols[0][1]]
        k1 = k0 if k_cols[1] == k_cols[0] else k_ref[0, key_rows, k_cols[1][0]:k_cols[1][1]]
        return k0, k1

    def values(chunk_idx, key_lanes):
        v0 = vt_ref[0, chunk_idx, v_rows[0][0]:v_rows[0][1], key_lanes]
        v1 = v0 if v_rows[1] == v_rows[0] else vt_ref[0, chunk_idx, v_rows[1][0]:v_rows[1][1], key_lanes]
        return v0, v1

    def update(st, cmax, vt_a, carry_a):
        m, acc = carry_a
        m_new = jnp.maximum(m, cmax)
        p = jnp.exp2(st - m_new).astype(BF16)
        return m_new, jnp.exp2(m - m_new) * acc + jnp.dot(vt_a, p, preferred_element_type=F32)

    init = tuple((jnp.full((1, tq), -jnp.inf, F32), jnp.zeros((n_rows, tq), F32)) for _ in (0, 1))

    def finish(carry):
        (_, acc_a), (_, acc_b) = carry
        oa = acc_a[0:vw] / acc_a[vw:vw + 1]
        ob = acc_b[0:vw] / acc_b[vw:vw + 1]
        if mode == "pair":
            out = jnp.concatenate([oa, ob], axis=0).T
        else:
            lam = lam_ref[...]
            lam_full = (jnp.exp(jnp.sum(lam[0:1] * lam[1:2], axis=-1, keepdims=True))
                        - jnp.exp(jnp.sum(lam[2:3] * lam[3:4], axis=-1, keepdims=True))
                        + (1.0 - diff_scale))
            o = (oa - lam_full * ob).T
            out = o * lax.rsqrt(jnp.mean(o * o, axis=-1, keepdims=True) + EPS) * subg_ref[...] * diff_scale
        o_ref[0] = out.astype(BF16)

    if ctx_keys:
        ks, vts = keys(slice(0, ctx_keys)), values(0, slice(0, ctx_keys))
        out = []
        for a in (0, 1):
            st = lax.dot_general(ks[a], qs[a], NT_DIMS, preferred_element_type=F32)
            out.append(update(st, jnp.max(st, axis=0, keepdims=True), vts[a], init[a]))
        finish(tuple(out))
        return

    (st_scr,) = scratch

    def chunk_keys(chunk_idx):
        start = chunk_idx * KV_CHUNK
        return keys(pl.ds(start if isinstance(start, int) else pl.multiple_of(start, KV_CHUNK), KV_CHUNK))

    def score(slot, a, ks):
        st = lax.dot_general(ks[a], qs[a], NT_DIMS, preferred_element_type=F32)
        st_scr[slot, a] = st
        return jnp.max(st, axis=0, keepdims=True)

    def scores(slot, chunk_idx):
        ks = chunk_keys(chunk_idx)
        return tuple(score(slot, a, ks) for a in (0, 1))

    def run(first, count, cmax, carry):
        for u in range(count):
            nxt, slot = first + u + 1, u % 2
            has_next = isinstance(nxt, jax.Array) or nxt < n_chunks
            m_new = [jnp.maximum(carry[a][0], cmax[a]) for a in (0, 1)]
            acc = [jnp.exp2(carry[a][0] - m_new[a]) * carry[a][1] for a in (0, 1)]
            cmax_next = [None, None]
            for j in range(KV_CHUNK // SUB_KEYS):
                rows = slice(j * SUB_KEYS, (j + 1) * SUB_KEYS)
                if has_next:
                    start = nxt * KV_CHUNK + j * SUB_KEYS
                    ks = keys(pl.ds(start if isinstance(start, int) else pl.multiple_of(start, SUB_KEYS), SUB_KEYS))
                    for a in (0, 1):
                        st = lax.dot_general(ks[a], qs[a], NT_DIMS, preferred_element_type=F32)
                        st_scr[1 - slot, a, rows, :] = st
                        mx = jnp.max(st, axis=0, keepdims=True)
                        cmax_next[a] = mx if cmax_next[a] is None else jnp.maximum(cmax_next[a], mx)
                vts = values(first + u, rows)
                for a in (0, 1):
                    p = jnp.exp2(st_scr[slot, a, rows, :] - m_new[a]).astype(BF16)
                    acc[a] = acc[a] + jnp.dot(vts[a], p, preferred_element_type=F32)
            carry = tuple((m_new[a], acc[a]) for a in (0, 1))
            cmax = tuple(cmax_next)
        return cmax, carry

    n_loop = (n_chunks - 1) // LOOP_CHUNKS
    cmax, carry = lax.fori_loop(0, n_loop, lambda i, st: run(i * LOOP_CHUNKS, LOOP_CHUNKS, *st),
                                (scores(0, 0), init))
    _, carry = run(n_loop * LOOP_CHUNKS, n_chunks - n_loop * LOOP_CHUNKS, cmax, carry)
    finish(carry)


def _flash(q, k, vt, lam, subg, *, mode, kw, k_cols, v_block_rows, v_rows, vw, k_index, v_index, n_ctx, diff_scale,
           name):
    b, s, nq = q.shape
    t = s - n_ctx
    n_pairs = nq // (2 * LANES)
    n_chunks = s // KV_CHUNK
    assert s % KV_CHUNK == 0 and n_ctx <= KV_CHUNK and t % FLASH_TQ == 0 and t % n_ctx == 0
    assert vt.shape[1] == n_chunks
    common = dict(mode=mode, k_cols=k_cols, v_rows=v_rows, vw=vw, diff_scale=diff_scale)
    o_x = pl.pallas_call(
        functools.partial(_flash_kernel, ctx_keys=0, **common),
        grid=(b, n_pairs, t // FLASH_TQ),
        in_specs=[
            pl.BlockSpec((1, FLASH_TQ, 2 * LANES), lambda bi, p, i: (bi, i, p)),
            pl.BlockSpec((1, s, kw), lambda bi, p, i: (bi, 0, k_index(p))),
            pl.BlockSpec((1, n_chunks, v_block_rows, KV_CHUNK), lambda bi, p, i: (bi, 0, v_index(p), 0)),
            _resident(lam.shape), _resident(subg.shape),
        ],
        out_specs=pl.BlockSpec((1, FLASH_TQ, LANES), lambda bi, p, i: (bi, i, p)),
        out_shape=jax.ShapeDtypeStruct((b, t, n_pairs * LANES), BF16),
        scratch_shapes=[pltpu.VMEM((2, 2, KV_CHUNK, FLASH_TQ), F32)],
        compiler_params=_cparams(3),
        name=name,
    )(q, k, vt, lam, subg)
    o_c = pl.pallas_call(
        functools.partial(_flash_kernel, ctx_keys=n_ctx, **common),
        grid=(b, n_pairs),
        in_specs=[
            pl.BlockSpec((1, n_ctx, 2 * LANES), lambda bi, p: (bi, t // n_ctx, p)),
            pl.BlockSpec((1, n_ctx, kw), lambda bi, p: (bi, 0, k_index(p))),
            pl.BlockSpec((1, 1, v_block_rows, KV_CHUNK), lambda bi, p: (bi, 0, v_index(p), 0)),
            _resident(lam.shape), _resident(subg.shape),
        ],
        out_specs=pl.BlockSpec((1, n_ctx, LANES), lambda bi, p: (bi, 0, p)),
        out_shape=jax.ShapeDtypeStruct((b, n_ctx, n_pairs * LANES), BF16),
        compiler_params=_cparams(2),
        name=name + "_ctx",
    )(q, k, vt, lam, subg)
    return o_x, o_c


def _na_tables(t):
    rows = t // GRID_W
    win_rows = min(NA_ROWS, rows)
    band_rows = min(win_rows + 1, rows)
    assert rows >= NA_WIN_ROWS and NA_QBLOCK % GRID_W == 0
    qrows = NA_QBLOCK // GRID_W
    nb = t // NA_QBLOCK
    nwin = NA_WIN_ROWS * GRID_W
    q_off, k_off = np.arange(NA_QBLOCK), np.arange(nwin)
    q_dr, q_col = q_off // GRID_W, q_off % GRID_W
    k_dr, k_col = k_off // GRID_W, k_off % GRID_W
    col_start = np.clip(q_col - NA_COLS // 2, 0, GRID_W - NA_COLS)
    col_in = (k_col[None, :] >= col_start[:, None]) & (k_col[None, :] < col_start[:, None] + NA_COLS)
    dc_idx = np.clip(k_col[None, :] - q_col[:, None] + NA_COLS - 1, 0, 2 * NA_COLS - 2)
    patterns, var, wstart = {}, [], []
    for j in range(nb):
        r = j * qrows + q_dr
        row_start = np.clip(r - win_rows // 2, 0, rows - win_rows)
        b0 = min(int(row_start[0]), rows - band_rows)
        w0 = min(b0 - b0 % 2, rows - NA_WIN_ROWS)
        assert w0 % qrows == 0 and w0 <= b0 and b0 + band_rows <= w0 + NA_WIN_ROWS
        k_row = w0 + k_dr
        in_band = (k_row >= b0) & (k_row < b0 + band_rows)
        in_win = (col_in & in_band[None, :] & (k_row[None, :] >= row_start[:, None])
                  & (k_row[None, :] < row_start[:, None] + win_rows))
        dr_idx = np.clip(k_row[None, :] - r[:, None] + NA_ROWS - 1, 0, 2 * NA_ROWS - 2)
        idx = np.where(in_win, dr_idx * (2 * NA_COLS - 1) + dc_idx, -1).astype(np.int32)
        key = idx.tobytes()
        if key not in patterns:
            patterns[key] = (len(patterns), idx)
        var.append(patterns[key][0])
        wstart.append(w0 * GRID_W)
    idx_all = np.stack([p[1] for p in sorted(patterns.values(), key=lambda p: p[0])])
    return idx_all.reshape(len(patterns), 1, -1), np.asarray(var, np.int32), np.asarray(wstart, np.int32)


def _na_bias_kernel(idx_ref, rpb_ref, o_ref):
    idx = idx_ref[0]
    r = rpb_ref[...]
    onehot = (lax.broadcasted_iota(jnp.int32, (r.shape[1], idx.shape[1]), 0) == idx).astype(BF16)
    b1 = r.astype(BF16)
    r2 = r - b1.astype(F32)
    b2 = r2.astype(BF16)
    b3 = (r2 - b2.astype(F32)).astype(BF16)
    out = (jnp.dot(b1, onehot, preferred_element_type=F32) + jnp.dot(b2, onehot, preferred_element_type=F32)
           + jnp.dot(b3, onehot, preferred_element_type=F32))
    o_ref[0] = jnp.where(idx >= 0, out, NEG_INF)


def _na_bias(idx_all, rpb):
    nvar, _, npos = idx_all.shape
    heads = rpb.shape[0]
    nr = (2 * NA_ROWS - 1) * (2 * NA_COLS - 1)
    nr_pad = pl.cdiv(nr, LANES) * LANES
    rflat = jnp.pad(rpb.reshape(heads, nr), ((0, 0), (0, nr_pad - nr)))
    tn = 2048
    assert npos % tn == 0
    out = pl.pallas_call(
        _na_bias_kernel,
        grid=(nvar, npos // tn),
        in_specs=[pl.BlockSpec((1, 1, tn), lambda v, j: (v, 0, j)), _resident(rflat.shape)],
        out_specs=pl.BlockSpec((1, heads, tn), lambda v, j: (v, 0, j)),
        out_shape=jax.ShapeDtypeStruct((nvar, heads, npos), F32),
        compiler_params=_cparams(2),
        name="na_bias",
    )(jnp.asarray(idx_all), rflat)
    return out.reshape(nvar, heads, NA_QBLOCK, npos // NA_QBLOCK)


def _halves(lo_src, hi_src):
    lane = lax.broadcasted_iota(jnp.int32, lo_src.shape, 1)
    return jnp.where(lane < LANES // 2, lo_src, hi_src)


def _na_kernel(var_ref, ws_ref, q_ref, k_ref, v_ref, b0_ref, b1_ref, o_ref, *, n_ctx, n_ctx_tiles):
    qi = pl.program_id(2)
    nwin = NA_WIN_ROWS * GRID_W

    @pl.when(qi < n_ctx_tiles)
    def _():
        kc, vc = k_ref[0, 0:n_ctx, :], v_ref[0, 0:n_ctx, :]
        outs = []
        for a in (0, 1):
            s_c = lax.dot_general(q_ref[0, :, a * LANES:(a + 1) * LANES], kc, NT_DIMS,
                                  preferred_element_type=F32)
            p_c = jnp.exp(s_c - jnp.max(s_c, axis=-1, keepdims=True))
            acc = jnp.dot(p_c.astype(BF16), vc, preferred_element_type=F32)
            outs.append(acc / jnp.sum(p_c, axis=-1, keepdims=True))
        o_ref[0] = _halves(outs[0], outs[1]).astype(BF16)

    @pl.when(qi >= n_ctx_tiles)
    def _():
        kc, vc = k_ref[0, 0:n_ctx, :], v_ref[0, 0:n_ctx, :]
        work = []
        for sub, bias_ref in ((0, b0_ref), (1, b1_ref)):
            j = (qi - n_ctx_tiles) * (TOKEN_TILE // NA_QBLOCK) + sub
            start = pl.multiple_of(n_ctx + ws_ref[j], NA_QBLOCK)
            kw, vw = k_ref[0, pl.ds(start, nwin), :], v_ref[0, pl.ds(start, nwin), :]
            for a in (0, 1):
                q = q_ref[0, sub * NA_QBLOCK:(sub + 1) * NA_QBLOCK, a * LANES:(a + 1) * LANES]
                s_c = lax.dot_general(q, kc, NT_DIMS, preferred_element_type=F32)
                s_w = lax.dot_general(q, kw, NT_DIMS, preferred_element_type=F32) + bias_ref[0, a]
                work.append((s_c, s_w, vw))
        outs = []
        for s_c, s_w, vw in work:
            m = jnp.maximum(jnp.max(s_c, axis=-1, keepdims=True), jnp.max(s_w, axis=-1, keepdims=True))
            p_c, p_w = jnp.exp(s_c - m), jnp.exp(s_w - m)
            l = jnp.sum(p_c, axis=-1, keepdims=True) + jnp.sum(p_w, axis=-1, keepdims=True)
            acc = (jnp.dot(p_c.astype(BF16), vc, preferred_element_type=F32)
                   + jnp.dot(p_w.astype(BF16), vw, preferred_element_type=F32))
            outs.append(acc / l)
        for sub in (0, 1):
            o_ref[0, sub * NA_QBLOCK:(sub + 1) * NA_QBLOCK, :] = _halves(outs[2 * sub], outs[2 * sub + 1]).astype(BF16)


def _na_attention(q, k, v, bias, var, wstart, *, n_ctx):
    b, s, nq = q.shape
    tq = TOKEN_TILE
    n_pairs = nq // (2 * LANES)
    n_ctx_tiles = n_ctx // tq
    per_tile = tq // NA_QBLOCK
    nwin = NA_WIN_ROWS * GRID_W

    def bias_spec(sub):
        def index(bi, p, i, var_ref, ws_ref):
            j = jnp.maximum(i - n_ctx_tiles, 0) * per_tile + sub
            return (var_ref[j], p, 0, 0)
        return pl.BlockSpec((1, 2, NA_QBLOCK, nwin), index)

    grid_spec = pltpu.PrefetchScalarGridSpec(
        num_scalar_prefetch=2,
        grid=(b, n_pairs, s // tq),
        in_specs=[
            pl.BlockSpec((1, tq, 2 * LANES), lambda bi, p, i, *_: (bi, i, p)),
            pl.BlockSpec((1, s, LANES), lambda bi, p, i, *_: (bi, 0, p)),
            pl.BlockSpec((1, s, LANES), lambda bi, p, i, *_: (bi, 0, p)),
            bias_spec(0), bias_spec(1),
        ],
        out_specs=pl.BlockSpec((1, tq, LANES), lambda bi, p, i, *_: (bi, i, p)),
    )
    return pl.pallas_call(
        functools.partial(_na_kernel, n_ctx=n_ctx, n_ctx_tiles=n_ctx_tiles),
        grid_spec=grid_spec,
        out_shape=jax.ShapeDtypeStruct((b, s, n_pairs * LANES), BF16),
        compiler_params=_cparams(3),
        name="na_attn",
    )(var, wstart, q, k, v, bias, bias)


def _outproj_kernel(x_ref, ox_ref, oc_ref, w_ref, mod_ref, y_ref, *, n_ctx_tiles, first_tile):
    o = jnp.where(pl.program_id(1) + first_tile < n_ctx_tiles, oc_ref[0], ox_ref[0])
    y = jnp.dot(o, w_ref[...], preferred_element_type=F32)
    y_ref[0] = x_ref[0] + mod_ref[0, 2:3, :] * y


def _outproj(xs, o_x, o_c, w, mods, *, n_ctx_tiles, x_offset, latent_only):
    b, s, d = xs.shape
    tm = TOKEN_TILE
    n = o_x.shape[2]
    first = n_ctx_tiles if latent_only else 0
    return pl.pallas_call(
        functools.partial(_outproj_kernel, n_ctx_tiles=n_ctx_tiles, first_tile=first),
        grid=(b, s // tm - first),
        in_specs=[
            pl.BlockSpec((1, tm, d), lambda bi, i: (bi, i + first, 0)),
            pl.BlockSpec((1, tm, n), lambda bi, i: (bi, jnp.maximum(i + first, n_ctx_tiles) - x_offset, 0)),
            pl.BlockSpec((1, tm, n), lambda bi, i: (bi, jnp.minimum(i + first, n_ctx_tiles - 1), 0)),
            _resident(w.shape),
            pl.BlockSpec((1, 6, d), lambda bi, i: (jnp.where(i + first < n_ctx_tiles, b, bi), 0, 0)),
        ],
        out_specs=pl.BlockSpec((1, tm, d), lambda bi, i: (bi, i, 0)),
        out_shape=jax.ShapeDtypeStruct((b, s - first * tm, d), F32),
        compiler_params=_cparams(2),
        name="outproj",
    )(xs, o_x, o_c, w, mods)


def _ffn_kernel(xp_ref, x_ref, xn_ref, mod_ref, g_ref, wup_ref, cw_ref, cb_ref, wdn_ref, fg_ref, y_ref, u_scr, a_scr,
                *, n_ctx_tiles, n_tiles, chunk, final_norm):
    i = pl.program_id(1)
    tm = x_ref.shape[1]
    has_prev = jnp.logical_and(i != 0, i != n_ctx_tiles).astype(F32)
    has_next = jnp.logical_and(i != n_ctx_tiles - 1, i != n_tiles - 1).astype(F32)
    g, shift, scale = g_ref[...], mod_ref[0, 3:4, :], mod_ref[0, 4:5, :]
    x = x_ref[0]
    h = jnp.concatenate([_norm_mod(xp_ref[0], g, shift, scale) * has_prev,
                         _norm_mod(x, g, shift, scale),
                         _norm_mod(xn_ref[0], g, shift, scale) * has_next], axis=0).astype(BF16)
    n_chunks = wup_ref.shape[1] // (2 * chunk)
    for j in range(n_chunks):
        cols = slice(j * 2 * chunk, (j + 1) * 2 * chunk)
        u_scr[...] = jnp.dot(h, wup_ref[:, cols], preferred_element_type=F32)
        cw = cw_ref[:, cols]
        u = (u_scr[HALO - 1:HALO - 1 + tm, :] * cw[0:1] + u_scr[HALO:HALO + tm, :] * cw[1:2]
             + u_scr[HALO + 1:HALO + 1 + tm, :] * cw[2:3] + cb_ref[:, cols])
        val, gate = u[:, :chunk], u[:, chunk:]
        a_scr[:, j * chunk:(j + 1) * chunk] = (gate * (1.0 / (1.0 + jnp.exp(-gate))) * val).astype(BF16)
    y = x + mod_ref[0, 5:6, :] * jnp.dot(a_scr[...], wdn_ref[...], preferred_element_type=F32)
    if final_norm:
        y = y * lax.rsqrt(jnp.mean(y * y, axis=-1, keepdims=True) + EPS) * fg_ref[...]
    y_ref[0] = y


def _ffn(xs, mods, g, wup, cw, cb, wdn, final_g, *, n_ctx_tiles, chunk, final_norm):
    b, s, d = xs.shape
    tm = TOKEN_TILE
    n_tiles = s // tm
    per = tm // HALO
    f = wdn.shape[0]
    return pl.pallas_call(
        functools.partial(_ffn_kernel, n_ctx_tiles=n_ctx_tiles, n_tiles=n_tiles, chunk=chunk,
                          final_norm=final_norm),
        grid=(b, n_tiles),
        in_specs=[
            pl.BlockSpec((1, HALO, d), lambda bi, i: (bi, jnp.maximum(i * per - 1, 0), 0)),
            pl.BlockSpec((1, tm, d), lambda bi, i: (bi, i, 0)),
            pl.BlockSpec((1, HALO, d), lambda bi, i: (bi, jnp.minimum((i + 1) * per, s // HALO - 1), 0)),
            pl.BlockSpec((1, 6, d), lambda bi, i: (jnp.where(i < n_ctx_tiles, b, bi), 0, 0)),
            _resident((1, d)),
            pl.BlockSpec(wup.shape, lambda bi, i: (0, 0), pipeline_mode=pl.Buffered(1)),
            _resident(cw.shape), _resident(cb.shape),
            pl.BlockSpec(wdn.shape, lambda bi, i: (0, 0), pipeline_mode=pl.Buffered(1)),
            _resident((1, d)),
        ],
        out_specs=pl.BlockSpec((1, tm, d), lambda bi, i: (bi, i, 0)),
        out_shape=jax.ShapeDtypeStruct((b, s, d), F32),
        scratch_shapes=[pltpu.VMEM((tm + 2 * HALO, 2 * chunk), F32), pltpu.VMEM((tm, f), BF16)],
        compiler_params=_cparams(2),
        name="ffn",
    )(xs, xs, xs, mods, g, wup, cw, cb, wdn, final_g)


def _rope_table(t, n_ctx, rot_dim, lane_lo):
    tt = jnp.arange(t)
    row = (tt // GRID_W).astype(F32)
    col = (tt % GRID_W).astype(F32)
    half = rot_dim // 2
    inv = ROPE_BASE ** (-jnp.arange(0, half, 2, dtype=F32) / half)
    ar, ac = row[:, None] * inv, col[:, None] * inv
    ang = jnp.concatenate([ar, ar, ac, ac], axis=-1)
    quarter = rot_dim // 4
    sign = jnp.where((jnp.arange(rot_dim) % (2 * quarter)) < quarter, -1.0, 1.0).astype(F32)
    cos, sin = jnp.cos(ang), jnp.sin(ang) * sign
    reps = (LANES - lane_lo) // rot_dim if lane_lo == 0 else 1
    cos_l = jnp.ones((t, LANES), F32).at[:, lane_lo:lane_lo + reps * rot_dim].set(jnp.tile(cos, (1, reps)))
    sin_l = jnp.zeros((t, LANES), F32).at[:, lane_lo:lane_lo + reps * rot_dim].set(jnp.tile(sin, (1, reps)))
    cos_s = jnp.concatenate([jnp.ones((n_ctx, LANES), F32), cos_l], axis=0)
    sin_s = jnp.concatenate([jnp.zeros((n_ctx, LANES), F32), sin_l], axis=0)
    return cos_s, sin_s


def _to_slots(w, n_heads, width, halves):
    d = w.shape[0]
    wh = w.reshape(d, n_heads, width)
    lo = jnp.pad(wh, ((0, 0), (0, 0), (0, LANES - width)))
    if not any(halves):
        return lo.reshape(d, n_heads * LANES)
    hi = jnp.pad(wh, ((0, 0), (0, 0), (LANES // 2, LANES // 2 - width)))
    in_hi = jnp.asarray(halves, jnp.int32).reshape(1, n_heads, 1) == 1
    return jnp.where(in_hi, hi, lo).reshape(d, n_heads * LANES)


def _vt_weights(wv, groups, width, tm):
    d = wv.shape[0]
    rows = width + 16
    wt = jnp.zeros((groups, rows, d), wv.dtype).at[:, :width, :].set(wv.T.reshape(groups, width, d))
    ones = jnp.zeros((groups, rows, tm), F32).at[:, width, :].set(1.0)
    return wt.reshape(groups * rows, d), ones.reshape(groups * rows, tm)


def _dup(g):
    return jnp.concatenate([g, g]).reshape(1, LANES).astype(F32)


def kernel(x, c, ctx, c_ctx, ada_w, ada_b, norm1_g, norm2_g, ffn_w_up, ffn_conv_w, ffn_conv_b, ffn_w_down,
           gqa_w_in, gqa_q_norm_g, gqa_k_norm_g, gqa_w_out,
           mla_w_in, mla_q_norm_g, mla_kv_norm_g, mla_w_uq, mla_w_ukv, mla_w_out,
           diff_w_in, diff_lambda, diff_subln_g, diff_w_out,
           na_w_in, na_rpb, na_w_out, final_norm_g):
    b, t, d = x.shape
    n_ctx = ctx.shape[1]
    depth = ada_w.shape[0]
    assert n_ctx % TOKEN_TILE == 0 and t % TOKEN_TILE == 0 and b + 1 <= 8
    nct = n_ctx // TOKEN_TILE
    f = ffn_w_down.shape[1]
    chunk = 2 * LANES
    assert f % chunk == 0

    xs = jnp.concatenate([ctx, x], axis=1)
    cin = jnp.zeros((8, d), F32).at[:b].set(c).at[b].set(c_ctx)
    mods_all = _ada_all(cin, ada_w, ada_b).reshape(depth, 8, 6, d)

    tm = TOKEN_TILE
    ones_l = jnp.ones((1, LANES), F32)
    dummy_tab = jnp.zeros((xs.shape[1], LANES), F32)
    dummy_ones = jnp.zeros((8, LANES), F32)
    cos64, sin64 = _rope_table(t, n_ctx, GQA_HEAD_DIM, 0)
    cos_mla, sin_mla = _rope_table(t, n_ctx, MLA_ROPE_DIM, MLA_NOPE_DIM)
    no_lam, no_g = jnp.zeros((4, DIFF_HEAD_DIM), F32), jnp.ones((1, LANES), F32)
    pair_rows = MLA_V_DIM + 16
    diff_rows = 2 * DIFF_HEAD_DIM + 16

    for i in range(depth):
        m, j = i % N_MIXERS, i // N_MIXERS
        mods = mods_all[i]
        g1 = norm1_g[i].reshape(1, d)
        if m == 0:
            w = gqa_w_in[j].astype(BF16)
            nqc, nkc = GQA_HEADS * GQA_HEAD_DIM, GQA_KV_HEADS * GQA_HEAD_DIM
            wq = _to_slots(w[:, :nqc], GQA_HEADS, GQA_HEAD_DIM, [0] * GQA_HEADS)
            wk = _to_slots(w[:, nqc:nqc + nkc], GQA_KV_HEADS, GQA_HEAD_DIM, [0] * GQA_KV_HEADS)
            wvt, vones = _vt_weights(w[:, nqc + nkc:], GQA_KV_HEADS, GQA_HEAD_DIM, tm)
            qc = GQA_HEAD_DIM ** -0.5 * LOG2E
            q, k, vt = _project(xs, mods, g1, wq, wk, wvt, _dup(gqa_q_norm_g[j]), _dup(gqa_k_norm_g[j]),
                                cos64 * qc, sin64 * qc, cos64, sin64, vones, qk_norm=True, rope=True,
                                head_dim=GQA_HEAD_DIM, v_transposed=True, q_rotated=True, n_ctx_tiles=nct,
                                name="proj_gqa")
            o_x, o_c = _flash(q, k, vt, no_lam, no_g, mode="pair", kw=LANES, k_cols=((0, LANES), (0, LANES)),
                       v_block_rows=pair_rows, v_rows=((0, pair_rows), (0, pair_rows)), vw=GQA_HEAD_DIM,
                       k_index=lambda p: p // 2, v_index=lambda p: p // 2, n_ctx=n_ctx, diff_scale=1.0,
                       name="attn_gqa")
            w_out = gqa_w_out[j]
        elif m == 1:
            w = mla_w_in[j].astype(BF16)
            nlat = MLA_Q_LORA + MLA_KV_LORA
            kr_slot = jnp.zeros((d, LANES), BF16).at[:, MLA_NOPE_DIM:MLA_NOPE_DIM + MLA_ROPE_DIM].set(w[:, nlat:])
            win = jnp.concatenate([w[:, :nlat], kr_slot], axis=1)
            qk_dim = MLA_NOPE_DIM + MLA_ROPE_DIM
            wuq = _to_slots(mla_w_uq[j].astype(BF16), MLA_HEADS, qk_dim, [0] * MLA_HEADS)
            wukv = mla_w_ukv[j].astype(BF16).reshape(MLA_KV_LORA, MLA_HEADS, MLA_NOPE_DIM + MLA_V_DIM)
            wuk = _to_slots(wukv[:, :, :MLA_NOPE_DIM].reshape(MLA_KV_LORA, -1), MLA_HEADS, MLA_NOPE_DIM,
                            [0] * MLA_HEADS)
            wuvt, vones = _vt_weights(wukv[:, :, MLA_NOPE_DIM:].reshape(MLA_KV_LORA, MLA_HEADS * MLA_V_DIM),
                                      MLA_HEADS, MLA_V_DIM, tm)
            qc = qk_dim ** -0.5 * LOG2E
            q, k, vt = _project_mla(xs, mods, g1, win, mla_q_norm_g[j].reshape(1, -1),
                                    mla_kv_norm_g[j].reshape(1, -1), wuq, wuk, wuvt,
                                    cos_mla * qc, sin_mla * qc, cos_mla, sin_mla, vones, n_ctx_tiles=nct)
            o_x, o_c = _flash(q, k, vt, no_lam, no_g, mode="pair", kw=2 * LANES, k_cols=((0, LANES), (LANES, 2 * LANES)),
                       v_block_rows=2 * pair_rows, v_rows=((0, pair_rows), (pair_rows, 2 * pair_rows)),
                       vw=MLA_V_DIM, k_index=lambda p: p, v_index=lambda p: p, n_ctx=n_ctx, diff_scale=1.0,
                       name="attn_mla")
            w_out = mla_w_out[j]
        elif m == 2:
            w = diff_w_in[j].astype(BF16)
            wq = _to_slots(w[:, :d], 2 * DIFF_HEADS, DIFF_HEAD_DIM, [0, 1] * DIFF_HEADS)
            wvt, vones = _vt_weights(w[:, 2 * d:], DIFF_HEADS, 2 * DIFF_HEAD_DIM, tm)
            lambda_init = 0.8 - 0.6 * math.exp(-0.3 * i)
            qc = DIFF_HEAD_DIM ** -0.5 * LOG2E
            q, k, vt = _project(xs, mods, g1, wq, w[:, d:2 * d], wvt, ones_l, ones_l,
                                cos64 * qc, sin64 * qc, cos64, sin64, vones, qk_norm=False, rope=True,
                                head_dim=DIFF_HEAD_DIM, v_transposed=True, q_rotated=True, n_ctx_tiles=nct,
                                name="proj_diff")
            o_x, o_c = _flash(q, k, vt, diff_lambda[j], diff_subln_g[j].reshape(1, LANES), mode="diff", kw=LANES,
                       k_cols=((0, LANES), (0, LANES)), v_block_rows=diff_rows,
                       v_rows=((0, diff_rows), (0, diff_rows)), vw=2 * DIFF_HEAD_DIM,
                       k_index=lambda p: p, v_index=lambda p: p, n_ctx=n_ctx, diff_scale=1.0 - lambda_init,
                       name="attn_diff")
            w_out = diff_w_out[j]
        else:
            w = na_w_in[j].astype(BF16)
            na_scale = NA_HEAD_DIM ** -0.5
            assert math.frexp(na_scale)[0] == 0.5
            wq = _to_slots(w[:, :d] * jnp.asarray(na_scale, BF16), NA_HEADS, NA_HEAD_DIM, [0, 1] * (NA_HEADS // 2))
            q, k, v = _project(xs, mods, g1, wq, w[:, d:2 * d], w[:, 2 * d:], ones_l, ones_l,
                               dummy_tab, dummy_tab, dummy_tab, dummy_tab, dummy_ones, qk_norm=False, rope=False,
                               head_dim=NA_HEAD_DIM, v_transposed=False, q_rotated=False, n_ctx_tiles=nct,
                               name="proj_na")
            idx_all, var, wstart = _na_tables(t)
            bias = _na_bias(idx_all, na_rpb[j])
            o_x = o_c = _na_attention(q, k, v, bias, jnp.asarray(var), jnp.asarray(wstart), n_ctx=n_ctx)
            w_out = na_w_out[j]
        last = i == depth - 1
        xs = _outproj(xs, o_x, o_c, w_out.astype(BF16), mods, n_ctx_tiles=nct, x_offset=0 if m == 3 else nct,
                      latent_only=last)

        def regroup(a):
            lead = a.shape[:-1]
            return jnp.stack([a[..., :f].reshape(*lead, f // chunk, chunk),
                              a[..., f:].reshape(*lead, f // chunk, chunk)], axis=-2).reshape(*lead, 2 * f)
        xs = _ffn(xs, mods, norm2_g[i].reshape(1, d), regroup(ffn_w_up[i].astype(BF16)), regroup(ffn_conv_w[i]),
                  regroup(ffn_conv_b[i]).reshape(1, 2 * f), ffn_w_down[i].astype(BF16), final_norm_g.reshape(1, d),
                  n_ctx_tiles=0 if last else nct, chunk=chunk, final_norm=last)
    return xs
```

```python
import functools
import math

import numpy as np
import jax
import jax.numpy as jnp
from jax import lax
from jax.experimental import pallas as pl
from jax.experimental.pallas import tpu as pltpu

F32 = jnp.float32
BF16 = jnp.bfloat16

LANES = 128
TOKEN_TILE = 256
HALO = 8
KV_CHUNK = 768
SUB_KEYS = 256
FLASH_TQ = 512
LOOP_CHUNKS = 4
VMEM_LIMIT = 56 * 1024 * 1024

GRID_W = 64
ROPE_BASE = 10000.0
EPS = 1e-6
NEG_INF = -1e30
LOG2E = 1.4426950408889634

GQA_HEADS, GQA_KV_HEADS, GQA_HEAD_DIM = 16, 4, 64
MLA_HEADS, MLA_NOPE_DIM, MLA_ROPE_DIM, MLA_V_DIM = 16, 64, 32, 64
MLA_Q_LORA, MLA_KV_LORA = 384, 256
DIFF_HEADS, DIFF_HEAD_DIM = 8, 64
NA_HEADS, NA_HEAD_DIM, NA_ROWS, NA_COLS = 16, 64, 8, 16
NA_QBLOCK = 128
NA_WIN_ROWS = 10
N_MIXERS = 4

NT_DIMS = (((1,), (1,)), ((), ()))


def _cparams(n_axes):
    return pltpu.CompilerParams(dimension_semantics=("arbitrary",) * n_axes, vmem_limit_bytes=VMEM_LIMIT)


def _resident(shape):
    nd = len(shape)
    return pl.BlockSpec(shape, lambda *_: (0,) * nd)


def _ada_kernel(c_ref, w_ref, b_ref, o_ref):
    c = c_ref[...]
    s = c * (1.0 / (1.0 + jnp.exp(-c)))
    o_ref[0] = jnp.dot(s.astype(BF16), w_ref[0].astype(BF16), preferred_element_type=F32) + b_ref[0]


def _ada_all(cin, ada_w, ada_b):
    depth, d, n = ada_w.shape
    tn = n // 4
    return pl.pallas_call(
        _ada_kernel,
        grid=(depth, n // tn),
        in_specs=[
            pl.BlockSpec((8, d), lambda l, j: (0, 0)),
            pl.BlockSpec((1, d, tn), lambda l, j: (l, 0, j)),
            pl.BlockSpec((1, 1, tn), lambda l, j: (l, 0, j)),
        ],
        out_specs=pl.BlockSpec((1, 8, tn), lambda l, j: (l, 0, j)),
        out_shape=jax.ShapeDtypeStruct((depth, 8, n), F32),
        compiler_params=_cparams(2),
        name="ada",
    )(cin, ada_w, ada_b.reshape(depth, 1, n))


def _norm_mod(x, g, shift, scale):
    ms = jnp.mean(x * x, axis=-1, keepdims=True)
    return (x * lax.rsqrt(ms + EPS) * g) * (1.0 + scale) + shift


def _rope_slot(y, cos, sin_signed, lo_mask, quarter):
    fwd = pltpu.roll(y, LANES - quarter, axis=1)
    bwd = pltpu.roll(y, quarter, axis=1)
    return y * cos + jnp.where(lo_mask, fwd, bwd) * sin_signed


def _lo_mask(shape, quarter):
    lane = lax.broadcasted_iota(jnp.int32, shape, 1)
    return (lane % (2 * quarter)) < quarter


def _q_spec(n, tm, n_tiles, n_ctx_tiles, rotate):
    if not rotate:
        return pl.BlockSpec((1, tm, n), lambda bi, i: (bi, i, 0))
    n_lat = n_tiles - n_ctx_tiles
    return pl.BlockSpec((1, tm, n), lambda bi, i: (bi, jnp.where(i < n_ctx_tiles, i + n_lat, i - n_ctx_tiles), 0))


def _vt_spec(rows, tm):
    per = KV_CHUNK // tm
    return pl.BlockSpec((1, 1, rows, tm), lambda bi, i: (bi, i // per, 0, i % per))


def _proj_kernel(x_ref, mod_ref, g_ref, wq_ref, wk_ref, wv_ref, qg_ref, kg_ref, cosq_ref, sinq_ref, cosk_ref, sink_ref,
                 vones_ref, q_ref, k_ref, v_ref, *, qk_norm, rope, head_dim, v_transposed):
    h = _norm_mod(x_ref[0], g_ref[...], mod_ref[0, 0:1, :], mod_ref[0, 1:2, :]).astype(BF16)
    tm = h.shape[0]
    lo = _lo_mask((tm, LANES), head_dim // 4) if rope else None

    def finish(w_ref, o_ref, gain_ref, cos_ref, sin_ref):
        n = w_ref.shape[1]
        for c0 in range(0, n, 2 * LANES):
            c1 = min(c0 + 2 * LANES, n)
            y2 = jnp.dot(h, w_ref[:, c0:c1], preferred_element_type=F32)
            for s0 in range(0, c1 - c0, LANES):
                y = y2[:, s0:s0 + LANES]
                if qk_norm:
                    ms = jnp.sum(y * y, axis=-1, keepdims=True) * (1.0 / head_dim)
                    y = y * lax.rsqrt(ms + EPS) * gain_ref[...]
                if rope:
                    y = _rope_slot(y, cos_ref[...], sin_ref[...], lo, head_dim // 4)
                o_ref[0, :, c0 + s0:c0 + s0 + LANES] = y.astype(BF16)

    finish(wq_ref, q_ref, qg_ref, cosq_ref, sinq_ref)
    finish(wk_ref, k_ref, kg_ref, cosk_ref, sink_ref)
    if v_transposed:
        vt = lax.dot_general(wv_ref[...], h, NT_DIMS, preferred_element_type=F32) + vones_ref[...]
        v_ref[0, 0] = vt.astype(BF16)
    else:
        v_ref[0] = jnp.dot(h, wv_ref[...], preferred_element_type=F32).astype(BF16)


def _project(xs, mods, g, wq, wk, wv, qg, kg, cosq, sinq, cosk, sink, vones, *, qk_norm, rope, head_dim, v_transposed,
             q_rotated, n_ctx_tiles, name):
    b, s, d = xs.shape
    tm = TOKEN_TILE
    nq, nk = wq.shape[1], wk.shape[1]
    tok = lambda n: pl.BlockSpec((1, tm, n), lambda bi, i: (bi, i, 0))
    tab = pl.BlockSpec((tm, LANES), lambda bi, i: (i, 0))
    if v_transposed:
        v_spec = _vt_spec(wv.shape[0], tm)
        v_shape = jax.ShapeDtypeStruct((b, s // KV_CHUNK, wv.shape[0], KV_CHUNK), BF16)
    else:
        v_spec, v_shape = tok(wv.shape[1]), jax.ShapeDtypeStruct((b, s, wv.shape[1]), BF16)
    return pl.pallas_call(
        functools.partial(_proj_kernel, qk_norm=qk_norm, rope=rope, head_dim=head_dim, v_transposed=v_transposed),
        grid=(b, s // tm),
        in_specs=[
            tok(d),
            pl.BlockSpec((1, 6, d), lambda bi, i: (jnp.where(i < n_ctx_tiles, b, bi), 0, 0)),
            _resident((1, d)),
            _resident(wq.shape), _resident(wk.shape), _resident(wv.shape),
            _resident((1, LANES)), _resident((1, LANES)),
            tab, tab, tab, tab,
            _resident(vones.shape),
        ],
        out_specs=[_q_spec(nq, tm, s // tm, n_ctx_tiles, q_rotated), tok(nk), v_spec],
        out_shape=[jax.ShapeDtypeStruct((b, s, nq), BF16), jax.ShapeDtypeStruct((b, s, nk), BF16), v_shape],
        compiler_params=_cparams(2),
        name=name,
    )(xs, mods, g, wq, wk, wv, qg, kg, cosq, sinq, cosk, sink, vones)


def _mla_proj_kernel(x_ref, mod_ref, g_ref, win_ref, qn_ref, kvn_ref, wuq_ref, wuk_ref, wuvt_ref,
                     cosq_ref, sinq_ref, cosk_ref, sink_ref, vones_ref, q_ref, k_ref, v_ref):
    h = _norm_mod(x_ref[0], g_ref[...], mod_ref[0, 0:1, :], mod_ref[0, 1:2, :]).astype(BF16)
    tm = h.shape[0]
    t1 = jnp.dot(h, win_ref[...], preferred_element_type=F32)
    cq = t1[:, :MLA_Q_LORA]
    ckv = t1[:, MLA_Q_LORA:MLA_Q_LORA + MLA_KV_LORA]
    kr = t1[:, MLA_Q_LORA + MLA_KV_LORA:]
    cqn = (cq * lax.rsqrt(jnp.mean(cq * cq, axis=-1, keepdims=True) + EPS) * qn_ref[...]).astype(BF16)
    ckvn = (ckv * lax.rsqrt(jnp.mean(ckv * ckv, axis=-1, keepdims=True) + EPS) * kvn_ref[...]).astype(BF16)
    quarter = MLA_ROPE_DIM // 4
    lo = _lo_mask((tm, LANES), quarter)
    kr = _rope_slot(kr, cosk_ref[...], sink_ref[...], lo, quarter)
    n = wuq_ref.shape[1]
    for c0 in range(0, n, 2 * LANES):
        q2 = jnp.dot(cqn, wuq_ref[:, c0:c0 + 2 * LANES], preferred_element_type=F32)
        k2 = jnp.dot(ckvn, wuk_ref[:, c0:c0 + 2 * LANES], preferred_element_type=F32)
        for s0 in (0, LANES):
            q_ref[0, :, c0 + s0:c0 + s0 + LANES] = _rope_slot(
                q2[:, s0:s0 + LANES], cosq_ref[...], sinq_ref[...], lo, quarter).astype(BF16)
            k_ref[0, :, c0 + s0:c0 + s0 + LANES] = (k2[:, s0:s0 + LANES] + kr).astype(BF16)
    vt = lax.dot_general(wuvt_ref[...], ckvn, NT_DIMS, preferred_element_type=F32) + vones_ref[...]
    v_ref[0, 0] = vt.astype(BF16)


def _project_mla(xs, mods, g, win, qn, kvn, wuq, wuk, wuvt, cosq, sinq, cosk, sink, vones, *, n_ctx_tiles):
    b, s, d = xs.shape
    tm = TOKEN_TILE
    nq, nk = wuq.shape[1], wuk.shape[1]
    tok = lambda n: pl.BlockSpec((1, tm, n), lambda bi, i: (bi, i, 0))
    tab = pl.BlockSpec((tm, LANES), lambda bi, i: (i, 0))
    return pl.pallas_call(
        _mla_proj_kernel,
        grid=(b, s // tm),
        in_specs=[
            tok(d),
            pl.BlockSpec((1, 6, d), lambda bi, i: (jnp.where(i < n_ctx_tiles, b, bi), 0, 0)),
            _resident((1, d)),
            _resident(win.shape), _resident(qn.shape), _resident(kvn.shape),
            _resident(wuq.shape), _resident(wuk.shape), _resident(wuvt.shape),
            tab, tab, tab, tab,
            _resident(vones.shape),
        ],
        out_specs=[_q_spec(nq, tm, s // tm, n_ctx_tiles, True), tok(nk), _vt_spec(wuvt.shape[0], tm)],
        out_shape=[jax.ShapeDtypeStruct((b, s, nq), BF16), jax.ShapeDtypeStruct((b, s, nk), BF16),
                   jax.ShapeDtypeStruct((b, s // KV_CHUNK, wuvt.shape[0], KV_CHUNK), BF16)],
        compiler_params=_cparams(2),
        name="proj_mla",
    )(xs, mods, g, win, qn, kvn, wuq, wuk, wuvt, cosq, sinq, cosk, sink, vones)


def _flash_kernel(q_ref, k_ref, vt_ref, lam_ref, subg_ref, o_ref, *scratch, mode, k_cols, v_rows, vw, diff_scale,
                  ctx_keys):
    tq = q_ref.shape[1]
    n_chunks = vt_ref.shape[1]
    qs = [q_ref[0, :, a * LANES:(a + 1) * LANES] for a in (0, 1)]
    n_rows = v_rows[0][1] - v_rows[0][0]

    def keys(key_rows):
        k0 = k_ref[0, key_rows, k_cols[0][0]:k_cols[0][1]]
        k1 = k0 if k_cols[1] == k_cols[0] else k_ref[0, key_rows, k_cols[1][0]:k_cols[1][1]]
        return k0, k1

    def values(chunk_idx, key_lanes):
        v0 = vt_ref[0, chunk_idx, v_rows[0][0]:v_rows[0][1], key_lanes]
        v1 = v0 if v_rows[1] == v_rows[0] else vt_ref[0, chunk_idx, v_rows[1][0]:v_rows[1][1], key_lanes]
        return v0, v1

    def update(st, cmax, vt_a, carry_a):
        m, acc = carry_a
        m_new = jnp.maximum(m, cmax)
        p = jnp.exp2(st - m_new).astype(BF16)
        return m_new, jnp.exp2(m - m_new) * acc + jnp.dot(vt_a, p, preferred_element_type=F32)

    init = tuple((jnp.full((1, tq), -jnp.inf, F32), jnp.zeros((n_rows, tq), F32)) for _ in (0, 1))

    def finish(carry):
        (_, acc_a), (_, acc_b) = carry
        oa = acc_a[0:vw] / acc_a[vw:vw + 1]
        ob = acc_b[0:vw] / acc_b[vw:vw + 1]
        if mode == "pair":
            out = jnp.concatenate([oa, ob], axis=0).T
        else:
            lam = lam_ref[...]
            lam_full = (jnp.exp(jnp.sum(lam[0:1] * lam[1:2], axis=-1, keepdims=True))
                        - jnp.exp(jnp.sum(lam[2:3] * lam[3:4], axis=-1, keepdims=True))
                        + (1.0 - diff_scale))
            o = (oa - lam_full * ob).T
            out = o * lax.rsqrt(jnp.mean(o * o, axis=-1, keepdims=True) + EPS) * subg_ref[...] * diff_scale
        o_ref[0] = out.astype(BF16)

    if ctx_keys:
        ks, vts = keys(slice(0, ctx_keys)), values(0, slice(0, ctx_keys))
        out = []
        for a in (0, 1):
            st = lax.dot_general(ks[a], qs[a], NT_DIMS, preferred_element_type=F32)
            out.append(update(st, jnp.max(st, axis=0, keepdims=True), vts[a], init[a]))
        finish(tuple(out))
        return

    (st_scr,) = scratch

    def chunk_keys(chunk_idx):
        start = chunk_idx * KV_CHUNK
        return keys(pl.ds(start if isinstance(start, int) else pl.multiple_of(start, KV_CHUNK), KV_CHUNK))

    def score(slot, a, ks):
        st = lax.dot_general(ks[a], qs[a], NT_DIMS, preferred_element_type=F32)
        st_scr[slot, a] = st
        return jnp.max(st, axis=0, keepdims=True)

    def scores(slot, chunk_idx):
        ks = chunk_keys(chunk_idx)
        return tuple(score(slot, a, ks) for a in (0, 1))

    def run(first, count, cmax, carry):
        for u in range(count):
            nxt, slot = first + u + 1, u % 2
            has_next = isinstance(nxt, jax.Array) or nxt < n_chunks
            m_new = [jnp.maximum(carry[a][0], cmax[a]) for a in (0, 1)]
            acc = [jnp.exp2(carry[a][0] - m_new[a]) * carry[a][1] for a in (0, 1)]
            cmax_next = [None, None]
            for j in range(KV_CHUNK // SUB_KEYS):
                rows = slice(j * SUB_KEYS, (j + 1) * SUB_KEYS)
                if has_next:
                    start = nxt * KV_CHUNK + j * SUB_KEYS
                    ks = keys(pl.ds(start if isinstance(start, int) else pl.multiple_of(start, SUB_KEYS), SUB_KEYS))
                    for a in (0, 1):
                        st = lax.dot_general(ks[a], qs[a], NT_DIMS, preferred_element_type=F32)
                        st_scr[1 - slot, a, rows, :] = st
                        mx = jnp.max(st, axis=0, keepdims=True)
                        cmax_next[a] = mx if cmax_next[a] is None else jnp.maximum(cmax_next[a], mx)
                vts = values(first + u, rows)
                for a in (0, 1):
                    p = jnp.exp2(st_scr[slot, a, rows, :] - m_new[a]).astype(BF16)
                    acc[a] = acc[a] + jnp.dot(vts[a], p, preferred_element_type=F32)
            carry = tuple((m_new[a], acc[a]) for a in (0, 1))
            cmax = tuple(cmax_next)
        return cmax, carry

    n_loop = (n_chunks - 1) // LOOP_CHUNKS
    cmax, carry = lax.fori_loop(0, n_loop, lambda i, st: run(i * LOOP_CHUNKS, LOOP_CHUNKS, *st),
                                (scores(0, 0), init))
    _, carry = run(n_loop * LOOP_CHUNKS, n_chunks - n_loop * LOOP_CHUNKS, cmax, carry)
    finish(carry)


def _flash(q, k, vt, lam, subg, *, mode, kw, k_cols, v_block_rows, v_rows, vw, k_index, v_index, n_ctx, diff_scale,
           name):
    b, s, nq = q.shape
    t = s - n_ctx
    n_pairs = nq // (2 * LANES)
    n_chunks = s // KV_CHUNK
    assert s % KV_CHUNK == 0 and n_ctx <= KV_CHUNK and t % FLASH_TQ == 0 and t % n_ctx == 0
    assert vt.shape[1] == n_chunks
    common = dict(mode=mode, k_cols=k_cols, v_rows=v_rows, vw=vw, diff_scale=diff_scale)
    o_x = pl.pallas_call(
        functools.partial(_flash_kernel, ctx_keys=0, **common),
        grid=(b, n_pairs, t // FLASH_TQ),
        in_specs=[
            pl.BlockSpec((1, FLASH_TQ, 2 * LANES), lambda bi, p, i: (bi, i, p)),
            pl.BlockSpec((1, s, kw), lambda bi, p, i: (bi, 0, k_index(p))),
            pl.BlockSpec((1, n_chunks, v_block_rows, KV_CHUNK), lambda bi, p, i: (bi, 0, v_index(p), 0)),
            _resident(lam.shape), _resident(subg.shape),
        ],
        out_specs=pl.BlockSpec((1, FLASH_TQ, LANES), lambda bi, p, i: (bi, i, p)),
        out_shape=jax.ShapeDtypeStruct((b, t, n_pairs * LANES), BF16),
        scratch_shapes=[pltpu.VMEM((2, 2, KV_CHUNK, FLASH_TQ), F32)],
        compiler_params=_cparams(3),
        name=name,
    )(q, k, vt, lam, subg)
    o_c = pl.pallas_call(
        functools.partial(_flash_kernel, ctx_keys=n_ctx, **common),
        grid=(b, n_pairs),
        in_specs=[
            pl.BlockSpec((1, n_ctx, 2 * LANES), lambda bi, p: (bi, t // n_ctx, p)),
            pl.BlockSpec((1, n_ctx, kw), lambda bi, p: (bi, 0, k_index(p))),
            pl.BlockSpec((1, 1, v_block_rows, KV_CHUNK), lambda bi, p: (bi, 0, v_index(p), 0)),
            _resident(lam.shape), _resident(subg.shape),
        ],
        out_specs=pl.BlockSpec((1, n_ctx, LANES), lambda bi, p: (bi, 0, p)),
        out_shape=jax.ShapeDtypeStruct((b, n_ctx, n_pairs * LANES), BF16),
        compiler_params=_cparams(2),
        name=name + "_ctx",
    )(q, k, vt, lam, subg)
    return o_x, o_c


def _na_tables(t):
    rows = t // GRID_W
    win_rows = min(NA_ROWS, rows)
    band_rows = min(win_rows + 1, rows)
    assert rows >= NA_WIN_ROWS and NA_QBLOCK % GRID_W == 0
    qrows = NA_QBLOCK // GRID_W
    nb = t // NA_QBLOCK
    nwin = NA_WIN_ROWS * GRID_W
    q_off, k_off = np.arange(NA_QBLOCK), np.arange(nwin)
    q_dr, q_col = q_off // GRID_W, q_off % GRID_W
    k_dr, k_col = k_off // GRID_W, k_off % GRID_W
    col_start = np.clip(q_col - NA_COLS // 2, 0, GRID_W - NA_COLS)
    col_in = (k_col[None, :] >= col_start[:, None]) & (k_col[None, :] < col_start[:, None] + NA_COLS)
    dc_idx = np.clip(k_col[None, :] - q_col[:, None] + NA_COLS - 1, 0, 2 * NA_COLS - 2)
    patterns, var, wstart = {}, [], []
    for j in range(nb):
        r = j * qrows + q_dr
        row_start = np.clip(r - win_rows // 2, 0, rows - win_rows)
        b0 = min(int(row_start[0]), rows - band_rows)
        w0 = min(b0 - b0 % 2, rows - NA_WIN_ROWS)
        assert w0 % qrows == 0 and w0 <= b0 and b0 + band_rows <= w0 + NA_WIN_ROWS
        k_row = w0 + k_dr
        in_band = (k_row >= b0) & (k_row < b0 + band_rows)
        in_win = (col_in & in_band[None, :] & (k_row[None, :] >= row_start[:, None])
                  & (k_row[None, :] < row_start[:, None] + win_rows))
        dr_idx = np.clip(k_row[None, :] - r[:, None] + NA_ROWS - 1, 0, 2 * NA_ROWS - 2)
        idx = np.where(in_win, dr_idx * (2 * NA_COLS - 1) + dc_idx, -1).astype(np.int32)
        key = idx.tobytes()
        if key not in patterns:
            patterns[key] = (len(patterns), idx)
        var.append(patterns[key][0])
        wstart.append(w0 * GRID_W)
    idx_all = np.stack([p[1] for p in sorted(patterns.values(), key=lambda p: p[0])])
    return idx_all.reshape(len(patterns), 1, -1), np.asarray(var, np.int32), np.asarray(wstart, np.int32)


def _na_bias_kernel(idx_ref, rpb_ref, o_ref):
    idx = idx_ref[0]
    r = rpb_ref[...]
    onehot = (lax.broadcasted_iota(jnp.int32, (r.shape[1], idx.shape[1]), 0) == idx).astype(BF16)
    b1 = r.astype(BF16)
    r2 = r - b1.astype(F32)
    b2 = r2.astype(BF16)
    b3 = (r2 - b2.astype(F32)).astype(BF16)
    out = (jnp.dot(b1, onehot, preferred_element_type=F32) + jnp.dot(b2, onehot, preferred_element_type=F32)
           + jnp.dot(b3, onehot, preferred_element_type=F32))
    o_ref[0] = jnp.where(idx >= 0, out, NEG_INF)


def _na_bias(idx_all, rpb):
    nvar, _, npos = idx_all.shape
    heads = rpb.shape[0]
    nr = (2 * NA_ROWS - 1) * (2 * NA_COLS - 1)
    nr_pad = pl.cdiv(nr, LANES) * LANES
    rflat = jnp.pad(rpb.reshape(heads, nr), ((0, 0), (0, nr_pad - nr)))
    tn = 2048
    assert npos % tn == 0
    out = pl.pallas_call(
        _na_bias_kernel,
        grid=(nvar, npos // tn),
        in_specs=[pl.BlockSpec((1, 1, tn), lambda v, j: (v, 0, j)), _resident(rflat.shape)],
        out_specs=pl.BlockSpec((1, heads, tn), lambda v, j: (v, 0, j)),
        out_shape=jax.ShapeDtypeStruct((nvar, heads, npos), F32),
        compiler_params=_cparams(2),
        name="na_bias",
    )(jnp.asarray(idx_all), rflat)
    return out.reshape(nvar, heads, NA_QBLOCK, npos // NA_QBLOCK)


def _halves(lo_src, hi_src):
    lane = lax.broadcasted_iota(jnp.int32, lo_src.shape, 1)
    return jnp.where(lane < LANES // 2, lo_src, hi_src)


def _na_kernel(var_ref, ws_ref, q_ref, k_ref, v_ref, b0_ref, b1_ref, o_ref, *, n_ctx, n_ctx_tiles):
    qi = pl.program_id(2)
    nwin = NA_WIN_ROWS * GRID_W

    @pl.when(qi < n_ctx_tiles)
    def _():
        kc, vc = k_ref[0, 0:n_ctx, :], v_ref[0, 0:n_ctx, :]
        outs = []
        for a in (0, 1):
            s_c = lax.dot_general(q_ref[0, :, a * LANES:(a + 1) * LANES], kc, NT_DIMS,
                                  preferred_element_type=F32)
            p_c = jnp.exp(s_c - jnp.max(s_c, axis=-1, keepdims=True))
            acc = jnp.dot(p_c.astype(BF16), vc, preferred_element_type=F32)
            outs.append(acc / jnp.sum(p_c, axis=-1, keepdims=True))
        o_ref[0] = _halves(outs[0], outs[1]).astype(BF16)

    @pl.when(qi >= n_ctx_tiles)
    def _():
        kc, vc = k_ref[0, 0:n_ctx, :], v_ref[0, 0:n_ctx, :]
        work = []
        for sub, bias_ref in ((0, b0_ref), (1, b1_ref)):
            j = (qi - n_ctx_tiles) * (TOKEN_TILE // NA_QBLOCK) + sub
            start = pl.multiple_of(n_ctx + ws_ref[j], NA_QBLOCK)
            kw, vw = k_ref[0, pl.ds(start, nwin), :], v_ref[0, pl.ds(start, nwin), :]
            for a in (0, 1):
                q = q_ref[0, sub * NA_QBLOCK:(sub + 1) * NA_QBLOCK, a * LANES:(a + 1) * LANES]
                s_c = lax.dot_general(q, kc, NT_DIMS, preferred_element_type=F32)
                s_w = lax.dot_general(q, kw, NT_DIMS, preferred_element_type=F32) + bias_ref[0, a]
                work.append((s_c, s_w, vw))
        outs = []
        for s_c, s_w, vw in work:
            m = jnp.maximum(jnp.max(s_c, axis=-1, keepdims=True), jnp.max(s_w, axis=-1, keepdims=True))
            p_c, p_w = jnp.exp(s_c - m), jnp.exp(s_w - m)
            l = jnp.sum(p_c, axis=-1, keepdims=True) + jnp.sum(p_w, axis=-1, keepdims=True)
            acc = (jnp.dot(p_c.astype(BF16), vc, preferred_element_type=F32)
                   + jnp.dot(p_w.astype(BF16), vw, preferred_element_type=F32))
            outs.append(acc / l)
        for sub in (0, 1):
            o_ref[0, sub * NA_QBLOCK:(sub + 1) * NA_QBLOCK, :] = _halves(outs[2 * sub], outs[2 * sub + 1]).astype(BF16)


def _na_attention(q, k, v, bias, var, wstart, *, n_ctx):
    b, s, nq = q.shape
    tq = TOKEN_TILE
    n_pairs = nq // (2 * LANES)
    n_ctx_tiles = n_ctx // tq
    per_tile = tq // NA_QBLOCK
    nwin = NA_WIN_ROWS * GRID_W

    def bias_spec(sub):
        def index(bi, p, i, var_ref, ws_ref):
            j = jnp.maximum(i - n_ctx_tiles, 0) * per_tile + sub
            return (var_ref[j], p, 0, 0)
        return pl.BlockSpec((1, 2, NA_QBLOCK, nwin), index)

    grid_spec = pltpu.PrefetchScalarGridSpec(
        num_scalar_prefetch=2,
        grid=(b, n_pairs, s // tq),
        in_specs=[
            pl.BlockSpec((1, tq, 2 * LANES), lambda bi, p, i, *_: (bi, i, p)),
            pl.BlockSpec((1, s, LANES), lambda bi, p, i, *_: (bi, 0, p)),
            pl.BlockSpec((1, s, LANES), lambda bi, p, i, *_: (bi, 0, p)),
            bias_spec(0), bias_spec(1),
        ],
        out_specs=pl.BlockSpec((1, tq, LANES), lambda bi, p, i, *_: (bi, i, p)),
    )
    return pl.pallas_call(
        functools.partial(_na_kernel, n_ctx=n_ctx, n_ctx_tiles=n_ctx_tiles),
        grid_spec=grid_spec,
        out_shape=jax.ShapeDtypeStruct((b, s, n_pairs * LANES), BF16),
        compiler_params=_cparams(3),
        name="na_attn",
    )(var, wstart, q, k, v, bias, bias)


def _outproj_kernel(x_ref, ox_ref, oc_ref, w_ref, mod_ref, y_ref, *, n_ctx_tiles, first_tile):
    o = jnp.where(pl.program_id(1) + first_tile < n_ctx_tiles, oc_ref[0], ox_ref[0])
    y = jnp.dot(o, w_ref[...], preferred_element_type=F32)
    y_ref[0] = x_ref[0] + mod_ref[0, 2:3, :] * y


def _outproj(xs, o_x, o_c, w, mods, *, n_ctx_tiles, x_offset, latent_only):
    b, s, d = xs.shape
    tm = TOKEN_TILE
    n = o_x.shape[2]
    first = n_ctx_tiles if latent_only else 0
    return pl.pallas_call(
        functools.partial(_outproj_kernel, n_ctx_tiles=n_ctx_tiles, first_tile=first),
        grid=(b, s // tm - first),
        in_specs=[
            pl.BlockSpec((1, tm, d), lambda bi, i: (bi, i + first, 0)),
            pl.BlockSpec((1, tm, n), lambda bi, i: (bi, jnp.maximum(i + first, n_ctx_tiles) - x_offset, 0)),
            pl.BlockSpec((1, tm, n), lambda bi, i: (bi, jnp.minimum(i + first, n_ctx_tiles - 1), 0)),
            _resident(w.shape),
            pl.BlockSpec((1, 6, d), lambda bi, i: (jnp.where(i + first < n_ctx_tiles, b, bi), 0, 0)),
        ],
        out_specs=pl.BlockSpec((1, tm, d), lambda bi, i: (bi, i, 0)),
        out_shape=jax.ShapeDtypeStruct((b, s - first * tm, d), F32),
        compiler_params=_cparams(2),
        name="outproj",
    )(xs, o_x, o_c, w, mods)


def _ffn_kernel(xp_ref, x_ref, xn_ref, mod_ref, g_ref, wup_ref, cw_ref, cb_ref, wdn_ref, fg_ref, y_ref, u_scr, a_scr,
                *, n_ctx_tiles, n_tiles, chunk, final_norm):
    i = pl.program_id(1)
    tm = x_ref.shape[1]
    has_prev = jnp.logical_and(i != 0, i != n_ctx_tiles).astype(F32)
    has_next = jnp.logical_and(i != n_ctx_tiles - 1, i != n_tiles - 1).astype(F32)
    g, shift, scale = g_ref[...], mod_ref[0, 3:4, :], mod_ref[0, 4:5, :]
    x = x_ref[0]
    h = jnp.concatenate([_norm_mod(xp_ref[0], g, shift, scale) * has_prev,
                         _norm_mod(x, g, shift, scale),
                         _norm_mod(xn_ref[0], g, shift, scale) * has_next], axis=0).astype(BF16)
    n_chunks = wup_ref.shape[1] // (2 * chunk)
    for j in range(n_chunks):
        cols = slice(j * 2 * chunk, (j + 1) * 2 * chunk)
        u_scr[...] = jnp.dot(h, wup_ref[:, cols], preferred_element_type=F32)
        cw = cw_ref[:, cols]
        u = (u_scr[HALO - 1:HALO - 1 + tm, :] * cw[0:1] + u_scr[HALO:HALO + tm, :] * cw[1:2]
             + u_scr[HALO + 1:HALO + 1 + tm, :] * cw[2:3] + cb_ref[:, cols])
        val, gate = u[:, :chunk], u[:, chunk:]
        a_scr[:, j * chunk:(j + 1) * chunk] = (gate * (1.0 / (1.0 + jnp.exp(-gate))) * val).astype(BF16)
    y = x + mod_ref[0, 5:6, :] * jnp.dot(a_scr[...], wdn_ref[...], preferred_element_type=F32)
    if final_norm:
        y = y * lax.rsqrt(jnp.mean(y * y, axis=-1, keepdims=True) + EPS) * fg_ref[...]
    y_ref[0] = y


def _ffn(xs, mods, g, wup, cw, cb, wdn, final_g, *, n_ctx_tiles, chunk, final_norm):
    b, s, d = xs.shape
    tm = TOKEN_TILE
    n_tiles = s // tm
    per = tm // HALO
    f = wdn.shape[0]
    return pl.pallas_call(
        functools.partial(_ffn_kernel, n_ctx_tiles=n_ctx_tiles, n_tiles=n_tiles, chunk=chunk,
                          final_norm=final_norm),
        grid=(b, n_tiles),
        in_specs=[
            pl.BlockSpec((1, HALO, d), lambda bi, i: (bi, jnp.maximum(i * per - 1, 0), 0)),
            pl.BlockSpec((1, tm, d), lambda bi, i: (bi, i, 0)),
            pl.BlockSpec((1, HALO, d), lambda bi, i: (bi, jnp.minimum((i + 1) * per, s // HALO - 1), 0)),
            pl.BlockSpec((1, 6, d), lambda bi, i: (jnp.where(i < n_ctx_tiles, b, bi), 0, 0)),
            _resident((1, d)),
            pl.BlockSpec(wup.shape, lambda bi, i: (0, 0), pipeline_mode=pl.Buffered(1)),
            _resident(cw.shape), _resident(cb.shape),
            pl.BlockSpec(wdn.shape, lambda bi, i: (0, 0), pipeline_mode=pl.Buffered(1)),
            _resident((1, d)),
        ],
        out_specs=pl.BlockSpec((1, tm, d), lambda bi, i: (bi, i, 0)),
        out_shape=jax.ShapeDtypeStruct((b, s, d), F32),
        scratch_shapes=[pltpu.VMEM((tm + 2 * HALO, 2 * chunk), F32), pltpu.VMEM((tm, f), BF16)],
        compiler_params=_cparams(2),
        name="ffn",
    )(xs, xs, xs, mods, g, wup, cw, cb, wdn, final_g)


def _rope_table(t, n_ctx, rot_dim, lane_lo):
    tt = jnp.arange(t)
    row = (tt // GRID_W).astype(F32)
    col = (tt % GRID_W).astype(F32)
    half = rot_dim // 2
    inv = ROPE_BASE ** (-jnp.arange(0, half, 2, dtype=F32) / half)
    ar, ac = row[:, None] * inv, col[:, None] * inv
    ang = jnp.concatenate([ar, ar, ac, ac], axis=-1)
    quarter = rot_dim // 4
    sign = jnp.where((jnp.arange(rot_dim) % (2 * quarter)) < quarter, -1.0, 1.0).astype(F32)
    cos, sin = jnp.cos(ang), jnp.sin(ang) * sign
    reps = (LANES - lane_lo) // rot_dim if lane_lo == 0 else 1
    cos_l = jnp.ones((t, LANES), F32).at[:, lane_lo:lane_lo + reps * rot_dim].set(jnp.tile(cos, (1, reps)))
    sin_l = jnp.zeros((t, LANES), F32).at[:, lane_lo:lane_lo + reps * rot_dim].set(jnp.tile(sin, (1, reps)))
    cos_s = jnp.concatenate([jnp.ones((n_ctx, LANES), F32), cos_l], axis=0)
    sin_s = jnp.concatenate([jnp.zeros((n_ctx, LANES), F32), sin_l], axis=0)
    return cos_s, sin_s


def _to_slots(w, n_heads, width, halves):
    d = w.shape[0]
    wh = w.reshape(d, n_heads, width)
    lo = jnp.pad(wh, ((0, 0), (0, 0), (0, LANES - width)))
    if not any(halves):
        return lo.reshape(d, n_heads * LANES)
    hi = jnp.pad(wh, ((0, 0), (0, 0), (LANES // 2, LANES // 2 - width)))
    in_hi = jnp.asarray(halves, jnp.int32).reshape(1, n_heads, 1) == 1
    return jnp.where(in_hi, hi, lo).reshape(d, n_heads * LANES)


def _vt_weights(wv, groups, width, tm):
    d = wv.shape[0]
    rows = width + 16
    wt = jnp.zeros((groups, rows, d), wv.dtype).at[:, :width, :].set(wv.T.reshape(groups, width, d))
    ones = jnp.zeros((groups, rows, tm), F32).at[:, width, :].set(1.0)
    return wt.reshape(groups * rows, d), ones.reshape(groups * rows, tm)


def _dup(g):
    return jnp.concatenate([g, g]).reshape(1, LANES).astype(F32)


def kernel(x, c, ctx, c_ctx, ada_w, ada_b, norm1_g, norm2_g, ffn_w_up, ffn_conv_w, ffn_conv_b, ffn_w_down,
           gqa_w_in, gqa_q_norm_g, gqa_k_norm_g, gqa_w_out,
           mla_w_in, mla_q_norm_g, mla_kv_norm_g, mla_w_uq, mla_w_ukv, mla_w_out,
           diff_w_in, diff_lambda, diff_subln_g, diff_w_out,
           na_w_in, na_rpb, na_w_out, final_norm_g):
    b, t, d = x.shape
    n_ctx = ctx.shape[1]
    depth = ada_w.shape[0]
    assert n_ctx % TOKEN_TILE == 0 and t % TOKEN_TILE == 0 and b + 1 <= 8
    nct = n_ctx // TOKEN_TILE
    f = ffn_w_down.shape[1]
    chunk = LANES
    assert f % chunk == 0

    xs = jnp.concatenate([ctx, x], axis=1)
    cin = jnp.zeros((8, d), F32).at[:b].set(c).at[b].set(c_ctx)
    mods_all = _ada_all(cin, ada_w, ada_b).reshape(depth, 8, 6, d)

    tm = TOKEN_TILE
    ones_l = jnp.ones((1, LANES), F32)
    dummy_tab = jnp.zeros((xs.shape[1], LANES), F32)
    dummy_ones = jnp.zeros((8, LANES), F32)
    cos64, sin64 = _rope_table(t, n_ctx, GQA_HEAD_DIM, 0)
    cos_mla, sin_mla = _rope_table(t, n_ctx, MLA_ROPE_DIM, MLA_NOPE_DIM)
    no_lam, no_g = jnp.zeros((4, DIFF_HEAD_DIM), F32), jnp.ones((1, LANES), F32)
    pair_rows = MLA_V_DIM + 16
    diff_rows = 2 * DIFF_HEAD_DIM + 16

    for i in range(depth):
        m, j = i % N_MIXERS, i // N_MIXERS
        mods = mods_all[i]
        g1 = norm1_g[i].reshape(1, d)
        if m == 0:
            w = gqa_w_in[j].astype(BF16)
            nqc, nkc = GQA_HEADS * GQA_HEAD_DIM, GQA_KV_HEADS * GQA_HEAD_DIM
            wq = _to_slots(w[:, :nqc], GQA_HEADS, GQA_HEAD_DIM, [0] * GQA_HEADS)
            wk = _to_slots(w[:, nqc:nqc + nkc], GQA_KV_HEADS, GQA_HEAD_DIM, [0] * GQA_KV_HEADS)
            wvt, vones = _vt_weights(w[:, nqc + nkc:], GQA_KV_HEADS, GQA_HEAD_DIM, tm)
            qc = GQA_HEAD_DIM ** -0.5 * LOG2E
            q, k, vt = _project(xs, mods, g1, wq, wk, wvt, _dup(gqa_q_norm_g[j]), _dup(gqa_k_norm_g[j]),
                                cos64 * qc, sin64 * qc, cos64, sin64, vones, qk_norm=True, rope=True,
                                head_dim=GQA_HEAD_DIM, v_transposed=True, q_rotated=True, n_ctx_tiles=nct,
                                name="proj_gqa")
            o_x, o_c = _flash(q, k, vt, no_lam, no_g, mode="pair", kw=LANES, k_cols=((0, LANES), (0, LANES)),
                       v_block_rows=pair_rows, v_rows=((0, pair_rows), (0, pair_rows)), vw=GQA_HEAD_DIM,
                       k_index=lambda p: p // 2, v_index=lambda p: p // 2, n_ctx=n_ctx, diff_scale=1.0,
                       name="attn_gqa")
            w_out = gqa_w_out[j]
        elif m == 1:
            w = mla_w_in[j].astype(BF16)
            nlat = MLA_Q_LORA + MLA_KV_LORA
            kr_slot = jnp.zeros((d, LANES), BF16).at[:, MLA_NOPE_DIM:MLA_NOPE_DIM + MLA_ROPE_DIM].set(w[:, nlat:])
            win = jnp.concatenate([w[:, :nlat], kr_slot], axis=1)
            qk_dim = MLA_NOPE_DIM + MLA_ROPE_DIM
            wuq = _to_slots(mla_w_uq[j].astype(BF16), MLA_HEADS, qk_dim, [0] * MLA_HEADS)
            wukv = mla_w_ukv[j].astype(BF16).reshape(MLA_KV_LORA, MLA_HEADS, MLA_NOPE_DIM + MLA_V_DIM)
            wuk = _to_slots(wukv[:, :, :MLA_NOPE_DIM].reshape(MLA_KV_LORA, -1), MLA_HEADS, MLA_NOPE_DIM,
                            [0] * MLA_HEADS)
            wuvt, vones = _vt_weights(wukv[:, :, MLA_NOPE_DIM:].reshape(MLA_KV_LORA, MLA_HEADS * MLA_V_DIM),
                                      MLA_HEADS, MLA_V_DIM, tm)
            qc = qk_dim ** -0.5 * LOG2E
            q, k, vt = _project_mla(xs, mods, g1, win, mla_q_norm_g[j].reshape(1, -1),
                                    mla_kv_norm_g[j].reshape(1, -1), wuq, wuk, wuvt,
                                    cos_mla * qc, sin_mla * qc, cos_mla, sin_mla, vones, n_ctx_tiles=nct)
            o_x, o_c = _flash(q, k, vt, no_lam, no_g, mode="pair", kw=2 * LANES, k_cols=((0, LANES), (LANES, 2 * LANES)),
                       v_block_rows=2 * pair_rows, v_rows=((0, pair_rows), (pair_rows, 2 * pair_rows)),
                       vw=MLA_V_DIM, k_index=lambda p: p, v_index=lambda p: p, n_ctx=n_ctx, diff_scale=1.0,
                       name="attn_mla")
            w_out = mla_w_out[j]
        elif m == 2:
            w = diff_w_in[j].astype(BF16)
            wq = _to_slots(w[:, :d], 2 * DIFF_HEADS, DIFF_HEAD_DIM, [0, 1] * DIFF_HEADS)
            wvt, vones = _vt_weights(w[:, 2 * d:], DIFF_HEADS, 2 * DIFF_HEAD_DIM, tm)
            lambda_init = 0.8 - 0.6 * math.exp(-0.3 * i)
            qc = DIFF_HEAD_DIM ** -0.5 * LOG2E
            q, k, vt = _project(xs, mods, g1, wq, w[:, d:2 * d], wvt, ones_l, ones_l,
                                cos64 * qc, sin64 * qc, cos64, sin64, vones, qk_norm=False, rope=True,
                                head_dim=DIFF_HEAD_DIM, v_transposed=True, q_rotated=True, n_ctx_tiles=nct,
                                name="proj_diff")
            o_x, o_c = _flash(q, k, vt, diff_lambda[j], diff_subln_g[j].reshape(1, LANES), mode="diff", kw=LANES,
                       k_cols=((0, LANES), (0, LANES)), v_block_rows=diff_rows,
                       v_rows=((0, diff_rows), (0, diff_rows)), vw=2 * DIFF_HEAD_DIM,
                       k_index=lambda p: p, v_index=lambda p: p, n_ctx=n_ctx, diff_scale=1.0 - lambda_init,
                       name="attn_diff")
            w_out = diff_w_out[j]
        else:
            w = na_w_in[j].astype(BF16)
            na_scale = NA_HEAD_DIM ** -0.5
            assert math.frexp(na_scale)[0] == 0.5
            wq = _to_slots(w[:, :d] * jnp.asarray(na_scale, BF16), NA_HEADS, NA_HEAD_DIM, [0, 1] * (NA_HEADS // 2))
            q, k, v = _project(xs, mods, g1, wq, w[:, d:2 * d], w[:, 2 * d:], ones_l, ones_l,
                               dummy_tab, dummy_tab, dummy_tab, dummy_tab, dummy_ones, qk_norm=False, rope=False,
                               head_dim=NA_HEAD_DIM, v_transposed=False, q_rotated=False, n_ctx_tiles=nct,
                               name="proj_na")
            idx_all, var, wstart = _na_tables(t)
            bias = _na_bias(idx_all, na_rpb[j])
            o_x = o_c = _na_attention(q, k, v, bias, jnp.asarray(var), jnp.asarray(wstart), n_ctx=n_ctx)
            w_out = na_w_out[j]
        last = i == depth - 1
        xs = _outproj(xs, o_x, o_c, w_out.astype(BF16), mods, n_ctx_tiles=nct, x_offset=0 if m == 3 else nct,
                      latent_only=last)

        def regroup(a):
            lead = a.shape[:-1]
            return jnp.stack([a[..., :f].reshape(*lead, f // chunk, chunk),
                              a[..., f:].reshape(*lead, f // chunk, chunk)], axis=-2).reshape(*lead, 2 * f)
        xs = _ffn(xs, mods, norm2_g[i].reshape(1, d), regroup(ffn_w_up[i].astype(BF16)), regroup(ffn_conv_w[i]),
                  regroup(ffn_conv_b[i]).reshape(1, 2 * f), ffn_w_down[i].astype(BF16), final_norm_g.reshape(1, d),
                  n_ctx_tiles=0 if last else nct, chunk=chunk, final_norm=last)
    return xs
```

```python
import functools
import math

import numpy as np
import jax
import jax.numpy as jnp
from jax import lax
from jax.experimental import pallas as pl
from jax.experimental.pallas import tpu as pltpu

F32 = jnp.float32
BF16 = jnp.bfloat16

LANES = 128
TOKEN_TILE = 256
HALO = 8
KV_CHUNK = 768
SUB_KEYS = 256
FLASH_TQ = 512
LOOP_CHUNKS = 4
VMEM_LIMIT = 56 * 1024 * 1024

GRID_W = 64
ROPE_BASE = 10000.0
EPS = 1e-6
NEG_INF = -1e30
LOG2E = 1.4426950408889634

GQA_HEADS, GQA_KV_HEADS, GQA_HEAD_DIM = 16, 4, 64
MLA_HEADS, MLA_NOPE_DIM, MLA_ROPE_DIM, MLA_V_DIM = 16, 64, 32, 64
MLA_Q_LORA, MLA_KV_LORA = 384, 256
DIFF_HEADS, DIFF_HEAD_DIM = 8, 64
NA_HEADS, NA_HEAD_DIM, NA_ROWS, NA_COLS = 16, 64, 8, 16
NA_QBLOCK = 128
NA_WIN_ROWS = 10
N_MIXERS = 4

NT_DIMS = (((1,), (1,)), ((), ()))


def _cparams(n_axes):
    return pltpu.CompilerParams(dimension_semantics=("arbitrary",) * n_axes, vmem_limit_bytes=VMEM_LIMIT)


def _resident(shape):
    nd = len(shape)
    return pl.BlockSpec(shape, lambda *_: (0,) * nd)


def _ada_kernel(c_ref, w_ref, b_ref, o_ref):
    c = c_ref[...]
    s = c * (1.0 / (1.0 + jnp.exp(-c)))
    o_ref[0] = jnp.dot(s.astype(BF16), w_ref[0].astype(BF16), preferred_element_type=F32) + b_ref[0]


def _ada_all(cin, ada_w, ada_b):
    depth, d, n = ada_w.shape
    tn = n // 4
    return pl.pallas_call(
        _ada_kernel,
        grid=(depth, n // tn),
        in_specs=[
            pl.BlockSpec((8, d), lambda l, j: (0, 0)),
            pl.BlockSpec((1, d, tn), lambda l, j: (l, 0, j)),
            pl.BlockSpec((1, 1, tn), lambda l, j: (l, 0, j)),
        ],
        out_specs=pl.BlockSpec((1, 8, tn), lambda l, j: (l, 0, j)),
        out_shape=jax.ShapeDtypeStruct((depth, 8, n), F32),
        compiler_params=_cparams(2),
        name="ada",
    )(cin, ada_w, ada_b.reshape(depth, 1, n))


def _norm_mod(x, g, shift, scale):
    ms = jnp.mean(x * x, axis=-1, keepdims=True)
    return (x * lax.rsqrt(ms + EPS) * g) * (1.0 + scale) + shift


def _rope_slot(y, cos, sin_signed, lo_mask, quarter):
    fwd = pltpu.roll(y, LANES - quarter, axis=1)
    bwd = pltpu.roll(y, quarter, axis=1)
    return y * cos + jnp.where(lo_mask, fwd, bwd) * sin_signed


def _lo_mask(shape, quarter):
    lane = lax.broadcasted_iota(jnp.int32, shape, 1)
    return (lane % (2 * quarter)) < quarter


def _q_spec(n, tm, n_tiles, n_ctx_tiles, rotate):
    if not rotate:
        return pl.BlockSpec((1, tm, n), lambda bi, i: (bi, i, 0))
    n_lat = n_tiles - n_ctx_tiles
    return pl.BlockSpec((1, tm, n), lambda bi, i: (bi, jnp.where(i < n_ctx_tiles, i + n_lat, i - n_ctx_tiles), 0))


def _vt_spec(rows, tm):
    per = KV_CHUNK // tm
    return pl.BlockSpec((1, 1, rows, tm), lambda bi, i: (bi, i // per, 0, i % per))


def _proj_kernel(x_ref, mod_ref, g_ref, wq_ref, wk_ref, wv_ref, qg_ref, kg_ref, cosq_ref, sinq_ref, cosk_ref, sink_ref,
                 vones_ref, q_ref, k_ref, v_ref, *, qk_norm, rope, head_dim, k_norm_dim, v_transposed):
    h = _norm_mod(x_ref[0], g_ref[...], mod_ref[0, 0:1, :], mod_ref[0, 1:2, :]).astype(BF16)
    tm = h.shape[0]
    lo = _lo_mask((tm, LANES), head_dim // 4) if rope else None
    low_half = lax.broadcasted_iota(jnp.int32, (tm, LANES), 1) < LANES // 2

    def chunks(w_ref):
        n = w_ref.shape[1]
        for c0 in range(0, n, 2 * LANES):
            c1 = min(c0 + 2 * LANES, n)
            y2 = jnp.dot(h, w_ref[:, c0:c1], preferred_element_type=F32)
            for s0 in range(0, c1 - c0, LANES):
                yield (c0 + s0) // LANES, y2[:, s0:s0 + LANES]

    for c, y in chunks(wq_ref):
        if qk_norm:
            sq = y * y
            ms_lo = jnp.sum(jnp.where(low_half, sq, 0.0), axis=-1, keepdims=True) * (1.0 / head_dim)
            ms_hi = jnp.sum(jnp.where(low_half, 0.0, sq), axis=-1, keepdims=True) * (1.0 / head_dim)
            y = y * jnp.where(low_half, lax.rsqrt(ms_lo + EPS), lax.rsqrt(ms_hi + EPS)) * qg_ref[...]
        if rope:
            y = _rope_slot(y, cosq_ref[...], sinq_ref[...], lo, head_dim // 4)
        q_ref[0, :, 2 * c * LANES:(2 * c + 1) * LANES] = jnp.where(low_half, y, 0.0).astype(BF16)
        q_ref[0, :, (2 * c + 1) * LANES:(2 * c + 2) * LANES] = jnp.where(low_half, 0.0, y).astype(BF16)
    for c, y in chunks(wk_ref):
        if qk_norm:
            ms = jnp.sum(y * y, axis=-1, keepdims=True) * (1.0 / k_norm_dim)
            y = y * lax.rsqrt(ms + EPS) * kg_ref[...]
        if rope:
            y = _rope_slot(y, cosk_ref[...], sink_ref[...], lo, head_dim // 4)
        k_ref[0, :, c * LANES:(c + 1) * LANES] = y.astype(BF16)
    if v_transposed:
        vt = lax.dot_general(wv_ref[...], h, NT_DIMS, preferred_element_type=F32) + vones_ref[...]
        v_ref[0, 0] = vt.astype(BF16)
    else:
        v_ref[0] = jnp.dot(h, wv_ref[...], preferred_element_type=F32).astype(BF16)


def _project(xs, mods, g, wq, wk, wv, qg, kg, cosq, sinq, cosk, sink, vones, *, qk_norm, rope, head_dim, k_norm_dim,
             v_transposed, q_rotated, n_ctx_tiles, name):
    b, s, d = xs.shape
    tm = TOKEN_TILE
    nq, nk = 2 * wq.shape[1], wk.shape[1]
    tok = lambda n: pl.BlockSpec((1, tm, n), lambda bi, i: (bi, i, 0))
    tab = pl.BlockSpec((tm, LANES), lambda bi, i: (i, 0))
    if v_transposed:
        v_spec = _vt_spec(wv.shape[0], tm)
        v_shape = jax.ShapeDtypeStruct((b, s // KV_CHUNK, wv.shape[0], KV_CHUNK), BF16)
    else:
        v_spec, v_shape = tok(wv.shape[1]), jax.ShapeDtypeStruct((b, s, wv.shape[1]), BF16)
    return pl.pallas_call(
        functools.partial(_proj_kernel, qk_norm=qk_norm, rope=rope, head_dim=head_dim, k_norm_dim=k_norm_dim,
                          v_transposed=v_transposed),
        grid=(b, s // tm),
        in_specs=[
            tok(d),
            pl.BlockSpec((1, 6, d), lambda bi, i: (jnp.where(i < n_ctx_tiles, b, bi), 0, 0)),
            _resident((1, d)),
            _resident(wq.shape), _resident(wk.shape), _resident(wv.shape),
            _resident((1, LANES)), _resident((1, LANES)),
            tab, tab, tab, tab,
            _resident(vones.shape),
        ],
        out_specs=[_q_spec(nq, tm, s // tm, n_ctx_tiles, q_rotated), tok(nk), v_spec],
        out_shape=[jax.ShapeDtypeStruct((b, s, nq), BF16), jax.ShapeDtypeStruct((b, s, nk), BF16), v_shape],
        compiler_params=_cparams(2),
        name=name,
    )(xs, mods, g, wq, wk, wv, qg, kg, cosq, sinq, cosk, sink, vones)


def _mla_proj_kernel(x_ref, mod_ref, g_ref, win_ref, qn_ref, kvn_ref, wuq_ref, wuk_ref, wuvt_ref,
                     cosq_ref, sinq_ref, cosk_ref, sink_ref, vones_ref, q_ref, k_ref, v_ref):
    h = _norm_mod(x_ref[0], g_ref[...], mod_ref[0, 0:1, :], mod_ref[0, 1:2, :]).astype(BF16)
    tm = h.shape[0]
    t1 = jnp.dot(h, win_ref[...], preferred_element_type=F32)
    cq = t1[:, :MLA_Q_LORA]
    ckv = t1[:, MLA_Q_LORA:MLA_Q_LORA + MLA_KV_LORA]
    kr = t1[:, MLA_Q_LORA + MLA_KV_LORA:]
    cqn = (cq * lax.rsqrt(jnp.mean(cq * cq, axis=-1, keepdims=True) + EPS) * qn_ref[...]).astype(BF16)
    ckvn = (ckv * lax.rsqrt(jnp.mean(ckv * ckv, axis=-1, keepdims=True) + EPS) * kvn_ref[...]).astype(BF16)
    quarter = MLA_ROPE_DIM // 4
    lo = _lo_mask((tm, LANES), quarter)
    kr = _rope_slot(kr, cosk_ref[...], sink_ref[...], lo, quarter)
    n = wuq_ref.shape[1]
    for c0 in range(0, n, 2 * LANES):
        q2 = jnp.dot(cqn, wuq_ref[:, c0:c0 + 2 * LANES], preferred_element_type=F32)
        k2 = jnp.dot(ckvn, wuk_ref[:, c0:c0 + 2 * LANES], preferred_element_type=F32)
        for s0 in (0, LANES):
            q_ref[0, :, c0 + s0:c0 + s0 + LANES] = _rope_slot(
                q2[:, s0:s0 + LANES], cosq_ref[...], sinq_ref[...], lo, quarter).astype(BF16)
            k_ref[0, :, c0 + s0:c0 + s0 + LANES] = (k2[:, s0:s0 + LANES] + kr).astype(BF16)
    vt = lax.dot_general(wuvt_ref[...], ckvn, NT_DIMS, preferred_element_type=F32) + vones_ref[...]
    v_ref[0, 0] = vt.astype(BF16)


def _project_mla(xs, mods, g, win, qn, kvn, wuq, wuk, wuvt, cosq, sinq, cosk, sink, vones, *, n_ctx_tiles):
    b, s, d = xs.shape
    tm = TOKEN_TILE
    nq, nk = wuq.shape[1], wuk.shape[1]
    tok = lambda n: pl.BlockSpec((1, tm, n), lambda bi, i: (bi, i, 0))
    tab = pl.BlockSpec((tm, LANES), lambda bi, i: (i, 0))
    return pl.pallas_call(
        _mla_proj_kernel,
        grid=(b, s // tm),
        in_specs=[
            tok(d),
            pl.BlockSpec((1, 6, d), lambda bi, i: (jnp.where(i < n_ctx_tiles, b, bi), 0, 0)),
            _resident((1, d)),
            _resident(win.shape), _resident(qn.shape), _resident(kvn.shape),
            _resident(wuq.shape), _resident(wuk.shape), _resident(wuvt.shape),
            tab, tab, tab, tab,
            _resident(vones.shape),
        ],
        out_specs=[_q_spec(nq, tm, s // tm, n_ctx_tiles, True), tok(nk), _vt_spec(wuvt.shape[0], tm)],
        out_shape=[jax.ShapeDtypeStruct((b, s, nq), BF16), jax.ShapeDtypeStruct((b, s, nk), BF16),
                   jax.ShapeDtypeStruct((b, s // KV_CHUNK, wuvt.shape[0], KV_CHUNK), BF16)],
        compiler_params=_cparams(2),
        name="proj_mla",
    )(xs, mods, g, win, qn, kvn, wuq, wuk, wuvt, cosq, sinq, cosk, sink, vones)


def _flash_kernel(q_ref, k_ref, vt_ref, lam_ref, subg_ref, o_ref, *scratch, mode, k_cols, v_rows, vw, diff_scale,
                  ctx_keys):
    tq = q_ref.shape[1]
    n_chunks = vt_ref.shape[1]
    qs = [q_ref[0, :, a * LANES:(a + 1) * LANES] for a in (0, 1)]
    n_rows = v_rows[0][1] - v_rows[0][0]

    def keys(key_rows):
        k0 = k_ref[0, key_rows, k_cols[0][0]:k_cols[0][1]]
        k1 = k0 if k_cols[1] == k_cols[0] else k_ref[0, key_rows, k_cols[1][0]:k_cols[1][1]]
        return k0, k1

    def values(chunk_idx, key_lanes):
        v0 = vt_ref[0, chunk_idx, v_rows[0][0]:v_rows[0][1], key_lanes]
        v1 = v0 if v_rows[1] == v_rows[0] else vt_ref[0, chunk_idx, v_rows[1][0]:v_rows[1][1], key_lanes]
        return v0, v1

    def update(st, cmax, vt_a, carry_a):
        m, acc = carry_a
        m_new = jnp.maximum(m, cmax)
        p = jnp.exp2(st - m_new).astype(BF16)
        return m_new, jnp.exp2(m - m_new) * acc + jnp.dot(vt_a, p, preferred_element_type=F32)

    init = tuple((jnp.full((1, tq), -jnp.inf, F32), jnp.zeros((n_rows, tq), F32)) for _ in (0, 1))

    def finish(carry):
        (_, acc_a), (_, acc_b) = carry
        oa = acc_a[0:vw] / acc_a[vw:vw + 1]
        ob = acc_b[0:vw] / acc_b[vw:vw + 1]
        if mode == "pair":
            out = jnp.concatenate([oa, ob], axis=0).T
        else:
            lam = lam_ref[...]
            lam_full = (jnp.exp(jnp.sum(lam[0:1] * lam[1:2], axis=-1, keepdims=True))
                        - jnp.exp(jnp.sum(lam[2:3] * lam[3:4], axis=-1, keepdims=True))
                        + (1.0 - diff_scale))
            o = (oa - lam_full * ob).T
            out = o * lax.rsqrt(jnp.mean(o * o, axis=-1, keepdims=True) + EPS) * subg_ref[...] * diff_scale
        o_ref[0] = out.astype(BF16)

    if ctx_keys:
        ks, vts = keys(slice(0, ctx_keys)), values(0, slice(0, ctx_keys))
        out = []
        for a in (0, 1):
            st = lax.dot_general(ks[a], qs[a], NT_DIMS, preferred_element_type=F32)
            out.append(update(st, jnp.max(st, axis=0, keepdims=True), vts[a], init[a]))
        finish(tuple(out))
        return

    (st_scr,) = scratch

    def chunk_keys(chunk_idx):
        start = chunk_idx * KV_CHUNK
        return keys(pl.ds(start if isinstance(start, int) else pl.multiple_of(start, KV_CHUNK), KV_CHUNK))

    def score(slot, a, ks):
        st = lax.dot_general(ks[a], qs[a], NT_DIMS, preferred_element_type=F32)
        st_scr[slot, a] = st
        return jnp.max(st, axis=0, keepdims=True)

    def scores(slot, chunk_idx):
        ks = chunk_keys(chunk_idx)
        return tuple(score(slot, a, ks) for a in (0, 1))

    def run(first, count, cmax, carry):
        for u in range(count):
            nxt, slot = first + u + 1, u % 2
            has_next = isinstance(nxt, jax.Array) or nxt < n_chunks
            m_new = [jnp.maximum(carry[a][0], cmax[a]) for a in (0, 1)]
            acc = [jnp.exp2(carry[a][0] - m_new[a]) * carry[a][1] for a in (0, 1)]
            cmax_next = [None, None]
            for j in range(KV_CHUNK // SUB_KEYS):
                rows = slice(j * SUB_KEYS, (j + 1) * SUB_KEYS)
                if has_next:
                    start = nxt * KV_CHUNK + j * SUB_KEYS
                    ks = keys(pl.ds(start if isinstance(start, int) else pl.multiple_of(start, SUB_KEYS), SUB_KEYS))
                    for a in (0, 1):
                        st = lax.dot_general(ks[a], qs[a], NT_DIMS, preferred_element_type=F32)
                        st_scr[1 - slot, a, rows, :] = st
                        mx = jnp.max(st, axis=0, keepdims=True)
                        cmax_next[a] = mx if cmax_next[a] is None else jnp.maximum(cmax_next[a], mx)
                vts = values(first + u, rows)
                for a in (0, 1):
                    p = jnp.exp2(st_scr[slot, a, rows, :] - m_new[a]).astype(BF16)
                    acc[a] = acc[a] + jnp.dot(vts[a], p, preferred_element_type=F32)
            carry = tuple((m_new[a], acc[a]) for a in (0, 1))
            cmax = tuple(cmax_next)
        return cmax, carry

    n_loop = (n_chunks - 1) // LOOP_CHUNKS
    cmax, carry = lax.fori_loop(0, n_loop, lambda i, st: run(i * LOOP_CHUNKS, LOOP_CHUNKS, *st),
                                (scores(0, 0), init))
    _, carry = run(n_loop * LOOP_CHUNKS, n_chunks - n_loop * LOOP_CHUNKS, cmax, carry)
    finish(carry)


def _flash(q, k, vt, lam, subg, *, mode, kw, k_cols, v_block_rows, v_rows, vw, k_index, v_index, n_ctx, diff_scale,
           name):
    b, s, nq = q.shape
    t = s - n_ctx
    n_pairs = nq // (2 * LANES)
    n_chunks = s // KV_CHUNK
    assert s % KV_CHUNK == 0 and n_ctx <= KV_CHUNK and t % FLASH_TQ == 0 and t % n_ctx == 0
    assert vt.shape[1] == n_chunks
    common = dict(mode=mode, k_cols=k_cols, v_rows=v_rows, vw=vw, diff_scale=diff_scale)
    o_x = pl.pallas_call(
        functools.partial(_flash_kernel, ctx_keys=0, **common),
        grid=(b, n_pairs, t // FLASH_TQ),
        in_specs=[
            pl.BlockSpec((1, FLASH_TQ, 2 * LANES), lambda bi, p, i: (bi, i, p)),
            pl.BlockSpec((1, s, kw), lambda bi, p, i: (bi, 0, k_index(p))),
            pl.BlockSpec((1, n_chunks, v_block_rows, KV_CHUNK), lambda bi, p, i: (bi, 0, v_index(p), 0)),
            _resident(lam.shape), _resident(subg.shape),
        ],
        out_specs=pl.BlockSpec((1, FLASH_TQ, LANES), lambda bi, p, i: (bi, i, p)),
        out_shape=jax.ShapeDtypeStruct((b, t, n_pairs * LANES), BF16),
        scratch_shapes=[pltpu.VMEM((2, 2, KV_CHUNK, FLASH_TQ), F32)],
        compiler_params=_cparams(3),
        name=name,
    )(q, k, vt, lam, subg)
    o_c = pl.pallas_call(
        functools.partial(_flash_kernel, ctx_keys=n_ctx, **common),
        grid=(b, n_pairs),
        in_specs=[
            pl.BlockSpec((1, n_ctx, 2 * LANES), lambda bi, p: (bi, t // n_ctx, p)),
            pl.BlockSpec((1, n_ctx, kw), lambda bi, p: (bi, 0, k_index(p))),
            pl.BlockSpec((1, 1, v_block_rows, KV_CHUNK), lambda bi, p: (bi, 0, v_index(p), 0)),
            _resident(lam.shape), _resident(subg.shape),
        ],
        out_specs=pl.BlockSpec((1, n_ctx, LANES), lambda bi, p: (bi, 0, p)),
        out_shape=jax.ShapeDtypeStruct((b, n_ctx, n_pairs * LANES), BF16),
        compiler_params=_cparams(2),
        name=name + "_ctx",
    )(q, k, vt, lam, subg)
    return o_x, o_c


def _na_tables(t):
    rows = t // GRID_W
    win_rows = min(NA_ROWS, rows)
    band_rows = min(win_rows + 1, rows)
    assert rows >= NA_WIN_ROWS and NA_QBLOCK % GRID_W == 0
    qrows = NA_QBLOCK // GRID_W
    nb = t // NA_QBLOCK
    nwin = NA_WIN_ROWS * GRID_W
    q_off, k_off = np.arange(NA_QBLOCK), np.arange(nwin)
    q_dr, q_col = q_off // GRID_W, q_off % GRID_W
    k_dr, k_col = k_off // GRID_W, k_off % GRID_W
    col_start = np.clip(q_col - NA_COLS // 2, 0, GRID_W - NA_COLS)
    col_in = (k_col[None, :] >= col_start[:, None]) & (k_col[None, :] < col_start[:, None] + NA_COLS)
    dc_idx = np.clip(k_col[None, :] - q_col[:, None] + NA_COLS - 1, 0, 2 * NA_COLS - 2)
    patterns, var, wstart = {}, [], []
    for j in range(nb):
        r = j * qrows + q_dr
        row_start = np.clip(r - win_rows // 2, 0, rows - win_rows)
        b0 = min(int(row_start[0]), rows - band_rows)
        w0 = min(b0 - b0 % 2, rows - NA_WIN_ROWS)
        assert w0 % qrows == 0 and w0 <= b0 and b0 + band_rows <= w0 + NA_WIN_ROWS
        k_row = w0 + k_dr
        in_band = (k_row >= b0) & (k_row < b0 + band_rows)
        in_win = (col_in & in_band[None, :] & (k_row[None, :] >= row_start[:, None])
                  & (k_row[None, :] < row_start[:, None] + win_rows))
        dr_idx = np.clip(k_row[None, :] - r[:, None] + NA_ROWS - 1, 0, 2 * NA_ROWS - 2)
        idx = np.where(in_win, dr_idx * (2 * NA_COLS - 1) + dc_idx, -1).astype(np.int32)
        key = idx.tobytes()
        if key not in patterns:
            patterns[key] = (len(patterns), idx)
        var.append(patterns[key][0])
        wstart.append(w0 * GRID_W)
    idx_all = np.stack([p[1] for p in sorted(patterns.values(), key=lambda p: p[0])])
    return idx_all.reshape(len(patterns), 1, -1), np.asarray(var, np.int32), np.asarray(wstart, np.int32)


def _na_bias_kernel(idx_ref, rpb_ref, o_ref):
    idx = idx_ref[0]
    r = rpb_ref[...]
    onehot = (lax.broadcasted_iota(jnp.int32, (r.shape[1], idx.shape[1]), 0) == idx).astype(BF16)
    b1 = r.astype(BF16)
    r2 = r - b1.astype(F32)
    b2 = r2.astype(BF16)
    b3 = (r2 - b2.astype(F32)).astype(BF16)
    out = (jnp.dot(b1, onehot, preferred_element_type=F32) + jnp.dot(b2, onehot, preferred_element_type=F32)
           + jnp.dot(b3, onehot, preferred_element_type=F32))
    o_ref[0] = jnp.where(idx >= 0, out, NEG_INF)


def _na_bias(idx_all, rpb):
    nvar, _, npos = idx_all.shape
    heads = rpb.shape[0]
    nr = (2 * NA_ROWS - 1) * (2 * NA_COLS - 1)
    nr_pad = pl.cdiv(nr, LANES) * LANES
    rflat = jnp.pad(rpb.reshape(heads, nr), ((0, 0), (0, nr_pad - nr)))
    tn = 2048
    assert npos % tn == 0
    out = pl.pallas_call(
        _na_bias_kernel,
        grid=(nvar, npos // tn),
        in_specs=[pl.BlockSpec((1, 1, tn), lambda v, j: (v, 0, j)), _resident(rflat.shape)],
        out_specs=pl.BlockSpec((1, heads, tn), lambda v, j: (v, 0, j)),
        out_shape=jax.ShapeDtypeStruct((nvar, heads, npos), F32),
        compiler_params=_cparams(2),
        name="na_bias",
    )(jnp.asarray(idx_all), rflat)
    return out.reshape(nvar, heads, NA_QBLOCK, npos // NA_QBLOCK)


def _halves(lo_src, hi_src):
    lane = lax.broadcasted_iota(jnp.int32, lo_src.shape, 1)
    return jnp.where(lane < LANES // 2, lo_src, hi_src)


def _na_kernel(var_ref, ws_ref, q_ref, k_ref, v_ref, b0_ref, b1_ref, o_ref, *, n_ctx, n_ctx_tiles):
    qi = pl.program_id(2)
    nwin = NA_WIN_ROWS * GRID_W

    @pl.when(qi < n_ctx_tiles)
    def _():
        kc, vc = k_ref[0, 0:n_ctx, :], v_ref[0, 0:n_ctx, :]
        outs = []
        for a in (0, 1):
            s_c = lax.dot_general(q_ref[0, :, a * LANES:(a + 1) * LANES], kc, NT_DIMS,
                                  preferred_element_type=F32)
            p_c = jnp.exp(s_c - jnp.max(s_c, axis=-1, keepdims=True))
            acc = jnp.dot(p_c.astype(BF16), vc, preferred_element_type=F32)
            outs.append(acc / jnp.sum(p_c, axis=-1, keepdims=True))
        o_ref[0] = _halves(outs[0], outs[1]).astype(BF16)

    @pl.when(qi >= n_ctx_tiles)
    def _():
        kc, vc = k_ref[0, 0:n_ctx, :], v_ref[0, 0:n_ctx, :]
        work = []
        for sub, bias_ref in ((0, b0_ref), (1, b1_ref)):
            j = (qi - n_ctx_tiles) * (TOKEN_TILE // NA_QBLOCK) + sub
            start = pl.multiple_of(n_ctx + ws_ref[j], NA_QBLOCK)
            kw, vw = k_ref[0, pl.ds(start, nwin), :], v_ref[0, pl.ds(start, nwin), :]
            for a in (0, 1):
                q = q_ref[0, sub * NA_QBLOCK:(sub + 1) * NA_QBLOCK, a * LANES:(a + 1) * LANES]
                s_c = lax.dot_general(q, kc, NT_DIMS, preferred_element_type=F32)
                s_w = lax.dot_general(q, kw, NT_DIMS, preferred_element_type=F32) + bias_ref[0, a]
                work.append((s_c, s_w, vw))
        outs = []
        for s_c, s_w, vw in work:
            m = jnp.maximum(jnp.max(s_c, axis=-1, keepdims=True), jnp.max(s_w, axis=-1, keepdims=True))
            p_c, p_w = jnp.exp(s_c - m), jnp.exp(s_w - m)
            l = jnp.sum(p_c, axis=-1, keepdims=True) + jnp.sum(p_w, axis=-1, keepdims=True)
            acc = (jnp.dot(p_c.astype(BF16), vc, preferred_element_type=F32)
                   + jnp.dot(p_w.astype(BF16), vw, preferred_element_type=F32))
            outs.append(acc / l)
        for sub in (0, 1):
            o_ref[0, sub * NA_QBLOCK:(sub + 1) * NA_QBLOCK, :] = _halves(outs[2 * sub], outs[2 * sub + 1]).astype(BF16)


def _na_attention(q, k, v, bias, var, wstart, *, n_ctx):
    b, s, nq = q.shape
    tq = TOKEN_TILE
    n_pairs = nq // (2 * LANES)
    n_ctx_tiles = n_ctx // tq
    per_tile = tq // NA_QBLOCK
    nwin = NA_WIN_ROWS * GRID_W

    def bias_spec(sub):
        def index(bi, p, i, var_ref, ws_ref):
            j = jnp.maximum(i - n_ctx_tiles, 0) * per_tile + sub
            return (var_ref[j], p, 0, 0)
        return pl.BlockSpec((1, 2, NA_QBLOCK, nwin), index)

    grid_spec = pltpu.PrefetchScalarGridSpec(
        num_scalar_prefetch=2,
        grid=(b, n_pairs, s // tq),
        in_specs=[
            pl.BlockSpec((1, tq, 2 * LANES), lambda bi, p, i, *_: (bi, i, p)),
            pl.BlockSpec((1, s, LANES), lambda bi, p, i, *_: (bi, 0, p)),
            pl.BlockSpec((1, s, LANES), lambda bi, p, i, *_: (bi, 0, p)),
            bias_spec(0), bias_spec(1),
        ],
        out_specs=pl.BlockSpec((1, tq, LANES), lambda bi, p, i, *_: (bi, i, p)),
    )
    return pl.pallas_call(
        functools.partial(_na_kernel, n_ctx=n_ctx, n_ctx_tiles=n_ctx_tiles),
        grid_spec=grid_spec,
        out_shape=jax.ShapeDtypeStruct((b, s, n_pairs * LANES), BF16),
        compiler_params=_cparams(3),
        name="na_attn",
    )(var, wstart, q, k, v, bias, bias)


def _outproj_kernel(x_ref, ox_ref, oc_ref, w_ref, mod_ref, y_ref, *, n_ctx_tiles, first_tile):
    o = jnp.where(pl.program_id(1) + first_tile < n_ctx_tiles, oc_ref[0], ox_ref[0])
    y = jnp.dot(o, w_ref[...], preferred_element_type=F32)
    y_ref[0] = x_ref[0] + mod_ref[0, 2:3, :] * y


def _outproj(xs, o_x, o_c, w, mods, *, n_ctx_tiles, x_offset, latent_only):
    b, s, d = xs.shape
    tm = TOKEN_TILE
    n = o_x.shape[2]
    first = n_ctx_tiles if latent_only else 0
    return pl.pallas_call(
        functools.partial(_outproj_kernel, n_ctx_tiles=n_ctx_tiles, first_tile=first),
        grid=(b, s // tm - first),
        in_specs=[
            pl.BlockSpec((1, tm, d), lambda bi, i: (bi, i + first, 0)),
            pl.BlockSpec((1, tm, n), lambda bi, i: (bi, jnp.maximum(i + first, n_ctx_tiles) - x_offset, 0)),
            pl.BlockSpec((1, tm, n), lambda bi, i: (bi, jnp.minimum(i + first, n_ctx_tiles - 1), 0)),
            _resident(w.shape),
            pl.BlockSpec((1, 6, d), lambda bi, i: (jnp.where(i + first < n_ctx_tiles, b, bi), 0, 0)),
        ],
        out_specs=pl.BlockSpec((1, tm, d), lambda bi, i: (bi, i, 0)),
        out_shape=jax.ShapeDtypeStruct((b, s - first * tm, d), F32),
        compiler_params=_cparams(2),
        name="outproj",
    )(xs, o_x, o_c, w, mods)


def _ffn_kernel(xp_ref, x_ref, xn_ref, mod_ref, g_ref, wup_ref, cw_ref, cb_ref, wdn_ref, fg_ref, y_ref, u_scr, a_scr,
                *, n_ctx_tiles, n_tiles, chunk, final_norm):
    i = pl.program_id(1)
    tm = x_ref.shape[1]
    has_prev = jnp.logical_and(i != 0, i != n_ctx_tiles).astype(F32)
    has_next = jnp.logical_and(i != n_ctx_tiles - 1, i != n_tiles - 1).astype(F32)
    g, shift, scale = g_ref[...], mod_ref[0, 3:4, :], mod_ref[0, 4:5, :]
    x = x_ref[0]
    h = jnp.concatenate([_norm_mod(xp_ref[0], g, shift, scale) * has_prev,
                         _norm_mod(x, g, shift, scale),
                         _norm_mod(xn_ref[0], g, shift, scale) * has_next], axis=0).astype(BF16)
    n_chunks = wup_ref.shape[1] // (2 * chunk)
    for j in range(n_chunks):
        cols = slice(j * 2 * chunk, (j + 1) * 2 * chunk)
        u_scr[...] = jnp.dot(h, wup_ref[:, cols], preferred_element_type=F32)
        cw = cw_ref[:, cols]
        u = (u_scr[HALO - 1:HALO - 1 + tm, :] * cw[0:1] + u_scr[HALO:HALO + tm, :] * cw[1:2]
             + u_scr[HALO + 1:HALO + 1 + tm, :] * cw[2:3] + cb_ref[:, cols])
        val, gate = u[:, :chunk], u[:, chunk:]
        a_scr[:, j * chunk:(j + 1) * chunk] = (gate * (1.0 / (1.0 + jnp.exp(-gate))) * val).astype(BF16)
    y = x + mod_ref[0, 5:6, :] * jnp.dot(a_scr[...], wdn_ref[...], preferred_element_type=F32)
    if final_norm:
        y = y * lax.rsqrt(jnp.mean(y * y, axis=-1, keepdims=True) + EPS) * fg_ref[...]
    y_ref[0] = y


def _ffn(xs, mods, g, wup, cw, cb, wdn, final_g, *, n_ctx_tiles, chunk, final_norm):
    b, s, d = xs.shape
    tm = TOKEN_TILE
    n_tiles = s // tm
    per = tm // HALO
    f = wdn.shape[0]
    return pl.pallas_call(
        functools.partial(_ffn_kernel, n_ctx_tiles=n_ctx_tiles, n_tiles=n_tiles, chunk=chunk,
                          final_norm=final_norm),
        grid=(b, n_tiles),
        in_specs=[
            pl.BlockSpec((1, HALO, d), lambda bi, i: (bi, jnp.maximum(i * per - 1, 0), 0)),
            pl.BlockSpec((1, tm, d), lambda bi, i: (bi, i, 0)),
            pl.BlockSpec((1, HALO, d), lambda bi, i: (bi, jnp.minimum((i + 1) * per, s // HALO - 1), 0)),
            pl.BlockSpec((1, 6, d), lambda bi, i: (jnp.where(i < n_ctx_tiles, b, bi), 0, 0)),
            _resident((1, d)),
            pl.BlockSpec(wup.shape, lambda bi, i: (0, 0), pipeline_mode=pl.Buffered(1)),
            _resident(cw.shape), _resident(cb.shape),
            pl.BlockSpec(wdn.shape, lambda bi, i: (0, 0), pipeline_mode=pl.Buffered(1)),
            _resident((1, d)),
        ],
        out_specs=pl.BlockSpec((1, tm, d), lambda bi, i: (bi, i, 0)),
        out_shape=jax.ShapeDtypeStruct((b, s, d), F32),
        scratch_shapes=[pltpu.VMEM((tm + 2 * HALO, 2 * chunk), F32), pltpu.VMEM((tm, f), BF16)],
        compiler_params=_cparams(2),
        name="ffn",
    )(xs, xs, xs, mods, g, wup, cw, cb, wdn, final_g)


def _rope_table(t, n_ctx, rot_dim, lane_lo):
    tt = jnp.arange(t)
    row = (tt // GRID_W).astype(F32)
    col = (tt % GRID_W).astype(F32)
    half = rot_dim // 2
    inv = ROPE_BASE ** (-jnp.arange(0, half, 2, dtype=F32) / half)
    ar, ac = row[:, None] * inv, col[:, None] * inv
    ang = jnp.concatenate([ar, ar, ac, ac], axis=-1)
    quarter = rot_dim // 4
    sign = jnp.where((jnp.arange(rot_dim) % (2 * quarter)) < quarter, -1.0, 1.0).astype(F32)
    cos, sin = jnp.cos(ang), jnp.sin(ang) * sign
    reps = (LANES - lane_lo) // rot_dim if lane_lo == 0 else 1
    cos_l = jnp.ones((t, LANES), F32).at[:, lane_lo:lane_lo + reps * rot_dim].set(jnp.tile(cos, (1, reps)))
    sin_l = jnp.zeros((t, LANES), F32).at[:, lane_lo:lane_lo + reps * rot_dim].set(jnp.tile(sin, (1, reps)))
    cos_s = jnp.concatenate([jnp.ones((n_ctx, LANES), F32), cos_l], axis=0)
    sin_s = jnp.concatenate([jnp.zeros((n_ctx, LANES), F32), sin_l], axis=0)
    return cos_s, sin_s


def _to_slots(w, n_heads, width, halves):
    d = w.shape[0]
    wh = w.reshape(d, n_heads, width)
    lo = jnp.pad(wh, ((0, 0), (0, 0), (0, LANES - width)))
    if not any(halves):
        return lo.reshape(d, n_heads * LANES)
    hi = jnp.pad(wh, ((0, 0), (0, 0), (LANES // 2, LANES // 2 - width)))
    in_hi = jnp.asarray(halves, jnp.int32).reshape(1, n_heads, 1) == 1
    return jnp.where(in_hi, hi, lo).reshape(d, n_heads * LANES)


def _vt_weights(wv, groups, width, tm):
    d = wv.shape[0]
    rows = width + 16
    wt = jnp.zeros((groups, rows, d), wv.dtype).at[:, :width, :].set(wv.T.reshape(groups, width, d))
    ones = jnp.zeros((groups, rows, tm), F32).at[:, width, :].set(1.0)
    return wt.reshape(groups * rows, d), ones.reshape(groups * rows, tm)


def _dup(g):
    return jnp.concatenate([g, g]).reshape(1, LANES).astype(F32)


def kernel(x, c, ctx, c_ctx, ada_w, ada_b, norm1_g, norm2_g, ffn_w_up, ffn_conv_w, ffn_conv_b, ffn_w_down,
           gqa_w_in, gqa_q_norm_g, gqa_k_norm_g, gqa_w_out,
           mla_w_in, mla_q_norm_g, mla_kv_norm_g, mla_w_uq, mla_w_ukv, mla_w_out,
           diff_w_in, diff_lambda, diff_subln_g, diff_w_out,
           na_w_in, na_rpb, na_w_out, final_norm_g):
    b, t, d = x.shape
    n_ctx = ctx.shape[1]
    depth = ada_w.shape[0]
    assert n_ctx % TOKEN_TILE == 0 and t % TOKEN_TILE == 0 and b + 1 <= 8
    nct = n_ctx // TOKEN_TILE
    f = ffn_w_down.shape[1]
    chunk = 2 * LANES
    assert f % chunk == 0

    xs = jnp.concatenate([ctx, x], axis=1)
    cin = jnp.zeros((8, d), F32).at[:b].set(c).at[b].set(c_ctx)
    mods_all = _ada_all(cin, ada_w, ada_b).reshape(depth, 8, 6, d)

    tm = TOKEN_TILE
    ones_l = jnp.ones((1, LANES), F32)
    dummy_tab = jnp.zeros((xs.shape[1], LANES), F32)
    dummy_ones = jnp.zeros((8, LANES), F32)
    cos64, sin64 = _rope_table(t, n_ctx, GQA_HEAD_DIM, 0)
    cos_mla, sin_mla = _rope_table(t, n_ctx, MLA_ROPE_DIM, MLA_NOPE_DIM)
    no_lam, no_g = jnp.zeros((4, DIFF_HEAD_DIM), F32), jnp.ones((1, LANES), F32)
    pair_rows = MLA_V_DIM + 16
    diff_rows = 2 * DIFF_HEAD_DIM + 16

    for i in range(depth):
        m, j = i % N_MIXERS, i // N_MIXERS
        mods = mods_all[i]
        g1 = norm1_g[i].reshape(1, d)
        if m == 0:
            w = gqa_w_in[j].astype(BF16)
            nqc, nkc = GQA_HEADS * GQA_HEAD_DIM, GQA_KV_HEADS * GQA_HEAD_DIM
            wk = w[:, nqc:nqc + nkc].reshape(d, GQA_KV_HEADS, 1, GQA_HEAD_DIM)
            wk = jnp.broadcast_to(wk, (d, GQA_KV_HEADS, 2, GQA_HEAD_DIM)).reshape(d, GQA_KV_HEADS * LANES)
            wvt, vones = _vt_weights(w[:, nqc + nkc:], GQA_KV_HEADS, GQA_HEAD_DIM, tm)
            qc = GQA_HEAD_DIM ** -0.5 * LOG2E
            q, k, vt = _project(xs, mods, g1, w[:, :nqc], wk, wvt, _dup(gqa_q_norm_g[j]), _dup(gqa_k_norm_g[j]),
                                cos64 * qc, sin64 * qc, cos64, sin64, vones, qk_norm=True, rope=True,
                                head_dim=GQA_HEAD_DIM, k_norm_dim=2 * GQA_HEAD_DIM, v_transposed=True, q_rotated=True, n_ctx_tiles=nct,
                                name="proj_gqa")
            o_x, o_c = _flash(q, k, vt, no_lam, no_g, mode="pair", kw=LANES, k_cols=((0, LANES), (0, LANES)),
                       v_block_rows=pair_rows, v_rows=((0, pair_rows), (0, pair_rows)), vw=GQA_HEAD_DIM,
                       k_index=lambda p: p // 2, v_index=lambda p: p // 2, n_ctx=n_ctx, diff_scale=1.0,
                       name="attn_gqa")
            w_out = gqa_w_out[j]
        elif m == 1:
            w = mla_w_in[j].astype(BF16)
            nlat = MLA_Q_LORA + MLA_KV_LORA
            kr_slot = jnp.zeros((d, LANES), BF16).at[:, MLA_NOPE_DIM:MLA_NOPE_DIM + MLA_ROPE_DIM].set(w[:, nlat:])
            win = jnp.concatenate([w[:, :nlat], kr_slot], axis=1)
            qk_dim = MLA_NOPE_DIM + MLA_ROPE_DIM
            wuq = _to_slots(mla_w_uq[j].astype(BF16), MLA_HEADS, qk_dim, [0] * MLA_HEADS)
            wukv = mla_w_ukv[j].astype(BF16).reshape(MLA_KV_LORA, MLA_HEADS, MLA_NOPE_DIM + MLA_V_DIM)
            wuk = _to_slots(wukv[:, :, :MLA_NOPE_DIM].reshape(MLA_KV_LORA, -1), MLA_HEADS, MLA_NOPE_DIM,
                            [0] * MLA_HEADS)
            wuvt, vones = _vt_weights(wukv[:, :, MLA_NOPE_DIM:].reshape(MLA_KV_LORA, MLA_HEADS * MLA_V_DIM),
                                      MLA_HEADS, MLA_V_DIM, tm)
            qc = qk_dim ** -0.5 * LOG2E
            q, k, vt = _project_mla(xs, mods, g1, win, mla_q_norm_g[j].reshape(1, -1),
                                    mla_kv_norm_g[j].reshape(1, -1), wuq, wuk, wuvt,
                                    cos_mla * qc, sin_mla * qc, cos_mla, sin_mla, vones, n_ctx_tiles=nct)
            o_x, o_c = _flash(q, k, vt, no_lam, no_g, mode="pair", kw=2 * LANES, k_cols=((0, LANES), (LANES, 2 * LANES)),
                       v_block_rows=2 * pair_rows, v_rows=((0, pair_rows), (pair_rows, 2 * pair_rows)),
                       vw=MLA_V_DIM, k_index=lambda p: p, v_index=lambda p: p, n_ctx=n_ctx, diff_scale=1.0,
                       name="attn_mla")
            w_out = mla_w_out[j]
        elif m == 2:
            w = diff_w_in[j].astype(BF16)
            wvt, vones = _vt_weights(w[:, 2 * d:], DIFF_HEADS, 2 * DIFF_HEAD_DIM, tm)
            lambda_init = 0.8 - 0.6 * math.exp(-0.3 * i)
            qc = DIFF_HEAD_DIM ** -0.5 * LOG2E
            q, k, vt = _project(xs, mods, g1, w[:, :d], w[:, d:2 * d], wvt, ones_l, ones_l,
                                cos64 * qc, sin64 * qc, cos64, sin64, vones, qk_norm=False, rope=True,
                                head_dim=DIFF_HEAD_DIM, k_norm_dim=DIFF_HEAD_DIM, v_transposed=True, q_rotated=True, n_ctx_tiles=nct,
                                name="proj_diff")
            o_x, o_c = _flash(q, k, vt, diff_lambda[j], diff_subln_g[j].reshape(1, LANES), mode="diff", kw=LANES,
                       k_cols=((0, LANES), (0, LANES)), v_block_rows=diff_rows,
                       v_rows=((0, diff_rows), (0, diff_rows)), vw=2 * DIFF_HEAD_DIM,
                       k_index=lambda p: p, v_index=lambda p: p, n_ctx=n_ctx, diff_scale=1.0 - lambda_init,
                       name="attn_diff")
            w_out = diff_w_out[j]
        else:
            w = na_w_in[j].astype(BF16)
            na_scale = NA_HEAD_DIM ** -0.5
            assert math.frexp(na_scale)[0] == 0.5
            q, k, v = _project(xs, mods, g1, w[:, :d] * jnp.asarray(na_scale, BF16), w[:, d:2 * d], w[:, 2 * d:],
                               ones_l, ones_l, dummy_tab, dummy_tab, dummy_tab, dummy_tab, dummy_ones,
                               qk_norm=False, rope=False, head_dim=NA_HEAD_DIM, k_norm_dim=NA_HEAD_DIM,
                               v_transposed=False, q_rotated=False, n_ctx_tiles=nct,
                               name="proj_na")
            idx_all, var, wstart = _na_tables(t)
            bias = _na_bias(idx_all, na_rpb[j])
            o_x = o_c = _na_attention(q, k, v, bias, jnp.asarray(var), jnp.asarray(wstart), n_ctx=n_ctx)
            w_out = na_w_out[j]
        last = i == depth - 1
        xs = _outproj(xs, o_x, o_c, w_out.astype(BF16), mods, n_ctx_tiles=nct, x_offset=0 if m == 3 else nct,
                      latent_only=last)

        def regroup(a):
            lead = a.shape[:-1]
            return jnp.stack([a[..., :f].reshape(*lead, f // chunk, chunk),
                              a[..., f:].reshape(*lead, f // chunk, chunk)], axis=-2).reshape(*lead, 2 * f)
        xs = _ffn(xs, mods, norm2_g[i].reshape(1, d), regroup(ffn_w_up[i].astype(BF16)), regroup(ffn_conv_w[i]),
                  regroup(ffn_conv_b[i]).reshape(1, 2 * f), ffn_w_down[i].astype(BF16), final_norm_g.reshape(1, d),
                  n_ctx_tiles=0 if last else nct, chunk=chunk, final_norm=last)
    return xs
```

```python
import functools
import math

import numpy as np
import jax
import jax.numpy as jnp
from jax import lax
from jax.experimental import pallas as pl
from jax.experimental.pallas import tpu as pltpu

F32 = jnp.float32
BF16 = jnp.bfloat16

LANES = 128
TOKEN_TILE = 256
HALO = 8
KV_CHUNK = 768
SUB_KEYS = 256
FLASH_TQ = 512
LOOP_CHUNKS = 4
VMEM_LIMIT = 56 * 1024 * 1024

GRID_W = 64
ROPE_BASE = 10000.0
EPS = 1e-6
NEG_INF = -1e30
LOG2E = 1.4426950408889634

GQA_HEADS, GQA_KV_HEADS, GQA_HEAD_DIM = 16, 4, 64
MLA_HEADS, MLA_NOPE_DIM, MLA_ROPE_DIM, MLA_V_DIM = 16, 64, 32, 64
MLA_Q_LORA, MLA_KV_LORA = 384, 256
DIFF_HEADS, DIFF_HEAD_DIM = 8, 64
NA_HEADS, NA_HEAD_DIM, NA_ROWS, NA_COLS = 16, 64, 8, 16
NA_QBLOCK = 128
NA_WIN_ROWS = 10
N_MIXERS = 4

NT_DIMS = (((1,), (1,)), ((), ()))


def _cparams(n_axes):
    return pltpu.CompilerParams(dimension_semantics=("arbitrary",) * n_axes, vmem_limit_bytes=VMEM_LIMIT)


def _resident(shape):
    nd = len(shape)
    return pl.BlockSpec(shape, lambda *_: (0,) * nd)


def _ada_kernel(c_ref, w_ref, b_ref, o_ref):
    c = c_ref[...]
    s = c * (1.0 / (1.0 + jnp.exp(-c)))
    o_ref[0] = jnp.dot(s.astype(BF16), w_ref[0].astype(BF16), preferred_element_type=F32) + b_ref[0]


def _ada_all(cin, ada_w, ada_b):
    depth, d, n = ada_w.shape
    tn = n // 4
    return pl.pallas_call(
        _ada_kernel,
        grid=(depth, n // tn),
        in_specs=[
            pl.BlockSpec((8, d), lambda l, j: (0, 0)),
            pl.BlockSpec((1, d, tn), lambda l, j: (l, 0, j)),
            pl.BlockSpec((1, 1, tn), lambda l, j: (l, 0, j)),
        ],
        out_specs=pl.BlockSpec((1, 8, tn), lambda l, j: (l, 0, j)),
        out_shape=jax.ShapeDtypeStruct((depth, 8, n), F32),
        compiler_params=_cparams(2),
        name="ada",
    )(cin, ada_w, ada_b.reshape(depth, 1, n))


def _norm_mod(x, g, shift, scale):
    ms = jnp.mean(x * x, axis=-1, keepdims=True)
    return (x * lax.rsqrt(ms + EPS) * g) * (1.0 + scale) + shift


def _rope_slot(y, cos, sin_signed, lo_mask, quarter):
    fwd = pltpu.roll(y, LANES - quarter, axis=1)
    bwd = pltpu.roll(y, quarter, axis=1)
    return y * cos + jnp.where(lo_mask, fwd, bwd) * sin_signed


def _lo_mask(shape, quarter):
    lane = lax.broadcasted_iota(jnp.int32, shape, 1)
    return (lane % (2 * quarter)) < quarter


def _q_spec(n, tm, n_tiles, n_ctx_tiles, rotate):
    if not rotate:
        return pl.BlockSpec((1, tm, n), lambda bi, i: (bi, i, 0))
    n_lat = n_tiles - n_ctx_tiles
    return pl.BlockSpec((1, tm, n), lambda bi, i: (bi, jnp.where(i < n_ctx_tiles, i + n_lat, i - n_ctx_tiles), 0))


def _vt_spec(rows, tm):
    per = KV_CHUNK // tm
    return pl.BlockSpec((1, 1, rows, tm), lambda bi, i: (bi, i // per, 0, i % per))


def _proj_kernel(x_ref, mod_ref, g_ref, wq_ref, wk_ref, wv_ref, qg_ref, kg_ref, cosq_ref, sinq_ref, cosk_ref, sink_ref,
                 vones_ref, q_ref, k_ref, v_ref, *, qk_norm, rope, head_dim, k_norm_dim, v_transposed):
    h = _norm_mod(x_ref[0], g_ref[...], mod_ref[0, 0:1, :], mod_ref[0, 1:2, :]).astype(BF16)
    tm = h.shape[0]
    lo = _lo_mask((tm, LANES), head_dim // 4) if rope else None
    low_half = lax.broadcasted_iota(jnp.int32, (tm, LANES), 1) < LANES // 2

    def chunks(w_ref):
        n = w_ref.shape[1]
        for c0 in range(0, n, 2 * LANES):
            c1 = min(c0 + 2 * LANES, n)
            y2 = jnp.dot(h, w_ref[:, c0:c1], preferred_element_type=F32)
            for s0 in range(0, c1 - c0, LANES):
                yield (c0 + s0) // LANES, y2[:, s0:s0 + LANES]

    for c, y in chunks(wq_ref):
        if qk_norm:
            sq = y * y
            ms_lo = jnp.sum(jnp.where(low_half, sq, 0.0), axis=-1, keepdims=True) * (1.0 / head_dim)
            ms_hi = jnp.sum(jnp.where(low_half, 0.0, sq), axis=-1, keepdims=True) * (1.0 / head_dim)
            y = y * jnp.where(low_half, lax.rsqrt(ms_lo + EPS), lax.rsqrt(ms_hi + EPS)) * qg_ref[...]
        if rope:
            y = _rope_slot(y, cosq_ref[...], sinq_ref[...], lo, head_dim // 4)
        q_ref[0, :, 2 * c * LANES:(2 * c + 1) * LANES] = jnp.where(low_half, y, 0.0).astype(BF16)
        q_ref[0, :, (2 * c + 1) * LANES:(2 * c + 2) * LANES] = jnp.where(low_half, 0.0, y).astype(BF16)
    for c, y in chunks(wk_ref):
        if qk_norm:
            ms = jnp.sum(y * y, axis=-1, keepdims=True) * (1.0 / k_norm_dim)
            y = y * lax.rsqrt(ms + EPS) * kg_ref[...]
        if rope:
            y = _rope_slot(y, cosk_ref[...], sink_ref[...], lo, head_dim // 4)
        k_ref[0, :, c * LANES:(c + 1) * LANES] = y.astype(BF16)
    if v_transposed:
        vt = lax.dot_general(wv_ref[...], h, NT_DIMS, preferred_element_type=F32) + vones_ref[...]
        v_ref[0, 0] = vt.astype(BF16)
    else:
        v_ref[0] = jnp.dot(h, wv_ref[...], preferred_element_type=F32).astype(BF16)


def _project(xs, mods, g, wq, wk, wv, qg, kg, cosq, sinq, cosk, sink, vones, *, qk_norm, rope, head_dim, k_norm_dim,
             v_transposed, q_rotated, n_ctx_tiles, name):
    b, s, d = xs.shape
    tm = TOKEN_TILE
    nq, nk = 2 * wq.shape[1], wk.shape[1]
    tok = lambda n: pl.BlockSpec((1, tm, n), lambda bi, i: (bi, i, 0))
    tab = pl.BlockSpec((tm, LANES), lambda bi, i: (i, 0))
    if v_transposed:
        v_spec = _vt_spec(wv.shape[0], tm)
        v_shape = jax.ShapeDtypeStruct((b, s // KV_CHUNK, wv.shape[0], KV_CHUNK), BF16)
    else:
        v_spec, v_shape = tok(wv.shape[1]), jax.ShapeDtypeStruct((b, s, wv.shape[1]), BF16)
    return pl.pallas_call(
        functools.partial(_proj_kernel, qk_norm=qk_norm, rope=rope, head_dim=head_dim, k_norm_dim=k_norm_dim,
                          v_transposed=v_transposed),
        grid=(b, s // tm),
        in_specs=[
            tok(d),
            pl.BlockSpec((1, 6, d), lambda bi, i: (jnp.where(i < n_ctx_tiles, b, bi), 0, 0)),
            _resident((1, d)),
            _resident(wq.shape), _resident(wk.shape), _resident(wv.shape),
            _resident((1, LANES)), _resident((1, LANES)),
            tab, tab, tab, tab,
            _resident(vones.shape),
        ],
        out_specs=[_q_spec(nq, tm, s // tm, n_ctx_tiles, q_rotated), tok(nk), v_spec],
        out_shape=[jax.ShapeDtypeStruct((b, s, nq), BF16), jax.ShapeDtypeStruct((b, s, nk), BF16), v_shape],
        compiler_params=_cparams(2),
        name=name,
    )(xs, mods, g, wq, wk, wv, qg, kg, cosq, sinq, cosk, sink, vones)


def _mla_proj_kernel(x_ref, mod_ref, g_ref, win_ref, qn_ref, kvn_ref, wuq_ref, wuk_ref, wuvt_ref,
                     cosq_ref, sinq_ref, cosk_ref, sink_ref, vones_ref, q_ref, k_ref, v_ref):
    h = _norm_mod(x_ref[0], g_ref[...], mod_ref[0, 0:1, :], mod_ref[0, 1:2, :]).astype(BF16)
    tm = h.shape[0]
    t1 = jnp.dot(h, win_ref[...], preferred_element_type=F32)
    cq = t1[:, :MLA_Q_LORA]
    ckv = t1[:, MLA_Q_LORA:MLA_Q_LORA + MLA_KV_LORA]
    kr = t1[:, MLA_Q_LORA + MLA_KV_LORA:]
    cqn = (cq * lax.rsqrt(jnp.mean(cq * cq, axis=-1, keepdims=True) + EPS) * qn_ref[...]).astype(BF16)
    ckvn = (ckv * lax.rsqrt(jnp.mean(ckv * ckv, axis=-1, keepdims=True) + EPS) * kvn_ref[...]).astype(BF16)
    quarter = MLA_ROPE_DIM // 4
    lo = _lo_mask((tm, LANES), quarter)
    kr = _rope_slot(kr, cosk_ref[...], sink_ref[...], lo, quarter)
    n = wuq_ref.shape[1]
    for c0 in range(0, n, 2 * LANES):
        q2 = jnp.dot(cqn, wuq_ref[:, c0:c0 + 2 * LANES], preferred_element_type=F32)
        k2 = jnp.dot(ckvn, wuk_ref[:, c0:c0 + 2 * LANES], preferred_element_type=F32)
        for s0 in (0, LANES):
            q_ref[0, :, c0 + s0:c0 + s0 + LANES] = _rope_slot(
                q2[:, s0:s0 + LANES], cosq_ref[...], sinq_ref[...], lo, quarter).astype(BF16)
            k_ref[0, :, c0 + s0:c0 + s0 + LANES] = (k2[:, s0:s0 + LANES] + kr).astype(BF16)
    vt = lax.dot_general(wuvt_ref[...], ckvn, NT_DIMS, preferred_element_type=F32) + vones_ref[...]
    v_ref[0, 0] = vt.astype(BF16)


def _project_mla(xs, mods, g, win, qn, kvn, wuq, wuk, wuvt, cosq, sinq, cosk, sink, vones, *, n_ctx_tiles):
    b, s, d = xs.shape
    tm = TOKEN_TILE
    nq, nk = wuq.shape[1], wuk.shape[1]
    tok = lambda n: pl.BlockSpec((1, tm, n), lambda bi, i: (bi, i, 0))
    tab = pl.BlockSpec((tm, LANES), lambda bi, i: (i, 0))
    return pl.pallas_call(
        _mla_proj_kernel,
        grid=(b, s // tm),
        in_specs=[
            tok(d),
            pl.BlockSpec((1, 6, d), lambda bi, i: (jnp.where(i < n_ctx_tiles, b, bi), 0, 0)),
            _resident((1, d)),
            _resident(win.shape), _resident(qn.shape), _resident(kvn.shape),
            _resident(wuq.shape), _resident(wuk.shape), _resident(wuvt.shape),
            tab, tab, tab, tab,
            _resident(vones.shape),
        ],
        out_specs=[_q_spec(nq, tm, s // tm, n_ctx_tiles, True), tok(nk), _vt_spec(wuvt.shape[0], tm)],
        out_shape=[jax.ShapeDtypeStruct((b, s, nq), BF16), jax.ShapeDtypeStruct((b, s, nk), BF16),
                   jax.ShapeDtypeStruct((b, s // KV_CHUNK, wuvt.shape[0], KV_CHUNK), BF16)],
        compiler_params=_cparams(2),
        name="proj_mla",
    )(xs, mods, g, win, qn, kvn, wuq, wuk, wuvt, cosq, sinq, cosk, sink, vones)


def _flash_kernel(q_ref, k_ref, vt_ref, lam_ref, subg_ref, o_ref, *scratch, mode, k_cols, v_rows, vw, diff_scale,
                  ctx_keys):
    tq = q_ref.shape[1]
    n_chunks = vt_ref.shape[1]
    qs = [q_ref[0, :, a * LANES:(a + 1) * LANES] for a in (0, 1)]
    n_rows = v_rows[0][1] - v_rows[0][0]

    def keys(key_rows):
        k0 = k_ref[0, key_rows, k_cols[0][0]:k_cols[0][1]]
        k1 = k0 if k_cols[1] == k_cols[0] else k_ref[0, key_rows, k_cols[1][0]:k_cols[1][1]]
        return k0, k1

    def values(chunk_idx, key_lanes):
        v0 = vt_ref[0, chunk_idx, v_rows[0][0]:v_rows[0][1], key_lanes]
        v1 = v0 if v_rows[1] == v_rows[0] else vt_ref[0, chunk_idx, v_rows[1][0]:v_rows[1][1], key_lanes]
        return v0, v1

    def update(st, cmax, vt_a, carry_a):
        m, acc = carry_a
        m_new = jnp.maximum(m, cmax)
        p = jnp.exp2(st - m_new).astype(BF16)
        return m_new, jnp.exp2(m - m_new) * acc + jnp.dot(vt_a, p, preferred_element_type=F32)

    init = tuple((jnp.full((1, tq), -jnp.inf, F32), jnp.zeros((n_rows, tq), F32)) for _ in (0, 1))

    def finish(carry):
        (_, acc_a), (_, acc_b) = carry
        oa = acc_a[0:vw] / acc_a[vw:vw + 1]
        ob = acc_b[0:vw] / acc_b[vw:vw + 1]
        if mode == "pair":
            out = jnp.concatenate([oa, ob], axis=0).T
        else:
            lam = lam_ref[...]
            lam_full = (jnp.exp(jnp.sum(lam[0:1] * lam[1:2], axis=-1, keepdims=True))
                        - jnp.exp(jnp.sum(lam[2:3] * lam[3:4], axis=-1, keepdims=True))
                        + (1.0 - diff_scale))
            o = (oa - lam_full * ob).T
            out = o * lax.rsqrt(jnp.mean(o * o, axis=-1, keepdims=True) + EPS) * subg_ref[...] * diff_scale
        o_ref[0] = out.astype(BF16)

    if ctx_keys:
        ks, vts = keys(slice(0, ctx_keys)), values(0, slice(0, ctx_keys))
        out = []
        for a in (0, 1):
            st = lax.dot_general(ks[a], qs[a], NT_DIMS, preferred_element_type=F32)
            out.append(update(st, jnp.max(st, axis=0, keepdims=True), vts[a], init[a]))
        finish(tuple(out))
        return

    (st_scr,) = scratch

    def chunk_keys(chunk_idx):
        start = chunk_idx * KV_CHUNK
        return keys(pl.ds(start if isinstance(start, int) else pl.multiple_of(start, KV_CHUNK), KV_CHUNK))

    def score(slot, a, ks):
        st = lax.dot_general(ks[a], qs[a], NT_DIMS, preferred_element_type=F32)
        st_scr[slot, a] = st
        return jnp.max(st, axis=0, keepdims=True)

    def scores(slot, chunk_idx):
        ks = chunk_keys(chunk_idx)
        return tuple(score(slot, a, ks) for a in (0, 1))

    def run(first, count, cmax, carry):
        for u in range(count):
            nxt, slot = first + u + 1, u % 2
            has_next = isinstance(nxt, jax.Array) or nxt < n_chunks
            m_new = [jnp.maximum(carry[a][0], cmax[a]) for a in (0, 1)]
            acc = [jnp.exp2(carry[a][0] - m_new[a]) * carry[a][1] for a in (0, 1)]
            cmax_next = [None, None]
            for j in range(KV_CHUNK // SUB_KEYS):
                rows = slice(j * SUB_KEYS, (j + 1) * SUB_KEYS)
                if has_next:
                    start = nxt * KV_CHUNK + j * SUB_KEYS
                    ks = keys(pl.ds(start if isinstance(start, int) else pl.multiple_of(start, SUB_KEYS), SUB_KEYS))
                    for a in (0, 1):
                        st = lax.dot_general(ks[a], qs[a], NT_DIMS, preferred_element_type=F32)
                        st_scr[1 - slot, a, rows, :] = st
                        mx = jnp.max(st, axis=0, keepdims=True)
                        cmax_next[a] = mx if cmax_next[a] is None else jnp.maximum(cmax_next[a], mx)
                vts = values(first + u, rows)
                for a in (0, 1):
                    p = jnp.exp2(st_scr[slot, a, rows, :] - m_new[a]).astype(BF16)
                    acc[a] = acc[a] + jnp.dot(vts[a], p, preferred_element_type=F32)
            carry = tuple((m_new[a], acc[a]) for a in (0, 1))
            cmax = tuple(cmax_next)
        return cmax, carry

    n_loop = (n_chunks - 1) // LOOP_CHUNKS
    cmax, carry = lax.fori_loop(0, n_loop, lambda i, st: run(i * LOOP_CHUNKS, LOOP_CHUNKS, *st),
                                (scores(0, 0), init))
    _, carry = run(n_loop * LOOP_CHUNKS, n_chunks - n_loop * LOOP_CHUNKS, cmax, carry)
    finish(carry)


def _flash(q, k, vt, lam, subg, *, mode, kw, k_cols, v_block_rows, v_rows, vw, k_index, v_index, n_ctx, diff_scale,
           name):
    b, s, nq = q.shape
    t = s - n_ctx
    n_pairs = nq // (2 * LANES)
    n_chunks = s // KV_CHUNK
    assert s % KV_CHUNK == 0 and n_ctx <= KV_CHUNK and t % FLASH_TQ == 0 and t % n_ctx == 0
    assert vt.shape[1] == n_chunks
    common = dict(mode=mode, k_cols=k_cols, v_rows=v_rows, vw=vw, diff_scale=diff_scale)
    o_x = pl.pallas_call(
        functools.partial(_flash_kernel, ctx_keys=0, **common),
        grid=(b, n_pairs, t // FLASH_TQ),
        in_specs=[
            pl.BlockSpec((1, FLASH_TQ, 2 * LANES), lambda bi, p, i: (bi, i, p)),
            pl.BlockSpec((1, s, kw), lambda bi, p, i: (bi, 0, k_index(p))),
            pl.BlockSpec((1, n_chunks, v_block_rows, KV_CHUNK), lambda bi, p, i: (bi, 0, v_index(p), 0)),
            _resident(lam.shape), _resident(subg.shape),
        ],
        out_specs=pl.BlockSpec((1, FLASH_TQ, LANES), lambda bi, p, i: (bi, i, p)),
        out_shape=jax.ShapeDtypeStruct((b, t, n_pairs * LANES), BF16),
        scratch_shapes=[pltpu.VMEM((2, 2, KV_CHUNK, FLASH_TQ), F32)],
        compiler_params=_cparams(3),
        name=name,
    )(q, k, vt, lam, subg)
    o_c = pl.pallas_call(
        functools.partial(_flash_kernel, ctx_keys=n_ctx, **common),
        grid=(b, n_pairs),
        in_specs=[
            pl.BlockSpec((1, n_ctx, 2 * LANES), lambda bi, p: (bi, t // n_ctx, p)),
            pl.BlockSpec((1, n_ctx, kw), lambda bi, p: (bi, 0, k_index(p))),
            pl.BlockSpec((1, 1, v_block_rows, KV_CHUNK), lambda bi, p: (bi, 0, v_index(p), 0)),
            _resident(lam.shape), _resident(subg.shape),
        ],
        out_specs=pl.BlockSpec((1, n_ctx, LANES), lambda bi, p: (bi, 0, p)),
        out_shape=jax.ShapeDtypeStruct((b, n_ctx, n_pairs * LANES), BF16),
        compiler_params=_cparams(2),
        name=name + "_ctx",
    )(q, k, vt, lam, subg)
    return o_x, o_c


def _na_tables(t):
    rows = t // GRID_W
    win_rows = min(NA_ROWS, rows)
    band_rows = min(win_rows + 1, rows)
    assert rows >= NA_WIN_ROWS and NA_QBLOCK % GRID_W == 0
    qrows = NA_QBLOCK // GRID_W
    nb = t // NA_QBLOCK
    nwin = NA_WIN_ROWS * GRID_W
    q_off, k_off = np.arange(NA_QBLOCK), np.arange(nwin)
    q_dr, q_col = q_off // GRID_W, q_off % GRID_W
    k_dr, k_col = k_off // GRID_W, k_off % GRID_W
    col_start = np.clip(q_col - NA_COLS // 2, 0, GRID_W - NA_COLS)
    col_in = (k_col[None, :] >= col_start[:, None]) & (k_col[None, :] < col_start[:, None] + NA_COLS)
    dc_idx = np.clip(k_col[None, :] - q_col[:, None] + NA_COLS - 1, 0, 2 * NA_COLS - 2)
    patterns, var, wstart = {}, [], []
    for j in range(nb):
        r = j * qrows + q_dr
        row_start = np.clip(r - win_rows // 2, 0, rows - win_rows)
        b0 = min(int(row_start[0]), rows - band_rows)
        w0 = min(b0 - b0 % 2, rows - NA_WIN_ROWS)
        assert w0 % qrows == 0 and w0 <= b0 and b0 + band_rows <= w0 + NA_WIN_ROWS
        k_row = w0 + k_dr
        in_band = (k_row >= b0) & (k_row < b0 + band_rows)
        in_win = (col_in & in_band[None, :] & (k_row[None, :] >= row_start[:, None])
                  & (k_row[None, :] < row_start[:, None] + win_rows))
        dr_idx = np.clip(k_row[None, :] - r[:, None] + NA_ROWS - 1, 0, 2 * NA_ROWS - 2)
        idx = np.where(in_win, dr_idx * (2 * NA_COLS - 1) + dc_idx, -1).astype(np.int32)
        key = idx.tobytes()
        if key not in patterns:
            patterns[key] = (len(patterns), idx)
        var.append(patterns[key][0])
        wstart.append(w0 * GRID_W)
    idx_all = np.stack([p[1] for p in sorted(patterns.values(), key=lambda p: p[0])])
    return idx_all.reshape(len(patterns), 1, -1), np.asarray(var, np.int32), np.asarray(wstart, np.int32)


def _na_bias_kernel(idx_ref, rpb_ref, o_ref):
    idx = idx_ref[0]
    r = rpb_ref[...]
    onehot = (lax.broadcasted_iota(jnp.int32, (r.shape[1], idx.shape[1]), 0) == idx).astype(BF16)
    b1 = r.astype(BF16)
    r2 = r - b1.astype(F32)
    b2 = r2.astype(BF16)
    b3 = (r2 - b2.astype(F32)).astype(BF16)
    out = (jnp.dot(b1, onehot, preferred_element_type=F32) + jnp.dot(b2, onehot, preferred_element_type=F32)
           + jnp.dot(b3, onehot, preferred_element_type=F32))
    o_ref[0] = jnp.where(idx >= 0, out, NEG_INF)


def _na_bias(idx_all, rpb):
    nvar, _, npos = idx_all.shape
    heads = rpb.shape[0]
    nr = (2 * NA_ROWS - 1) * (2 * NA_COLS - 1)
    nr_pad = pl.cdiv(nr, LANES) * LANES
    rflat = jnp.pad(rpb.reshape(heads, nr), ((0, 0), (0, nr_pad - nr)))
    tn = 2048
    assert npos % tn == 0
    out = pl.pallas_call(
        _na_bias_kernel,
        grid=(nvar, npos // tn),
        in_specs=[pl.BlockSpec((1, 1, tn), lambda v, j: (v, 0, j)), _resident(rflat.shape)],
        out_specs=pl.BlockSpec((1, heads, tn), lambda v, j: (v, 0, j)),
        out_shape=jax.ShapeDtypeStruct((nvar, heads, npos), F32),
        compiler_params=_cparams(2),
        name="na_bias",
    )(jnp.asarray(idx_all), rflat)
    return out.reshape(nvar, heads, NA_QBLOCK, npos // NA_QBLOCK)


def _halves(lo_src, hi_src):
    lane = lax.broadcasted_iota(jnp.int32, lo_src.shape, 1)
    return jnp.where(lane < LANES // 2, lo_src, hi_src)


def _na_kernel(var_ref, ws_ref, q_ref, k_ref, v_ref, b0_ref, b1_ref, o_ref, *, n_ctx, n_ctx_tiles):
    qi = pl.program_id(2)
    nwin = NA_WIN_ROWS * GRID_W

    @pl.when(qi < n_ctx_tiles)
    def _():
        kc, vc = k_ref[0, 0:n_ctx, :], v_ref[0, 0:n_ctx, :]
        outs = []
        for a in (0, 1):
            s_c = lax.dot_general(q_ref[0, :, a * LANES:(a + 1) * LANES], kc, NT_DIMS,
                                  preferred_element_type=F32)
            p_c = jnp.exp(s_c - jnp.max(s_c, axis=-1, keepdims=True))
            acc = jnp.dot(p_c.astype(BF16), vc, preferred_element_type=F32)
            outs.append(acc / jnp.sum(p_c, axis=-1, keepdims=True))
        o_ref[0] = _halves(outs[0], outs[1]).astype(BF16)

    @pl.when(qi >= n_ctx_tiles)
    def _():
        kc, vc = k_ref[0, 0:n_ctx, :], v_ref[0, 0:n_ctx, :]
        work = []
        for sub, bias_ref in ((0, b0_ref), (1, b1_ref)):
            j = (qi - n_ctx_tiles) * (TOKEN_TILE // NA_QBLOCK) + sub
            start = pl.multiple_of(n_ctx + ws_ref[j], NA_QBLOCK)
            kw, vw = k_ref[0, pl.ds(start, nwin), :], v_ref[0, pl.ds(start, nwin), :]
            for a in (0, 1):
                q = q_ref[0, sub * NA_QBLOCK:(sub + 1) * NA_QBLOCK, a * LANES:(a + 1) * LANES]
                s_c = lax.dot_general(q, kc, NT_DIMS, preferred_element_type=F32)
                s_w = lax.dot_general(q, kw, NT_DIMS, preferred_element_type=F32) + bias_ref[0, a]
                work.append((s_c, s_w, vw))
        outs = []
        for s_c, s_w, vw in work:
            m = jnp.maximum(jnp.max(s_c, axis=-1, keepdims=True), jnp.max(s_w, axis=-1, keepdims=True))
            p_c, p_w = jnp.exp(s_c - m), jnp.exp(s_w - m)
            l = jnp.sum(p_c, axis=-1, keepdims=True) + jnp.sum(p_w, axis=-1, keepdims=True)
            acc = (jnp.dot(p_c.astype(BF16), vc, preferred_element_type=F32)
                   + jnp.dot(p_w.astype(BF16), vw, preferred_element_type=F32))
            outs.append(acc / l)
        for sub in (0, 1):
            o_ref[0, sub * NA_QBLOCK:(sub + 1) * NA_QBLOCK, :] = _halves(outs[2 * sub], outs[2 * sub + 1]).astype(BF16)


def _na_attention(q, k, v, bias, var, wstart, *, n_ctx):
    b, s, nq = q.shape
    tq = TOKEN_TILE
    n_pairs = nq // (2 * LANES)
    n_ctx_tiles = n_ctx // tq
    per_tile = tq // NA_QBLOCK
    nwin = NA_WIN_ROWS * GRID_W

    def bias_spec(sub):
        def index(bi, p, i, var_ref, ws_ref):
            j = jnp.maximum(i - n_ctx_tiles, 0) * per_tile + sub
            return (var_ref[j], p, 0, 0)
        return pl.BlockSpec((1, 2, NA_QBLOCK, nwin), index)

    grid_spec = pltpu.PrefetchScalarGridSpec(
        num_scalar_prefetch=2,
        grid=(b, n_pairs, s // tq),
        in_specs=[
            pl.BlockSpec((1, tq, 2 * LANES), lambda bi, p, i, *_: (bi, i, p)),
            pl.BlockSpec((1, s, LANES), lambda bi, p, i, *_: (bi, 0, p)),
            pl.BlockSpec((1, s, LANES), lambda bi, p, i, *_: (bi, 0, p)),
            bias_spec(0), bias_spec(1),
        ],
        out_specs=pl.BlockSpec((1, tq, LANES), lambda bi, p, i, *_: (bi, i, p)),
    )
    return pl.pallas_call(
        functools.partial(_na_kernel, n_ctx=n_ctx, n_ctx_tiles=n_ctx_tiles),
        grid_spec=grid_spec,
        out_shape=jax.ShapeDtypeStruct((b, s, n_pairs * LANES), BF16),
        compiler_params=_cparams(3),
        name="na_attn",
    )(var, wstart, q, k, v, bias, bias)


O_HALO = 16


def _mix_ffn_kernel(xp_ref, x_ref, xn_ref, op_ref, ox_ref, oc_ref, on_ref, wo_ref, mod_ref, g_ref, wup_ref, cw_ref,
                    cb_ref, wdn_ref, fg_ref, y_ref, u_scr, a_scr, *, n_ctx_tiles, n_tiles, first_tile, chunk,
                    final_norm):
    t = pl.program_id(1) + first_tile
    tm = x_ref.shape[1]
    has_prev = jnp.logical_and(t != 0, t != n_ctx_tiles).astype(F32)
    has_next = jnp.logical_and(t != n_ctx_tiles - 1, t != n_tiles - 1).astype(F32)
    o = jnp.where(t < n_ctx_tiles, oc_ref[0], ox_ref[0]).astype(F32)
    o_cat = jnp.concatenate([op_ref[0].astype(F32)[O_HALO - HALO:], o, on_ref[0].astype(F32)[:HALO]], axis=0)
    x_cat = jnp.concatenate([xp_ref[0], x_ref[0], xn_ref[0]], axis=0)
    x1 = x_cat + mod_ref[0, 2:3, :] * jnp.dot(o_cat.astype(BF16), wo_ref[...], preferred_element_type=F32)
    g, shift, scale = g_ref[...], mod_ref[0, 3:4, :], mod_ref[0, 4:5, :]
    x = x1[HALO:HALO + tm]
    h = jnp.concatenate([_norm_mod(x1[:HALO], g, shift, scale) * has_prev,
                         _norm_mod(x, g, shift, scale),
                         _norm_mod(x1[HALO + tm:], g, shift, scale) * has_next], axis=0).astype(BF16)
    n_chunks = wup_ref.shape[1] // (2 * chunk)
    for j in range(n_chunks):
        cols = slice(j * 2 * chunk, (j + 1) * 2 * chunk)
        u_scr[...] = jnp.dot(h, wup_ref[:, cols], preferred_element_type=F32)
        cw = cw_ref[:, cols]
        u = (u_scr[HALO - 1:HALO - 1 + tm, :] * cw[0:1] + u_scr[HALO:HALO + tm, :] * cw[1:2]
             + u_scr[HALO + 1:HALO + 1 + tm, :] * cw[2:3] + cb_ref[:, cols])
        val, gate = u[:, :chunk], u[:, chunk:]
        a_scr[:, j * chunk:(j + 1) * chunk] = (gate * (1.0 / (1.0 + jnp.exp(-gate))) * val).astype(BF16)
    y = x + mod_ref[0, 5:6, :] * jnp.dot(a_scr[...], wdn_ref[...], preferred_element_type=F32)
    if final_norm:
        y = y * lax.rsqrt(jnp.mean(y * y, axis=-1, keepdims=True) + EPS) * fg_ref[...]
    y_ref[0] = y


def _mix_ffn(xs, o_x, o_c, w_out, mods, g, wup, cw, cb, wdn, final_g, *, n_ctx_tiles, x_offset, latent_only, chunk,
             final_norm):
    b, s, d = xs.shape
    tm = TOKEN_TILE
    n_tiles = s // tm
    n = o_x.shape[2]
    f = wdn.shape[0]
    first = n_ctx_tiles if latent_only else 0
    assert n_ctx_tiles == 1
    per, per_o = tm // HALO, tm // O_HALO
    lat = lambda i: jnp.maximum(i + first, n_ctx_tiles) - x_offset
    return pl.pallas_call(
        functools.partial(_mix_ffn_kernel, n_ctx_tiles=n_ctx_tiles, n_tiles=n_tiles, first_tile=first, chunk=chunk,
                          final_norm=final_norm),
        grid=(b, n_tiles - first),
        in_specs=[
            pl.BlockSpec((1, HALO, d), lambda bi, i: (bi, jnp.maximum((i + first) * per - 1, 0), 0)),
            pl.BlockSpec((1, tm, d), lambda bi, i: (bi, i + first, 0)),
            pl.BlockSpec((1, HALO, d), lambda bi, i: (bi, jnp.minimum((i + first + 1) * per, s // HALO - 1), 0)),
            pl.BlockSpec((1, O_HALO, n), lambda bi, i: (bi, jnp.maximum(lat(i) * per_o - 1, 0), 0)),
            pl.BlockSpec((1, tm, n), lambda bi, i: (bi, lat(i), 0)),
            pl.BlockSpec((1, tm, n), lambda bi, i: (bi, jnp.minimum(i + first, n_ctx_tiles - 1), 0)),
            pl.BlockSpec((1, O_HALO, n),
                         lambda bi, i: (bi, jnp.minimum((lat(i) + 1) * per_o, o_x.shape[1] // O_HALO - 1), 0)),
            _resident(w_out.shape),
            pl.BlockSpec((1, 6, d), lambda bi, i: (jnp.where(i + first < n_ctx_tiles, b, bi), 0, 0)),
            _resident((1, d)),
            pl.BlockSpec(wup.shape, lambda bi, i: (0, 0), pipeline_mode=pl.Buffered(1)),
            _resident(cw.shape), _resident(cb.shape),
            pl.BlockSpec(wdn.shape, lambda bi, i: (0, 0), pipeline_mode=pl.Buffered(1)),
            _resident((1, d)),
        ],
        out_specs=pl.BlockSpec((1, tm, d), lambda bi, i: (bi, i, 0)),
        out_shape=jax.ShapeDtypeStruct((b, s - first * tm, d), F32),
        scratch_shapes=[pltpu.VMEM((tm + 2 * HALO, 2 * chunk), F32), pltpu.VMEM((tm, f), BF16)],
        compiler_params=_cparams(2),
        name="mix_ffn",
    )(xs, xs, xs, o_x, o_x, o_c, o_x, w_out, mods, g, wup, cw, cb, wdn, final_g)


def _rope_table(t, n_ctx, rot_dim, lane_lo):
    tt = jnp.arange(t)
    row = (tt // GRID_W).astype(F32)
    col = (tt % GRID_W).astype(F32)
    half = rot_dim // 2
    inv = ROPE_BASE ** (-jnp.arange(0, half, 2, dtype=F32) / half)
    ar, ac = row[:, None] * inv, col[:, None] * inv
    ang = jnp.concatenate([ar, ar, ac, ac], axis=-1)
    quarter = rot_dim // 4
    sign = jnp.where((jnp.arange(rot_dim) % (2 * quarter)) < quarter, -1.0, 1.0).astype(F32)
    cos, sin = jnp.cos(ang), jnp.sin(ang) * sign
    reps = (LANES - lane_lo) // rot_dim if lane_lo == 0 else 1
    cos_l = jnp.ones((t, LANES), F32).at[:, lane_lo:lane_lo + reps * rot_dim].set(jnp.tile(cos, (1, reps)))
    sin_l = jnp.zeros((t, LANES), F32).at[:, lane_lo:lane_lo + reps * rot_dim].set(jnp.tile(sin, (1, reps)))
    cos_s = jnp.concatenate([jnp.ones((n_ctx, LANES), F32), cos_l], axis=0)
    sin_s = jnp.concatenate([jnp.zeros((n_ctx, LANES), F32), sin_l], axis=0)
    return cos_s, sin_s


def _to_slots(w, n_heads, width, halves):
    d = w.shape[0]
    wh = w.reshape(d, n_heads, width)
    lo = jnp.pad(wh, ((0, 0), (0, 0), (0, LANES - width)))
    if not any(halves):
        return lo.reshape(d, n_heads * LANES)
    hi = jnp.pad(wh, ((0, 0), (0, 0), (LANES // 2, LANES // 2 - width)))
    in_hi = jnp.asarray(halves, jnp.int32).reshape(1, n_heads, 1) == 1
    return jnp.where(in_hi, hi, lo).reshape(d, n_heads * LANES)


def _vt_weights(wv, groups, width, tm):
    d = wv.shape[0]
    rows = width + 16
    wt = jnp.zeros((groups, rows, d), wv.dtype).at[:, :width, :].set(wv.T.reshape(groups, width, d))
    ones = jnp.zeros((groups, rows, tm), F32).at[:, width, :].set(1.0)
    return wt.reshape(groups * rows, d), ones.reshape(groups * rows, tm)


def _dup(g):
    return jnp.concatenate([g, g]).reshape(1, LANES).astype(F32)


def kernel(x, c, ctx, c_ctx, ada_w, ada_b, norm1_g, norm2_g, ffn_w_up, ffn_conv_w, ffn_conv_b, ffn_w_down,
           gqa_w_in, gqa_q_norm_g, gqa_k_norm_g, gqa_w_out,
           mla_w_in, mla_q_norm_g, mla_kv_norm_g, mla_w_uq, mla_w_ukv, mla_w_out,
           diff_w_in, diff_lambda, diff_subln_g, diff_w_out,
           na_w_in, na_rpb, na_w_out, final_norm_g):
    b, t, d = x.shape
    n_ctx = ctx.shape[1]
    depth = ada_w.shape[0]
    assert n_ctx % TOKEN_TILE == 0 and t % TOKEN_TILE == 0 and b + 1 <= 8
    nct = n_ctx // TOKEN_TILE
    f = ffn_w_down.shape[1]
    chunk = 2 * LANES
    assert f % chunk == 0

    xs = jnp.concatenate([ctx, x], axis=1)
    cin = jnp.zeros((8, d), F32).at[:b].set(c).at[b].set(c_ctx)
    mods_all = _ada_all(cin, ada_w, ada_b).reshape(depth, 8, 6, d)

    tm = TOKEN_TILE
    ones_l = jnp.ones((1, LANES), F32)
    dummy_tab = jnp.zeros((xs.shape[1], LANES), F32)
    dummy_ones = jnp.zeros((8, LANES), F32)
    cos64, sin64 = _rope_table(t, n_ctx, GQA_HEAD_DIM, 0)
    cos_mla, sin_mla = _rope_table(t, n_ctx, MLA_ROPE_DIM, MLA_NOPE_DIM)
    no_lam, no_g = jnp.zeros((4, DIFF_HEAD_DIM), F32), jnp.ones((1, LANES), F32)
    pair_rows = MLA_V_DIM + 16
    diff_rows = 2 * DIFF_HEAD_DIM + 16

    for i in range(depth):
        m, j = i % N_MIXERS, i // N_MIXERS
        mods = mods_all[i]
        g1 = norm1_g[i].reshape(1, d)
        if m == 0:
            w = gqa_w_in[j].astype(BF16)
            nqc, nkc = GQA_HEADS * GQA_HEAD_DIM, GQA_KV_HEADS * GQA_HEAD_DIM
            wk = w[:, nqc:nqc + nkc].reshape(d, GQA_KV_HEADS, 1, GQA_HEAD_DIM)
            wk = jnp.broadcast_to(wk, (d, GQA_KV_HEADS, 2, GQA_HEAD_DIM)).reshape(d, GQA_KV_HEADS * LANES)
            wvt, vones = _vt_weights(w[:, nqc + nkc:], GQA_KV_HEADS, GQA_HEAD_DIM, tm)
            qc = GQA_HEAD_DIM ** -0.5 * LOG2E
            q, k, vt = _project(xs, mods, g1, w[:, :nqc], wk, wvt, _dup(gqa_q_norm_g[j]), _dup(gqa_k_norm_g[j]),
                                cos64 * qc, sin64 * qc, cos64, sin64, vones, qk_norm=True, rope=True,
                                head_dim=GQA_HEAD_DIM, k_norm_dim=2 * GQA_HEAD_DIM, v_transposed=True, q_rotated=True, n_ctx_tiles=nct,
                                name="proj_gqa")
            o_x, o_c = _flash(q, k, vt, no_lam, no_g, mode="pair", kw=LANES, k_cols=((0, LANES), (0, LANES)),
                       v_block_rows=pair_rows, v_rows=((0, pair_rows), (0, pair_rows)), vw=GQA_HEAD_DIM,
                       k_index=lambda p: p // 2, v_index=lambda p: p // 2, n_ctx=n_ctx, diff_scale=1.0,
                       name="attn_gqa")
            w_out = gqa_w_out[j]
        elif m == 1:
            w = mla_w_in[j].astype(BF16)
            nlat = MLA_Q_LORA + MLA_KV_LORA
            kr_slot = jnp.zeros((d, LANES), BF16).at[:, MLA_NOPE_DIM:MLA_NOPE_DIM + MLA_ROPE_DIM].set(w[:, nlat:])
            win = jnp.concatenate([w[:, :nlat], kr_slot], axis=1)
            qk_dim = MLA_NOPE_DIM + MLA_ROPE_DIM
            wuq = _to_slots(mla_w_uq[j].astype(BF16), MLA_HEADS, qk_dim, [0] * MLA_HEADS)
            wukv = mla_w_ukv[j].astype(BF16).reshape(MLA_KV_LORA, MLA_HEADS, MLA_NOPE_DIM + MLA_V_DIM)
            wuk = _to_slots(wukv[:, :, :MLA_NOPE_DIM].reshape(MLA_KV_LORA, -1), MLA_HEADS, MLA_NOPE_DIM,
                            [0] * MLA_HEADS)
            wuvt, vones = _vt_weights(wukv[:, :, MLA_NOPE_DIM:].reshape(MLA_KV_LORA, MLA_HEADS * MLA_V_DIM),
                                      MLA_HEADS, MLA_V_DIM, tm)
            qc = qk_dim ** -0.5 * LOG2E
            q, k, vt = _project_mla(xs, mods, g1, win, mla_q_norm_g[j].reshape(1, -1),
                                    mla_kv_norm_g[j].reshape(1, -1), wuq, wuk, wuvt,
                                    cos_mla * qc, sin_mla * qc, cos_mla, sin_mla, vones, n_ctx_tiles=nct)
            o_x, o_c = _flash(q, k, vt, no_lam, no_g, mode="pair", kw=2 * LANES, k_cols=((0, LANES), (LANES, 2 * LANES)),
                       v_block_rows=2 * pair_rows, v_rows=((0, pair_rows), (pair_rows, 2 * pair_rows)),
                       vw=MLA_V_DIM, k_index=lambda p: p, v_index=lambda p: p, n_ctx=n_ctx, diff_scale=1.0,
                       name="attn_mla")
            w_out = mla_w_out[j]
        elif m == 2:
            w = diff_w_in[j].astype(BF16)
            wvt, vones = _vt_weights(w[:, 2 * d:], DIFF_HEADS, 2 * DIFF_HEAD_DIM, tm)
            lambda_init = 0.8 - 0.6 * math.exp(-0.3 * i)
            qc = DIFF_HEAD_DIM ** -0.5 * LOG2E
            q, k, vt = _project(xs, mods, g1, w[:, :d], w[:, d:2 * d], wvt, ones_l, ones_l,
                                cos64 * qc, sin64 * qc, cos64, sin64, vones, qk_norm=False, rope=True,
                                head_dim=DIFF_HEAD_DIM, k_norm_dim=DIFF_HEAD_DIM, v_transposed=True, q_rotated=True, n_ctx_tiles=nct,
                                name="proj_diff")
            o_x, o_c = _flash(q, k, vt, diff_lambda[j], diff_subln_g[j].reshape(1, LANES), mode="diff", kw=LANES,
                       k_cols=((0, LANES), (0, LANES)), v_block_rows=diff_rows,
                       v_rows=((0, diff_rows), (0, diff_rows)), vw=2 * DIFF_HEAD_DIM,
                       k_index=lambda p: p, v_index=lambda p: p, n_ctx=n_ctx, diff_scale=1.0 - lambda_init,
                       name="attn_diff")
            w_out = diff_w_out[j]
        else:
            w = na_w_in[j].astype(BF16)
            na_scale = NA_HEAD_DIM ** -0.5
            assert math.frexp(na_scale)[0] == 0.5
            q, k, v = _project(xs, mods, g1, w[:, :d] * jnp.asarray(na_scale, BF16), w[:, d:2 * d], w[:, 2 * d:],
                               ones_l, ones_l, dummy_tab, dummy_tab, dummy_tab, dummy_tab, dummy_ones,
                               qk_norm=False, rope=False, head_dim=NA_HEAD_DIM, k_norm_dim=NA_HEAD_DIM,
                               v_transposed=False, q_rotated=False, n_ctx_tiles=nct,
                               name="proj_na")
            idx_all, var, wstart = _na_tables(t)
            bias = _na_bias(idx_all, na_rpb[j])
            o_x = o_c = _na_attention(q, k, v, bias, jnp.asarray(var), jnp.asarray(wstart), n_ctx=n_ctx)
            w_out = na_w_out[j]
        last = i == depth - 1

        def regroup(a):
            lead = a.shape[:-1]
            return jnp.stack([a[..., :f].reshape(*lead, f // chunk, chunk),
                              a[..., f:].reshape(*lead, f // chunk, chunk)], axis=-2).reshape(*lead, 2 * f)
        xs = _mix_ffn(xs, o_x, o_c, w_out.astype(BF16), mods, norm2_g[i].reshape(1, d),
                      regroup(ffn_w_up[i].astype(BF16)), regroup(ffn_conv_w[i]), regroup(ffn_conv_b[i]).reshape(1, 2 * f),
                      ffn_w_down[i].astype(BF16), final_norm_g.reshape(1, d), n_ctx_tiles=nct,
                      x_offset=0 if m == 3 else nct, latent_only=last, chunk=chunk, final_norm=last)
    return xs
```

```python
import functools
import math

import numpy as np
import jax
import jax.numpy as jnp
from jax import lax
from jax.experimental import pallas as pl
from jax.experimental.pallas import tpu as pltpu

F32 = jnp.float32
BF16 = jnp.bfloat16

LANES = 128
TOKEN_TILE = 256
HALO = 8
KV_CHUNK = 768
SUB_KEYS = 256
FLASH_TQ = 512
LOOP_CHUNKS = 4
VMEM_LIMIT = 56 * 1024 * 1024

GRID_W = 64
ROPE_BASE = 10000.0
EPS = 1e-6
NEG_INF = -1e30
LOG2E = 1.4426950408889634

GQA_HEADS, GQA_KV_HEADS, GQA_HEAD_DIM = 16, 4, 64
MLA_HEADS, MLA_NOPE_DIM, MLA_ROPE_DIM, MLA_V_DIM = 16, 64, 32, 64
MLA_Q_LORA, MLA_KV_LORA = 384, 256
DIFF_HEADS, DIFF_HEAD_DIM = 8, 64
NA_HEADS, NA_HEAD_DIM, NA_ROWS, NA_COLS = 16, 64, 8, 16
NA_QBLOCK = 128
NA_WIN_ROWS = 10
N_MIXERS = 4

NT_DIMS = (((1,), (1,)), ((), ()))


def _cparams(n_axes):
    return pltpu.CompilerParams(dimension_semantics=("arbitrary",) * n_axes, vmem_limit_bytes=VMEM_LIMIT)


def _resident(shape):
    nd = len(shape)
    return pl.BlockSpec(shape, lambda *_: (0,) * nd)


def _ada_kernel(c_ref, w_ref, b_ref, o_ref):
    c = c_ref[...]
    s = c * (1.0 / (1.0 + jnp.exp(-c)))
    o_ref[0] = jnp.dot(s.astype(BF16), w_ref[0].astype(BF16), preferred_element_type=F32) + b_ref[0]


def _ada_all(cin, ada_w, ada_b):
    depth, d, n = ada_w.shape
    tn = n // 4
    return pl.pallas_call(
        _ada_kernel,
        grid=(depth, n // tn),
        in_specs=[
            pl.BlockSpec((8, d), lambda l, j: (0, 0)),
            pl.BlockSpec((1, d, tn), lambda l, j: (l, 0, j)),
            pl.BlockSpec((1, 1, tn), lambda l, j: (l, 0, j)),
        ],
        out_specs=pl.BlockSpec((1, 8, tn), lambda l, j: (l, 0, j)),
        out_shape=jax.ShapeDtypeStruct((depth, 8, n), F32),
        compiler_params=_cparams(2),
        name="ada",
    )(cin, ada_w, ada_b.reshape(depth, 1, n))


def _norm_mod(x, g, shift, scale):
    ms = jnp.mean(x * x, axis=-1, keepdims=True)
    return (x * lax.rsqrt(ms + EPS) * g) * (1.0 + scale) + shift


def _rope_slot(y, cos, sin_signed, lo_mask, quarter):
    fwd = pltpu.roll(y, LANES - quarter, axis=1)
    bwd = pltpu.roll(y, quarter, axis=1)
    return y * cos + jnp.where(lo_mask, fwd, bwd) * sin_signed


def _lo_mask(shape, quarter):
    lane = lax.broadcasted_iota(jnp.int32, shape, 1)
    return (lane % (2 * quarter)) < quarter


def _q_spec(n, tm, n_tiles, n_ctx_tiles, rotate):
    if not rotate:
        return pl.BlockSpec((1, tm, n), lambda bi, i: (bi, i, 0))
    n_lat = n_tiles - n_ctx_tiles
    return pl.BlockSpec((1, tm, n), lambda bi, i: (bi, jnp.where(i < n_ctx_tiles, i + n_lat, i - n_ctx_tiles), 0))


def _vt_spec(rows, tm):
    per = KV_CHUNK // tm
    return pl.BlockSpec((1, 1, rows, tm), lambda bi, i: (bi, i // per, 0, i % per))


def _proj_kernel(x_ref, mod_ref, g_ref, wq_ref, wk_ref, wv_ref, qg_ref, kg_ref, cosq_ref, sinq_ref, cosk_ref, sink_ref,
                 vones_ref, q_ref, k_ref, v_ref, *, qk_norm, rope, head_dim, k_norm_dim, v_transposed):
    h = _norm_mod(x_ref[0], g_ref[...], mod_ref[0, 0:1, :], mod_ref[0, 1:2, :]).astype(BF16)
    tm = h.shape[0]
    lo = _lo_mask((tm, LANES), head_dim // 4) if rope else None
    low_half = lax.broadcasted_iota(jnp.int32, (tm, LANES), 1) < LANES // 2

    def chunks(w_ref):
        n = w_ref.shape[1]
        for c0 in range(0, n, 2 * LANES):
            c1 = min(c0 + 2 * LANES, n)
            y2 = jnp.dot(h, w_ref[:, c0:c1], preferred_element_type=F32)
            for s0 in range(0, c1 - c0, LANES):
                yield (c0 + s0) // LANES, y2[:, s0:s0 + LANES]

    for c, y in chunks(wq_ref):
        if qk_norm:
            sq = y * y
            ms_lo = jnp.sum(jnp.where(low_half, sq, 0.0), axis=-1, keepdims=True) * (1.0 / head_dim)
            ms_hi = jnp.sum(jnp.where(low_half, 0.0, sq), axis=-1, keepdims=True) * (1.0 / head_dim)
            y = y * jnp.where(low_half, lax.rsqrt(ms_lo + EPS), lax.rsqrt(ms_hi + EPS)) * qg_ref[...]
        if rope:
            y = _rope_slot(y, cosq_ref[...], sinq_ref[...], lo, head_dim // 4)
        q_ref[0, :, 2 * c * LANES:(2 * c + 1) * LANES] = jnp.where(low_half, y, 0.0).astype(BF16)
        q_ref[0, :, (2 * c + 1) * LANES:(2 * c + 2) * LANES] = jnp.where(low_half, 0.0, y).astype(BF16)
    for c, y in chunks(wk_ref):
        if qk_norm:
            ms = jnp.sum(y * y, axis=-1, keepdims=True) * (1.0 / k_norm_dim)
            y = y * lax.rsqrt(ms + EPS) * kg_ref[...]
        if rope:
            y = _rope_slot(y, cosk_ref[...], sink_ref[...], lo, head_dim // 4)
        k_ref[0, :, c * LANES:(c + 1) * LANES] = y.astype(BF16)
    if v_transposed:
        vt = lax.dot_general(wv_ref[...], h, NT_DIMS, preferred_element_type=F32) + vones_ref[...]
        v_ref[0, 0] = vt.astype(BF16)
    else:
        v_ref[0] = jnp.dot(h, wv_ref[...], preferred_element_type=F32).astype(BF16)


def _project(xs, mods, g, wq, wk, wv, qg, kg, cosq, sinq, cosk, sink, vones, *, qk_norm, rope, head_dim, k_norm_dim,
             v_transposed, q_rotated, n_ctx_tiles, name):
    b, s, d = xs.shape
    tm = TOKEN_TILE
    nq, nk = 2 * wq.shape[1], wk.shape[1]
    tok = lambda n: pl.BlockSpec((1, tm, n), lambda bi, i: (bi, i, 0))
    tab = pl.BlockSpec((tm, LANES), lambda bi, i: (i, 0))
    if v_transposed:
        v_spec = _vt_spec(wv.shape[0], tm)
        v_shape = jax.ShapeDtypeStruct((b, s // KV_CHUNK, wv.shape[0], KV_CHUNK), BF16)
    else:
        v_spec, v_shape = tok(wv.shape[1]), jax.ShapeDtypeStruct((b, s, wv.shape[1]), BF16)
    return pl.pallas_call(
        functools.partial(_proj_kernel, qk_norm=qk_norm, rope=rope, head_dim=head_dim, k_norm_dim=k_norm_dim,
                          v_transposed=v_transposed),
        grid=(b, s // tm),
        in_specs=[
            tok(d),
            pl.BlockSpec((1, 6, d), lambda bi, i: (jnp.where(i < n_ctx_tiles, b, bi), 0, 0)),
            _resident((1, d)),
            _resident(wq.shape), _resident(wk.shape), _resident(wv.shape),
            _resident((1, LANES)), _resident((1, LANES)),
            tab, tab, tab, tab,
            _resident(vones.shape),
        ],
        out_specs=[_q_spec(nq, tm, s // tm, n_ctx_tiles, q_rotated), tok(nk), v_spec],
        out_shape=[jax.ShapeDtypeStruct((b, s, nq), BF16), jax.ShapeDtypeStruct((b, s, nk), BF16), v_shape],
        compiler_params=_cparams(2),
        name=name,
    )(xs, mods, g, wq, wk, wv, qg, kg, cosq, sinq, cosk, sink, vones)


def _mla_proj_kernel(x_ref, mod_ref, g_ref, win_ref, qn_ref, kvn_ref, wuq_ref, wuk_ref, wuvt_ref,
                     cosq_ref, sinq_ref, cosk_ref, sink_ref, vones_ref, q_ref, k_ref, v_ref):
    h = _norm_mod(x_ref[0], g_ref[...], mod_ref[0, 0:1, :], mod_ref[0, 1:2, :]).astype(BF16)
    tm = h.shape[0]
    t1 = jnp.dot(h, win_ref[...], preferred_element_type=F32)
    cq = t1[:, :MLA_Q_LORA]
    ckv = t1[:, MLA_Q_LORA:MLA_Q_LORA + MLA_KV_LORA]
    kr = t1[:, MLA_Q_LORA + MLA_KV_LORA:]
    cqn = (cq * lax.rsqrt(jnp.mean(cq * cq, axis=-1, keepdims=True) + EPS) * qn_ref[...]).astype(BF16)
    ckvn = (ckv * lax.rsqrt(jnp.mean(ckv * ckv, axis=-1, keepdims=True) + EPS) * kvn_ref[...]).astype(BF16)
    quarter = MLA_ROPE_DIM // 4
    lo = _lo_mask((tm, LANES), quarter)
    kr = _rope_slot(kr, cosk_ref[...], sink_ref[...], lo, quarter)
    n = wuq_ref.shape[1]
    for c0 in range(0, n, 2 * LANES):
        q2 = jnp.dot(cqn, wuq_ref[:, c0:c0 + 2 * LANES], preferred_element_type=F32)
        k2 = jnp.dot(ckvn, wuk_ref[:, c0:c0 + 2 * LANES], preferred_element_type=F32)
        for s0 in (0, LANES):
            q_ref[0, :, c0 + s0:c0 + s0 + LANES] = _rope_slot(
                q2[:, s0:s0 + LANES], cosq_ref[...], sinq_ref[...], lo, quarter).astype(BF16)
            k_ref[0, :, c0 + s0:c0 + s0 + LANES] = (k2[:, s0:s0 + LANES] + kr).astype(BF16)
    vt = lax.dot_general(wuvt_ref[...], ckvn, NT_DIMS, preferred_element_type=F32) + vones_ref[...]
    v_ref[0, 0] = vt.astype(BF16)


def _project_mla(xs, mods, g, win, qn, kvn, wuq, wuk, wuvt, cosq, sinq, cosk, sink, vones, *, n_ctx_tiles):
    b, s, d = xs.shape
    tm = TOKEN_TILE
    nq, nk = wuq.shape[1], wuk.shape[1]
    tok = lambda n: pl.BlockSpec((1, tm, n), lambda bi, i: (bi, i, 0))
    tab = pl.BlockSpec((tm, LANES), lambda bi, i: (i, 0))
    return pl.pallas_call(
        _mla_proj_kernel,
        grid=(b, s // tm),
        in_specs=[
            tok(d),
            pl.BlockSpec((1, 6, d), lambda bi, i: (jnp.where(i < n_ctx_tiles, b, bi), 0, 0)),
            _resident((1, d)),
            _resident(win.shape), _resident(qn.shape), _resident(kvn.shape),
            _resident(wuq.shape), _resident(wuk.shape), _resident(wuvt.shape),
            tab, tab, tab, tab,
            _resident(vones.shape),
        ],
        out_specs=[_q_spec(nq, tm, s // tm, n_ctx_tiles, True), tok(nk), _vt_spec(wuvt.shape[0], tm)],
        out_shape=[jax.ShapeDtypeStruct((b, s, nq), BF16), jax.ShapeDtypeStruct((b, s, nk), BF16),
                   jax.ShapeDtypeStruct((b, s // KV_CHUNK, wuvt.shape[0], KV_CHUNK), BF16)],
        compiler_params=_cparams(2),
        name="proj_mla",
    )(xs, mods, g, win, qn, kvn, wuq, wuk, wuvt, cosq, sinq, cosk, sink, vones)


def _flash_kernel(q_ref, k_ref, vt_ref, lam_ref, subg_ref, o_ref, *scratch, mode, k_cols, v_rows, vw, diff_scale,
                  ctx_keys):
    tq = q_ref.shape[1]
    n_chunks = vt_ref.shape[1]
    qs = [q_ref[0, :, a * LANES:(a + 1) * LANES] for a in (0, 1)]
    n_rows = v_rows[0][1] - v_rows[0][0]

    def keys(key_rows):
        k0 = k_ref[0, key_rows, k_cols[0][0]:k_cols[0][1]]
        k1 = k0 if k_cols[1] == k_cols[0] else k_ref[0, key_rows, k_cols[1][0]:k_cols[1][1]]
        return k0, k1

    def values(chunk_idx, key_lanes):
        v0 = vt_ref[0, chunk_idx, v_rows[0][0]:v_rows[0][1], key_lanes]
        v1 = v0 if v_rows[1] == v_rows[0] else vt_ref[0, chunk_idx, v_rows[1][0]:v_rows[1][1], key_lanes]
        return v0, v1

    def update(st, cmax, vt_a, carry_a):
        m, acc = carry_a
        m_new = jnp.maximum(m, cmax)
        p = jnp.exp2(st - m_new).astype(BF16)
        return m_new, jnp.exp2(m - m_new) * acc + jnp.dot(vt_a, p, preferred_element_type=F32)

    init = tuple((jnp.full((1, tq), -jnp.inf, F32), jnp.zeros((n_rows, tq), F32)) for _ in (0, 1))

    def finish(carry):
        (_, acc_a), (_, acc_b) = carry
        oa = acc_a[0:vw] / acc_a[vw:vw + 1]
        ob = acc_b[0:vw] / acc_b[vw:vw + 1]
        if mode == "pair":
            out = jnp.concatenate([oa, ob], axis=0).T
        else:
            lam = lam_ref[...]
            lam_full = (jnp.exp(jnp.sum(lam[0:1] * lam[1:2], axis=-1, keepdims=True))
                        - jnp.exp(jnp.sum(lam[2:3] * lam[3:4], axis=-1, keepdims=True))
                        + (1.0 - diff_scale))
            o = (oa - lam_full * ob).T
            out = o * lax.rsqrt(jnp.mean(o * o, axis=-1, keepdims=True) + EPS) * subg_ref[...] * diff_scale
        o_ref[0] = out.astype(BF16)

    if ctx_keys:
        ks, vts = keys(slice(0, ctx_keys)), values(0, slice(0, ctx_keys))
        out = []
        for a in (0, 1):
            st = lax.dot_general(ks[a], qs[a], NT_DIMS, preferred_element_type=F32)
            out.append(update(st, jnp.max(st, axis=0, keepdims=True), vts[a], init[a]))
        finish(tuple(out))
        return

    (st_scr,) = scratch

    def chunk_keys(chunk_idx):
        start = chunk_idx * KV_CHUNK
        return keys(pl.ds(start if isinstance(start, int) else pl.multiple_of(start, KV_CHUNK), KV_CHUNK))

    def score(slot, a, ks):
        st = lax.dot_general(ks[a], qs[a], NT_DIMS, preferred_element_type=F32)
        st_scr[slot, a] = st
        return jnp.max(st, axis=0, keepdims=True)

    def scores(slot, chunk_idx):
        ks = chunk_keys(chunk_idx)
        return tuple(score(slot, a, ks) for a in (0, 1))

    def run(first, count, cmax, carry):
        for u in range(count):
            nxt, slot = first + u + 1, u % 2
            has_next = isinstance(nxt, jax.Array) or nxt < n_chunks
            m_new = [jnp.maximum(carry[a][0], cmax[a]) for a in (0, 1)]
            acc = [jnp.exp2(carry[a][0] - m_new[a]) * carry[a][1] for a in (0, 1)]
            cmax_next = [None, None]
            for j in range(KV_CHUNK // SUB_KEYS):
                rows = slice(j * SUB_KEYS, (j + 1) * SUB_KEYS)
                if has_next:
                    start = nxt * KV_CHUNK + j * SUB_KEYS
                    ks = keys(pl.ds(start if isinstance(start, int) else pl.multiple_of(start, SUB_KEYS), SUB_KEYS))
                    for a in (0, 1):
                        st = lax.dot_general(ks[a], qs[a], NT_DIMS, preferred_element_type=F32)
                        st_scr[1 - slot, a, rows, :] = st
                        mx = jnp.max(st, axis=0, keepdims=True)
                        cmax_next[a] = mx if cmax_next[a] is None else jnp.maximum(cmax_next[a], mx)
                vts = values(first + u, rows)
                for a in (0, 1):
                    p = jnp.exp2(st_scr[slot, a, rows, :] - m_new[a]).astype(BF16)
                    acc[a] = acc[a] + jnp.dot(vts[a], p, preferred_element_type=F32)
            carry = tuple((m_new[a], acc[a]) for a in (0, 1))
            cmax = tuple(cmax_next)
        return cmax, carry

    n_loop = (n_chunks - 1) // LOOP_CHUNKS
    cmax, carry = lax.fori_loop(0, n_loop, lambda i, st: run(i * LOOP_CHUNKS, LOOP_CHUNKS, *st),
                                (scores(0, 0), init))
    _, carry = run(n_loop * LOOP_CHUNKS, n_chunks - n_loop * LOOP_CHUNKS, cmax, carry)
    finish(carry)


def _flash(q, k, vt, lam, subg, *, mode, kw, k_cols, v_block_rows, v_rows, vw, k_index, v_index, n_ctx, diff_scale,
           name):
    b, s, nq = q.shape
    t = s - n_ctx
    n_pairs = nq // (2 * LANES)
    n_chunks = s // KV_CHUNK
    assert s % KV_CHUNK == 0 and n_ctx <= KV_CHUNK and t % FLASH_TQ == 0 and t % n_ctx == 0
    assert vt.shape[1] == n_chunks
    common = dict(mode=mode, k_cols=k_cols, v_rows=v_rows, vw=vw, diff_scale=diff_scale)
    o_x = pl.pallas_call(
        functools.partial(_flash_kernel, ctx_keys=0, **common),
        grid=(b, n_pairs, t // FLASH_TQ),
        in_specs=[
            pl.BlockSpec((1, FLASH_TQ, 2 * LANES), lambda bi, p, i: (bi, i, p)),
            pl.BlockSpec((1, s, kw), lambda bi, p, i: (bi, 0, k_index(p))),
            pl.BlockSpec((1, n_chunks, v_block_rows, KV_CHUNK), lambda bi, p, i: (bi, 0, v_index(p), 0)),
            _resident(lam.shape), _resident(subg.shape),
        ],
        out_specs=pl.BlockSpec((1, FLASH_TQ, LANES), lambda bi, p, i: (bi, i, p)),
        out_shape=jax.ShapeDtypeStruct((b, t, n_pairs * LANES), BF16),
        scratch_shapes=[pltpu.VMEM((2, 2, KV_CHUNK, FLASH_TQ), F32)],
        compiler_params=_cparams(3),
        name=name,
    )(q, k, vt, lam, subg)
    o_c = pl.pallas_call(
        functools.partial(_flash_kernel, ctx_keys=n_ctx, **common),
        grid=(b, n_pairs),
        in_specs=[
            pl.BlockSpec((1, n_ctx, 2 * LANES), lambda bi, p: (bi, t // n_ctx, p)),
            pl.BlockSpec((1, n_ctx, kw), lambda bi, p: (bi, 0, k_index(p))),
            pl.BlockSpec((1, 1, v_block_rows, KV_CHUNK), lambda bi, p: (bi, 0, v_index(p), 0)),
            _resident(lam.shape), _resident(subg.shape),
        ],
        out_specs=pl.BlockSpec((1, n_ctx, LANES), lambda bi, p: (bi, 0, p)),
        out_shape=jax.ShapeDtypeStruct((b, n_ctx, n_pairs * LANES), BF16),
        compiler_params=_cparams(2),
        name=name + "_ctx",
    )(q, k, vt, lam, subg)
    return o_x, o_c


def _na_tables(t):
    rows = t // GRID_W
    win_rows = min(NA_ROWS, rows)
    band_rows = min(win_rows + 1, rows)
    assert rows >= NA_WIN_ROWS and NA_QBLOCK % GRID_W == 0
    qrows = NA_QBLOCK // GRID_W
    nb = t // NA_QBLOCK
    nwin = NA_WIN_ROWS * GRID_W
    q_off, k_off = np.arange(NA_QBLOCK), np.arange(nwin)
    q_dr, q_col = q_off // GRID_W, q_off % GRID_W
    k_dr, k_col = k_off // GRID_W, k_off % GRID_W
    col_start = np.clip(q_col - NA_COLS // 2, 0, GRID_W - NA_COLS)
    col_in = (k_col[None, :] >= col_start[:, None]) & (k_col[None, :] < col_start[:, None] + NA_COLS)
    dc_idx = np.clip(k_col[None, :] - q_col[:, None] + NA_COLS - 1, 0, 2 * NA_COLS - 2)
    patterns, var, wstart = {}, [], []
    for j in range(nb):
        r = j * qrows + q_dr
        row_start = np.clip(r - win_rows // 2, 0, rows - win_rows)
        b0 = min(int(row_start[0]), rows - band_rows)
        w0 = min(b0 - b0 % 2, rows - NA_WIN_ROWS)
        assert w0 % qrows == 0 and w0 <= b0 and b0 + band_rows <= w0 + NA_WIN_ROWS
        k_row = w0 + k_dr
        in_band = (k_row >= b0) & (k_row < b0 + band_rows)
        in_win = (col_in & in_band[None, :] & (k_row[None, :] >= row_start[:, None])
                  & (k_row[None, :] < row_start[:, None] + win_rows))
        dr_idx = np.clip(k_row[None, :] - r[:, None] + NA_ROWS - 1, 0, 2 * NA_ROWS - 2)
        idx = np.where(in_win, dr_idx * (2 * NA_COLS - 1) + dc_idx, -1).astype(np.int32)
        key = idx.tobytes()
        if key not in patterns:
            patterns[key] = (len(patterns), idx)
        var.append(patterns[key][0])
        wstart.append(w0 * GRID_W)
    idx_all = np.stack([p[1] for p in sorted(patterns.values(), key=lambda p: p[0])])
    return idx_all.reshape(len(patterns), 1, -1), np.asarray(var, np.int32), np.asarray(wstart, np.int32)


def _na_bias_kernel(idx_ref, rpb_ref, o_ref):
    idx = idx_ref[0]
    r = rpb_ref[...]
    onehot = (lax.broadcasted_iota(jnp.int32, (r.shape[1], idx.shape[1]), 0) == idx).astype(BF16)
    b1 = r.astype(BF16)
    r2 = r - b1.astype(F32)
    b2 = r2.astype(BF16)
    b3 = (r2 - b2.astype(F32)).astype(BF16)
    out = (jnp.dot(b1, onehot, preferred_element_type=F32) + jnp.dot(b2, onehot, preferred_element_type=F32)
           + jnp.dot(b3, onehot, preferred_element_type=F32))
    o_ref[0] = jnp.where(idx >= 0, out, NEG_INF)


def _na_bias(idx_all, rpb):
    nvar, _, npos = idx_all.shape
    heads = rpb.shape[0]
    nr = (2 * NA_ROWS - 1) * (2 * NA_COLS - 1)
    nr_pad = pl.cdiv(nr, LANES) * LANES
    rflat = jnp.pad(rpb.reshape(heads, nr), ((0, 0), (0, nr_pad - nr)))
    tn = 2048
    assert npos % tn == 0
    out = pl.pallas_call(
        _na_bias_kernel,
        grid=(nvar, npos // tn),
        in_specs=[pl.BlockSpec((1, 1, tn), lambda v, j: (v, 0, j)), _resident(rflat.shape)],
        out_specs=pl.BlockSpec((1, heads, tn), lambda v, j: (v, 0, j)),
        out_shape=jax.ShapeDtypeStruct((nvar, heads, npos), F32),
        compiler_params=_cparams(2),
        name="na_bias",
    )(jnp.asarray(idx_all), rflat)
    return out.reshape(nvar, heads, NA_QBLOCK, npos // NA_QBLOCK)


def _halves(lo_src, hi_src):
    lane = lax.broadcasted_iota(jnp.int32, lo_src.shape, 1)
    return jnp.where(lane < LANES // 2, lo_src, hi_src)


def _na_kernel(var_ref, ws_ref, q_ref, k_ref, v_ref, b0_ref, b1_ref, o_ref, *, n_ctx, n_ctx_tiles):
    qi = pl.program_id(2)
    nwin = NA_WIN_ROWS * GRID_W

    @pl.when(qi < n_ctx_tiles)
    def _():
        kc, vc = k_ref[0, 0:n_ctx, :], v_ref[0, 0:n_ctx, :]
        outs = []
        for a in (0, 1):
            s_c = lax.dot_general(q_ref[0, :, a * LANES:(a + 1) * LANES], kc, NT_DIMS,
                                  preferred_element_type=F32)
            p_c = jnp.exp(s_c - jnp.max(s_c, axis=-1, keepdims=True))
            acc = jnp.dot(p_c.astype(BF16), vc, preferred_element_type=F32)
            outs.append(acc / jnp.sum(p_c, axis=-1, keepdims=True))
        o_ref[0] = _halves(outs[0], outs[1]).astype(BF16)

    @pl.when(qi >= n_ctx_tiles)
    def _():
        kc, vc = k_ref[0, 0:n_ctx, :], v_ref[0, 0:n_ctx, :]
        work = []
        for sub, bias_ref in ((0, b0_ref), (1, b1_ref)):
            j = (qi - n_ctx_tiles) * (TOKEN_TILE // NA_QBLOCK) + sub
            start = pl.multiple_of(n_ctx + ws_ref[j], NA_QBLOCK)
            kw, vw = k_ref[0, pl.ds(start, nwin), :], v_ref[0, pl.ds(start, nwin), :]
            for a in (0, 1):
                q = q_ref[0, sub * NA_QBLOCK:(sub + 1) * NA_QBLOCK, a * LANES:(a + 1) * LANES]
                s_c = lax.dot_general(q, kc, NT_DIMS, preferred_element_type=F32)
                s_w = lax.dot_general(q, kw, NT_DIMS, preferred_element_type=F32) + bias_ref[0, a]
                work.append((s_c, s_w, vw))
        outs = []
        for s_c, s_w, vw in work:
            m = jnp.maximum(jnp.max(s_c, axis=-1, keepdims=True), jnp.max(s_w, axis=-1, keepdims=True))
            p_c, p_w = jnp.exp(s_c - m), jnp.exp(s_w - m)
            l = jnp.sum(p_c, axis=-1, keepdims=True) + jnp.sum(p_w, axis=-1, keepdims=True)
            acc = (jnp.dot(p_c.astype(BF16), vc, preferred_element_type=F32)
                   + jnp.dot(p_w.astype(BF16), vw, preferred_element_type=F32))
            outs.append(acc / l)
        for sub in (0, 1):
            o_ref[0, sub * NA_QBLOCK:(sub + 1) * NA_QBLOCK, :] = _halves(outs[2 * sub], outs[2 * sub + 1]).astype(BF16)


def _na_attention(q, k, v, bias, var, wstart, *, n_ctx):
    b, s, nq = q.shape
    tq = TOKEN_TILE
    n_pairs = nq // (2 * LANES)
    n_ctx_tiles = n_ctx // tq
    per_tile = tq // NA_QBLOCK
    nwin = NA_WIN_ROWS * GRID_W

    def bias_spec(sub):
        def index(bi, p, i, var_ref, ws_ref):
            j = jnp.maximum(i - n_ctx_tiles, 0) * per_tile + sub
            return (var_ref[j], p, 0, 0)
        return pl.BlockSpec((1, 2, NA_QBLOCK, nwin), index)

    grid_spec = pltpu.PrefetchScalarGridSpec(
        num_scalar_prefetch=2,
        grid=(b, n_pairs, s // tq),
        in_specs=[
            pl.BlockSpec((1, tq, 2 * LANES), lambda bi, p, i, *_: (bi, i, p)),
            pl.BlockSpec((1, s, LANES), lambda bi, p, i, *_: (bi, 0, p)),
            pl.BlockSpec((1, s, LANES), lambda bi, p, i, *_: (bi, 0, p)),
            bias_spec(0), bias_spec(1),
        ],
        out_specs=pl.BlockSpec((1, tq, LANES), lambda bi, p, i, *_: (bi, i, p)),
    )
    return pl.pallas_call(
        functools.partial(_na_kernel, n_ctx=n_ctx, n_ctx_tiles=n_ctx_tiles),
        grid_spec=grid_spec,
        out_shape=jax.ShapeDtypeStruct((b, s, n_pairs * LANES), BF16),
        compiler_params=_cparams(3),
        name="na_attn",
    )(var, wstart, q, k, v, bias, bias)


O_HALO = 16


def _mix_ffn_kernel(xp_ref, x_ref, xn_ref, op_ref, ox_ref, oc_ref, on_ref, wo_ref, mod_ref, g_ref, wup_ref, cw_ref,
                    cb_ref, wdn_ref, fg_ref, y_ref, u_scr, a_scr, *, n_ctx_tiles, n_tiles, first_tile, chunk,
                    final_norm):
    t = pl.program_id(1) + first_tile
    tm = x_ref.shape[1]
    has_prev = jnp.logical_and(t != 0, t != n_ctx_tiles).astype(F32)
    has_next = jnp.logical_and(t != n_ctx_tiles - 1, t != n_tiles - 1).astype(F32)
    o = jnp.where(t < n_ctx_tiles, oc_ref[0], ox_ref[0]).astype(F32)
    o_cat = jnp.concatenate([op_ref[0].astype(F32)[O_HALO - HALO:], o, on_ref[0].astype(F32)[:HALO]], axis=0)
    x_cat = jnp.concatenate([xp_ref[0], x_ref[0], xn_ref[0]], axis=0)
    x1 = x_cat + mod_ref[0, 2:3, :] * jnp.dot(o_cat.astype(BF16), wo_ref[...], preferred_element_type=F32)
    g, shift, scale = g_ref[...], mod_ref[0, 3:4, :], mod_ref[0, 4:5, :]
    x = x1[HALO:HALO + tm]
    h = jnp.concatenate([_norm_mod(x1[:HALO], g, shift, scale) * has_prev,
                         _norm_mod(x, g, shift, scale),
                         _norm_mod(x1[HALO + tm:], g, shift, scale) * has_next], axis=0).astype(BF16)
    f = wdn_ref.shape[0]

    def val_gate(ref, j):
        return jnp.concatenate([ref[:, j * chunk:(j + 1) * chunk], ref[:, f + j * chunk:f + (j + 1) * chunk]], axis=1)

    for j in range(f // chunk):
        u_scr[...] = jnp.dot(h, val_gate(wup_ref, j), preferred_element_type=F32)
        cw = val_gate(cw_ref, j)
        u = (u_scr[HALO - 1:HALO - 1 + tm, :] * cw[0:1] + u_scr[HALO:HALO + tm, :] * cw[1:2]
             + u_scr[HALO + 1:HALO + 1 + tm, :] * cw[2:3] + val_gate(cb_ref, j))
        val, gate = u[:, :chunk], u[:, chunk:]
        a_scr[:, j * chunk:(j + 1) * chunk] = (gate * (1.0 / (1.0 + jnp.exp(-gate))) * val).astype(BF16)
    y = x + mod_ref[0, 5:6, :] * jnp.dot(a_scr[...], wdn_ref[...], preferred_element_type=F32)
    if final_norm:
        y = y * lax.rsqrt(jnp.mean(y * y, axis=-1, keepdims=True) + EPS) * fg_ref[...]
    y_ref[0] = y


def _mix_ffn(xs, o_x, o_c, w_out, mods, g, wup, cw, cb, wdn, final_g, *, n_ctx_tiles, x_offset, latent_only, chunk,
             final_norm):
    b, s, d = xs.shape
    tm = TOKEN_TILE
    n_tiles = s // tm
    n = o_x.shape[2]
    f = wdn.shape[0]
    first = n_ctx_tiles if latent_only else 0
    assert n_ctx_tiles == 1
    per, per_o = tm // HALO, tm // O_HALO
    lat = lambda i: jnp.maximum(i + first, n_ctx_tiles) - x_offset
    return pl.pallas_call(
        functools.partial(_mix_ffn_kernel, n_ctx_tiles=n_ctx_tiles, n_tiles=n_tiles, first_tile=first, chunk=chunk,
                          final_norm=final_norm),
        grid=(b, n_tiles - first),
        in_specs=[
            pl.BlockSpec((1, HALO, d), lambda bi, i: (bi, jnp.maximum((i + first) * per - 1, 0), 0)),
            pl.BlockSpec((1, tm, d), lambda bi, i: (bi, i + first, 0)),
            pl.BlockSpec((1, HALO, d), lambda bi, i: (bi, jnp.minimum((i + first + 1) * per, s // HALO - 1), 0)),
            pl.BlockSpec((1, O_HALO, n), lambda bi, i: (bi, jnp.maximum(lat(i) * per_o - 1, 0), 0)),
            pl.BlockSpec((1, tm, n), lambda bi, i: (bi, lat(i), 0)),
            pl.BlockSpec((1, tm, n), lambda bi, i: (bi, jnp.minimum(i + first, n_ctx_tiles - 1), 0)),
            pl.BlockSpec((1, O_HALO, n),
                         lambda bi, i: (bi, jnp.minimum((lat(i) + 1) * per_o, o_x.shape[1] // O_HALO - 1), 0)),
            _resident(w_out.shape),
            pl.BlockSpec((1, 6, d), lambda bi, i: (jnp.where(i + first < n_ctx_tiles, b, bi), 0, 0)),
            _resident((1, d)),
            pl.BlockSpec(wup.shape, lambda bi, i: (0, 0), pipeline_mode=pl.Buffered(1)),
            _resident(cw.shape), _resident(cb.shape),
            pl.BlockSpec(wdn.shape, lambda bi, i: (0, 0), pipeline_mode=pl.Buffered(1)),
            _resident((1, d)),
        ],
        out_specs=pl.BlockSpec((1, tm, d), lambda bi, i: (bi, i, 0)),
        out_shape=jax.ShapeDtypeStruct((b, s - first * tm, d), F32),
        scratch_shapes=[pltpu.VMEM((tm + 2 * HALO, 2 * chunk), F32), pltpu.VMEM((tm, f), BF16)],
        compiler_params=_cparams(2),
        name="mix_ffn",
    )(xs, xs, xs, o_x, o_x, o_c, o_x, w_out, mods, g, wup, cw, cb, wdn, final_g)


def _rope_table(t, n_ctx, rot_dim, lane_lo):
    tt = jnp.arange(t)
    row = (tt // GRID_W).astype(F32)
    col = (tt % GRID_W).astype(F32)
    half = rot_dim // 2
    inv = ROPE_BASE ** (-jnp.arange(0, half, 2, dtype=F32) / half)
    ar, ac = row[:, None] * inv, col[:, None] * inv
    ang = jnp.concatenate([ar, ar, ac, ac], axis=-1)
    quarter = rot_dim // 4
    sign = jnp.where((jnp.arange(rot_dim) % (2 * quarter)) < quarter, -1.0, 1.0).astype(F32)
    cos, sin = jnp.cos(ang), jnp.sin(ang) * sign
    reps = (LANES - lane_lo) // rot_dim if lane_lo == 0 else 1
    cos_l = jnp.ones((t, LANES), F32).at[:, lane_lo:lane_lo + reps * rot_dim].set(jnp.tile(cos, (1, reps)))
    sin_l = jnp.zeros((t, LANES), F32).at[:, lane_lo:lane_lo + reps * rot_dim].set(jnp.tile(sin, (1, reps)))
    cos_s = jnp.concatenate([jnp.ones((n_ctx, LANES), F32), cos_l], axis=0)
    sin_s = jnp.concatenate([jnp.zeros((n_ctx, LANES), F32), sin_l], axis=0)
    return cos_s, sin_s


def _to_slots(w, n_heads, width, halves):
    d = w.shape[0]
    wh = w.reshape(d, n_heads, width)
    lo = jnp.pad(wh, ((0, 0), (0, 0), (0, LANES - width)))
    if not any(halves):
        return lo.reshape(d, n_heads * LANES)
    hi = jnp.pad(wh, ((0, 0), (0, 0), (LANES // 2, LANES // 2 - width)))
    in_hi = jnp.asarray(halves, jnp.int32).reshape(1, n_heads, 1) == 1
    return jnp.where(in_hi, hi, lo).reshape(d, n_heads * LANES)


def _vt_weights(wv, groups, width, tm):
    d = wv.shape[0]
    rows = width + 16
    wt = jnp.zeros((groups, rows, d), wv.dtype).at[:, :width, :].set(wv.T.reshape(groups, width, d))
    ones = jnp.zeros((groups, rows, tm), F32).at[:, width, :].set(1.0)
    return wt.reshape(groups * rows, d), ones.reshape(groups * rows, tm)


def _dup(g):
    return jnp.concatenate([g, g]).reshape(1, LANES).astype(F32)


def kernel(x, c, ctx, c_ctx, ada_w, ada_b, norm1_g, norm2_g, ffn_w_up, ffn_conv_w, ffn_conv_b, ffn_w_down,
           gqa_w_in, gqa_q_norm_g, gqa_k_norm_g, gqa_w_out,
           mla_w_in, mla_q_norm_g, mla_kv_norm_g, mla_w_uq, mla_w_ukv, mla_w_out,
           diff_w_in, diff_lambda, diff_subln_g, diff_w_out,
           na_w_in, na_rpb, na_w_out, final_norm_g):
    b, t, d = x.shape
    n_ctx = ctx.shape[1]
    depth = ada_w.shape[0]
    assert n_ctx % TOKEN_TILE == 0 and t % TOKEN_TILE == 0 and b + 1 <= 8
    nct = n_ctx // TOKEN_TILE
    f = ffn_w_down.shape[1]
    chunk = 2 * LANES
    assert f % chunk == 0

    xs = jnp.concatenate([ctx, x], axis=1)
    cin = jnp.zeros((8, d), F32).at[:b].set(c).at[b].set(c_ctx)
    mods_all = _ada_all(cin, ada_w, ada_b).reshape(depth, 8, 6, d)

    tm = TOKEN_TILE
    ones_l = jnp.ones((1, LANES), F32)
    dummy_tab = jnp.zeros((xs.shape[1], LANES), F32)
    dummy_ones = jnp.zeros((8, LANES), F32)
    cos64, sin64 = _rope_table(t, n_ctx, GQA_HEAD_DIM, 0)
    cos_mla, sin_mla = _rope_table(t, n_ctx, MLA_ROPE_DIM, MLA_NOPE_DIM)
    no_lam, no_g = jnp.zeros((4, DIFF_HEAD_DIM), F32), jnp.ones((1, LANES), F32)
    pair_rows = MLA_V_DIM + 16
    diff_rows = 2 * DIFF_HEAD_DIM + 16

    for i in range(depth):
        m, j = i % N_MIXERS, i // N_MIXERS
        mods = mods_all[i]
        g1 = norm1_g[i].reshape(1, d)
        if m == 0:
            w = gqa_w_in[j].astype(BF16)
            nqc, nkc = GQA_HEADS * GQA_HEAD_DIM, GQA_KV_HEADS * GQA_HEAD_DIM
            wk = w[:, nqc:nqc + nkc].reshape(d, GQA_KV_HEADS, 1, GQA_HEAD_DIM)
            wk = jnp.broadcast_to(wk, (d, GQA_KV_HEADS, 2, GQA_HEAD_DIM)).reshape(d, GQA_KV_HEADS * LANES)
            wvt, vones = _vt_weights(w[:, nqc + nkc:], GQA_KV_HEADS, GQA_HEAD_DIM, tm)
            qc = GQA_HEAD_DIM ** -0.5 * LOG2E
            q, k, vt = _project(xs, mods, g1, w[:, :nqc], wk, wvt, _dup(gqa_q_norm_g[j]), _dup(gqa_k_norm_g[j]),
                                cos64 * qc, sin64 * qc, cos64, sin64, vones, qk_norm=True, rope=True,
                                head_dim=GQA_HEAD_DIM, k_norm_dim=2 * GQA_HEAD_DIM, v_transposed=True, q_rotated=True, n_ctx_tiles=nct,
                                name="proj_gqa")
            o_x, o_c = _flash(q, k, vt, no_lam, no_g, mode="pair", kw=LANES, k_cols=((0, LANES), (0, LANES)),
                       v_block_rows=pair_rows, v_rows=((0, pair_rows), (0, pair_rows)), vw=GQA_HEAD_DIM,
                       k_index=lambda p: p // 2, v_index=lambda p: p // 2, n_ctx=n_ctx, diff_scale=1.0,
                       name="attn_gqa")
            w_out = gqa_w_out[j]
        elif m == 1:
            w = mla_w_in[j].astype(BF16)
            nlat = MLA_Q_LORA + MLA_KV_LORA
            kr_slot = jnp.zeros((d, LANES), BF16).at[:, MLA_NOPE_DIM:MLA_NOPE_DIM + MLA_ROPE_DIM].set(w[:, nlat:])
            win = jnp.concatenate([w[:, :nlat], kr_slot], axis=1)
            qk_dim = MLA_NOPE_DIM + MLA_ROPE_DIM
            wuq = _to_slots(mla_w_uq[j].astype(BF16), MLA_HEADS, qk_dim, [0] * MLA_HEADS)
            wukv = mla_w_ukv[j].astype(BF16).reshape(MLA_KV_LORA, MLA_HEADS, MLA_NOPE_DIM + MLA_V_DIM)
            wuk = _to_slots(wukv[:, :, :MLA_NOPE_DIM].reshape(MLA_KV_LORA, -1), MLA_HEADS, MLA_NOPE_DIM,
                            [0] * MLA_HEADS)
            wuvt, vones = _vt_weights(wukv[:, :, MLA_NOPE_DIM:].reshape(MLA_KV_LORA, MLA_HEADS * MLA_V_DIM),
                                      MLA_HEADS, MLA_V_DIM, tm)
            qc = qk_dim ** -0.5 * LOG2E
            q, k, vt = _project_mla(xs, mods, g1, win, mla_q_norm_g[j].reshape(1, -1),
                                    mla_kv_norm_g[j].reshape(1, -1), wuq, wuk, wuvt,
                                    cos_mla * qc, sin_mla * qc, cos_mla, sin_mla, vones, n_ctx_tiles=nct)
            o_x, o_c = _flash(q, k, vt, no_lam, no_g, mode="pair", kw=2 * LANES, k_cols=((0, LANES), (LANES, 2 * LANES)),
                       v_block_rows=2 * pair_rows, v_rows=((0, pair_rows), (pair_rows, 2 * pair_rows)),
                       vw=MLA_V_DIM, k_index=lambda p: p, v_index=lambda p: p, n_ctx=n_ctx, diff_scale=1.0,
                       name="attn_mla")
            w_out = mla_w_out[j]
        elif m == 2:
            w = diff_w_in[j].astype(BF16)
            wvt, vones = _vt_weights(w[:, 2 * d:], DIFF_HEADS, 2 * DIFF_HEAD_DIM, tm)
            lambda_init = 0.8 - 0.6 * math.exp(-0.3 * i)
            qc = DIFF_HEAD_DIM ** -0.5 * LOG2E
            q, k, vt = _project(xs, mods, g1, w[:, :d], w[:, d:2 * d], wvt, ones_l, ones_l,
                                cos64 * qc, sin64 * qc, cos64, sin64, vones, qk_norm=False, rope=True,
                                head_dim=DIFF_HEAD_DIM, k_norm_dim=DIFF_HEAD_DIM, v_transposed=True, q_rotated=True, n_ctx_tiles=nct,
                                name="proj_diff")
            o_x, o_c = _flash(q, k, vt, diff_lambda[j], diff_subln_g[j].reshape(1, LANES), mode="diff", kw=LANES,
                       k_cols=((0, LANES), (0, LANES)), v_block_rows=diff_rows,
                       v_rows=((0, diff_rows), (0, diff_rows)), vw=2 * DIFF_HEAD_DIM,
                       k_index=lambda p: p, v_index=lambda p: p, n_ctx=n_ctx, diff_scale=1.0 - lambda_init,
                       name="attn_diff")
            w_out = diff_w_out[j]
        else:
            w = na_w_in[j].astype(BF16)
            na_scale = NA_HEAD_DIM ** -0.5
            assert math.frexp(na_scale)[0] == 0.5
            q, k, v = _project(xs, mods, g1, w[:, :d] * jnp.asarray(na_scale, BF16), w[:, d:2 * d], w[:, 2 * d:],
                               ones_l, ones_l, dummy_tab, dummy_tab, dummy_tab, dummy_tab, dummy_ones,
                               qk_norm=False, rope=False, head_dim=NA_HEAD_DIM, k_norm_dim=NA_HEAD_DIM,
                               v_transposed=False, q_rotated=False, n_ctx_tiles=nct,
                               name="proj_na")
            idx_all, var, wstart = _na_tables(t)
            bias = _na_bias(idx_all, na_rpb[j])
            o_x = o_c = _na_attention(q, k, v, bias, jnp.asarray(var), jnp.asarray(wstart), n_ctx=n_ctx)
            w_out = na_w_out[j]
        last = i == depth - 1

        xs = _mix_ffn(xs, o_x, o_c, w_out.astype(BF16), mods, norm2_g[i].reshape(1, d),
                      ffn_w_up[i].astype(BF16), ffn_conv_w[i], ffn_conv_b[i].reshape(1, 2 * f),
                      ffn_w_down[i].astype(BF16), final_norm_g.reshape(1, d), n_ctx_tiles=nct,
                      x_offset=0 if m == 3 else nct, latent_only=last, chunk=chunk, final_norm=last)
    return xs
```

```python
import functools
import math

import numpy as np
import jax
import jax.numpy as jnp
from jax import lax
from jax.experimental import pallas as pl
from jax.experimental.pallas import tpu as pltpu

F32 = jnp.float32
BF16 = jnp.bfloat16

LANES = 128
TOKEN_TILE = 256
HALO = 8
KV_CHUNK = 768
SUB_KEYS = 256
FLASH_TQ = 512
LOOP_CHUNKS = 4
VMEM_LIMIT = 56 * 1024 * 1024

GRID_W = 64
ROPE_BASE = 10000.0
EPS = 1e-6
NEG_INF = -1e30
LOG2E = 1.4426950408889634

GQA_HEADS, GQA_KV_HEADS, GQA_HEAD_DIM = 16, 4, 64
MLA_HEADS, MLA_NOPE_DIM, MLA_ROPE_DIM, MLA_V_DIM = 16, 64, 32, 64
MLA_Q_LORA, MLA_KV_LORA = 384, 256
DIFF_HEADS, DIFF_HEAD_DIM = 8, 64
NA_HEADS, NA_HEAD_DIM, NA_ROWS, NA_COLS = 16, 64, 8, 16
NA_QBLOCK = 128
NA_BIAS_TILE = 2048
NA_WIN_ROWS = 10
N_MIXERS = 4

NT_DIMS = (((1,), (1,)), ((), ()))


def _cparams(n_axes):
    return pltpu.CompilerParams(dimension_semantics=("arbitrary",) * n_axes, vmem_limit_bytes=VMEM_LIMIT)


def _resident(shape):
    nd = len(shape)
    return pl.BlockSpec(shape, lambda *_: (0,) * nd)


def _ada_kernel(c_ref, w_ref, b_ref, o_ref):
    c = c_ref[...]
    s = c * (1.0 / (1.0 + jnp.exp(-c)))
    o_ref[0] = jnp.dot(s.astype(BF16), w_ref[0].astype(BF16), preferred_element_type=F32) + b_ref[0]


def _ada_all(cin, ada_w, ada_b):
    depth, d, n = ada_w.shape
    tn = n // 4
    return pl.pallas_call(
        _ada_kernel,
        grid=(depth, n // tn),
        in_specs=[
            pl.BlockSpec((8, d), lambda l, j: (0, 0)),
            pl.BlockSpec((1, d, tn), lambda l, j: (l, 0, j)),
            pl.BlockSpec((1, 1, tn), lambda l, j: (l, 0, j)),
        ],
        out_specs=pl.BlockSpec((1, 8, tn), lambda l, j: (l, 0, j)),
        out_shape=jax.ShapeDtypeStruct((depth, 8, n), F32),
        compiler_params=_cparams(2),
        name="ada",
    )(cin, ada_w, ada_b.reshape(depth, 1, n))


def _norm_mod(x, g, shift, scale):
    ms = jnp.mean(x * x, axis=-1, keepdims=True)
    return (x * lax.rsqrt(ms + EPS) * g) * (1.0 + scale) + shift


def _rope_slot(y, cos, sin_signed, lo_mask, quarter):
    fwd = pltpu.roll(y, LANES - quarter, axis=1)
    bwd = pltpu.roll(y, quarter, axis=1)
    return y * cos + jnp.where(lo_mask, fwd, bwd) * sin_signed


def _lo_mask(shape, quarter):
    lane = lax.broadcasted_iota(jnp.int32, shape, 1)
    return (lane % (2 * quarter)) < quarter


def _q_spec(n, tm, n_tiles, n_ctx_tiles, rotate):
    if not rotate:
        return pl.BlockSpec((1, tm, n), lambda bi, i: (bi, i, 0))
    n_lat = n_tiles - n_ctx_tiles
    return pl.BlockSpec((1, tm, n), lambda bi, i: (bi, jnp.where(i < n_ctx_tiles, i + n_lat, i - n_ctx_tiles), 0))


def _vt_spec(rows, tm):
    per = KV_CHUNK // tm
    return pl.BlockSpec((1, 1, rows, tm), lambda bi, i: (bi, i // per, 0, i % per))


def _proj_kernel(x_ref, mod_ref, g_ref, wq_ref, wk_ref, wv_ref, qg_ref, kg_ref, cosq_ref, sinq_ref, cosk_ref, sink_ref,
                 vones_ref, q_ref, k_ref, v_ref, *, qk_norm, rope, head_dim, k_norm_dim, v_transposed):
    h = _norm_mod(x_ref[0], g_ref[...], mod_ref[0, 0:1, :], mod_ref[0, 1:2, :]).astype(BF16)
    tm = h.shape[0]
    lo = _lo_mask((tm, LANES), head_dim // 4) if rope else None
    low_half = lax.broadcasted_iota(jnp.int32, (tm, LANES), 1) < LANES // 2

    def chunks(w_ref):
        n = w_ref.shape[1]
        for c0 in range(0, n, 2 * LANES):
            c1 = min(c0 + 2 * LANES, n)
            y2 = jnp.dot(h, w_ref[:, c0:c1], preferred_element_type=F32)
            for s0 in range(0, c1 - c0, LANES):
                yield (c0 + s0) // LANES, y2[:, s0:s0 + LANES]

    for c, y in chunks(wq_ref):
        if qk_norm:
            sq = y * y
            ms_lo = jnp.sum(jnp.where(low_half, sq, 0.0), axis=-1, keepdims=True) * (1.0 / head_dim)
            ms_hi = jnp.sum(jnp.where(low_half, 0.0, sq), axis=-1, keepdims=True) * (1.0 / head_dim)
            y = y * jnp.where(low_half, lax.rsqrt(ms_lo + EPS), lax.rsqrt(ms_hi + EPS)) * qg_ref[...]
        if rope:
            y = _rope_slot(y, cosq_ref[...], sinq_ref[...], lo, head_dim // 4)
        q_ref[0, :, 2 * c * LANES:(2 * c + 1) * LANES] = jnp.where(low_half, y, 0.0).astype(BF16)
        q_ref[0, :, (2 * c + 1) * LANES:(2 * c + 2) * LANES] = jnp.where(low_half, 0.0, y).astype(BF16)
    for c, y in chunks(wk_ref):
        if qk_norm:
            ms = jnp.sum(y * y, axis=-1, keepdims=True) * (1.0 / k_norm_dim)
            y = y * lax.rsqrt(ms + EPS) * kg_ref[...]
        if rope:
            y = _rope_slot(y, cosk_ref[...], sink_ref[...], lo, head_dim // 4)
        k_ref[0, :, c * LANES:(c + 1) * LANES] = y.astype(BF16)
    if v_transposed:
        vt = lax.dot_general(wv_ref[...], h, NT_DIMS, preferred_element_type=F32) + vones_ref[...]
        v_ref[0, 0] = vt.astype(BF16)
    else:
        v_ref[0] = jnp.dot(h, wv_ref[...], preferred_element_type=F32).astype(BF16)


def _project(xs, mods, g, wq, wk, wv, qg, kg, cosq, sinq, cosk, sink, vones, *, qk_norm, rope, head_dim, k_norm_dim,
             v_transposed, q_rotated, n_ctx_tiles, name):
    b, s, d = xs.shape
    tm = TOKEN_TILE
    nq, nk = 2 * wq.shape[1], wk.shape[1]
    tok = lambda n: pl.BlockSpec((1, tm, n), lambda bi, i: (bi, i, 0))
    tab = pl.BlockSpec((tm, LANES), lambda bi, i: (i, 0))
    if v_transposed:
        v_spec = _vt_spec(wv.shape[0], tm)
        v_shape = jax.ShapeDtypeStruct((b, s // KV_CHUNK, wv.shape[0], KV_CHUNK), BF16)
    else:
        v_spec, v_shape = tok(wv.shape[1]), jax.ShapeDtypeStruct((b, s, wv.shape[1]), BF16)
    return pl.pallas_call(
        functools.partial(_proj_kernel, qk_norm=qk_norm, rope=rope, head_dim=head_dim, k_norm_dim=k_norm_dim,
                          v_transposed=v_transposed),
        grid=(b, s // tm),
        in_specs=[
            tok(d),
            pl.BlockSpec((1, 6, d), lambda bi, i: (jnp.where(i < n_ctx_tiles, b, bi), 0, 0)),
            _resident((1, d)),
            _resident(wq.shape), _resident(wk.shape), _resident(wv.shape),
            _resident((1, LANES)), _resident((1, LANES)),
            tab, tab, tab, tab,
            _resident(vones.shape),
        ],
        out_specs=[_q_spec(nq, tm, s // tm, n_ctx_tiles, q_rotated), tok(nk), v_spec],
        out_shape=[jax.ShapeDtypeStruct((b, s, nq), BF16), jax.ShapeDtypeStruct((b, s, nk), BF16), v_shape],
        compiler_params=_cparams(2),
        name=name,
    )(xs, mods, g, wq, wk, wv, qg, kg, cosq, sinq, cosk, sink, vones)


def _mla_proj_kernel(x_ref, mod_ref, g_ref, win_ref, qn_ref, kvn_ref, wuq_ref, wuk_ref, wuvt_ref,
                     cosq_ref, sinq_ref, cosk_ref, sink_ref, vones_ref, q_ref, k_ref, v_ref):
    h = _norm_mod(x_ref[0], g_ref[...], mod_ref[0, 0:1, :], mod_ref[0, 1:2, :]).astype(BF16)
    tm = h.shape[0]
    t1 = jnp.dot(h, win_ref[...], preferred_element_type=F32)
    cq = t1[:, :MLA_Q_LORA]
    ckv = t1[:, MLA_Q_LORA:MLA_Q_LORA + MLA_KV_LORA]
    kr = t1[:, MLA_Q_LORA + MLA_KV_LORA:]
    cqn = (cq * lax.rsqrt(jnp.mean(cq * cq, axis=-1, keepdims=True) + EPS) * qn_ref[...]).astype(BF16)
    ckvn = (ckv * lax.rsqrt(jnp.mean(ckv * ckv, axis=-1, keepdims=True) + EPS) * kvn_ref[...]).astype(BF16)
    quarter = MLA_ROPE_DIM // 4
    lo = _lo_mask((tm, LANES), quarter)
    kr = _rope_slot(kr, cosk_ref[...], sink_ref[...], lo, quarter)
    n = wuq_ref.shape[1]
    for c0 in range(0, n, 2 * LANES):
        q2 = jnp.dot(cqn, wuq_ref[:, c0:c0 + 2 * LANES], preferred_element_type=F32)
        k2 = jnp.dot(ckvn, wuk_ref[:, c0:c0 + 2 * LANES], preferred_element_type=F32)
        for s0 in (0, LANES):
            q_ref[0, :, c0 + s0:c0 + s0 + LANES] = _rope_slot(
                q2[:, s0:s0 + LANES], cosq_ref[...], sinq_ref[...], lo, quarter).astype(BF16)
            k_ref[0, :, c0 + s0:c0 + s0 + LANES] = (k2[:, s0:s0 + LANES] + kr).astype(BF16)
    vt = lax.dot_general(wuvt_ref[...], ckvn, NT_DIMS, preferred_element_type=F32) + vones_ref[...]
    v_ref[0, 0] = vt.astype(BF16)


def _project_mla(xs, mods, g, win, qn, kvn, wuq, wuk, wuvt, cosq, sinq, cosk, sink, vones, *, n_ctx_tiles):
    b, s, d = xs.shape
    tm = TOKEN_TILE
    nq, nk = wuq.shape[1], wuk.shape[1]
    tok = lambda n: pl.BlockSpec((1, tm, n), lambda bi, i: (bi, i, 0))
    tab = pl.BlockSpec((tm, LANES), lambda bi, i: (i, 0))
    return pl.pallas_call(
        _mla_proj_kernel,
        grid=(b, s // tm),
        in_specs=[
            tok(d),
            pl.BlockSpec((1, 6, d), lambda bi, i: (jnp.where(i < n_ctx_tiles, b, bi), 0, 0)),
            _resident((1, d)),
            _resident(win.shape), _resident(qn.shape), _resident(kvn.shape),
            _resident(wuq.shape), _resident(wuk.shape), _resident(wuvt.shape),
            tab, tab, tab, tab,
            _resident(vones.shape),
        ],
        out_specs=[_q_spec(nq, tm, s // tm, n_ctx_tiles, True), tok(nk), _vt_spec(wuvt.shape[0], tm)],
        out_shape=[jax.ShapeDtypeStruct((b, s, nq), BF16), jax.ShapeDtypeStruct((b, s, nk), BF16),
                   jax.ShapeDtypeStruct((b, s // KV_CHUNK, wuvt.shape[0], KV_CHUNK), BF16)],
        compiler_params=_cparams(2),
        name="proj_mla",
    )(xs, mods, g, win, qn, kvn, wuq, wuk, wuvt, cosq, sinq, cosk, sink, vones)


def _flash_kernel(q_ref, k_ref, vt_ref, lam_ref, subg_ref, o_ref, *scratch, mode, k_cols, v_rows, vw, diff_scale,
                  ctx_keys):
    tq = q_ref.shape[1]
    n_chunks = vt_ref.shape[1]
    qs = [q_ref[0, :, a * LANES:(a + 1) * LANES] for a in (0, 1)]
    n_rows = v_rows[0][1] - v_rows[0][0]

    def keys(key_rows):
        k0 = k_ref[0, key_rows, k_cols[0][0]:k_cols[0][1]]
        k1 = k0 if k_cols[1] == k_cols[0] else k_ref[0, key_rows, k_cols[1][0]:k_cols[1][1]]
        return k0, k1

    def values(chunk_idx, key_lanes):
        v0 = vt_ref[0, chunk_idx, v_rows[0][0]:v_rows[0][1], key_lanes]
        v1 = v0 if v_rows[1] == v_rows[0] else vt_ref[0, chunk_idx, v_rows[1][0]:v_rows[1][1], key_lanes]
        return v0, v1

    def update(st, cmax, vt_a, carry_a):
        m, acc = carry_a
        m_new = jnp.maximum(m, cmax)
        p = jnp.exp2(st - m_new).astype(BF16)
        return m_new, jnp.exp2(m - m_new) * acc + jnp.dot(vt_a, p, preferred_element_type=F32)

    init = tuple((jnp.full((1, tq), -jnp.inf, F32), jnp.zeros((n_rows, tq), F32)) for _ in (0, 1))

    def finish(carry):
        (_, acc_a), (_, acc_b) = carry
        oa = acc_a[0:vw] / acc_a[vw:vw + 1]
        ob = acc_b[0:vw] / acc_b[vw:vw + 1]
        if mode == "pair":
            out = jnp.concatenate([oa, ob], axis=0).T
        else:
            lam = lam_ref[...]
            lam_full = (jnp.exp(jnp.sum(lam[0:1] * lam[1:2], axis=-1, keepdims=True))
                        - jnp.exp(jnp.sum(lam[2:3] * lam[3:4], axis=-1, keepdims=True))
                        + (1.0 - diff_scale))
            o = (oa - lam_full * ob).T
            out = o * lax.rsqrt(jnp.mean(o * o, axis=-1, keepdims=True) + EPS) * subg_ref[...] * diff_scale
        o_ref[0] = out.astype(BF16)

    if ctx_keys:
        ks, vts = keys(slice(0, ctx_keys)), values(0, slice(0, ctx_keys))
        out = []
        for a in (0, 1):
            st = lax.dot_general(ks[a], qs[a], NT_DIMS, preferred_element_type=F32)
            out.append(update(st, jnp.max(st, axis=0, keepdims=True), vts[a], init[a]))
        finish(tuple(out))
        return

    (st_scr,) = scratch

    def chunk_keys(chunk_idx):
        start = chunk_idx * KV_CHUNK
        return keys(pl.ds(start if isinstance(start, int) else pl.multiple_of(start, KV_CHUNK), KV_CHUNK))

    def score(slot, a, ks):
        st = lax.dot_general(ks[a], qs[a], NT_DIMS, preferred_element_type=F32)
        st_scr[slot, a] = st
        return jnp.max(st, axis=0, keepdims=True)

    def scores(slot, chunk_idx):
        ks = chunk_keys(chunk_idx)
        return tuple(score(slot, a, ks) for a in (0, 1))

    def run(first, count, cmax, carry):
        for u in range(count):
            nxt, slot = first + u + 1, u % 2
            has_next = isinstance(nxt, jax.Array) or nxt < n_chunks
            m_new = [jnp.maximum(carry[a][0], cmax[a]) for a in (0, 1)]
            acc = [jnp.exp2(carry[a][0] - m_new[a]) * carry[a][1] for a in (0, 1)]
            cmax_next = [None, None]
            for j in range(KV_CHUNK // SUB_KEYS):
                rows = slice(j * SUB_KEYS, (j + 1) * SUB_KEYS)
                if has_next:
                    start = nxt * KV_CHUNK + j * SUB_KEYS
                    ks = keys(pl.ds(start if isinstance(start, int) else pl.multiple_of(start, SUB_KEYS), SUB_KEYS))
                    for a in (0, 1):
                        st = lax.dot_general(ks[a], qs[a], NT_DIMS, preferred_element_type=F32)
                        st_scr[1 - slot, a, rows, :] = st
                        mx = jnp.max(st, axis=0, keepdims=True)
                        cmax_next[a] = mx if cmax_next[a] is None else jnp.maximum(cmax_next[a], mx)
                vts = values(first + u, rows)
                for a in (0, 1):
                    p = jnp.exp2(st_scr[slot, a, rows, :] - m_new[a]).astype(BF16)
                    acc[a] = acc[a] + jnp.dot(vts[a], p, preferred_element_type=F32)
            carry = tuple((m_new[a], acc[a]) for a in (0, 1))
            cmax = tuple(cmax_next)
        return cmax, carry

    n_loop = (n_chunks - 1) // LOOP_CHUNKS
    cmax, carry = lax.fori_loop(0, n_loop, lambda i, st: run(i * LOOP_CHUNKS, LOOP_CHUNKS, *st),
                                (scores(0, 0), init))
    _, carry = run(n_loop * LOOP_CHUNKS, n_chunks - n_loop * LOOP_CHUNKS, cmax, carry)
    finish(carry)


def _flash(q, k, vt, lam, subg, *, mode, kw, k_cols, v_block_rows, v_rows, vw, k_index, v_index, n_ctx, diff_scale,
           name):
    b, s, nq = q.shape
    t = s - n_ctx
    n_pairs = nq // (2 * LANES)
    n_chunks = s // KV_CHUNK
    assert s % KV_CHUNK == 0 and n_ctx <= KV_CHUNK and t % FLASH_TQ == 0 and t % n_ctx == 0
    assert vt.shape[1] == n_chunks
    common = dict(mode=mode, k_cols=k_cols, v_rows=v_rows, vw=vw, diff_scale=diff_scale)
    o_x = pl.pallas_call(
        functools.partial(_flash_kernel, ctx_keys=0, **common),
        grid=(b, n_pairs, t // FLASH_TQ),
        in_specs=[
            pl.BlockSpec((1, FLASH_TQ, 2 * LANES), lambda bi, p, i: (bi, i, p)),
            pl.BlockSpec((1, s, kw), lambda bi, p, i: (bi, 0, k_index(p))),
            pl.BlockSpec((1, n_chunks, v_block_rows, KV_CHUNK), lambda bi, p, i: (bi, 0, v_index(p), 0)),
            _resident(lam.shape), _resident(subg.shape),
        ],
        out_specs=pl.BlockSpec((1, FLASH_TQ, LANES), lambda bi, p, i: (bi, i, p)),
        out_shape=jax.ShapeDtypeStruct((b, t, n_pairs * LANES), BF16),
        scratch_shapes=[pltpu.VMEM((2, 2, KV_CHUNK, FLASH_TQ), F32)],
        compiler_params=_cparams(3),
        name=name,
    )(q, k, vt, lam, subg)
    o_c = pl.pallas_call(
        functools.partial(_flash_kernel, ctx_keys=n_ctx, **common),
        grid=(b, n_pairs),
        in_specs=[
            pl.BlockSpec((1, n_ctx, 2 * LANES), lambda bi, p: (bi, t // n_ctx, p)),
            pl.BlockSpec((1, n_ctx, kw), lambda bi, p: (bi, 0, k_index(p))),
            pl.BlockSpec((1, 1, v_block_rows, KV_CHUNK), lambda bi, p: (bi, 0, v_index(p), 0)),
            _resident(lam.shape), _resident(subg.shape),
        ],
        out_specs=pl.BlockSpec((1, n_ctx, LANES), lambda bi, p: (bi, 0, p)),
        out_shape=jax.ShapeDtypeStruct((b, n_ctx, n_pairs * LANES), BF16),
        compiler_params=_cparams(2),
        name=name + "_ctx",
    )(q, k, vt, lam, subg)
    return o_x, o_c


def _na_tables(t):
    rows = t // GRID_W
    win_rows = min(NA_ROWS, rows)
    band_rows = min(win_rows + 1, rows)
    assert rows >= NA_WIN_ROWS and NA_QBLOCK % GRID_W == 0
    qrows = NA_QBLOCK // GRID_W
    nb = t // NA_QBLOCK
    nwin = NA_WIN_ROWS * GRID_W
    q_off, k_off = np.arange(NA_QBLOCK), np.arange(nwin)
    q_dr, q_col = q_off // GRID_W, q_off % GRID_W
    k_dr, k_col = k_off // GRID_W, k_off % GRID_W
    col_start = np.clip(q_col - NA_COLS // 2, 0, GRID_W - NA_COLS)
    col_in = (k_col[None, :] >= col_start[:, None]) & (k_col[None, :] < col_start[:, None] + NA_COLS)
    dc_idx = np.clip(k_col[None, :] - q_col[:, None] + NA_COLS - 1, 0, 2 * NA_COLS - 2)
    patterns, var, wstart = {}, [], []
    for j in range(nb):
        r = j * qrows + q_dr
        row_start = np.clip(r - win_rows // 2, 0, rows - win_rows)
        b0 = min(int(row_start[0]), rows - band_rows)
        w0 = min(b0 - b0 % 2, rows - NA_WIN_ROWS)
        assert w0 % qrows == 0 and w0 <= b0 and b0 + band_rows <= w0 + NA_WIN_ROWS
        k_row = w0 + k_dr
        in_band = (k_row >= b0) & (k_row < b0 + band_rows)
        in_win = (col_in & in_band[None, :] & (k_row[None, :] >= row_start[:, None])
                  & (k_row[None, :] < row_start[:, None] + win_rows))
        dr_idx = np.clip(k_row[None, :] - r[:, None] + NA_ROWS - 1, 0, 2 * NA_ROWS - 2)
        idx = np.where(in_win, dr_idx * (2 * NA_COLS - 1) + dc_idx, -1).astype(np.int32)
        key = idx.tobytes()
        if key not in patterns:
            patterns[key] = (len(patterns), idx)
        var.append(patterns[key][0])
        wstart.append(w0 * GRID_W)
    idx_all = np.stack([p[1] for p in sorted(patterns.values(), key=lambda p: p[0])])
    return idx_all.reshape(len(patterns), 1, -1), np.asarray(var, np.int32), np.asarray(wstart, np.int32)


def _na_bias_kernel(idx_ref, rpb_ref, o_ref):
    idx = idx_ref[0]
    r = rpb_ref[...]
    onehot = (lax.broadcasted_iota(jnp.int32, (r.shape[1], idx.shape[1]), 0) == idx).astype(BF16)
    b1 = r.astype(BF16)
    r2 = r - b1.astype(F32)
    b2 = r2.astype(BF16)
    b3 = (r2 - b2.astype(F32)).astype(BF16)
    heads = r.shape[0]
    parts = jnp.dot(jnp.concatenate([b1, b2, b3], axis=0), onehot, preferred_element_type=F32)
    out = parts[:heads] + parts[heads:2 * heads] + parts[2 * heads:]
    o_ref[0] = jnp.where(idx >= 0, out, NEG_INF)


def _na_bias(idx_all, rpb):
    nvar, _, npos = idx_all.shape
    heads = rpb.shape[0]
    nr = (2 * NA_ROWS - 1) * (2 * NA_COLS - 1)
    nr_pad = pl.cdiv(nr, LANES) * LANES
    rflat = jnp.pad(rpb.reshape(heads, nr), ((0, 0), (0, nr_pad - nr)))
    tn = NA_BIAS_TILE
    assert npos % tn == 0
    out = pl.pallas_call(
        _na_bias_kernel,
        grid=(nvar, npos // tn),
        in_specs=[pl.BlockSpec((1, 1, tn), lambda v, j: (v, 0, j)), _resident(rflat.shape)],
        out_specs=pl.BlockSpec((1, heads, tn), lambda v, j: (v, 0, j)),
        out_shape=jax.ShapeDtypeStruct((nvar, heads, npos), F32),
        compiler_params=_cparams(2),
        name="na_bias",
    )(jnp.asarray(idx_all), rflat)
    return out.reshape(nvar, heads, NA_QBLOCK, npos // NA_QBLOCK)


def _halves(lo_src, hi_src):
    lane = lax.broadcasted_iota(jnp.int32, lo_src.shape, 1)
    return jnp.where(lane < LANES // 2, lo_src, hi_src)


def _na_kernel(var_ref, ws_ref, q_ref, k_ref, v_ref, b0_ref, b1_ref, o_ref, *, n_ctx, n_ctx_tiles):
    qi = pl.program_id(2)
    nwin = NA_WIN_ROWS * GRID_W

    @pl.when(qi < n_ctx_tiles)
    def _():
        kc, vc = k_ref[0, 0:n_ctx, :], v_ref[0, 0:n_ctx, :]
        outs = []
        for a in (0, 1):
            s_c = lax.dot_general(q_ref[0, :, a * LANES:(a + 1) * LANES], kc, NT_DIMS,
                                  preferred_element_type=F32)
            p_c = jnp.exp(s_c - jnp.max(s_c, axis=-1, keepdims=True))
            acc = jnp.dot(p_c.astype(BF16), vc, preferred_element_type=F32)
            outs.append(acc / jnp.sum(p_c, axis=-1, keepdims=True))
        o_ref[0] = _halves(outs[0], outs[1]).astype(BF16)

    @pl.when(qi >= n_ctx_tiles)
    def _():
        kc, vc = k_ref[0, 0:n_ctx, :], v_ref[0, 0:n_ctx, :]
        work = []
        for sub, bias_ref in ((0, b0_ref), (1, b1_ref)):
            j = (qi - n_ctx_tiles) * (TOKEN_TILE // NA_QBLOCK) + sub
            start = pl.multiple_of(n_ctx + ws_ref[j], NA_QBLOCK)
            kw, vw = k_ref[0, pl.ds(start, nwin), :], v_ref[0, pl.ds(start, nwin), :]
            for a in (0, 1):
                q = q_ref[0, sub * NA_QBLOCK:(sub + 1) * NA_QBLOCK, a * LANES:(a + 1) * LANES]
                s_c = lax.dot_general(q, kc, NT_DIMS, preferred_element_type=F32)
                s_w = lax.dot_general(q, kw, NT_DIMS, preferred_element_type=F32) + bias_ref[0, a]
                work.append((s_c, s_w, vw))
        outs = []
        for s_c, s_w, vw in work:
            m = jnp.maximum(jnp.max(s_c, axis=-1, keepdims=True), jnp.max(s_w, axis=-1, keepdims=True))
            p_c, p_w = jnp.exp(s_c - m), jnp.exp(s_w - m)
            l = jnp.sum(p_c, axis=-1, keepdims=True) + jnp.sum(p_w, axis=-1, keepdims=True)
            acc = (jnp.dot(p_c.astype(BF16), vc, preferred_element_type=F32)
                   + jnp.dot(p_w.astype(BF16), vw, preferred_element_type=F32))
            outs.append(acc / l)
        for sub in (0, 1):
            o_ref[0, sub * NA_QBLOCK:(sub + 1) * NA_QBLOCK, :] = _halves(outs[2 * sub], outs[2 * sub + 1]).astype(BF16)


def _na_attention(q, k, v, bias, var, wstart, *, n_ctx):
    b, s, nq = q.shape
    tq = TOKEN_TILE
    n_pairs = nq // (2 * LANES)
    n_ctx_tiles = n_ctx // tq
    per_tile = tq // NA_QBLOCK
    nwin = NA_WIN_ROWS * GRID_W

    def bias_spec(sub):
        def index(bi, p, i, var_ref, ws_ref):
            j = jnp.maximum(i - n_ctx_tiles, 0) * per_tile + sub
            return (var_ref[j], p, 0, 0)
        return pl.BlockSpec((1, 2, NA_QBLOCK, nwin), index)

    grid_spec = pltpu.PrefetchScalarGridSpec(
        num_scalar_prefetch=2,
        grid=(b, n_pairs, s // tq),
        in_specs=[
            pl.BlockSpec((1, tq, 2 * LANES), lambda bi, p, i, *_: (bi, i, p)),
            pl.BlockSpec((1, s, LANES), lambda bi, p, i, *_: (bi, 0, p)),
            pl.BlockSpec((1, s, LANES), lambda bi, p, i, *_: (bi, 0, p)),
            bias_spec(0), bias_spec(1),
        ],
        out_specs=pl.BlockSpec((1, tq, LANES), lambda bi, p, i, *_: (bi, i, p)),
    )
    return pl.pallas_call(
        functools.partial(_na_kernel, n_ctx=n_ctx, n_ctx_tiles=n_ctx_tiles),
        grid_spec=grid_spec,
        out_shape=jax.ShapeDtypeStruct((b, s, n_pairs * LANES), BF16),
        compiler_params=_cparams(3),
        name="na_attn",
    )(var, wstart, q, k, v, bias, bias)


O_HALO = 16


def _mix_ffn_kernel(xp_ref, x_ref, xn_ref, op_ref, ox_ref, oc_ref, on_ref, wo_ref, mod_ref, g_ref, wup_ref, cw_ref,
                    cb_ref, wdn_ref, fg_ref, y_ref, u_scr, a_scr, *, n_ctx_tiles, n_tiles, first_tile, chunk,
                    final_norm):
    t = pl.program_id(1) + first_tile
    tm = x_ref.shape[1]
    has_prev = jnp.logical_and(t != 0, t != n_ctx_tiles).astype(F32)
    has_next = jnp.logical_and(t != n_ctx_tiles - 1, t != n_tiles - 1).astype(F32)
    o = jnp.where(t < n_ctx_tiles, oc_ref[0], ox_ref[0]).astype(F32)
    o_cat = jnp.concatenate([op_ref[0].astype(F32)[O_HALO - HALO:], o, on_ref[0].astype(F32)[:HALO]], axis=0)
    x_cat = jnp.concatenate([xp_ref[0], x_ref[0], xn_ref[0]], axis=0)
    x1 = x_cat + mod_ref[0, 2:3, :] * jnp.dot(o_cat.astype(BF16), wo_ref[...], preferred_element_type=F32)
    g, shift, scale = g_ref[...], mod_ref[0, 3:4, :], mod_ref[0, 4:5, :]
    x = x1[HALO:HALO + tm]
    h = jnp.concatenate([_norm_mod(x1[:HALO], g, shift, scale) * has_prev,
                         _norm_mod(x, g, shift, scale),
                         _norm_mod(x1[HALO + tm:], g, shift, scale) * has_next], axis=0).astype(BF16)
    f = wdn_ref.shape[0]

    def val_gate(ref, j):
        return jnp.concatenate([ref[:, j * chunk:(j + 1) * chunk], ref[:, f + j * chunk:f + (j + 1) * chunk]], axis=1)

    for j in range(f // chunk):
        u_scr[...] = jnp.dot(h, val_gate(wup_ref, j), preferred_element_type=F32)
        cw = val_gate(cw_ref, j)
        u = (u_scr[HALO - 1:HALO - 1 + tm, :] * cw[0:1] + u_scr[HALO:HALO + tm, :] * cw[1:2]
             + u_scr[HALO + 1:HALO + 1 + tm, :] * cw[2:3] + val_gate(cb_ref, j))
        val, gate = u[:, :chunk], u[:, chunk:]
        a_scr[:, j * chunk:(j + 1) * chunk] = (gate * (1.0 / (1.0 + jnp.exp(-gate))) * val).astype(BF16)
    y = x + mod_ref[0, 5:6, :] * jnp.dot(a_scr[...], wdn_ref[...], preferred_element_type=F32)
    if final_norm:
        y = y * lax.rsqrt(jnp.mean(y * y, axis=-1, keepdims=True) + EPS) * fg_ref[...]
    y_ref[0] = y


def _mix_ffn(xs, o_x, o_c, w_out, mods, g, wup, cw, cb, wdn, final_g, *, n_ctx_tiles, x_offset, latent_only, chunk,
             final_norm):
    b, s, d = xs.shape
    tm = TOKEN_TILE
    n_tiles = s // tm
    n = o_x.shape[2]
    f = wdn.shape[0]
    first = n_ctx_tiles if latent_only else 0
    assert n_ctx_tiles == 1
    per, per_o = tm // HALO, tm // O_HALO
    lat = lambda i: jnp.maximum(i + first, n_ctx_tiles) - x_offset
    return pl.pallas_call(
        functools.partial(_mix_ffn_kernel, n_ctx_tiles=n_ctx_tiles, n_tiles=n_tiles, first_tile=first, chunk=chunk,
                          final_norm=final_norm),
        grid=(b, n_tiles - first),
        in_specs=[
            pl.BlockSpec((1, HALO, d), lambda bi, i: (bi, jnp.maximum((i + first) * per - 1, 0), 0)),
            pl.BlockSpec((1, tm, d), lambda bi, i: (bi, i + first, 0)),
            pl.BlockSpec((1, HALO, d), lambda bi, i: (bi, jnp.minimum((i + first + 1) * per, s // HALO - 1), 0)),
            pl.BlockSpec((1, O_HALO, n), lambda bi, i: (bi, jnp.maximum(lat(i) * per_o - 1, 0), 0)),
            pl.BlockSpec((1, tm, n), lambda bi, i: (bi, lat(i), 0)),
            pl.BlockSpec((1, tm, n), lambda bi, i: (bi, jnp.minimum(i + first, n_ctx_tiles - 1), 0)),
            pl.BlockSpec((1, O_HALO, n),
                         lambda bi, i: (bi, jnp.minimum((lat(i) + 1) * per_o, o_x.shape[1] // O_HALO - 1), 0)),
            _resident(w_out.shape),
            pl.BlockSpec((1, 6, d), lambda bi, i: (jnp.where(i + first < n_ctx_tiles, b, bi), 0, 0)),
            _resident((1, d)),
            pl.BlockSpec(wup.shape, lambda bi, i: (0, 0), pipeline_mode=pl.Buffered(1)),
            _resident(cw.shape), _resident(cb.shape),
            pl.BlockSpec(wdn.shape, lambda bi, i: (0, 0), pipeline_mode=pl.Buffered(1)),
            _resident((1, d)),
        ],
        out_specs=pl.BlockSpec((1, tm, d), lambda bi, i: (bi, i, 0)),
        out_shape=jax.ShapeDtypeStruct((b, s - first * tm, d), F32),
        scratch_shapes=[pltpu.VMEM((tm + 2 * HALO, 2 * chunk), F32), pltpu.VMEM((tm, f), BF16)],
        compiler_params=_cparams(2),
        name="mix_ffn",
    )(xs, xs, xs, o_x, o_x, o_c, o_x, w_out, mods, g, wup, cw, cb, wdn, final_g)


def _rope_table(t, n_ctx, rot_dim, lane_lo):
    tt = jnp.arange(t)
    row = (tt // GRID_W).astype(F32)
    col = (tt % GRID_W).astype(F32)
    half = rot_dim // 2
    inv = ROPE_BASE ** (-jnp.arange(0, half, 2, dtype=F32) / half)
    ar, ac = row[:, None] * inv, col[:, None] * inv
    ang = jnp.concatenate([ar, ar, ac, ac], axis=-1)
    quarter = rot_dim // 4
    sign = jnp.where((jnp.arange(rot_dim) % (2 * quarter)) < quarter, -1.0, 1.0).astype(F32)
    cos, sin = jnp.cos(ang), jnp.sin(ang) * sign
    reps = (LANES - lane_lo) // rot_dim if lane_lo == 0 else 1
    cos_l = jnp.ones((t, LANES), F32).at[:, lane_lo:lane_lo + reps * rot_dim].set(jnp.tile(cos, (1, reps)))
    sin_l = jnp.zeros((t, LANES), F32).at[:, lane_lo:lane_lo + reps * rot_dim].set(jnp.tile(sin, (1, reps)))
    cos_s = jnp.concatenate([jnp.ones((n_ctx, LANES), F32), cos_l], axis=0)
    sin_s = jnp.concatenate([jnp.zeros((n_ctx, LANES), F32), sin_l], axis=0)
    return cos_s, sin_s


def _to_slots(w, n_heads, width, halves):
    d = w.shape[0]
    wh = w.reshape(d, n_heads, width)
    lo = jnp.pad(wh, ((0, 0), (0, 0), (0, LANES - width)))
    if not any(halves):
        return lo.reshape(d, n_heads * LANES)
    hi = jnp.pad(wh, ((0, 0), (0, 0), (LANES // 2, LANES // 2 - width)))
    in_hi = jnp.asarray(halves, jnp.int32).reshape(1, n_heads, 1) == 1
    return jnp.where(in_hi, hi, lo).reshape(d, n_heads * LANES)


def _vt_weights(wv, groups, width, tm):
    d = wv.shape[0]
    rows = width + O_HALO
    wt = jnp.zeros((groups, rows, d), wv.dtype).at[:, :width, :].set(wv.T.reshape(groups, width, d))
    ones = jnp.zeros((groups, rows, tm), F32).at[:, width, :].set(1.0)
    return wt.reshape(groups * rows, d), ones.reshape(groups * rows, tm)


def _dup(g):
    return jnp.concatenate([g, g]).reshape(1, LANES).astype(F32)


def kernel(x, c, ctx, c_ctx, ada_w, ada_b, norm1_g, norm2_g, ffn_w_up, ffn_conv_w, ffn_conv_b, ffn_w_down,
           gqa_w_in, gqa_q_norm_g, gqa_k_norm_g, gqa_w_out,
           mla_w_in, mla_q_norm_g, mla_kv_norm_g, mla_w_uq, mla_w_ukv, mla_w_out,
           diff_w_in, diff_lambda, diff_subln_g, diff_w_out,
           na_w_in, na_rpb, na_w_out, final_norm_g):
    b, t, d = x.shape
    n_ctx = ctx.shape[1]
    depth = ada_w.shape[0]
    assert n_ctx % TOKEN_TILE == 0 and t % TOKEN_TILE == 0 and b + 1 <= 8
    nct = n_ctx // TOKEN_TILE
    f = ffn_w_down.shape[1]
    chunk = 2 * LANES
    assert f % chunk == 0

    xs = jnp.concatenate([ctx, x], axis=1)
    cin = jnp.zeros((8, d), F32).at[:b].set(c).at[b].set(c_ctx)
    mods_all = _ada_all(cin, ada_w, ada_b).reshape(depth, 8, 6, d)

    tm = TOKEN_TILE
    ones_l = jnp.ones((1, LANES), F32)
    dummy_tab = jnp.zeros((xs.shape[1], LANES), F32)
    dummy_ones = jnp.zeros((8, LANES), F32)
    cos64, sin64 = _rope_table(t, n_ctx, GQA_HEAD_DIM, 0)
    cos_mla, sin_mla = _rope_table(t, n_ctx, MLA_ROPE_DIM, MLA_NOPE_DIM)
    no_lam, no_g = jnp.zeros((4, DIFF_HEAD_DIM), F32), jnp.ones((1, LANES), F32)
    pair_rows = MLA_V_DIM + O_HALO
    diff_rows = 2 * DIFF_HEAD_DIM + O_HALO

    for i in range(depth):
        m, j = i % N_MIXERS, i // N_MIXERS
        mods = mods_all[i]
        g1 = norm1_g[i].reshape(1, d)
        if m == 0:
            w = gqa_w_in[j].astype(BF16)
            nqc, nkc = GQA_HEADS * GQA_HEAD_DIM, GQA_KV_HEADS * GQA_HEAD_DIM
            wk = w[:, nqc:nqc + nkc].reshape(d, GQA_KV_HEADS, 1, GQA_HEAD_DIM)
            wk = jnp.broadcast_to(wk, (d, GQA_KV_HEADS, 2, GQA_HEAD_DIM)).reshape(d, GQA_KV_HEADS * LANES)
            wvt, vones = _vt_weights(w[:, nqc + nkc:], GQA_KV_HEADS, GQA_HEAD_DIM, tm)
            qc = GQA_HEAD_DIM ** -0.5 * LOG2E
            q, k, vt = _project(xs, mods, g1, w[:, :nqc], wk, wvt, _dup(gqa_q_norm_g[j]), _dup(gqa_k_norm_g[j]),
                                cos64 * qc, sin64 * qc, cos64, sin64, vones, qk_norm=True, rope=True,
                                head_dim=GQA_HEAD_DIM, k_norm_dim=2 * GQA_HEAD_DIM, v_transposed=True, q_rotated=True, n_ctx_tiles=nct,
                                name="proj_gqa")
            o_x, o_c = _flash(q, k, vt, no_lam, no_g, mode="pair", kw=LANES, k_cols=((0, LANES), (0, LANES)),
                       v_block_rows=pair_rows, v_rows=((0, pair_rows), (0, pair_rows)), vw=GQA_HEAD_DIM,
                       k_index=lambda p: p // 2, v_index=lambda p: p // 2, n_ctx=n_ctx, diff_scale=1.0,
                       name="attn_gqa")
            w_out = gqa_w_out[j]
        elif m == 1:
            w = mla_w_in[j].astype(BF16)
            nlat = MLA_Q_LORA + MLA_KV_LORA
            kr_slot = jnp.zeros((d, LANES), BF16).at[:, MLA_NOPE_DIM:MLA_NOPE_DIM + MLA_ROPE_DIM].set(w[:, nlat:])
            win = jnp.concatenate([w[:, :nlat], kr_slot], axis=1)
            qk_dim = MLA_NOPE_DIM + MLA_ROPE_DIM
            wuq = _to_slots(mla_w_uq[j].astype(BF16), MLA_HEADS, qk_dim, [0] * MLA_HEADS)
            wukv = mla_w_ukv[j].astype(BF16).reshape(MLA_KV_LORA, MLA_HEADS, MLA_NOPE_DIM + MLA_V_DIM)
            wuk = _to_slots(wukv[:, :, :MLA_NOPE_DIM].reshape(MLA_KV_LORA, -1), MLA_HEADS, MLA_NOPE_DIM,
                            [0] * MLA_HEADS)
            wuvt, vones = _vt_weights(wukv[:, :, MLA_NOPE_DIM:].reshape(MLA_KV_LORA, MLA_HEADS * MLA_V_DIM),
                                      MLA_HEADS, MLA_V_DIM, tm)
            qc = qk_dim ** -0.5 * LOG2E
            q, k, vt = _project_mla(xs, mods, g1, win, mla_q_norm_g[j].reshape(1, -1),
                                    mla_kv_norm_g[j].reshape(1, -1), wuq, wuk, wuvt,
                                    cos_mla * qc, sin_mla * qc, cos_mla, sin_mla, vones, n_ctx_tiles=nct)
            o_x, o_c = _flash(q, k, vt, no_lam, no_g, mode="pair", kw=2 * LANES, k_cols=((0, LANES), (LANES, 2 * LANES)),
                       v_block_rows=2 * pair_rows, v_rows=((0, pair_rows), (pair_rows, 2 * pair_rows)),
                       vw=MLA_V_DIM, k_index=lambda p: p, v_index=lambda p: p, n_ctx=n_ctx, diff_scale=1.0,
                       name="attn_mla")
            w_out = mla_w_out[j]
        elif m == 2:
            w = diff_w_in[j].astype(BF16)
            wvt, vones = _vt_weights(w[:, 2 * d:], DIFF_HEADS, 2 * DIFF_HEAD_DIM, tm)
            lambda_init = 0.8 - 0.6 * math.exp(-0.3 * i)
            qc = DIFF_HEAD_DIM ** -0.5 * LOG2E
            q, k, vt = _project(xs, mods, g1, w[:, :d], w[:, d:2 * d], wvt, ones_l, ones_l,
                                cos64 * qc, sin64 * qc, cos64, sin64, vones, qk_norm=False, rope=True,
                                head_dim=DIFF_HEAD_DIM, k_norm_dim=DIFF_HEAD_DIM, v_transposed=True, q_rotated=True, n_ctx_tiles=nct,
                                name="proj_diff")
            o_x, o_c = _flash(q, k, vt, diff_lambda[j], diff_subln_g[j].reshape(1, LANES), mode="diff", kw=LANES,
                       k_cols=((0, LANES), (0, LANES)), v_block_rows=diff_rows,
                       v_rows=((0, diff_rows), (0, diff_rows)), vw=2 * DIFF_HEAD_DIM,
                       k_index=lambda p: p, v_index=lambda p: p, n_ctx=n_ctx, diff_scale=1.0 - lambda_init,
                       name="attn_diff")
            w_out = diff_w_out[j]
        else:
            w = na_w_in[j].astype(BF16)
            na_scale = NA_HEAD_DIM ** -0.5
            assert math.frexp(na_scale)[0] == 0.5
            q, k, v = _project(xs, mods, g1, w[:, :d] * jnp.asarray(na_scale, BF16), w[:, d:2 * d], w[:, 2 * d:],
                               ones_l, ones_l, dummy_tab, dummy_tab, dummy_tab, dummy_tab, dummy_ones,
                               qk_norm=False, rope=False, head_dim=NA_HEAD_DIM, k_norm_dim=NA_HEAD_DIM,
                               v_transposed=False, q_rotated=False, n_ctx_tiles=nct,
                               name="proj_na")
            idx_all, var, wstart = _na_tables(t)
            bias = _na_bias(idx_all, na_rpb[j])
            o_x = o_c = _na_attention(q, k, v, bias, jnp.asarray(var), jnp.asarray(wstart), n_ctx=n_ctx)
            w_out = na_w_out[j]
        last = i == depth - 1

        xs = _mix_ffn(xs, o_x, o_c, w_out.astype(BF16), mods, norm2_g[i].reshape(1, d),
                      ffn_w_up[i].astype(BF16), ffn_conv_w[i], ffn_conv_b[i].reshape(1, 2 * f),
                      ffn_w_down[i].astype(BF16), final_norm_g.reshape(1, d), n_ctx_tiles=nct,
                      x_offset=0 if m == 3 else nct, latent_only=last, chunk=chunk, final_norm=last)
    return xs
```

```python
import functools
import math

import numpy as np
import jax
import jax.numpy as jnp
from jax import lax
from jax.experimental import pallas as pl
from jax.experimental.pallas import tpu as pltpu

F32 = jnp.float32
BF16 = jnp.bfloat16

LANES = 128
TOKEN_TILE = 256
HALO = 8
KV_CHUNK = 768
SUB_KEYS = 256
FLASH_TQ = 512
LOOP_CHUNKS = 2
VMEM_LIMIT = 56 * 1024 * 1024

GRID_W = 64
ROPE_BASE = 10000.0
EPS = 1e-6
NEG_INF = -1e30
LOG2E = 1.4426950408889634

GQA_HEADS, GQA_KV_HEADS, GQA_HEAD_DIM = 16, 4, 64
MLA_HEADS, MLA_NOPE_DIM, MLA_ROPE_DIM, MLA_V_DIM = 16, 64, 32, 64
MLA_Q_LORA, MLA_KV_LORA = 384, 256
DIFF_HEADS, DIFF_HEAD_DIM = 8, 64
NA_HEADS, NA_HEAD_DIM, NA_ROWS, NA_COLS = 16, 64, 8, 16
NA_QBLOCK = 128
NA_BIAS_TILE = 2048
NA_WIN_ROWS = 10
N_MIXERS = 4

NT_DIMS = (((1,), (1,)), ((), ()))


def _cparams(n_axes):
    return pltpu.CompilerParams(dimension_semantics=("arbitrary",) * n_axes, vmem_limit_bytes=VMEM_LIMIT)


def _resident(shape):
    nd = len(shape)
    return pl.BlockSpec(shape, lambda *_: (0,) * nd)


def _ada_kernel(c_ref, w_ref, b_ref, o_ref):
    c = c_ref[...]
    s = c * (1.0 / (1.0 + jnp.exp(-c)))
    o_ref[0] = jnp.dot(s.astype(BF16), w_ref[0].astype(BF16), preferred_element_type=F32) + b_ref[0]


def _ada_all(cin, ada_w, ada_b):
    depth, d, n = ada_w.shape
    tn = n // 4
    return pl.pallas_call(
        _ada_kernel,
        grid=(depth, n // tn),
        in_specs=[
            pl.BlockSpec((8, d), lambda l, j: (0, 0)),
            pl.BlockSpec((1, d, tn), lambda l, j: (l, 0, j)),
            pl.BlockSpec((1, 1, tn), lambda l, j: (l, 0, j)),
        ],
        out_specs=pl.BlockSpec((1, 8, tn), lambda l, j: (l, 0, j)),
        out_shape=jax.ShapeDtypeStruct((depth, 8, n), F32),
        compiler_params=_cparams(2),
        name="ada",
    )(cin, ada_w, ada_b.reshape(depth, 1, n))


def _norm_mod(x, g, shift, scale):
    ms = jnp.mean(x * x, axis=-1, keepdims=True)
    return (x * lax.rsqrt(ms + EPS) * g) * (1.0 + scale) + shift


def _rope_slot(y, cos, sin_signed, lo_mask, quarter):
    fwd = pltpu.roll(y, LANES - quarter, axis=1)
    bwd = pltpu.roll(y, quarter, axis=1)
    return y * cos + jnp.where(lo_mask, fwd, bwd) * sin_signed


def _lo_mask(shape, quarter):
    lane = lax.broadcasted_iota(jnp.int32, shape, 1)
    return (lane % (2 * quarter)) < quarter


def _q_spec(n, tm, n_tiles, n_ctx_tiles, rotate):
    if not rotate:
        return pl.BlockSpec((1, tm, n), lambda bi, i: (bi, i, 0))
    n_lat = n_tiles - n_ctx_tiles
    return pl.BlockSpec((1, tm, n), lambda bi, i: (bi, jnp.where(i < n_ctx_tiles, i + n_lat, i - n_ctx_tiles), 0))


def _vt_spec(rows, tm):
    per = KV_CHUNK // tm
    return pl.BlockSpec((1, 1, rows, tm), lambda bi, i: (bi, i // per, 0, i % per))


def _proj_kernel(x_ref, mod_ref, g_ref, wq_ref, wk_ref, wv_ref, qg_ref, kg_ref, cosq_ref, sinq_ref, cosk_ref, sink_ref,
                 vones_ref, q_ref, k_ref, v_ref, *, qk_norm, rope, head_dim, k_norm_dim, v_transposed):
    h = _norm_mod(x_ref[0], g_ref[...], mod_ref[0, 0:1, :], mod_ref[0, 1:2, :]).astype(BF16)
    tm = h.shape[0]
    lo = _lo_mask((tm, LANES), head_dim // 4) if rope else None
    low_half = lax.broadcasted_iota(jnp.int32, (tm, LANES), 1) < LANES // 2

    def chunks(w_ref):
        n = w_ref.shape[1]
        for c0 in range(0, n, 2 * LANES):
            c1 = min(c0 + 2 * LANES, n)
            y2 = jnp.dot(h, w_ref[:, c0:c1], preferred_element_type=F32)
            for s0 in range(0, c1 - c0, LANES):
                yield (c0 + s0) // LANES, y2[:, s0:s0 + LANES]

    for c, y in chunks(wq_ref):
        if qk_norm:
            sq = y * y
            ms_lo = jnp.sum(jnp.where(low_half, sq, 0.0), axis=-1, keepdims=True) * (1.0 / head_dim)
            ms_hi = jnp.sum(jnp.where(low_half, 0.0, sq), axis=-1, keepdims=True) * (1.0 / head_dim)
            y = y * jnp.where(low_half, lax.rsqrt(ms_lo + EPS), lax.rsqrt(ms_hi + EPS)) * qg_ref[...]
        if rope:
            y = _rope_slot(y, cosq_ref[...], sinq_ref[...], lo, head_dim // 4)
        q_ref[0, :, 2 * c * LANES:(2 * c + 1) * LANES] = jnp.where(low_half, y, 0.0).astype(BF16)
        q_ref[0, :, (2 * c + 1) * LANES:(2 * c + 2) * LANES] = jnp.where(low_half, 0.0, y).astype(BF16)
    for c, y in chunks(wk_ref):
        if qk_norm:
            ms = jnp.sum(y * y, axis=-1, keepdims=True) * (1.0 / k_norm_dim)
            y = y * lax.rsqrt(ms + EPS) * kg_ref[...]
        if rope:
            y = _rope_slot(y, cosk_ref[...], sink_ref[...], lo, head_dim // 4)
        k_ref[0, :, c * LANES:(c + 1) * LANES] = y.astype(BF16)
    if v_transposed:
        vt = lax.dot_general(wv_ref[...], h, NT_DIMS, preferred_element_type=F32) + vones_ref[...]
        v_ref[0, 0] = vt.astype(BF16)
    else:
        v_ref[0] = jnp.dot(h, wv_ref[...], preferred_element_type=F32).astype(BF16)


def _project(xs, mods, g, wq, wk, wv, qg, kg, cosq, sinq, cosk, sink, vones, *, qk_norm, rope, head_dim, k_norm_dim,
             v_transposed, q_rotated, n_ctx_tiles, name):
    b, s, d = xs.shape
    tm = TOKEN_TILE
    nq, nk = 2 * wq.shape[1], wk.shape[1]
    tok = lambda n: pl.BlockSpec((1, tm, n), lambda bi, i: (bi, i, 0))
    tab = pl.BlockSpec((tm, LANES), lambda bi, i: (i, 0))
    if v_transposed:
        v_spec = _vt_spec(wv.shape[0], tm)
        v_shape = jax.ShapeDtypeStruct((b, s // KV_CHUNK, wv.shape[0], KV_CHUNK), BF16)
    else:
        v_spec, v_shape = tok(wv.shape[1]), jax.ShapeDtypeStruct((b, s, wv.shape[1]), BF16)
    return pl.pallas_call(
        functools.partial(_proj_kernel, qk_norm=qk_norm, rope=rope, head_dim=head_dim, k_norm_dim=k_norm_dim,
                          v_transposed=v_transposed),
        grid=(b, s // tm),
        in_specs=[
            tok(d),
            pl.BlockSpec((1, 6, d), lambda bi, i: (jnp.where(i < n_ctx_tiles, b, bi), 0, 0)),
            _resident((1, d)),
            _resident(wq.shape), _resident(wk.shape), _resident(wv.shape),
            _resident((1, LANES)), _resident((1, LANES)),
            tab, tab, tab, tab,
            _resident(vones.shape),
        ],
        out_specs=[_q_spec(nq, tm, s // tm, n_ctx_tiles, q_rotated), tok(nk), v_spec],
        out_shape=[jax.ShapeDtypeStruct((b, s, nq), BF16), jax.ShapeDtypeStruct((b, s, nk), BF16), v_shape],
        compiler_params=_cparams(2),
        name=name,
    )(xs, mods, g, wq, wk, wv, qg, kg, cosq, sinq, cosk, sink, vones)


def _mla_proj_kernel(x_ref, mod_ref, g_ref, win_ref, qn_ref, kvn_ref, wuq_ref, wuk_ref, wuvt_ref,
                     cosq_ref, sinq_ref, cosk_ref, sink_ref, vones_ref, q_ref, k_ref, v_ref):
    h = _norm_mod(x_ref[0], g_ref[...], mod_ref[0, 0:1, :], mod_ref[0, 1:2, :]).astype(BF16)
    tm = h.shape[0]
    t1 = jnp.dot(h, win_ref[...], preferred_element_type=F32)
    cq = t1[:, :MLA_Q_LORA]
    ckv = t1[:, MLA_Q_LORA:MLA_Q_LORA + MLA_KV_LORA]
    kr = t1[:, MLA_Q_LORA + MLA_KV_LORA:]
    cqn = (cq * lax.rsqrt(jnp.mean(cq * cq, axis=-1, keepdims=True) + EPS) * qn_ref[...]).astype(BF16)
    ckvn = (ckv * lax.rsqrt(jnp.mean(ckv * ckv, axis=-1, keepdims=True) + EPS) * kvn_ref[...]).astype(BF16)
    quarter = MLA_ROPE_DIM // 4
    lo = _lo_mask((tm, LANES), quarter)
    kr = _rope_slot(kr, cosk_ref[...], sink_ref[...], lo, quarter)
    n = wuq_ref.shape[1]
    for c0 in range(0, n, 2 * LANES):
        q2 = jnp.dot(cqn, wuq_ref[:, c0:c0 + 2 * LANES], preferred_element_type=F32)
        k2 = jnp.dot(ckvn, wuk_ref[:, c0:c0 + 2 * LANES], preferred_element_type=F32)
        for s0 in (0, LANES):
            q_ref[0, :, c0 + s0:c0 + s0 + LANES] = _rope_slot(
                q2[:, s0:s0 + LANES], cosq_ref[...], sinq_ref[...], lo, quarter).astype(BF16)
            k_ref[0, :, c0 + s0:c0 + s0 + LANES] = (k2[:, s0:s0 + LANES] + kr).astype(BF16)
    vt = lax.dot_general(wuvt_ref[...], ckvn, NT_DIMS, preferred_element_type=F32) + vones_ref[...]
    v_ref[0, 0] = vt.astype(BF16)


def _project_mla(xs, mods, g, win, qn, kvn, wuq, wuk, wuvt, cosq, sinq, cosk, sink, vones, *, n_ctx_tiles):
    b, s, d = xs.shape
    tm = TOKEN_TILE
    nq, nk = wuq.shape[1], wuk.shape[1]
    tok = lambda n: pl.BlockSpec((1, tm, n), lambda bi, i: (bi, i, 0))
    tab = pl.BlockSpec((tm, LANES), lambda bi, i: (i, 0))
    return pl.pallas_call(
        _mla_proj_kernel,
        grid=(b, s // tm),
        in_specs=[
            tok(d),
            pl.BlockSpec((1, 6, d), lambda bi, i: (jnp.where(i < n_ctx_tiles, b, bi), 0, 0)),
            _resident((1, d)),
            _resident(win.shape), _resident(qn.shape), _resident(kvn.shape),
            _resident(wuq.shape), _resident(wuk.shape), _resident(wuvt.shape),
            tab, tab, tab, tab,
            _resident(vones.shape),
        ],
        out_specs=[_q_spec(nq, tm, s // tm, n_ctx_tiles, True), tok(nk), _vt_spec(wuvt.shape[0], tm)],
        out_shape=[jax.ShapeDtypeStruct((b, s, nq), BF16), jax.ShapeDtypeStruct((b, s, nk), BF16),
                   jax.ShapeDtypeStruct((b, s // KV_CHUNK, wuvt.shape[0], KV_CHUNK), BF16)],
        compiler_params=_cparams(2),
        name="proj_mla",
    )(xs, mods, g, win, qn, kvn, wuq, wuk, wuvt, cosq, sinq, cosk, sink, vones)


def _flash_kernel(q_ref, k_ref, vt_ref, lam_ref, subg_ref, o_ref, *scratch, mode, k_cols, v_rows, vw, diff_scale,
                  ctx_keys):
    tq = q_ref.shape[1]
    n_chunks = vt_ref.shape[1]
    qs = [q_ref[0, :, a * LANES:(a + 1) * LANES] for a in (0, 1)]
    n_rows = v_rows[0][1] - v_rows[0][0]

    def keys(key_rows):
        k0 = k_ref[0, key_rows, k_cols[0][0]:k_cols[0][1]]
        k1 = k0 if k_cols[1] == k_cols[0] else k_ref[0, key_rows, k_cols[1][0]:k_cols[1][1]]
        return k0, k1

    def values(chunk_idx, key_lanes):
        v0 = vt_ref[0, chunk_idx, v_rows[0][0]:v_rows[0][1], key_lanes]
        v1 = v0 if v_rows[1] == v_rows[0] else vt_ref[0, chunk_idx, v_rows[1][0]:v_rows[1][1], key_lanes]
        return v0, v1

    def update(st, cmax, vt_a, carry_a):
        m, acc = carry_a
        m_new = jnp.maximum(m, cmax)
        p = jnp.exp2(st - m_new).astype(BF16)
        return m_new, jnp.exp2(m - m_new) * acc + jnp.dot(vt_a, p, preferred_element_type=F32)

    init = tuple((jnp.full((1, tq), -jnp.inf, F32), jnp.zeros((n_rows, tq), F32)) for _ in (0, 1))

    def finish(carry):
        (_, acc_a), (_, acc_b) = carry
        oa = acc_a[0:vw] / acc_a[vw:vw + 1]
        ob = acc_b[0:vw] / acc_b[vw:vw + 1]
        if mode == "pair":
            out = jnp.concatenate([oa, ob], axis=0).T
        else:
            lam = lam_ref[...]
            lam_full = (jnp.exp(jnp.sum(lam[0:1] * lam[1:2], axis=-1, keepdims=True))
                        - jnp.exp(jnp.sum(lam[2:3] * lam[3:4], axis=-1, keepdims=True))
                        + (1.0 - diff_scale))
            o = (oa - lam_full * ob).T
            out = o * lax.rsqrt(jnp.mean(o * o, axis=-1, keepdims=True) + EPS) * subg_ref[...] * diff_scale
        o_ref[0] = out.astype(BF16)

    if ctx_keys:
        ks, vts = keys(slice(0, ctx_keys)), values(0, slice(0, ctx_keys))
        out = []
        for a in (0, 1):
            st = lax.dot_general(ks[a], qs[a], NT_DIMS, preferred_element_type=F32)
            out.append(update(st, jnp.max(st, axis=0, keepdims=True), vts[a], init[a]))
        finish(tuple(out))
        return

    (st_scr,) = scratch

    def chunk_keys(chunk_idx):
        start = chunk_idx * KV_CHUNK
        return keys(pl.ds(start if isinstance(start, int) else pl.multiple_of(start, KV_CHUNK), KV_CHUNK))

    def score(slot, a, ks):
        st = lax.dot_general(ks[a], qs[a], NT_DIMS, preferred_element_type=F32)
        st_scr[slot, a] = st
        return jnp.max(st, axis=0, keepdims=True)

    def scores(slot, chunk_idx):
        ks = chunk_keys(chunk_idx)
        return tuple(score(slot, a, ks) for a in (0, 1))

    def run(first, count, cmax, carry):
        for u in range(count):
            nxt, slot = first + u + 1, u % 2
            has_next = isinstance(nxt, jax.Array) or nxt < n_chunks
            m_new = [jnp.maximum(carry[a][0], cmax[a]) for a in (0, 1)]
            acc = [jnp.exp2(carry[a][0] - m_new[a]) * carry[a][1] for a in (0, 1)]
            cmax_next = [None, None]
            for j in range(KV_CHUNK // SUB_KEYS):
                rows = slice(j * SUB_KEYS, (j + 1) * SUB_KEYS)
                if has_next:
                    start = nxt * KV_CHUNK + j * SUB_KEYS
                    ks = keys(pl.ds(start if isinstance(start, int) else pl.multiple_of(start, SUB_KEYS), SUB_KEYS))
                    for a in (0, 1):
                        st = lax.dot_general(ks[a], qs[a], NT_DIMS, preferred_element_type=F32)
                        st_scr[1 - slot, a, rows, :] = st
                        mx = jnp.max(st, axis=0, keepdims=True)
                        cmax_next[a] = mx if cmax_next[a] is None else jnp.maximum(cmax_next[a], mx)
                vts = values(first + u, rows)
                for a in (0, 1):
                    p = jnp.exp2(st_scr[slot, a, rows, :] - m_new[a]).astype(BF16)
                    acc[a] = acc[a] + jnp.dot(vts[a], p, preferred_element_type=F32)
            carry = tuple((m_new[a], acc[a]) for a in (0, 1))
            cmax = tuple(cmax_next)
        return cmax, carry

    n_loop = (n_chunks - 1) // LOOP_CHUNKS
    cmax, carry = lax.fori_loop(0, n_loop, lambda i, st: run(i * LOOP_CHUNKS, LOOP_CHUNKS, *st),
                                (scores(0, 0), init))
    _, carry = run(n_loop * LOOP_CHUNKS, n_chunks - n_loop * LOOP_CHUNKS, cmax, carry)
    finish(carry)


def _flash(q, k, vt, lam, subg, *, mode, kw, k_cols, v_block_rows, v_rows, vw, k_index, v_index, n_ctx, diff_scale,
           name):
    b, s, nq = q.shape
    t = s - n_ctx
    n_pairs = nq // (2 * LANES)
    n_chunks = s // KV_CHUNK
    assert s % KV_CHUNK == 0 and n_ctx <= KV_CHUNK and t % FLASH_TQ == 0 and t % n_ctx == 0
    assert vt.shape[1] == n_chunks
    common = dict(mode=mode, k_cols=k_cols, v_rows=v_rows, vw=vw, diff_scale=diff_scale)
    o_x = pl.pallas_call(
        functools.partial(_flash_kernel, ctx_keys=0, **common),
        grid=(b, n_pairs, t // FLASH_TQ),
        in_specs=[
            pl.BlockSpec((1, FLASH_TQ, 2 * LANES), lambda bi, p, i: (bi, i, p)),
            pl.BlockSpec((1, s, kw), lambda bi, p, i: (bi, 0, k_index(p))),
            pl.BlockSpec((1, n_chunks, v_block_rows, KV_CHUNK), lambda bi, p, i: (bi, 0, v_index(p), 0)),
            _resident(lam.shape), _resident(subg.shape),
        ],
        out_specs=pl.BlockSpec((1, FLASH_TQ, LANES), lambda bi, p, i: (bi, i, p)),
        out_shape=jax.ShapeDtypeStruct((b, t, n_pairs * LANES), BF16),
        scratch_shapes=[pltpu.VMEM((2, 2, KV_CHUNK, FLASH_TQ), F32)],
        compiler_params=_cparams(3),
        name=name,
    )(q, k, vt, lam, subg)
    o_c = pl.pallas_call(
        functools.partial(_flash_kernel, ctx_keys=n_ctx, **common),
        grid=(b, n_pairs),
        in_specs=[
            pl.BlockSpec((1, n_ctx, 2 * LANES), lambda bi, p: (bi, t // n_ctx, p)),
            pl.BlockSpec((1, n_ctx, kw), lambda bi, p: (bi, 0, k_index(p))),
            pl.BlockSpec((1, 1, v_block_rows, KV_CHUNK), lambda bi, p: (bi, 0, v_index(p), 0)),
            _resident(lam.shape), _resident(subg.shape),
        ],
        out_specs=pl.BlockSpec((1, n_ctx, LANES), lambda bi, p: (bi, 0, p)),
        out_shape=jax.ShapeDtypeStruct((b, n_ctx, n_pairs * LANES), BF16),
        compiler_params=_cparams(2),
        name=name + "_ctx",
    )(q, k, vt, lam, subg)
    return o_x, o_c


def _na_tables(t):
    rows = t // GRID_W
    win_rows = min(NA_ROWS, rows)
    band_rows = min(win_rows + 1, rows)
    assert rows >= NA_WIN_ROWS and NA_QBLOCK % GRID_W == 0
    qrows = NA_QBLOCK // GRID_W
    nb = t // NA_QBLOCK
    nwin = NA_WIN_ROWS * GRID_W
    q_off, k_off = np.arange(NA_QBLOCK), np.arange(nwin)
    q_dr, q_col = q_off // GRID_W, q_off % GRID_W
    k_dr, k_col = k_off // GRID_W, k_off % GRID_W
    col_start = np.clip(q_col - NA_COLS // 2, 0, GRID_W - NA_COLS)
    col_in = (k_col[None, :] >= col_start[:, None]) & (k_col[None, :] < col_start[:, None] + NA_COLS)
    dc_idx = np.clip(k_col[None, :] - q_col[:, None] + NA_COLS - 1, 0, 2 * NA_COLS - 2)
    patterns, var, wstart = {}, [], []
    for j in range(nb):
        r = j * qrows + q_dr
        row_start = np.clip(r - win_rows // 2, 0, rows - win_rows)
        b0 = min(int(row_start[0]), rows - band_rows)
        w0 = min(b0 - b0 % 2, rows - NA_WIN_ROWS)
        assert w0 % qrows == 0 and w0 <= b0 and b0 + band_rows <= w0 + NA_WIN_ROWS
        k_row = w0 + k_dr
        in_band = (k_row >= b0) & (k_row < b0 + band_rows)
        in_win = (col_in & in_band[None, :] & (k_row[None, :] >= row_start[:, None])
                  & (k_row[None, :] < row_start[:, None] + win_rows))
        dr_idx = np.clip(k_row[None, :] - r[:, None] + NA_ROWS - 1, 0, 2 * NA_ROWS - 2)
        idx = np.where(in_win, dr_idx * (2 * NA_COLS - 1) + dc_idx, -1).astype(np.int32)
        key = idx.tobytes()
        if key not in patterns:
            patterns[key] = (len(patterns), idx)
        var.append(patterns[key][0])
        wstart.append(w0 * GRID_W)
    idx_all = np.stack([p[1] for p in sorted(patterns.values(), key=lambda p: p[0])])
    return idx_all.reshape(len(patterns), 1, -1), np.asarray(var, np.int32), np.asarray(wstart, np.int32)


def _na_bias_kernel(idx_ref, rpb_ref, o_ref):
    idx = idx_ref[0]
    r = rpb_ref[...]
    onehot = (lax.broadcasted_iota(jnp.int32, (r.shape[1], idx.shape[1]), 0) == idx).astype(BF16)
    b1 = r.astype(BF16)
    r2 = r - b1.astype(F32)
    b2 = r2.astype(BF16)
    b3 = (r2 - b2.astype(F32)).astype(BF16)
    heads = r.shape[0]
    parts = jnp.dot(jnp.concatenate([b1, b2, b3], axis=0), onehot, preferred_element_type=F32)
    out = parts[:heads] + parts[heads:2 * heads] + parts[2 * heads:]
    o_ref[0] = jnp.where(idx >= 0, out, NEG_INF)


def _na_bias(idx_all, rpb):
    nvar, _, npos = idx_all.shape
    heads = rpb.shape[0]
    nr = (2 * NA_ROWS - 1) * (2 * NA_COLS - 1)
    nr_pad = pl.cdiv(nr, LANES) * LANES
    rflat = jnp.pad(rpb.reshape(heads, nr), ((0, 0), (0, nr_pad - nr)))
    tn = NA_BIAS_TILE
    assert npos % tn == 0
    out = pl.pallas_call(
        _na_bias_kernel,
        grid=(nvar, npos // tn),
        in_specs=[pl.BlockSpec((1, 1, tn), lambda v, j: (v, 0, j)), _resident(rflat.shape)],
        out_specs=pl.BlockSpec((1, heads, tn), lambda v, j: (v, 0, j)),
        out_shape=jax.ShapeDtypeStruct((nvar, heads, npos), F32),
        compiler_params=_cparams(2),
        name="na_bias",
    )(jnp.asarray(idx_all), rflat)
    return out.reshape(nvar, heads, NA_QBLOCK, npos // NA_QBLOCK)


def _halves(lo_src, hi_src):
    lane = lax.broadcasted_iota(jnp.int32, lo_src.shape, 1)
    return jnp.where(lane < LANES // 2, lo_src, hi_src)


def _na_kernel(var_ref, ws_ref, q_ref, k_ref, v_ref, b0_ref, b1_ref, o_ref, *, n_ctx, n_ctx_tiles):
    qi = pl.program_id(2)
    nwin = NA_WIN_ROWS * GRID_W

    @pl.when(qi < n_ctx_tiles)
    def _():
        kc, vc = k_ref[0, 0:n_ctx, :], v_ref[0, 0:n_ctx, :]
        outs = []
        for a in (0, 1):
            s_c = lax.dot_general(q_ref[0, :, a * LANES:(a + 1) * LANES], kc, NT_DIMS,
                                  preferred_element_type=F32)
            p_c = jnp.exp(s_c - jnp.max(s_c, axis=-1, keepdims=True))
            acc = jnp.dot(p_c.astype(BF16), vc, preferred_element_type=F32)
            outs.append(acc / jnp.sum(p_c, axis=-1, keepdims=True))
        o_ref[0] = _halves(outs[0], outs[1]).astype(BF16)

    @pl.when(qi >= n_ctx_tiles)
    def _():
        kc, vc = k_ref[0, 0:n_ctx, :], v_ref[0, 0:n_ctx, :]
        work = []
        for sub, bias_ref in ((0, b0_ref), (1, b1_ref)):
            j = (qi - n_ctx_tiles) * (TOKEN_TILE // NA_QBLOCK) + sub
            start = pl.multiple_of(n_ctx + ws_ref[j], NA_QBLOCK)
            kw, vw = k_ref[0, pl.ds(start, nwin), :], v_ref[0, pl.ds(start, nwin), :]
            for a in (0, 1):
                q = q_ref[0, sub * NA_QBLOCK:(sub + 1) * NA_QBLOCK, a * LANES:(a + 1) * LANES]
                s_c = lax.dot_general(q, kc, NT_DIMS, preferred_element_type=F32)
                s_w = lax.dot_general(q, kw, NT_DIMS, preferred_element_type=F32) + bias_ref[0, a]
                work.append((s_c, s_w, vw))
        outs = []
        for s_c, s_w, vw in work:
            m = jnp.maximum(jnp.max(s_c, axis=-1, keepdims=True), jnp.max(s_w, axis=-1, keepdims=True))
            p_c, p_w = jnp.exp(s_c - m), jnp.exp(s_w - m)
            l = jnp.sum(p_c, axis=-1, keepdims=True) + jnp.sum(p_w, axis=-1, keepdims=True)
            acc = (jnp.dot(p_c.astype(BF16), vc, preferred_element_type=F32)
                   + jnp.dot(p_w.astype(BF16), vw, preferred_element_type=F32))
            outs.append(acc / l)
        for sub in (0, 1):
            o_ref[0, sub * NA_QBLOCK:(sub + 1) * NA_QBLOCK, :] = _halves(outs[2 * sub], outs[2 * sub + 1]).astype(BF16)


def _na_attention(q, k, v, bias, var, wstart, *, n_ctx):
    b, s, nq = q.shape
    tq = TOKEN_TILE
    n_pairs = nq // (2 * LANES)
    n_ctx_tiles = n_ctx // tq
    per_tile = tq // NA_QBLOCK
    nwin = NA_WIN_ROWS * GRID_W

    def bias_spec(sub):
        def index(bi, p, i, var_ref, ws_ref):
            j = jnp.maximum(i - n_ctx_tiles, 0) * per_tile + sub
            return (var_ref[j], p, 0, 0)
        return pl.BlockSpec((1, 2, NA_QBLOCK, nwin), index)

    grid_spec = pltpu.PrefetchScalarGridSpec(
        num_scalar_prefetch=2,
        grid=(b, n_pairs, s // tq),
        in_specs=[
            pl.BlockSpec((1, tq, 2 * LANES), lambda bi, p, i, *_: (bi, i, p)),
            pl.BlockSpec((1, s, LANES), lambda bi, p, i, *_: (bi, 0, p)),
            pl.BlockSpec((1, s, LANES), lambda bi, p, i, *_: (bi, 0, p)),
            bias_spec(0), bias_spec(1),
        ],
        out_specs=pl.BlockSpec((1, tq, LANES), lambda bi, p, i, *_: (bi, i, p)),
    )
    return pl.pallas_call(
        functools.partial(_na_kernel, n_ctx=n_ctx, n_ctx_tiles=n_ctx_tiles),
        grid_spec=grid_spec,
        out_shape=jax.ShapeDtypeStruct((b, s, n_pairs * LANES), BF16),
        compiler_params=_cparams(3),
        name="na_attn",
    )(var, wstart, q, k, v, bias, bias)


O_HALO = 16


def _mix_ffn_kernel(xp_ref, x_ref, xn_ref, op_ref, ox_ref, oc_ref, on_ref, wo_ref, mod_ref, g_ref, wup_ref, cw_ref,
                    cb_ref, wdn_ref, fg_ref, y_ref, u_scr, a_scr, *, n_ctx_tiles, n_tiles, first_tile, chunk,
                    final_norm):
    t = pl.program_id(1) + first_tile
    tm = x_ref.shape[1]
    has_prev = jnp.logical_and(t != 0, t != n_ctx_tiles).astype(F32)
    has_next = jnp.logical_and(t != n_ctx_tiles - 1, t != n_tiles - 1).astype(F32)
    o = jnp.where(t < n_ctx_tiles, oc_ref[0], ox_ref[0]).astype(F32)
    o_cat = jnp.concatenate([op_ref[0].astype(F32)[O_HALO - HALO:], o, on_ref[0].astype(F32)[:HALO]], axis=0)
    x_cat = jnp.concatenate([xp_ref[0], x_ref[0], xn_ref[0]], axis=0)
    x1 = x_cat + mod_ref[0, 2:3, :] * jnp.dot(o_cat.astype(BF16), wo_ref[...], preferred_element_type=F32)
    g, shift, scale = g_ref[...], mod_ref[0, 3:4, :], mod_ref[0, 4:5, :]
    x = x1[HALO:HALO + tm]
    h = jnp.concatenate([_norm_mod(x1[:HALO], g, shift, scale) * has_prev,
                         _norm_mod(x, g, shift, scale),
                         _norm_mod(x1[HALO + tm:], g, shift, scale) * has_next], axis=0).astype(BF16)
    f = wdn_ref.shape[0]

    def val_gate(ref, j):
        return jnp.concatenate([ref[:, j * chunk:(j + 1) * chunk], ref[:, f + j * chunk:f + (j + 1) * chunk]], axis=1)

    for j in range(f // chunk):
        u_scr[...] = jnp.dot(h, val_gate(wup_ref, j), preferred_element_type=F32)
        cw = val_gate(cw_ref, j)
        u = (u_scr[HALO - 1:HALO - 1 + tm, :] * cw[0:1] + u_scr[HALO:HALO + tm, :] * cw[1:2]
             + u_scr[HALO + 1:HALO + 1 + tm, :] * cw[2:3] + val_gate(cb_ref, j))
        val, gate = u[:, :chunk], u[:, chunk:]
        a_scr[:, j * chunk:(j + 1) * chunk] = (gate * (1.0 / (1.0 + jnp.exp(-gate))) * val).astype(BF16)
    y = x + mod_ref[0, 5:6, :] * jnp.dot(a_scr[...], wdn_ref[...], preferred_element_type=F32)
    if final_norm:
        y = y * lax.rsqrt(jnp.mean(y * y, axis=-1, keepdims=True) + EPS) * fg_ref[...]
    y_ref[0] = y


def _mix_ffn(xs, o_x, o_c, w_out, mods, g, wup, cw, cb, wdn, final_g, *, n_ctx_tiles, x_offset, latent_only, chunk,
             final_norm):
    b, s, d = xs.shape
    tm = TOKEN_TILE
    n_tiles = s // tm
    n = o_x.shape[2]
    f = wdn.shape[0]
    first = n_ctx_tiles if latent_only else 0
    assert n_ctx_tiles == 1
    per, per_o = tm // HALO, tm // O_HALO
    lat = lambda i: jnp.maximum(i + first, n_ctx_tiles) - x_offset
    return pl.pallas_call(
        functools.partial(_mix_ffn_kernel, n_ctx_tiles=n_ctx_tiles, n_tiles=n_tiles, first_tile=first, chunk=chunk,
                          final_norm=final_norm),
        grid=(b, n_tiles - first),
        in_specs=[
            pl.BlockSpec((1, HALO, d), lambda bi, i: (bi, jnp.maximum((i + first) * per - 1, 0), 0)),
            pl.BlockSpec((1, tm, d), lambda bi, i: (bi, i + first, 0)),
            pl.BlockSpec((1, HALO, d), lambda bi, i: (bi, jnp.minimum((i + first + 1) * per, s // HALO - 1), 0)),
            pl.BlockSpec((1, O_HALO, n), lambda bi, i: (bi, jnp.maximum(lat(i) * per_o - 1, 0), 0)),
            pl.BlockSpec((1, tm, n), lambda bi, i: (bi, lat(i), 0)),
            pl.BlockSpec((1, tm, n), lambda bi, i: (bi, jnp.minimum(i + first, n_ctx_tiles - 1), 0)),
            pl.BlockSpec((1, O_HALO, n),
                         lambda bi, i: (bi, jnp.minimum((lat(i) + 1) * per_o, o_x.shape[1] // O_HALO - 1), 0)),
            _resident(w_out.shape),
            pl.BlockSpec((1, 6, d), lambda bi, i: (jnp.where(i + first < n_ctx_tiles, b, bi), 0, 0)),
            _resident((1, d)),
            pl.BlockSpec(wup.shape, lambda bi, i: (0, 0), pipeline_mode=pl.Buffered(1)),
            _resident(cw.shape), _resident(cb.shape),
            pl.BlockSpec(wdn.shape, lambda bi, i: (0, 0), pipeline_mode=pl.Buffered(1)),
            _resident((1, d)),
        ],
        out_specs=pl.BlockSpec((1, tm, d), lambda bi, i: (bi, i, 0)),
        out_shape=jax.ShapeDtypeStruct((b, s - first * tm, d), F32),
        scratch_shapes=[pltpu.VMEM((tm + 2 * HALO, 2 * chunk), F32), pltpu.VMEM((tm, f), BF16)],
        compiler_params=_cparams(2),
        name="mix_ffn",
    )(xs, xs, xs, o_x, o_x, o_c, o_x, w_out, mods, g, wup, cw, cb, wdn, final_g)


def _rope_table(t, n_ctx, rot_dim, lane_lo):
    tt = jnp.arange(t)
    row = (tt // GRID_W).astype(F32)
    col = (tt % GRID_W).astype(F32)
    half = rot_dim // 2
    inv = ROPE_BASE ** (-jnp.arange(0, half, 2, dtype=F32) / half)
    ar, ac = row[:, None] * inv, col[:, None] * inv
    ang = jnp.concatenate([ar, ar, ac, ac], axis=-1)
    quarter = rot_dim // 4
    sign = jnp.where((jnp.arange(rot_dim) % (2 * quarter)) < quarter, -1.0, 1.0).astype(F32)
    cos, sin = jnp.cos(ang), jnp.sin(ang) * sign
    reps = (LANES - lane_lo) // rot_dim if lane_lo == 0 else 1
    cos_l = jnp.ones((t, LANES), F32).at[:, lane_lo:lane_lo + reps * rot_dim].set(jnp.tile(cos, (1, reps)))
    sin_l = jnp.zeros((t, LANES), F32).at[:, lane_lo:lane_lo + reps * rot_dim].set(jnp.tile(sin, (1, reps)))
    cos_s = jnp.concatenate([jnp.ones((n_ctx, LANES), F32), cos_l], axis=0)
    sin_s = jnp.concatenate([jnp.zeros((n_ctx, LANES), F32), sin_l], axis=0)
    return cos_s, sin_s


def _to_slots(w, n_heads, width, halves):
    d = w.shape[0]
    wh = w.reshape(d, n_heads, width)
    lo = jnp.pad(wh, ((0, 0), (0, 0), (0, LANES - width)))
    if not any(halves):
        return lo.reshape(d, n_heads * LANES)
    hi = jnp.pad(wh, ((0, 0), (0, 0), (LANES // 2, LANES // 2 - width)))
    in_hi = jnp.asarray(halves, jnp.int32).reshape(1, n_heads, 1) == 1
    return jnp.where(in_hi, hi, lo).reshape(d, n_heads * LANES)


def _vt_weights(wv, groups, width, tm):
    d = wv.shape[0]
    rows = width + O_HALO
    wt = jnp.zeros((groups, rows, d), wv.dtype).at[:, :width, :].set(wv.T.reshape(groups, width, d))
    ones = jnp.zeros((groups, rows, tm), F32).at[:, width, :].set(1.0)
    return wt.reshape(groups * rows, d), ones.reshape(groups * rows, tm)


def _dup(g):
    return jnp.concatenate([g, g]).reshape(1, LANES).astype(F32)


def kernel(x, c, ctx, c_ctx, ada_w, ada_b, norm1_g, norm2_g, ffn_w_up, ffn_conv_w, ffn_conv_b, ffn_w_down,
           gqa_w_in, gqa_q_norm_g, gqa_k_norm_g, gqa_w_out,
           mla_w_in, mla_q_norm_g, mla_kv_norm_g, mla_w_uq, mla_w_ukv, mla_w_out,
           diff_w_in, diff_lambda, diff_subln_g, diff_w_out,
           na_w_in, na_rpb, na_w_out, final_norm_g):
    b, t, d = x.shape
    n_ctx = ctx.shape[1]
    depth = ada_w.shape[0]
    assert n_ctx % TOKEN_TILE == 0 and t % TOKEN_TILE == 0 and b + 1 <= 8
    nct = n_ctx // TOKEN_TILE
    f = ffn_w_down.shape[1]
    chunk = 2 * LANES
    assert f % chunk == 0

    xs = jnp.concatenate([ctx, x], axis=1)
    cin = jnp.zeros((8, d), F32).at[:b].set(c).at[b].set(c_ctx)
    mods_all = _ada_all(cin, ada_w, ada_b).reshape(depth, 8, 6, d)

    tm = TOKEN_TILE
    ones_l = jnp.ones((1, LANES), F32)
    dummy_tab = jnp.zeros((xs.shape[1], LANES), F32)
    dummy_ones = jnp.zeros((8, LANES), F32)
    cos64, sin64 = _rope_table(t, n_ctx, GQA_HEAD_DIM, 0)
    cos_mla, sin_mla = _rope_table(t, n_ctx, MLA_ROPE_DIM, MLA_NOPE_DIM)
    no_lam, no_g = jnp.zeros((4, DIFF_HEAD_DIM), F32), jnp.ones((1, LANES), F32)
    pair_rows = MLA_V_DIM + O_HALO
    diff_rows = 2 * DIFF_HEAD_DIM + O_HALO

    for i in range(depth):
        m, j = i % N_MIXERS, i // N_MIXERS
        mods = mods_all[i]
        g1 = norm1_g[i].reshape(1, d)
        if m == 0:
            w = gqa_w_in[j].astype(BF16)
            nqc, nkc = GQA_HEADS * GQA_HEAD_DIM, GQA_KV_HEADS * GQA_HEAD_DIM
            wk = w[:, nqc:nqc + nkc].reshape(d, GQA_KV_HEADS, 1, GQA_HEAD_DIM)
            wk = jnp.broadcast_to(wk, (d, GQA_KV_HEADS, 2, GQA_HEAD_DIM)).reshape(d, GQA_KV_HEADS * LANES)
            wvt, vones = _vt_weights(w[:, nqc + nkc:], GQA_KV_HEADS, GQA_HEAD_DIM, tm)
            qc = GQA_HEAD_DIM ** -0.5 * LOG2E
            q, k, vt = _project(xs, mods, g1, w[:, :nqc], wk, wvt, _dup(gqa_q_norm_g[j]), _dup(gqa_k_norm_g[j]),
                                cos64 * qc, sin64 * qc, cos64, sin64, vones, qk_norm=True, rope=True,
                                head_dim=GQA_HEAD_DIM, k_norm_dim=2 * GQA_HEAD_DIM, v_transposed=True, q_rotated=True, n_ctx_tiles=nct,
                                name="proj_gqa")
            o_x, o_c = _flash(q, k, vt, no_lam, no_g, mode="pair", kw=LANES, k_cols=((0, LANES), (0, LANES)),
                       v_block_rows=pair_rows, v_rows=((0, pair_rows), (0, pair_rows)), vw=GQA_HEAD_DIM,
                       k_index=lambda p: p // 2, v_index=lambda p: p // 2, n_ctx=n_ctx, diff_scale=1.0,
                       name="attn_gqa")
            w_out = gqa_w_out[j]
        elif m == 1:
            w = mla_w_in[j].astype(BF16)
            nlat = MLA_Q_LORA + MLA_KV_LORA
            kr_slot = jnp.zeros((d, LANES), BF16).at[:, MLA_NOPE_DIM:MLA_NOPE_DIM + MLA_ROPE_DIM].set(w[:, nlat:])
            win = jnp.concatenate([w[:, :nlat], kr_slot], axis=1)
            qk_dim = MLA_NOPE_DIM + MLA_ROPE_DIM
            wuq = _to_slots(mla_w_uq[j].astype(BF16), MLA_HEADS, qk_dim, [0] * MLA_HEADS)
            wukv = mla_w_ukv[j].astype(BF16).reshape(MLA_KV_LORA, MLA_HEADS, MLA_NOPE_DIM + MLA_V_DIM)
            wuk = _to_slots(wukv[:, :, :MLA_NOPE_DIM].reshape(MLA_KV_LORA, -1), MLA_HEADS, MLA_NOPE_DIM,
                            [0] * MLA_HEADS)
            wuvt, vones = _vt_weights(wukv[:, :, MLA_NOPE_DIM:].reshape(MLA_KV_LORA, MLA_HEADS * MLA_V_DIM),
                                      MLA_HEADS, MLA_V_DIM, tm)
            qc = qk_dim ** -0.5 * LOG2E
            q, k, vt = _project_mla(xs, mods, g1, win, mla_q_norm_g[j].reshape(1, -1),
                                    mla_kv_norm_g[j].reshape(1, -1), wuq, wuk, wuvt,
                                    cos_mla * qc, sin_mla * qc, cos_mla, sin_mla, vones, n_ctx_tiles=nct)
            o_x, o_c = _flash(q, k, vt, no_lam, no_g, mode="pair", kw=2 * LANES, k_cols=((0, LANES), (LANES, 2 * LANES)),
                       v_block_rows=2 * pair_rows, v_rows=((0, pair_rows), (pair_rows, 2 * pair_rows)),
                       vw=MLA_V_DIM, k_index=lambda p: p, v_index=lambda p: p, n_ctx=n_ctx, diff_scale=1.0,
                       name="attn_mla")
            w_out = mla_w_out[j]
        elif m == 2:
            w = diff_w_in[j].astype(BF16)
            wvt, vones = _vt_weights(w[:, 2 * d:], DIFF_HEADS, 2 * DIFF_HEAD_DIM, tm)
            lambda_init = 0.8 - 0.6 * math.exp(-0.3 * i)
            qc = DIFF_HEAD_DIM ** -0.5 * LOG2E
            q, k, vt = _project(xs, mods, g1, w[:, :d], w[:, d:2 * d], wvt, ones_l, ones_l,
                                cos64 * qc, sin64 * qc, cos64, sin64, vones, qk_norm=False, rope=True,
                                head_dim=DIFF_HEAD_DIM, k_norm_dim=DIFF_HEAD_DIM, v_transposed=True, q_rotated=True, n_ctx_tiles=nct,
                                name="proj_diff")
            o_x, o_c = _flash(q, k, vt, diff_lambda[j], diff_subln_g[j].reshape(1, LANES), mode="diff", kw=LANES,
                       k_cols=((0, LANES), (0, LANES)), v_block_rows=diff_rows,
                       v_rows=((0, diff_rows), (0, diff_rows)), vw=2 * DIFF_HEAD_DIM,
                       k_index=lambda p: p, v_index=lambda p: p, n_ctx=n_ctx, diff_scale=1.0 - lambda_init,
                       name="attn_diff")
            w_out = diff_w_out[j]
        else:
            w = na_w_in[j].astype(BF16)
            na_scale = NA_HEAD_DIM ** -0.5
            assert math.frexp(na_scale)[0] == 0.5
            q, k, v = _project(xs, mods, g1, w[:, :d] * jnp.asarray(na_scale, BF16), w[:, d:2 * d], w[:, 2 * d:],
                               ones_l, ones_l, dummy_tab, dummy_tab, dummy_tab, dummy_tab, dummy_ones,
                               qk_norm=False, rope=False, head_dim=NA_HEAD_DIM, k_norm_dim=NA_HEAD_DIM,
                               v_transposed=False, q_rotated=False, n_ctx_tiles=nct,
                               name="proj_na")
            idx_all, var, wstart = _na_tables(t)
            bias = _na_bias(idx_all, na_rpb[j])
            o_x = o_c = _na_attention(q, k, v, bias, jnp.asarray(var), jnp.asarray(wstart), n_ctx=n_ctx)
            w_out = na_w_out[j]
        last = i == depth - 1

        xs = _mix_ffn(xs, o_x, o_c, w_out.astype(BF16), mods, norm2_g[i].reshape(1, d),
                      ffn_w_up[i].astype(BF16), ffn_conv_w[i], ffn_conv_b[i].reshape(1, 2 * f),
                      ffn_w_down[i].astype(BF16), final_norm_g.reshape(1, d), n_ctx_tiles=nct,
                      x_offset=0 if m == 3 else nct, latent_only=last, chunk=chunk, final_norm=last)
    return xs
```

```python
import functools
import math

import numpy as np
import jax
import jax.numpy as jnp
from jax import lax
from jax.experimental import pallas as pl
from jax.experimental.pallas import tpu as pltpu

F32 = jnp.float32
BF16 = jnp.bfloat16

LANES = 128
TOKEN_TILE = 256
HALO = 8
KV_CHUNK = 256
SUB_KEYS = 256
FLASH_TQ = 512
LOOP_CHUNKS = 8
VMEM_LIMIT = 56 * 1024 * 1024

GRID_W = 64
ROPE_BASE = 10000.0
EPS = 1e-6
NEG_INF = -1e30
LOG2E = 1.4426950408889634

GQA_HEADS, GQA_KV_HEADS, GQA_HEAD_DIM = 16, 4, 64
MLA_HEADS, MLA_NOPE_DIM, MLA_ROPE_DIM, MLA_V_DIM = 16, 64, 32, 64
MLA_Q_LORA, MLA_KV_LORA = 384, 256
DIFF_HEADS, DIFF_HEAD_DIM = 8, 64
NA_HEADS, NA_HEAD_DIM, NA_ROWS, NA_COLS = 16, 64, 8, 16
NA_QBLOCK = 128
NA_BIAS_TILE = 2048
NA_WIN_ROWS = 10
N_MIXERS = 4

NT_DIMS = (((1,), (1,)), ((), ()))


def _cparams(n_axes):
    return pltpu.CompilerParams(dimension_semantics=("arbitrary",) * n_axes, vmem_limit_bytes=VMEM_LIMIT)


def _resident(shape):
    nd = len(shape)
    return pl.BlockSpec(shape, lambda *_: (0,) * nd)


def _ada_kernel(c_ref, w_ref, b_ref, o_ref):
    c = c_ref[...]
    s = c * (1.0 / (1.0 + jnp.exp(-c)))
    o_ref[0] = jnp.dot(s.astype(BF16), w_ref[0].astype(BF16), preferred_element_type=F32) + b_ref[0]


def _ada_all(cin, ada_w, ada_b):
    depth, d, n = ada_w.shape
    tn = n // 4
    return pl.pallas_call(
        _ada_kernel,
        grid=(depth, n // tn),
        in_specs=[
            pl.BlockSpec((8, d), lambda l, j: (0, 0)),
            pl.BlockSpec((1, d, tn), lambda l, j: (l, 0, j)),
            pl.BlockSpec((1, 1, tn), lambda l, j: (l, 0, j)),
        ],
        out_specs=pl.BlockSpec((1, 8, tn), lambda l, j: (l, 0, j)),
        out_shape=jax.ShapeDtypeStruct((depth, 8, n), F32),
        compiler_params=_cparams(2),
        name="ada",
    )(cin, ada_w, ada_b.reshape(depth, 1, n))


def _norm_mod(x, g, shift, scale):
    ms = jnp.mean(x * x, axis=-1, keepdims=True)
    return (x * lax.rsqrt(ms + EPS) * g) * (1.0 + scale) + shift


def _rope_slot(y, cos, sin_signed, lo_mask, quarter):
    fwd = pltpu.roll(y, LANES - quarter, axis=1)
    bwd = pltpu.roll(y, quarter, axis=1)
    return y * cos + jnp.where(lo_mask, fwd, bwd) * sin_signed


def _lo_mask(shape, quarter):
    lane = lax.broadcasted_iota(jnp.int32, shape, 1)
    return (lane % (2 * quarter)) < quarter


def _q_spec(n, tm, n_tiles, n_ctx_tiles, rotate):
    if not rotate:
        return pl.BlockSpec((1, tm, n), lambda bi, i: (bi, i, 0))
    n_lat = n_tiles - n_ctx_tiles
    return pl.BlockSpec((1, tm, n), lambda bi, i: (bi, jnp.where(i < n_ctx_tiles, i + n_lat, i - n_ctx_tiles), 0))


def _vt_spec(rows, tm):
    per = KV_CHUNK // tm
    return pl.BlockSpec((1, 1, rows, tm), lambda bi, i: (bi, i // per, 0, i % per))


def _proj_kernel(x_ref, mod_ref, g_ref, wq_ref, wk_ref, wv_ref, qg_ref, kg_ref, cosq_ref, sinq_ref, cosk_ref, sink_ref,
                 vones_ref, q_ref, k_ref, v_ref, *, qk_norm, rope, head_dim, k_norm_dim, v_transposed):
    h = _norm_mod(x_ref[0], g_ref[...], mod_ref[0, 0:1, :], mod_ref[0, 1:2, :]).astype(BF16)
    tm = h.shape[0]
    lo = _lo_mask((tm, LANES), head_dim // 4) if rope else None
    low_half = lax.broadcasted_iota(jnp.int32, (tm, LANES), 1) < LANES // 2

    def chunks(w_ref):
        n = w_ref.shape[1]
        for c0 in range(0, n, 2 * LANES):
            c1 = min(c0 + 2 * LANES, n)
            y2 = jnp.dot(h, w_ref[:, c0:c1], preferred_element_type=F32)
            for s0 in range(0, c1 - c0, LANES):
                yield (c0 + s0) // LANES, y2[:, s0:s0 + LANES]

    for c, y in chunks(wq_ref):
        if qk_norm:
            sq = y * y
            ms_lo = jnp.sum(jnp.where(low_half, sq, 0.0), axis=-1, keepdims=True) * (1.0 / head_dim)
            ms_hi = jnp.sum(jnp.where(low_half, 0.0, sq), axis=-1, keepdims=True) * (1.0 / head_dim)
            y = y * jnp.where(low_half, lax.rsqrt(ms_lo + EPS), lax.rsqrt(ms_hi + EPS)) * qg_ref[...]
        if rope:
            y = _rope_slot(y, cosq_ref[...], sinq_ref[...], lo, head_dim // 4)
        q_ref[0, :, 2 * c * LANES:(2 * c + 1) * LANES] = jnp.where(low_half, y, 0.0).astype(BF16)
        q_ref[0, :, (2 * c + 1) * LANES:(2 * c + 2) * LANES] = jnp.where(low_half, 0.0, y).astype(BF16)
    for c, y in chunks(wk_ref):
        if qk_norm:
            ms = jnp.sum(y * y, axis=-1, keepdims=True) * (1.0 / k_norm_dim)
            y = y * lax.rsqrt(ms + EPS) * kg_ref[...]
        if rope:
            y = _rope_slot(y, cosk_ref[...], sink_ref[...], lo, head_dim // 4)
        k_ref[0, :, c * LANES:(c + 1) * LANES] = y.astype(BF16)
    if v_transposed:
        vt = lax.dot_general(wv_ref[...], h, NT_DIMS, preferred_element_type=F32) + vones_ref[...]
        v_ref[0, 0] = vt.astype(BF16)
    else:
        v_ref[0] = jnp.dot(h, wv_ref[...], preferred_element_type=F32).astype(BF16)


def _project(xs, mods, g, wq, wk, wv, qg, kg, cosq, sinq, cosk, sink, vones, *, qk_norm, rope, head_dim, k_norm_dim,
             v_transposed, q_rotated, n_ctx_tiles, name):
    b, s, d = xs.shape
    tm = TOKEN_TILE
    nq, nk = 2 * wq.shape[1], wk.shape[1]
    tok = lambda n: pl.BlockSpec((1, tm, n), lambda bi, i: (bi, i, 0))
    tab = pl.BlockSpec((tm, LANES), lambda bi, i: (i, 0))
    if v_transposed:
        v_spec = _vt_spec(wv.shape[0], tm)
        v_shape = jax.ShapeDtypeStruct((b, s // KV_CHUNK, wv.shape[0], KV_CHUNK), BF16)
    else:
        v_spec, v_shape = tok(wv.shape[1]), jax.ShapeDtypeStruct((b, s, wv.shape[1]), BF16)
    return pl.pallas_call(
        functools.partial(_proj_kernel, qk_norm=qk_norm, rope=rope, head_dim=head_dim, k_norm_dim=k_norm_dim,
                          v_transposed=v_transposed),
        grid=(b, s // tm),
        in_specs=[
            tok(d),
            pl.BlockSpec((1, 6, d), lambda bi, i: (jnp.where(i < n_ctx_tiles, b, bi), 0, 0)),
            _resident((1, d)),
            _resident(wq.shape), _resident(wk.shape), _resident(wv.shape),
            _resident((1, LANES)), _resident((1, LANES)),
            tab, tab, tab, tab,
            _resident(vones.shape),
        ],
        out_specs=[_q_spec(nq, tm, s // tm, n_ctx_tiles, q_rotated), tok(nk), v_spec],
        out_shape=[jax.ShapeDtypeStruct((b, s, nq), BF16), jax.ShapeDtypeStruct((b, s, nk), BF16), v_shape],
        compiler_params=_cparams(2),
        name=name,
    )(xs, mods, g, wq, wk, wv, qg, kg, cosq, sinq, cosk, sink, vones)


def _mla_proj_kernel(x_ref, mod_ref, g_ref, win_ref, qn_ref, kvn_ref, wuq_ref, wuk_ref, wuvt_ref,
                     cosq_ref, sinq_ref, cosk_ref, sink_ref, vones_ref, q_ref, k_ref, v_ref):
    h = _norm_mod(x_ref[0], g_ref[...], mod_ref[0, 0:1, :], mod_ref[0, 1:2, :]).astype(BF16)
    tm = h.shape[0]
    t1 = jnp.dot(h, win_ref[...], preferred_element_type=F32)
    cq = t1[:, :MLA_Q_LORA]
    ckv = t1[:, MLA_Q_LORA:MLA_Q_LORA + MLA_KV_LORA]
    kr = t1[:, MLA_Q_LORA + MLA_KV_LORA:]
    cqn = (cq * lax.rsqrt(jnp.mean(cq * cq, axis=-1, keepdims=True) + EPS) * qn_ref[...]).astype(BF16)
    ckvn = (ckv * lax.rsqrt(jnp.mean(ckv * ckv, axis=-1, keepdims=True) + EPS) * kvn_ref[...]).astype(BF16)
    quarter = MLA_ROPE_DIM // 4
    lo = _lo_mask((tm, LANES), quarter)
    kr = _rope_slot(kr, cosk_ref[...], sink_ref[...], lo, quarter)
    n = wuq_ref.shape[1]
    for c0 in range(0, n, 2 * LANES):
        q2 = jnp.dot(cqn, wuq_ref[:, c0:c0 + 2 * LANES], preferred_element_type=F32)
        k2 = jnp.dot(ckvn, wuk_ref[:, c0:c0 + 2 * LANES], preferred_element_type=F32)
        for s0 in (0, LANES):
            q_ref[0, :, c0 + s0:c0 + s0 + LANES] = _rope_slot(
                q2[:, s0:s0 + LANES], cosq_ref[...], sinq_ref[...], lo, quarter).astype(BF16)
            k_ref[0, :, c0 + s0:c0 + s0 + LANES] = (k2[:, s0:s0 + LANES] + kr).astype(BF16)
    vt = lax.dot_general(wuvt_ref[...], ckvn, NT_DIMS, preferred_element_type=F32) + vones_ref[...]
    v_ref[0, 0] = vt.astype(BF16)


def _project_mla(xs, mods, g, win, qn, kvn, wuq, wuk, wuvt, cosq, sinq, cosk, sink, vones, *, n_ctx_tiles):
    b, s, d = xs.shape
    tm = TOKEN_TILE
    nq, nk = wuq.shape[1], wuk.shape[1]
    tok = lambda n: pl.BlockSpec((1, tm, n), lambda bi, i: (bi, i, 0))
    tab = pl.BlockSpec((tm, LANES), lambda bi, i: (i, 0))
    return pl.pallas_call(
        _mla_proj_kernel,
        grid=(b, s // tm),
        in_specs=[
            tok(d),
            pl.BlockSpec((1, 6, d), lambda bi, i: (jnp.where(i < n_ctx_tiles, b, bi), 0, 0)),
            _resident((1, d)),
            _resident(win.shape), _resident(qn.shape), _resident(kvn.shape),
            _resident(wuq.shape), _resident(wuk.shape), _resident(wuvt.shape),
            tab, tab, tab, tab,
            _resident(vones.shape),
        ],
        out_specs=[_q_spec(nq, tm, s // tm, n_ctx_tiles, True), tok(nk), _vt_spec(wuvt.shape[0], tm)],
        out_shape=[jax.ShapeDtypeStruct((b, s, nq), BF16), jax.ShapeDtypeStruct((b, s, nk), BF16),
                   jax.ShapeDtypeStruct((b, s // KV_CHUNK, wuvt.shape[0], KV_CHUNK), BF16)],
        compiler_params=_cparams(2),
        name="proj_mla",
    )(xs, mods, g, win, qn, kvn, wuq, wuk, wuvt, cosq, sinq, cosk, sink, vones)


def _flash_kernel(q_ref, k_ref, vt_ref, lam_ref, subg_ref, o_ref, *scratch, mode, k_cols, v_rows, vw, diff_scale,
                  ctx_keys):
    tq = q_ref.shape[1]
    n_chunks = vt_ref.shape[1]
    qs = [q_ref[0, :, a * LANES:(a + 1) * LANES] for a in (0, 1)]
    n_rows = v_rows[0][1] - v_rows[0][0]

    def keys(key_rows):
        k0 = k_ref[0, key_rows, k_cols[0][0]:k_cols[0][1]]
        k1 = k0 if k_cols[1] == k_cols[0] else k_ref[0, key_rows, k_cols[1][0]:k_cols[1][1]]
        return k0, k1

    def values(chunk_idx, key_lanes):
        v0 = vt_ref[0, chunk_idx, v_rows[0][0]:v_rows[0][1], key_lanes]
        v1 = v0 if v_rows[1] == v_rows[0] else vt_ref[0, chunk_idx, v_rows[1][0]:v_rows[1][1], key_lanes]
        return v0, v1

    def update(st, cmax, vt_a, carry_a):
        m, acc = carry_a
        m_new = jnp.maximum(m, cmax)
        p = jnp.exp2(st - m_new).astype(BF16)
        return m_new, jnp.exp2(m - m_new) * acc + jnp.dot(vt_a, p, preferred_element_type=F32)

    init = tuple((jnp.full((1, tq), -jnp.inf, F32), jnp.zeros((n_rows, tq), F32)) for _ in (0, 1))

    def finish(carry):
        (_, acc_a), (_, acc_b) = carry
        oa = acc_a[0:vw] / acc_a[vw:vw + 1]
        ob = acc_b[0:vw] / acc_b[vw:vw + 1]
        if mode == "pair":
            out = jnp.concatenate([oa, ob], axis=0).T
        else:
            lam = lam_ref[...]
            lam_full = (jnp.exp(jnp.sum(lam[0:1] * lam[1:2], axis=-1, keepdims=True))
                        - jnp.exp(jnp.sum(lam[2:3] * lam[3:4], axis=-1, keepdims=True))
                        + (1.0 - diff_scale))
            o = (oa - lam_full * ob).T
            out = o * lax.rsqrt(jnp.mean(o * o, axis=-1, keepdims=True) + EPS) * subg_ref[...] * diff_scale
        o_ref[0] = out.astype(BF16)

    if ctx_keys:
        ks, vts = keys(slice(0, ctx_keys)), values(0, slice(0, ctx_keys))
        out = []
        for a in (0, 1):
            st = lax.dot_general(ks[a], qs[a], NT_DIMS, preferred_element_type=F32)
            out.append(update(st, jnp.max(st, axis=0, keepdims=True), vts[a], init[a]))
        finish(tuple(out))
        return

    (st_scr,) = scratch

    def chunk_keys(chunk_idx):
        start = chunk_idx * KV_CHUNK
        return keys(pl.ds(start if isinstance(start, int) else pl.multiple_of(start, KV_CHUNK), KV_CHUNK))

    def score(slot, a, ks):
        st = lax.dot_general(ks[a], qs[a], NT_DIMS, preferred_element_type=F32)
        st_scr[slot, a] = st
        return jnp.max(st, axis=0, keepdims=True)

    def scores(slot, chunk_idx):
        ks = chunk_keys(chunk_idx)
        return tuple(score(slot, a, ks) for a in (0, 1))

    def run(first, count, cmax, carry):
        for u in range(count):
            nxt, slot = first + u + 1, u % 2
            has_next = isinstance(nxt, jax.Array) or nxt < n_chunks
            m_new = [jnp.maximum(carry[a][0], cmax[a]) for a in (0, 1)]
            acc = [jnp.exp2(carry[a][0] - m_new[a]) * carry[a][1] for a in (0, 1)]
            cmax_next = [None, None]
            for j in range(KV_CHUNK // SUB_KEYS):
                rows = slice(j * SUB_KEYS, (j + 1) * SUB_KEYS)
                if has_next:
                    start = nxt * KV_CHUNK + j * SUB_KEYS
                    ks = keys(pl.ds(start if isinstance(start, int) else pl.multiple_of(start, SUB_KEYS), SUB_KEYS))
                    for a in (0, 1):
                        st = lax.dot_general(ks[a], qs[a], NT_DIMS, preferred_element_type=F32)
                        st_scr[1 - slot, a, rows, :] = st
                        mx = jnp.max(st, axis=0, keepdims=True)
                        cmax_next[a] = mx if cmax_next[a] is None else jnp.maximum(cmax_next[a], mx)
                vts = values(first + u, rows)
                for a in (0, 1):
                    p = jnp.exp2(st_scr[slot, a, rows, :] - m_new[a]).astype(BF16)
                    acc[a] = acc[a] + jnp.dot(vts[a], p, preferred_element_type=F32)
            carry = tuple((m_new[a], acc[a]) for a in (0, 1))
            cmax = tuple(cmax_next)
        return cmax, carry

    n_loop = (n_chunks - 1) // LOOP_CHUNKS
    cmax, carry = lax.fori_loop(0, n_loop, lambda i, st: run(i * LOOP_CHUNKS, LOOP_CHUNKS, *st),
                                (scores(0, 0), init))
    _, carry = run(n_loop * LOOP_CHUNKS, n_chunks - n_loop * LOOP_CHUNKS, cmax, carry)
    finish(carry)


def _flash(q, k, vt, lam, subg, *, mode, kw, k_cols, v_block_rows, v_rows, vw, k_index, v_index, n_ctx, diff_scale,
           name):
    b, s, nq = q.shape
    t = s - n_ctx
    n_pairs = nq // (2 * LANES)
    n_chunks = s // KV_CHUNK
    assert s % KV_CHUNK == 0 and n_ctx <= KV_CHUNK and t % FLASH_TQ == 0 and t % n_ctx == 0
    assert vt.shape[1] == n_chunks
    common = dict(mode=mode, k_cols=k_cols, v_rows=v_rows, vw=vw, diff_scale=diff_scale)
    o_x = pl.pallas_call(
        functools.partial(_flash_kernel, ctx_keys=0, **common),
        grid=(b, n_pairs, t // FLASH_TQ),
        in_specs=[
            pl.BlockSpec((1, FLASH_TQ, 2 * LANES), lambda bi, p, i: (bi, i, p)),
            pl.BlockSpec((1, s, kw), lambda bi, p, i: (bi, 0, k_index(p))),
            pl.BlockSpec((1, n_chunks, v_block_rows, KV_CHUNK), lambda bi, p, i: (bi, 0, v_index(p), 0)),
            _resident(lam.shape), _resident(subg.shape),
        ],
        out_specs=pl.BlockSpec((1, FLASH_TQ, LANES), lambda bi, p, i: (bi, i, p)),
        out_shape=jax.ShapeDtypeStruct((b, t, n_pairs * LANES), BF16),
        scratch_shapes=[pltpu.VMEM((2, 2, KV_CHUNK, FLASH_TQ), F32)],
        compiler_params=_cparams(3),
        name=name,
    )(q, k, vt, lam, subg)
    o_c = pl.pallas_call(
        functools.partial(_flash_kernel, ctx_keys=n_ctx, **common),
        grid=(b, n_pairs),
        in_specs=[
            pl.BlockSpec((1, n_ctx, 2 * LANES), lambda bi, p: (bi, t // n_ctx, p)),
            pl.BlockSpec((1, n_ctx, kw), lambda bi, p: (bi, 0, k_index(p))),
            pl.BlockSpec((1, 1, v_block_rows, KV_CHUNK), lambda bi, p: (bi, 0, v_index(p), 0)),
            _resident(lam.shape), _resident(subg.shape),
        ],
        out_specs=pl.BlockSpec((1, n_ctx, LANES), lambda bi, p: (bi, 0, p)),
        out_shape=jax.ShapeDtypeStruct((b, n_ctx, n_pairs * LANES), BF16),
        compiler_params=_cparams(2),
        name=name + "_ctx",
    )(q, k, vt, lam, subg)
    return o_x, o_c


def _na_tables(t):
    rows = t // GRID_W
    win_rows = min(NA_ROWS, rows)
    band_rows = min(win_rows + 1, rows)
    assert rows >= NA_WIN_ROWS and NA_QBLOCK % GRID_W == 0
    qrows = NA_QBLOCK // GRID_W
    nb = t // NA_QBLOCK
    nwin = NA_WIN_ROWS * GRID_W
    q_off, k_off = np.arange(NA_QBLOCK), np.arange(nwin)
    q_dr, q_col = q_off // GRID_W, q_off % GRID_W
    k_dr, k_col = k_off // GRID_W, k_off % GRID_W
    col_start = np.clip(q_col - NA_COLS // 2, 0, GRID_W - NA_COLS)
    col_in = (k_col[None, :] >= col_start[:, None]) & (k_col[None, :] < col_start[:, None] + NA_COLS)
    dc_idx = np.clip(k_col[None, :] - q_col[:, None] + NA_COLS - 1, 0, 2 * NA_COLS - 2)
    patterns, var, wstart = {}, [], []
    for j in range(nb):
        r = j * qrows + q_dr
        row_start = np.clip(r - win_rows // 2, 0, rows - win_rows)
        b0 = min(int(row_start[0]), rows - band_rows)
        w0 = min(b0 - b0 % 2, rows - NA_WIN_ROWS)
        assert w0 % qrows == 0 and w0 <= b0 and b0 + band_rows <= w0 + NA_WIN_ROWS
        k_row = w0 + k_dr
        in_band = (k_row >= b0) & (k_row < b0 + band_rows)
        in_win = (col_in & in_band[None, :] & (k_row[None, :] >= row_start[:, None])
                  & (k_row[None, :] < row_start[:, None] + win_rows))
        dr_idx = np.clip(k_row[None, :] - r[:, None] + NA_ROWS - 1, 0, 2 * NA_ROWS - 2)
        idx = np.where(in_win, dr_idx * (2 * NA_COLS - 1) + dc_idx, -1).astype(np.int32)
        key = idx.tobytes()
        if key not in patterns:
            patterns[key] = (len(patterns), idx)
        var.append(patterns[key][0])
        wstart.append(w0 * GRID_W)
    idx_all = np.stack([p[1] for p in sorted(patterns.values(), key=lambda p: p[0])])
    return idx_all.reshape(len(patterns), 1, -1), np.asarray(var, np.int32), np.asarray(wstart, np.int32)


def _na_bias_kernel(idx_ref, rpb_ref, o_ref):
    idx = idx_ref[0]
    r = rpb_ref[...]
    onehot = (lax.broadcasted_iota(jnp.int32, (r.shape[1], idx.shape[1]), 0) == idx).astype(BF16)
    b1 = r.astype(BF16)
    r2 = r - b1.astype(F32)
    b2 = r2.astype(BF16)
    b3 = (r2 - b2.astype(F32)).astype(BF16)
    heads = r.shape[0]
    parts = jnp.dot(jnp.concatenate([b1, b2, b3], axis=0), onehot, preferred_element_type=F32)
    out = parts[:heads] + parts[heads:2 * heads] + parts[2 * heads:]
    o_ref[0] = jnp.where(idx >= 0, out, NEG_INF)


def _na_bias(idx_all, rpb):
    nvar, _, npos = idx_all.shape
    heads = rpb.shape[0]
    nr = (2 * NA_ROWS - 1) * (2 * NA_COLS - 1)
    nr_pad = pl.cdiv(nr, LANES) * LANES
    rflat = jnp.pad(rpb.reshape(heads, nr), ((0, 0), (0, nr_pad - nr)))
    tn = NA_BIAS_TILE
    assert npos % tn == 0
    out = pl.pallas_call(
        _na_bias_kernel,
        grid=(nvar, npos // tn),
        in_specs=[pl.BlockSpec((1, 1, tn), lambda v, j: (v, 0, j)), _resident(rflat.shape)],
        out_specs=pl.BlockSpec((1, heads, tn), lambda v, j: (v, 0, j)),
        out_shape=jax.ShapeDtypeStruct((nvar, heads, npos), F32),
        compiler_params=_cparams(2),
        name="na_bias",
    )(jnp.asarray(idx_all), rflat)
    return out.reshape(nvar, heads, NA_QBLOCK, npos // NA_QBLOCK)


def _halves(lo_src, hi_src):
    lane = lax.broadcasted_iota(jnp.int32, lo_src.shape, 1)
    return jnp.where(lane < LANES // 2, lo_src, hi_src)


def _na_kernel(var_ref, ws_ref, q_ref, k_ref, v_ref, b0_ref, b1_ref, o_ref, *, n_ctx, n_ctx_tiles):
    qi = pl.program_id(2)
    nwin = NA_WIN_ROWS * GRID_W

    @pl.when(qi < n_ctx_tiles)
    def _():
        kc, vc = k_ref[0, 0:n_ctx, :], v_ref[0, 0:n_ctx, :]
        outs = []
        for a in (0, 1):
            s_c = lax.dot_general(q_ref[0, :, a * LANES:(a + 1) * LANES], kc, NT_DIMS,
                                  preferred_element_type=F32)
            p_c = jnp.exp(s_c - jnp.max(s_c, axis=-1, keepdims=True))
            acc = jnp.dot(p_c.astype(BF16), vc, preferred_element_type=F32)
            outs.append(acc / jnp.sum(p_c, axis=-1, keepdims=True))
        o_ref[0] = _halves(outs[0], outs[1]).astype(BF16)

    @pl.when(qi >= n_ctx_tiles)
    def _():
        kc, vc = k_ref[0, 0:n_ctx, :], v_ref[0, 0:n_ctx, :]
        work = []
        for sub, bias_ref in ((0, b0_ref), (1, b1_ref)):
            j = (qi - n_ctx_tiles) * (TOKEN_TILE // NA_QBLOCK) + sub
            start = pl.multiple_of(n_ctx + ws_ref[j], NA_QBLOCK)
            kw, vw = k_ref[0, pl.ds(start, nwin), :], v_ref[0, pl.ds(start, nwin), :]
            for a in (0, 1):
                q = q_ref[0, sub * NA_QBLOCK:(sub + 1) * NA_QBLOCK, a * LANES:(a + 1) * LANES]
                s_c = lax.dot_general(q, kc, NT_DIMS, preferred_element_type=F32)
                s_w = lax.dot_general(q, kw, NT_DIMS, preferred_element_type=F32) + bias_ref[0, a]
                work.append((s_c, s_w, vw))
        outs = []
        for s_c, s_w, vw in work:
            m = jnp.maximum(jnp.max(s_c, axis=-1, keepdims=True), jnp.max(s_w, axis=-1, keepdims=True))
            p_c, p_w = jnp.exp(s_c - m), jnp.exp(s_w - m)
            l = jnp.sum(p_c, axis=-1, keepdims=True) + jnp.sum(p_w, axis=-1, keepdims=True)
            acc = (jnp.dot(p_c.astype(BF16), vc, preferred_element_type=F32)
                   + jnp.dot(p_w.astype(BF16), vw, preferred_element_type=F32))
            outs.append(acc / l)
        for sub in (0, 1):
            o_ref[0, sub * NA_QBLOCK:(sub + 1) * NA_QBLOCK, :] = _halves(outs[2 * sub], outs[2 * sub + 1]).astype(BF16)


def _na_attention(q, k, v, bias, var, wstart, *, n_ctx):
    b, s, nq = q.shape
    tq = TOKEN_TILE
    n_pairs = nq // (2 * LANES)
    n_ctx_tiles = n_ctx // tq
    per_tile = tq // NA_QBLOCK
    nwin = NA_WIN_ROWS * GRID_W

    def bias_spec(sub):
        def index(bi, p, i, var_ref, ws_ref):
            j = jnp.maximum(i - n_ctx_tiles, 0) * per_tile + sub
            return (var_ref[j], p, 0, 0)
        return pl.BlockSpec((1, 2, NA_QBLOCK, nwin), index)

    grid_spec = pltpu.PrefetchScalarGridSpec(
        num_scalar_prefetch=2,
        grid=(b, n_pairs, s // tq),
        in_specs=[
            pl.BlockSpec((1, tq, 2 * LANES), lambda bi, p, i, *_: (bi, i, p)),
            pl.BlockSpec((1, s, LANES), lambda bi, p, i, *_: (bi, 0, p)),
            pl.BlockSpec((1, s, LANES), lambda bi, p, i, *_: (bi, 0, p)),
            bias_spec(0), bias_spec(1),
        ],
        out_specs=pl.BlockSpec((1, tq, LANES), lambda bi, p, i, *_: (bi, i, p)),
    )
    return pl.pallas_call(
        functools.partial(_na_kernel, n_ctx=n_ctx, n_ctx_tiles=n_ctx_tiles),
        grid_spec=grid_spec,
        out_shape=jax.ShapeDtypeStruct((b, s, n_pairs * LANES), BF16),
        compiler_params=_cparams(3),
        name="na_attn",
    )(var, wstart, q, k, v, bias, bias)


O_HALO = 16


def _mix_ffn_kernel(xp_ref, x_ref, xn_ref, op_ref, ox_ref, oc_ref, on_ref, wo_ref, mod_ref, g_ref, wup_ref, cw_ref,
                    cb_ref, wdn_ref, fg_ref, y_ref, u_scr, a_scr, *, n_ctx_tiles, n_tiles, first_tile, chunk,
                    final_norm):
    t = pl.program_id(1) + first_tile
    tm = x_ref.shape[1]
    has_prev = jnp.logical_and(t != 0, t != n_ctx_tiles).astype(F32)
    has_next = jnp.logical_and(t != n_ctx_tiles - 1, t != n_tiles - 1).astype(F32)
    o = jnp.where(t < n_ctx_tiles, oc_ref[0], ox_ref[0]).astype(F32)
    o_cat = jnp.concatenate([op_ref[0].astype(F32)[O_HALO - HALO:], o, on_ref[0].astype(F32)[:HALO]], axis=0)
    x_cat = jnp.concatenate([xp_ref[0], x_ref[0], xn_ref[0]], axis=0)
    x1 = x_cat + mod_ref[0, 2:3, :] * jnp.dot(o_cat.astype(BF16), wo_ref[...], preferred_element_type=F32)
    g, shift, scale = g_ref[...], mod_ref[0, 3:4, :], mod_ref[0, 4:5, :]
    x = x1[HALO:HALO + tm]
    h = jnp.concatenate([_norm_mod(x1[:HALO], g, shift, scale) * has_prev,
                         _norm_mod(x, g, shift, scale),
                         _norm_mod(x1[HALO + tm:], g, shift, scale) * has_next], axis=0).astype(BF16)
    f = wdn_ref.shape[0]

    def val_gate(ref, j):
        return jnp.concatenate([ref[:, j * chunk:(j + 1) * chunk], ref[:, f + j * chunk:f + (j + 1) * chunk]], axis=1)

    for j in range(f // chunk):
        u_scr[...] = jnp.dot(h, val_gate(wup_ref, j), preferred_element_type=F32)
        cw = val_gate(cw_ref, j)
        u = (u_scr[HALO - 1:HALO - 1 + tm, :] * cw[0:1] + u_scr[HALO:HALO + tm, :] * cw[1:2]
             + u_scr[HALO + 1:HALO + 1 + tm, :] * cw[2:3] + val_gate(cb_ref, j))
        val, gate = u[:, :chunk], u[:, chunk:]
        a_scr[:, j * chunk:(j + 1) * chunk] = (gate * (1.0 / (1.0 + jnp.exp(-gate))) * val).astype(BF16)
    y = x + mod_ref[0, 5:6, :] * jnp.dot(a_scr[...], wdn_ref[...], preferred_element_type=F32)
    if final_norm:
        y = y * lax.rsqrt(jnp.mean(y * y, axis=-1, keepdims=True) + EPS) * fg_ref[...]
    y_ref[0] = y


def _mix_ffn(xs, o_x, o_c, w_out, mods, g, wup, cw, cb, wdn, final_g, *, n_ctx_tiles, x_offset, latent_only, chunk,
             final_norm):
    b, s, d = xs.shape
    tm = TOKEN_TILE
    n_tiles = s // tm
    n = o_x.shape[2]
    f = wdn.shape[0]
    first = n_ctx_tiles if latent_only else 0
    assert n_ctx_tiles == 1
    per, per_o = tm // HALO, tm // O_HALO
    lat = lambda i: jnp.maximum(i + first, n_ctx_tiles) - x_offset
    return pl.pallas_call(
        functools.partial(_mix_ffn_kernel, n_ctx_tiles=n_ctx_tiles, n_tiles=n_tiles, first_tile=first, chunk=chunk,
                          final_norm=final_norm),
        grid=(b, n_tiles - first),
        in_specs=[
            pl.BlockSpec((1, HALO, d), lambda bi, i: (bi, jnp.maximum((i + first) * per - 1, 0), 0)),
            pl.BlockSpec((1, tm, d), lambda bi, i: (bi, i + first, 0)),
            pl.BlockSpec((1, HALO, d), lambda bi, i: (bi, jnp.minimum((i + first + 1) * per, s // HALO - 1), 0)),
            pl.BlockSpec((1, O_HALO, n), lambda bi, i: (bi, jnp.maximum(lat(i) * per_o - 1, 0), 0)),
            pl.BlockSpec((1, tm, n), lambda bi, i: (bi, lat(i), 0)),
            pl.BlockSpec((1, tm, n), lambda bi, i: (bi, jnp.minimum(i + first, n_ctx_tiles - 1), 0)),
            pl.BlockSpec((1, O_HALO, n),
                         lambda bi, i: (bi, jnp.minimum((lat(i) + 1) * per_o, o_x.shape[1] // O_HALO - 1), 0)),
            _resident(w_out.shape),
            pl.BlockSpec((1, 6, d), lambda bi, i: (jnp.where(i + first < n_ctx_tiles, b, bi), 0, 0)),
            _resident((1, d)),
            pl.BlockSpec(wup.shape, lambda bi, i: (0, 0), pipeline_mode=pl.Buffered(1)),
            _resident(cw.shape), _resident(cb.shape),
            pl.BlockSpec(wdn.shape, lambda bi, i: (0, 0), pipeline_mode=pl.Buffered(1)),
            _resident((1, d)),
        ],
        out_specs=pl.BlockSpec((1, tm, d), lambda bi, i: (bi, i, 0)),
        out_shape=jax.ShapeDtypeStruct((b, s - first * tm, d), F32),
        scratch_shapes=[pltpu.VMEM((tm + 2 * HALO, 2 * chunk), F32), pltpu.VMEM((tm, f), BF16)],
        compiler_params=_cparams(2),
        name="mix_ffn",
    )(xs, xs, xs, o_x, o_x, o_c, o_x, w_out, mods, g, wup, cw, cb, wdn, final_g)


def _rope_table(t, n_ctx, rot_dim, lane_lo):
    tt = jnp.arange(t)
    row = (tt // GRID_W).astype(F32)
    col = (tt % GRID_W).astype(F32)
    half = rot_dim // 2
    inv = ROPE_BASE ** (-jnp.arange(0, half, 2, dtype=F32) / half)
    ar, ac = row[:, None] * inv, col[:, None] * inv
    ang = jnp.concatenate([ar, ar, ac, ac], axis=-1)
    quarter = rot_dim // 4
    sign = jnp.where((jnp.arange(rot_dim) % (2 * quarter)) < quarter, -1.0, 1.0).astype(F32)
    cos, sin = jnp.cos(ang), jnp.sin(ang) * sign
    reps = (LANES - lane_lo) // rot_dim if lane_lo == 0 else 1
    cos_l = jnp.ones((t, LANES), F32).at[:, lane_lo:lane_lo + reps * rot_dim].set(jnp.tile(cos, (1, reps)))
    sin_l = jnp.zeros((t, LANES), F32).at[:, lane_lo:lane_lo + reps * rot_dim].set(jnp.tile(sin, (1, reps)))
    cos_s = jnp.concatenate([jnp.ones((n_ctx, LANES), F32), cos_l], axis=0)
    sin_s = jnp.concatenate([jnp.zeros((n_ctx, LANES), F32), sin_l], axis=0)
    return cos_s, sin_s


def _to_slots(w, n_heads, width, halves):
    d = w.shape[0]
    wh = w.reshape(d, n_heads, width)
    lo = jnp.pad(wh, ((0, 0), (0, 0), (0, LANES - width)))
    if not any(halves):
        return lo.reshape(d, n_heads * LANES)
    hi = jnp.pad(wh, ((0, 0), (0, 0), (LANES // 2, LANES // 2 - width)))
    in_hi = jnp.asarray(halves, jnp.int32).reshape(1, n_heads, 1) == 1
    return jnp.where(in_hi, hi, lo).reshape(d, n_heads * LANES)


def _vt_weights(wv, groups, width, tm):
    d = wv.shape[0]
    rows = width + O_HALO
    wt = jnp.zeros((groups, rows, d), wv.dtype).at[:, :width, :].set(wv.T.reshape(groups, width, d))
    ones = jnp.zeros((groups, rows, tm), F32).at[:, width, :].set(1.0)
    return wt.reshape(groups * rows, d), ones.reshape(groups * rows, tm)


def _dup(g):
    return jnp.concatenate([g, g]).reshape(1, LANES).astype(F32)


def kernel(x, c, ctx, c_ctx, ada_w, ada_b, norm1_g, norm2_g, ffn_w_up, ffn_conv_w, ffn_conv_b, ffn_w_down,
           gqa_w_in, gqa_q_norm_g, gqa_k_norm_g, gqa_w_out,
           mla_w_in, mla_q_norm_g, mla_kv_norm_g, mla_w_uq, mla_w_ukv, mla_w_out,
           diff_w_in, diff_lambda, diff_subln_g, diff_w_out,
           na_w_in, na_rpb, na_w_out, final_norm_g):
    b, t, d = x.shape
    n_ctx = ctx.shape[1]
    depth = ada_w.shape[0]
    assert n_ctx % TOKEN_TILE == 0 and t % TOKEN_TILE == 0 and b + 1 <= 8
    nct = n_ctx // TOKEN_TILE
    f = ffn_w_down.shape[1]
    chunk = 2 * LANES
    assert f % chunk == 0

    xs = jnp.concatenate([ctx, x], axis=1)
    cin = jnp.zeros((8, d), F32).at[:b].set(c).at[b].set(c_ctx)
    mods_all = _ada_all(cin, ada_w, ada_b).reshape(depth, 8, 6, d)

    tm = TOKEN_TILE
    ones_l = jnp.ones((1, LANES), F32)
    dummy_tab = jnp.zeros((xs.shape[1], LANES), F32)
    dummy_ones = jnp.zeros((8, LANES), F32)
    cos64, sin64 = _rope_table(t, n_ctx, GQA_HEAD_DIM, 0)
    cos_mla, sin_mla = _rope_table(t, n_ctx, MLA_ROPE_DIM, MLA_NOPE_DIM)
    no_lam, no_g = jnp.zeros((4, DIFF_HEAD_DIM), F32), jnp.ones((1, LANES), F32)
    pair_rows = MLA_V_DIM + O_HALO
    diff_rows = 2 * DIFF_HEAD_DIM + O_HALO

    for i in range(depth):
        m, j = i % N_MIXERS, i // N_MIXERS
        mods = mods_all[i]
        g1 = norm1_g[i].reshape(1, d)
        if m == 0:
            w = gqa_w_in[j].astype(BF16)
            nqc, nkc = GQA_HEADS * GQA_HEAD_DIM, GQA_KV_HEADS * GQA_HEAD_DIM
            wk = w[:, nqc:nqc + nkc].reshape(d, GQA_KV_HEADS, 1, GQA_HEAD_DIM)
            wk = jnp.broadcast_to(wk, (d, GQA_KV_HEADS, 2, GQA_HEAD_DIM)).reshape(d, GQA_KV_HEADS * LANES)
            wvt, vones = _vt_weights(w[:, nqc + nkc:], GQA_KV_HEADS, GQA_HEAD_DIM, tm)
            qc = GQA_HEAD_DIM ** -0.5 * LOG2E
            q, k, vt = _project(xs, mods, g1, w[:, :nqc], wk, wvt, _dup(gqa_q_norm_g[j]), _dup(gqa_k_norm_g[j]),
                                cos64 * qc, sin64 * qc, cos64, sin64, vones, qk_norm=True, rope=True,
                                head_dim=GQA_HEAD_DIM, k_norm_dim=2 * GQA_HEAD_DIM, v_transposed=True, q_rotated=True, n_ctx_tiles=nct,
                                name="proj_gqa")
            o_x, o_c = _flash(q, k, vt, no_lam, no_g, mode="pair", kw=LANES, k_cols=((0, LANES), (0, LANES)),
                       v_block_rows=pair_rows, v_rows=((0, pair_rows), (0, pair_rows)), vw=GQA_HEAD_DIM,
                       k_index=lambda p: p // 2, v_index=lambda p: p // 2, n_ctx=n_ctx, diff_scale=1.0,
                       name="attn_gqa")
            w_out = gqa_w_out[j]
        elif m == 1:
            w = mla_w_in[j].astype(BF16)
            nlat = MLA_Q_LORA + MLA_KV_LORA
            kr_slot = jnp.zeros((d, LANES), BF16).at[:, MLA_NOPE_DIM:MLA_NOPE_DIM + MLA_ROPE_DIM].set(w[:, nlat:])
            win = jnp.concatenate([w[:, :nlat], kr_slot], axis=1)
            qk_dim = MLA_NOPE_DIM + MLA_ROPE_DIM
            wuq = _to_slots(mla_w_uq[j].astype(BF16), MLA_HEADS, qk_dim, [0] * MLA_HEADS)
            wukv = mla_w_ukv[j].astype(BF16).reshape(MLA_KV_LORA, MLA_HEADS, MLA_NOPE_DIM + MLA_V_DIM)
            wuk = _to_slots(wukv[:, :, :MLA_NOPE_DIM].reshape(MLA_KV_LORA, -1), MLA_HEADS, MLA_NOPE_DIM,
                            [0] * MLA_HEADS)
            wuvt, vones = _vt_weights(wukv[:, :, MLA_NOPE_DIM:].reshape(MLA_KV_LORA, MLA_HEADS * MLA_V_DIM),
                                      MLA_HEADS, MLA_V_DIM, tm)
            qc = qk_dim ** -0.5 * LOG2E
            q, k, vt = _project_mla(xs, mods, g1, win, mla_q_norm_g[j].reshape(1, -1),
                                    mla_kv_norm_g[j].reshape(1, -1), wuq, wuk, wuvt,
                                    cos_mla * qc, sin_mla * qc, cos_mla, sin_mla, vones, n_ctx_tiles=nct)
            o_x, o_c = _flash(q, k, vt, no_lam, no_g, mode="pair", kw=2 * LANES, k_cols=((0, LANES), (LANES, 2 * LANES)),
                       v_block_rows=2 * pair_rows, v_rows=((0, pair_rows), (pair_rows, 2 * pair_rows)),
                       vw=MLA_V_DIM, k_index=lambda p: p, v_index=lambda p: p, n_ctx=n_ctx, diff_scale=1.0,
                       name="attn_mla")
            w_out = mla_w_out[j]
        elif m == 2:
            w = diff_w_in[j].astype(BF16)
            wvt, vones = _vt_weights(w[:, 2 * d:], DIFF_HEADS, 2 * DIFF_HEAD_DIM, tm)
            lambda_init = 0.8 - 0.6 * math.exp(-0.3 * i)
            qc = DIFF_HEAD_DIM ** -0.5 * LOG2E
            q, k, vt = _project(xs, mods, g1, w[:, :d], w[:, d:2 * d], wvt, ones_l, ones_l,
                                cos64 * qc, sin64 * qc, cos64, sin64, vones, qk_norm=False, rope=True,
                                head_dim=DIFF_HEAD_DIM, k_norm_dim=DIFF_HEAD_DIM, v_transposed=True, q_rotated=True, n_ctx_tiles=nct,
                                name="proj_diff")
            o_x, o_c = _flash(q, k, vt, diff_lambda[j], diff_subln_g[j].reshape(1, LANES), mode="diff", kw=LANES,
                       k_cols=((0, LANES), (0, LANES)), v_block_rows=diff_rows,
                       v_rows=((0, diff_rows), (0, diff_rows)), vw=2 * DIFF_HEAD_DIM,
                       k_index=lambda p: p, v_index=lambda p: p, n_ctx=n_ctx, diff_scale=1.0 - lambda_init,
                       name="attn_diff")
            w_out = diff_w_out[j]
        else:
            w = na_w_in[j].astype(BF16)
            na_scale = NA_HEAD_DIM ** -0.5
            assert math.frexp(na_scale)[0] == 0.5
            q, k, v = _project(xs, mods, g1, w[:, :d] * jnp.asarray(na_scale, BF16), w[:, d:2 * d], w[:, 2 * d:],
                               ones_l, ones_l, dummy_tab, dummy_tab, dummy_tab, dummy_tab, dummy_ones,
                               qk_norm=False, rope=False, head_dim=NA_HEAD_DIM, k_norm_dim=NA_HEAD_DIM,
                               v_transposed=False, q_rotated=False, n_ctx_tiles=nct,
                               name="proj_na")
            idx_all, var, wstart = _na_tables(t)
            bias = _na_bias(idx_all, na_rpb[j])
            o_x = o_c = _na_attention(q, k, v, bias, jnp.asarray(var), jnp.asarray(wstart), n_ctx=n_ctx)
            w_out = na_w_out[j]
        last = i == depth - 1

        xs = _mix_ffn(xs, o_x, o_c, w_out.astype(BF16), mods, norm2_g[i].reshape(1, d),
                      ffn_w_up[i].astype(BF16), ffn_conv_w[i], ffn_conv_b[i].reshape(1, 2 * f),
                      ffn_w_down[i].astype(BF16), final_norm_g.reshape(1, d), n_ctx_tiles=nct,
                      x_offset=0 if m == 3 else nct, latent_only=last, chunk=chunk, final_norm=last)
    return xs
```

```python
import functools
import math

import numpy as np
import jax
import jax.numpy as jnp
from jax import lax
from jax.experimental import pallas as pl
from jax.experimental.pallas import tpu as pltpu

F32 = jnp.float32
BF16 = jnp.bfloat16

LANES = 128
TOKEN_TILE = 256
HALO = 8
KV_CHUNK = 768
SUB_KEYS = 256
FLASH_TQ = 512
LOOP_CHUNKS = 4
VMEM_LIMIT = 56 * 1024 * 1024

GRID_W = 64
ROPE_BASE = 10000.0
EPS = 1e-6
NEG_INF = -1e30
LOG2E = 1.4426950408889634

GQA_HEADS, GQA_KV_HEADS, GQA_HEAD_DIM = 16, 4, 64
MLA_HEADS, MLA_NOPE_DIM, MLA_ROPE_DIM, MLA_V_DIM = 16, 64, 32, 64
MLA_Q_LORA, MLA_KV_LORA = 384, 256
DIFF_HEADS, DIFF_HEAD_DIM = 8, 64
NA_HEADS, NA_HEAD_DIM, NA_ROWS, NA_COLS = 16, 64, 8, 16
NA_QBLOCK = 128
NA_BIAS_TILE = 2048
NA_WIN_ROWS = 10
N_MIXERS = 4

NT_DIMS = (((1,), (1,)), ((), ()))


def _cparams(n_axes):
    return pltpu.CompilerParams(dimension_semantics=("arbitrary",) * n_axes, vmem_limit_bytes=VMEM_LIMIT)


def _resident(shape):
    nd = len(shape)
    return pl.BlockSpec(shape, lambda *_: (0,) * nd)


def _ada_kernel(c_ref, w_ref, b_ref, o_ref):
    c = c_ref[...]
    s = c * (1.0 / (1.0 + jnp.exp(-c)))
    o_ref[0] = jnp.dot(s.astype(BF16), w_ref[0].astype(BF16), preferred_element_type=F32) + b_ref[0]


def _ada_all(cin, ada_w, ada_b):
    depth, d, n = ada_w.shape
    tn = n // 4
    return pl.pallas_call(
        _ada_kernel,
        grid=(depth, n // tn),
        in_specs=[
            pl.BlockSpec((8, d), lambda l, j: (0, 0)),
            pl.BlockSpec((1, d, tn), lambda l, j: (l, 0, j)),
            pl.BlockSpec((1, 1, tn), lambda l, j: (l, 0, j)),
        ],
        out_specs=pl.BlockSpec((1, 8, tn), lambda l, j: (l, 0, j)),
        out_shape=jax.ShapeDtypeStruct((depth, 8, n), F32),
        compiler_params=_cparams(2),
        name="ada",
    )(cin, ada_w, ada_b.reshape(depth, 1, n))


def _norm_mod(x, g, shift, scale):
    ms = jnp.mean(x * x, axis=-1, keepdims=True)
    return (x * lax.rsqrt(ms + EPS) * g) * (1.0 + scale) + shift


def _rope_slot(y, cos, sin_signed, lo_mask, quarter):
    fwd = pltpu.roll(y, LANES - quarter, axis=1)
    bwd = pltpu.roll(y, quarter, axis=1)
    return y * cos + jnp.where(lo_mask, fwd, bwd) * sin_signed


def _lo_mask(shape, quarter):
    lane = lax.broadcasted_iota(jnp.int32, shape, 1)
    return (lane % (2 * quarter)) < quarter


def _q_spec(n, tm, n_tiles, n_ctx_tiles, rotate):
    if not rotate:
        return pl.BlockSpec((1, tm, n), lambda bi, i: (bi, i, 0))
    n_lat = n_tiles - n_ctx_tiles
    return pl.BlockSpec((1, tm, n), lambda bi, i: (bi, jnp.where(i < n_ctx_tiles, i + n_lat, i - n_ctx_tiles), 0))


def _vt_spec(rows, tm):
    per = KV_CHUNK // tm
    return pl.BlockSpec((1, 1, rows, tm), lambda bi, i: (bi, i // per, 0, i % per))


def _proj_kernel(x_ref, mod_ref, g_ref, wq_ref, wk_ref, wv_ref, qg_ref, kg_ref, cosq_ref, sinq_ref, cosk_ref, sink_ref,
                 vones_ref, q_ref, k_ref, v_ref, *, qk_norm, rope, head_dim, k_norm_dim, v_transposed):
    h = _norm_mod(x_ref[0], g_ref[...], mod_ref[0, 0:1, :], mod_ref[0, 1:2, :]).astype(BF16)
    tm = h.shape[0]
    lo = _lo_mask((tm, LANES), head_dim // 4) if rope else None
    low_half = lax.broadcasted_iota(jnp.int32, (tm, LANES), 1) < LANES // 2

    def chunks(w_ref):
        n = w_ref.shape[1]
        for c0 in range(0, n, 2 * LANES):
            c1 = min(c0 + 2 * LANES, n)
            y2 = jnp.dot(h, w_ref[:, c0:c1], preferred_element_type=F32)
            for s0 in range(0, c1 - c0, LANES):
                yield (c0 + s0) // LANES, y2[:, s0:s0 + LANES]

    for c, y in chunks(wq_ref):
        if qk_norm:
            sq = y * y
            ms_lo = jnp.sum(jnp.where(low_half, sq, 0.0), axis=-1, keepdims=True) * (1.0 / head_dim)
            ms_hi = jnp.sum(jnp.where(low_half, 0.0, sq), axis=-1, keepdims=True) * (1.0 / head_dim)
            y = y * jnp.where(low_half, lax.rsqrt(ms_lo + EPS), lax.rsqrt(ms_hi + EPS)) * qg_ref[...]
        if rope:
            y = _rope_slot(y, cosq_ref[...], sinq_ref[...], lo, head_dim // 4)
        q_ref[0, :, 2 * c * LANES:(2 * c + 1) * LANES] = jnp.where(low_half, y, 0.0).astype(BF16)
        q_ref[0, :, (2 * c + 1) * LANES:(2 * c + 2) * LANES] = jnp.where(low_half, 0.0, y).astype(BF16)
    for c, y in chunks(wk_ref):
        if qk_norm:
            ms = jnp.sum(y * y, axis=-1, keepdims=True) * (1.0 / k_norm_dim)
            y = y * lax.rsqrt(ms + EPS) * kg_ref[...]
        if rope:
            y = _rope_slot(y, cosk_ref[...], sink_ref[...], lo, head_dim // 4)
        k_ref[0, :, c * LANES:(c + 1) * LANES] = y.astype(BF16)
    if v_transposed:
        vt = lax.dot_general(wv_ref[...], h, NT_DIMS, preferred_element_type=F32) + vones_ref[...]
        v_ref[0, 0] = vt.astype(BF16)
    else:
        v_ref[0] = jnp.dot(h, wv_ref[...], preferred_element_type=F32).astype(BF16)


def _project(xs, mods, g, wq, wk, wv, qg, kg, cosq, sinq, cosk, sink, vones, *, qk_norm, rope, head_dim, k_norm_dim,
             v_transposed, q_rotated, n_ctx_tiles, name):
    b, s, d = xs.shape
    tm = TOKEN_TILE
    nq, nk = 2 * wq.shape[1], wk.shape[1]
    tok = lambda n: pl.BlockSpec((1, tm, n), lambda bi, i: (bi, i, 0))
    tab = pl.BlockSpec((tm, LANES), lambda bi, i: (i, 0))
    if v_transposed:
        v_spec = _vt_spec(wv.shape[0], tm)
        v_shape = jax.ShapeDtypeStruct((b, s // KV_CHUNK, wv.shape[0], KV_CHUNK), BF16)
    else:
        v_spec, v_shape = tok(wv.shape[1]), jax.ShapeDtypeStruct((b, s, wv.shape[1]), BF16)
    return pl.pallas_call(
        functools.partial(_proj_kernel, qk_norm=qk_norm, rope=rope, head_dim=head_dim, k_norm_dim=k_norm_dim,
                          v_transposed=v_transposed),
        grid=(b, s // tm),
        in_specs=[
            tok(d),
            pl.BlockSpec((1, 6, d), lambda bi, i: (jnp.where(i < n_ctx_tiles, b, bi), 0, 0)),
            _resident((1, d)),
            _resident(wq.shape), _resident(wk.shape), _resident(wv.shape),
            _resident((1, LANES)), _resident((1, LANES)),
            tab, tab, tab, tab,
            _resident(vones.shape),
        ],
        out_specs=[_q_spec(nq, tm, s // tm, n_ctx_tiles, q_rotated), tok(nk), v_spec],
        out_shape=[jax.ShapeDtypeStruct((b, s, nq), BF16), jax.ShapeDtypeStruct((b, s, nk), BF16), v_shape],
        compiler_params=_cparams(2),
        name=name,
    )(xs, mods, g, wq, wk, wv, qg, kg, cosq, sinq, cosk, sink, vones)


def _mla_proj_kernel(x_ref, mod_ref, g_ref, win_ref, qn_ref, kvn_ref, wuq_ref, wuk_ref, wuvt_ref,
                     cosq_ref, sinq_ref, cosk_ref, sink_ref, vones_ref, q_ref, k_ref, v_ref):
    h = _norm_mod(x_ref[0], g_ref[...], mod_ref[0, 0:1, :], mod_ref[0, 1:2, :]).astype(BF16)
    tm = h.shape[0]
    t1 = jnp.dot(h, win_ref[...], preferred_element_type=F32)
    cq = t1[:, :MLA_Q_LORA]
    ckv = t1[:, MLA_Q_LORA:MLA_Q_LORA + MLA_KV_LORA]
    kr = t1[:, MLA_Q_LORA + MLA_KV_LORA:]
    cqn = (cq * lax.rsqrt(jnp.mean(cq * cq, axis=-1, keepdims=True) + EPS) * qn_ref[...]).astype(BF16)
    ckvn = (ckv * lax.rsqrt(jnp.mean(ckv * ckv, axis=-1, keepdims=True) + EPS) * kvn_ref[...]).astype(BF16)
    quarter = MLA_ROPE_DIM // 4
    lo = _lo_mask((tm, LANES), quarter)
    kr = _rope_slot(kr, cosk_ref[...], sink_ref[...], lo, quarter)
    n = wuq_ref.shape[1]
    for c0 in range(0, n, 2 * LANES):
        q2 = jnp.dot(cqn, wuq_ref[:, c0:c0 + 2 * LANES], preferred_element_type=F32)
        k2 = jnp.dot(ckvn, wuk_ref[:, c0:c0 + 2 * LANES], preferred_element_type=F32)
        for s0 in (0, LANES):
            q_ref[0, :, c0 + s0:c0 + s0 + LANES] = _rope_slot(
                q2[:, s0:s0 + LANES], cosq_ref[...], sinq_ref[...], lo, quarter).astype(BF16)
            k_ref[0, :, c0 + s0:c0 + s0 + LANES] = (k2[:, s0:s0 + LANES] + kr).astype(BF16)
    vt = lax.dot_general(wuvt_ref[...], ckvn, NT_DIMS, preferred_element_type=F32) + vones_ref[...]
    v_ref[0, 0] = vt.astype(BF16)


def _project_mla(xs, mods, g, win, qn, kvn, wuq, wuk, wuvt, cosq, sinq, cosk, sink, vones, *, n_ctx_tiles):
    b, s, d = xs.shape
    tm = TOKEN_TILE
    nq, nk = wuq.shape[1], wuk.shape[1]
    tok = lambda n: pl.BlockSpec((1, tm, n), lambda bi, i: (bi, i, 0))
    tab = pl.BlockSpec((tm, LANES), lambda bi, i: (i, 0))
    return pl.pallas_call(
        _mla_proj_kernel,
        grid=(b, s // tm),
        in_specs=[
            tok(d),
            pl.BlockSpec((1, 6, d), lambda bi, i: (jnp.where(i < n_ctx_tiles, b, bi), 0, 0)),
            _resident((1, d)),
            _resident(win.shape), _resident(qn.shape), _resident(kvn.shape),
            _resident(wuq.shape), _resident(wuk.shape), _resident(wuvt.shape),
            tab, tab, tab, tab,
            _resident(vones.shape),
        ],
        out_specs=[_q_spec(nq, tm, s // tm, n_ctx_tiles, True), tok(nk), _vt_spec(wuvt.shape[0], tm)],
        out_shape=[jax.ShapeDtypeStruct((b, s, nq), BF16), jax.ShapeDtypeStruct((b, s, nk), BF16),
                   jax.ShapeDtypeStruct((b, s // KV_CHUNK, wuvt.shape[0], KV_CHUNK), BF16)],
        compiler_params=_cparams(2),
        name="proj_mla",
    )(xs, mods, g, win, qn, kvn, wuq, wuk, wuvt, cosq, sinq, cosk, sink, vones)


def _flash_kernel(q_ref, k_ref, vt_ref, lam_ref, subg_ref, o_ref, *scratch, mode, k_cols, v_rows, vw, diff_scale,
                  ctx_keys):
    tq = q_ref.shape[1]
    n_chunks = vt_ref.shape[1]
    qs = [q_ref[0, :, a * LANES:(a + 1) * LANES] for a in (0, 1)]
    n_rows = v_rows[0][1] - v_rows[0][0]

    def keys(key_rows):
        k0 = k_ref[0, key_rows, k_cols[0][0]:k_cols[0][1]]
        k1 = k0 if k_cols[1] == k_cols[0] else k_ref[0, key_rows, k_cols[1][0]:k_cols[1][1]]
        return k0, k1

    def values(chunk_idx, key_lanes):
        v0 = vt_ref[0, chunk_idx, v_rows[0][0]:v_rows[0][1], key_lanes]
        v1 = v0 if v_rows[1] == v_rows[0] else vt_ref[0, chunk_idx, v_rows[1][0]:v_rows[1][1], key_lanes]
        return v0, v1

    def update(st, cmax, vt_a, carry_a):
        m, acc = carry_a
        m_new = jnp.maximum(m, cmax)
        p = jnp.exp2(st - m_new).astype(BF16)
        return m_new, jnp.exp2(m - m_new) * acc + jnp.dot(vt_a, p, preferred_element_type=F32)

    init = tuple((jnp.full((1, tq), -jnp.inf, F32), jnp.zeros((n_rows, tq), F32)) for _ in (0, 1))

    def finish(carry):
        (_, acc_a), (_, acc_b) = carry
        oa = acc_a[0:vw] / acc_a[vw:vw + 1]
        ob = acc_b[0:vw] / acc_b[vw:vw + 1]
        if mode == "pair":
            out = jnp.concatenate([oa, ob], axis=0).T
        else:
            lam = lam_ref[...]
            lam_full = (jnp.exp(jnp.sum(lam[0:1] * lam[1:2], axis=-1, keepdims=True))
                        - jnp.exp(jnp.sum(lam[2:3] * lam[3:4], axis=-1, keepdims=True))
                        + (1.0 - diff_scale))
            o = (oa - lam_full * ob).T
            out = o * lax.rsqrt(jnp.mean(o * o, axis=-1, keepdims=True) + EPS) * subg_ref[...] * diff_scale
        o_ref[0] = out.astype(BF16)

    if ctx_keys:
        ks, vts = keys(slice(0, ctx_keys)), values(0, slice(0, ctx_keys))
        out = []
        for a in (0, 1):
            st = lax.dot_general(ks[a], qs[a], NT_DIMS, preferred_element_type=F32)
            out.append(update(st, jnp.max(st, axis=0, keepdims=True), vts[a], init[a]))
        finish(tuple(out))
        return

    (st_scr,) = scratch

    def chunk_keys(chunk_idx):
        start = chunk_idx * KV_CHUNK
        return keys(pl.ds(start if isinstance(start, int) else pl.multiple_of(start, KV_CHUNK), KV_CHUNK))

    def score(slot, a, ks):
        st = lax.dot_general(ks[a], qs[a], NT_DIMS, preferred_element_type=F32)
        st_scr[slot, a] = st
        return jnp.max(st, axis=0, keepdims=True)

    def scores(slot, chunk_idx):
        ks = chunk_keys(chunk_idx)
        return tuple(score(slot, a, ks) for a in (0, 1))

    def run(first, count, cmax, carry):
        for u in range(count):
            nxt, slot = first + u + 1, u % 2
            has_next = isinstance(nxt, jax.Array) or nxt < n_chunks
            m_new = [jnp.maximum(carry[a][0], cmax[a]) for a in (0, 1)]
            acc = [jnp.exp2(carry[a][0] - m_new[a]) * carry[a][1] for a in (0, 1)]
            cmax_next = [None, None]
            for j in range(KV_CHUNK // SUB_KEYS):
                rows = slice(j * SUB_KEYS, (j + 1) * SUB_KEYS)
                if has_next:
                    start = nxt * KV_CHUNK + j * SUB_KEYS
                    ks = keys(pl.ds(start if isinstance(start, int) else pl.multiple_of(start, SUB_KEYS), SUB_KEYS))
                    for a in (0, 1):
                        st = lax.dot_general(ks[a], qs[a], NT_DIMS, preferred_element_type=F32)
                        st_scr[1 - slot, a, rows, :] = st
                        mx = jnp.max(st, axis=0, keepdims=True)
                        cmax_next[a] = mx if cmax_next[a] is None else jnp.maximum(cmax_next[a], mx)
                vts = values(first + u, rows)
                for a in (0, 1):
                    p = jnp.exp2(st_scr[slot, a, rows, :] - m_new[a]).astype(BF16)
                    acc[a] = acc[a] + jnp.dot(vts[a], p, preferred_element_type=F32)
            carry = tuple((m_new[a], acc[a]) for a in (0, 1))
            cmax = tuple(cmax_next)
        return cmax, carry

    n_loop = (n_chunks - 1) // LOOP_CHUNKS
    cmax, carry = lax.fori_loop(0, n_loop, lambda i, st: run(i * LOOP_CHUNKS, LOOP_CHUNKS, *st),
                                (scores(0, 0), init))
    _, carry = run(n_loop * LOOP_CHUNKS, n_chunks - n_loop * LOOP_CHUNKS, cmax, carry)
    finish(carry)


def _flash(q, k, vt, lam, subg, *, mode, kw, k_cols, v_block_rows, v_rows, vw, k_index, v_index, n_ctx, diff_scale,
           name):
    b, s, nq = q.shape
    t = s - n_ctx
    n_pairs = nq // (2 * LANES)
    n_chunks = s // KV_CHUNK
    assert s % KV_CHUNK == 0 and n_ctx <= KV_CHUNK and t % FLASH_TQ == 0 and t % n_ctx == 0
    assert vt.shape[1] == n_chunks
    common = dict(mode=mode, k_cols=k_cols, v_rows=v_rows, vw=vw, diff_scale=diff_scale)
    o_x = pl.pallas_call(
        functools.partial(_flash_kernel, ctx_keys=0, **common),
        grid=(b, n_pairs, t // FLASH_TQ),
        in_specs=[
            pl.BlockSpec((1, FLASH_TQ, 2 * LANES), lambda bi, p, i: (bi, i, p)),
            pl.BlockSpec((1, s, kw), lambda bi, p, i: (bi, 0, k_index(p))),
            pl.BlockSpec((1, n_chunks, v_block_rows, KV_CHUNK), lambda bi, p, i: (bi, 0, v_index(p), 0)),
            _resident(lam.shape), _resident(subg.shape),
        ],
        out_specs=pl.BlockSpec((1, FLASH_TQ, LANES), lambda bi, p, i: (bi, i, p)),
        out_shape=jax.ShapeDtypeStruct((b, t, n_pairs * LANES), BF16),
        scratch_shapes=[pltpu.VMEM((2, 2, KV_CHUNK, FLASH_TQ), F32)],
        compiler_params=_cparams(3),
        name=name,
    )(q, k, vt, lam, subg)
    o_c = pl.pallas_call(
        functools.partial(_flash_kernel, ctx_keys=n_ctx, **common),
        grid=(b, n_pairs),
        in_specs=[
            pl.BlockSpec((1, n_ctx, 2 * LANES), lambda bi, p: (bi, t // n_ctx, p)),
            pl.BlockSpec((1, n_ctx, kw), lambda bi, p: (bi, 0, k_index(p))),
            pl.BlockSpec((1, 1, v_block_rows, KV_CHUNK), lambda bi, p: (bi, 0, v_index(p), 0)),
            _resident(lam.shape), _resident(subg.shape),
        ],
        out_specs=pl.BlockSpec((1, n_ctx, LANES), lambda bi, p: (bi, 0, p)),
        out_shape=jax.ShapeDtypeStruct((b, n_ctx, n_pairs * LANES), BF16),
        compiler_params=_cparams(2),
        name=name + "_ctx",
    )(q, k, vt, lam, subg)
    return o_x, o_c


def _na_tables(t):
    rows = t // GRID_W
    win_rows = min(NA_ROWS, rows)
    band_rows = min(win_rows + 1, rows)
    assert rows >= NA_WIN_ROWS and NA_QBLOCK % GRID_W == 0
    qrows = NA_QBLOCK // GRID_W
    nb = t // NA_QBLOCK
    nwin = NA_WIN_ROWS * GRID_W
    q_off, k_off = np.arange(NA_QBLOCK), np.arange(nwin)
    q_dr, q_col = q_off // GRID_W, q_off % GRID_W
    k_dr, k_col = k_off // GRID_W, k_off % GRID_W
    col_start = np.clip(q_col - NA_COLS // 2, 0, GRID_W - NA_COLS)
    col_in = (k_col[None, :] >= col_start[:, None]) & (k_col[None, :] < col_start[:, None] + NA_COLS)
    dc_idx = np.clip(k_col[None, :] - q_col[:, None] + NA_COLS - 1, 0, 2 * NA_COLS - 2)
    patterns, var, wstart = {}, [], []
    for j in range(nb):
        r = j * qrows + q_dr
        row_start = np.clip(r - win_rows // 2, 0, rows - win_rows)
        b0 = min(int(row_start[0]), rows - band_rows)
        w0 = min(b0 - b0 % 2, rows - NA_WIN_ROWS)
        assert w0 % qrows == 0 and w0 <= b0 and b0 + band_rows <= w0 + NA_WIN_ROWS
        k_row = w0 + k_dr
        in_band = (k_row >= b0) & (k_row < b0 + band_rows)
        in_win = (col_in & in_band[None, :] & (k_row[None, :] >= row_start[:, None])
                  & (k_row[None, :] < row_start[:, None] + win_rows))
        dr_idx = np.clip(k_row[None, :] - r[:, None] + NA_ROWS - 1, 0, 2 * NA_ROWS - 2)
        idx = np.where(in_win, dr_idx * (2 * NA_COLS - 1) + dc_idx, -1).astype(np.int32)
        key = idx.tobytes()
        if key not in patterns:
            patterns[key] = (len(patterns), idx)
        var.append(patterns[key][0])
        wstart.append(w0 * GRID_W)
    idx_all = np.stack([p[1] for p in sorted(patterns.values(), key=lambda p: p[0])])
    return idx_all.reshape(len(patterns), 1, -1), np.asarray(var, np.int32), np.asarray(wstart, np.int32)


def _na_bias_kernel(idx_ref, rpb_ref, o_ref):
    idx = idx_ref[0]
    r = rpb_ref[...]
    onehot = (lax.broadcasted_iota(jnp.int32, (r.shape[1], idx.shape[1]), 0) == idx).astype(BF16)
    b1 = r.astype(BF16)
    r2 = r - b1.astype(F32)
    b2 = r2.astype(BF16)
    b3 = (r2 - b2.astype(F32)).astype(BF16)
    heads = r.shape[0]
    parts = jnp.dot(jnp.concatenate([b1, b2, b3], axis=0), onehot, preferred_element_type=F32)
    out = parts[:heads] + parts[heads:2 * heads] + parts[2 * heads:]
    o_ref[0] = jnp.where(idx >= 0, out, NEG_INF)


def _na_bias(idx_all, rpb):
    nvar, _, npos = idx_all.shape
    heads = rpb.shape[0]
    nr = (2 * NA_ROWS - 1) * (2 * NA_COLS - 1)
    nr_pad = pl.cdiv(nr, LANES) * LANES
    rflat = jnp.pad(rpb.reshape(heads, nr), ((0, 0), (0, nr_pad - nr)))
    tn = NA_BIAS_TILE
    assert npos % tn == 0
    out = pl.pallas_call(
        _na_bias_kernel,
        grid=(nvar, npos // tn),
        in_specs=[pl.BlockSpec((1, 1, tn), lambda v, j: (v, 0, j)), _resident(rflat.shape)],
        out_specs=pl.BlockSpec((1, heads, tn), lambda v, j: (v, 0, j)),
        out_shape=jax.ShapeDtypeStruct((nvar, heads, npos), F32),
        compiler_params=_cparams(2),
        name="na_bias",
    )(jnp.asarray(idx_all), rflat)
    return out.reshape(nvar, heads, NA_QBLOCK, npos // NA_QBLOCK)


def _halves(lo_src, hi_src):
    lane = lax.broadcasted_iota(jnp.int32, lo_src.shape, 1)
    return jnp.where(lane < LANES // 2, lo_src, hi_src)


def _na_kernel(var_ref, ws_ref, q_ref, k_ref, v_ref, b0_ref, b1_ref, o_ref, *, n_ctx, n_ctx_tiles):
    qi = pl.program_id(2)
    nwin = NA_WIN_ROWS * GRID_W

    @pl.when(qi < n_ctx_tiles)
    def _():
        kc, vc = k_ref[0, 0:n_ctx, :], v_ref[0, 0:n_ctx, :]
        outs = []
        for a in (0, 1):
            s_c = lax.dot_general(q_ref[0, :, a * LANES:(a + 1) * LANES], kc, NT_DIMS,
                                  preferred_element_type=F32)
            p_c = jnp.exp(s_c - jnp.max(s_c, axis=-1, keepdims=True))
            acc = jnp.dot(p_c.astype(BF16), vc, preferred_element_type=F32)
            outs.append(acc / jnp.sum(p_c, axis=-1, keepdims=True))
        o_ref[0] = _halves(outs[0], outs[1]).astype(BF16)

    @pl.when(qi >= n_ctx_tiles)
    def _():
        kc, vc = k_ref[0, 0:n_ctx, :], v_ref[0, 0:n_ctx, :]
        work = []
        for sub, bias_ref in ((0, b0_ref), (1, b1_ref)):
            j = (qi - n_ctx_tiles) * (TOKEN_TILE // NA_QBLOCK) + sub
            start = pl.multiple_of(n_ctx + ws_ref[j], NA_QBLOCK)
            kw, vw = k_ref[0, pl.ds(start, nwin), :], v_ref[0, pl.ds(start, nwin), :]
            for a in (0, 1):
                q = q_ref[0, sub * NA_QBLOCK:(sub + 1) * NA_QBLOCK, a * LANES:(a + 1) * LANES]
                s_c = lax.dot_general(q, kc, NT_DIMS, preferred_element_type=F32)
                s_w = lax.dot_general(q, kw, NT_DIMS, preferred_element_type=F32) + bias_ref[0, a]
                work.append((s_c, s_w, vw))
        outs = []
        for s_c, s_w, vw in work:
            m = jnp.maximum(jnp.max(s_c, axis=-1, keepdims=True), jnp.max(s_w, axis=-1, keepdims=True))
            p_c, p_w = jnp.exp(s_c - m), jnp.exp(s_w - m)
            l = jnp.sum(p_c, axis=-1, keepdims=True) + jnp.sum(p_w, axis=-1, keepdims=True)
            acc = (jnp.dot(p_c.astype(BF16), vc, preferred_element_type=F32)
                   + jnp.dot(p_w.astype(BF16), vw, preferred_element_type=F32))
            outs.append(acc / l)
        for sub in (0, 1):
            o_ref[0, sub * NA_QBLOCK:(sub + 1) * NA_QBLOCK, :] = _halves(outs[2 * sub], outs[2 * sub + 1]).astype(BF16)


def _na_attention(q, k, v, bias, var, wstart, *, n_ctx):
    b, s, nq = q.shape
    tq = TOKEN_TILE
    n_pairs = nq // (2 * LANES)
    n_ctx_tiles = n_ctx // tq
    per_tile = tq // NA_QBLOCK
    nwin = NA_WIN_ROWS * GRID_W

    def bias_spec(sub):
        def index(bi, p, i, var_ref, ws_ref):
            j = jnp.maximum(i - n_ctx_tiles, 0) * per_tile + sub
            return (var_ref[j], p, 0, 0)
        return pl.BlockSpec((1, 2, NA_QBLOCK, nwin), index)

    grid_spec = pltpu.PrefetchScalarGridSpec(
        num_scalar_prefetch=2,
        grid=(b, n_pairs, s // tq),
        in_specs=[
            pl.BlockSpec((1, tq, 2 * LANES), lambda bi, p, i, *_: (bi, i, p)),
            pl.BlockSpec((1, s, LANES), lambda bi, p, i, *_: (bi, 0, p)),
            pl.BlockSpec((1, s, LANES), lambda bi, p, i, *_: (bi, 0, p)),
            bias_spec(0), bias_spec(1),
        ],
        out_specs=pl.BlockSpec((1, tq, LANES), lambda bi, p, i, *_: (bi, i, p)),
    )
    return pl.pallas_call(
        functools.partial(_na_kernel, n_ctx=n_ctx, n_ctx_tiles=n_ctx_tiles),
        grid_spec=grid_spec,
        out_shape=jax.ShapeDtypeStruct((b, s, n_pairs * LANES), BF16),
        compiler_params=_cparams(3),
        name="na_attn",
    )(var, wstart, q, k, v, bias, bias)


O_HALO = 16


def _mix_ffn_kernel(xp_ref, x_ref, xn_ref, op_ref, ox_ref, oc_ref, on_ref, wo_ref, mod_ref, g_ref, wup_ref, cw_ref,
                    cb_ref, wdn_ref, fg_ref, y_ref, u_scr, a_scr, *, n_ctx_tiles, n_tiles, first_tile, chunk,
                    final_norm):
    t = pl.program_id(1) + first_tile
    tm = x_ref.shape[1]
    has_prev = jnp.logical_and(t != 0, t != n_ctx_tiles).astype(F32)
    has_next = jnp.logical_and(t != n_ctx_tiles - 1, t != n_tiles - 1).astype(F32)
    o = jnp.where(t < n_ctx_tiles, oc_ref[0], ox_ref[0]).astype(F32)
    o_cat = jnp.concatenate([op_ref[0].astype(F32)[O_HALO - HALO:], o, on_ref[0].astype(F32)[:HALO]], axis=0)
    x_cat = jnp.concatenate([xp_ref[0], x_ref[0], xn_ref[0]], axis=0)
    x1 = x_cat + mod_ref[0, 2:3, :] * jnp.dot(o_cat.astype(BF16), wo_ref[...], preferred_element_type=F32)
    g, shift, scale = g_ref[...], mod_ref[0, 3:4, :], mod_ref[0, 4:5, :]
    x = x1[HALO:HALO + tm]
    h = jnp.concatenate([_norm_mod(x1[:HALO], g, shift, scale) * has_prev,
                         _norm_mod(x, g, shift, scale),
                         _norm_mod(x1[HALO + tm:], g, shift, scale) * has_next], axis=0).astype(BF16)
    f = wdn_ref.shape[0]

    def val_gate(ref, j):
        return jnp.concatenate([ref[:, j * chunk:(j + 1) * chunk], ref[:, f + j * chunk:f + (j + 1) * chunk]], axis=1)

    for j in range(f // chunk):
        u_scr[...] = jnp.dot(h, val_gate(wup_ref, j), preferred_element_type=F32)
        cw, cb = val_gate(cw_ref, j), val_gate(cb_ref, j)
        half = tm // 2
        for r0 in (0, half):
            u = (u_scr[HALO - 1 + r0:HALO - 1 + r0 + half, :] * cw[0:1] + u_scr[HALO + r0:HALO + r0 + half, :] * cw[1:2]
                 + u_scr[HALO + 1 + r0:HALO + 1 + r0 + half, :] * cw[2:3] + cb)
            val, gate = u[:, :chunk], u[:, chunk:]
            a_scr[r0:r0 + half, j * chunk:(j + 1) * chunk] = (gate * (1.0 / (1.0 + jnp.exp(-gate))) * val).astype(BF16)
    y = x + mod_ref[0, 5:6, :] * jnp.dot(a_scr[...], wdn_ref[...], preferred_element_type=F32)
    if final_norm:
        y = y * lax.rsqrt(jnp.mean(y * y, axis=-1, keepdims=True) + EPS) * fg_ref[...]
    y_ref[0] = y


def _mix_ffn(xs, o_x, o_c, w_out, mods, g, wup, cw, cb, wdn, final_g, *, n_ctx_tiles, x_offset, latent_only, chunk,
             final_norm):
    b, s, d = xs.shape
    tm = TOKEN_TILE
    n_tiles = s // tm
    n = o_x.shape[2]
    f = wdn.shape[0]
    first = n_ctx_tiles if latent_only else 0
    assert n_ctx_tiles == 1
    per, per_o = tm // HALO, tm // O_HALO
    lat = lambda i: jnp.maximum(i + first, n_ctx_tiles) - x_offset
    return pl.pallas_call(
        functools.partial(_mix_ffn_kernel, n_ctx_tiles=n_ctx_tiles, n_tiles=n_tiles, first_tile=first, chunk=chunk,
                          final_norm=final_norm),
        grid=(b, n_tiles - first),
        in_specs=[
            pl.BlockSpec((1, HALO, d), lambda bi, i: (bi, jnp.maximum((i + first) * per - 1, 0), 0)),
            pl.BlockSpec((1, tm, d), lambda bi, i: (bi, i + first, 0)),
            pl.BlockSpec((1, HALO, d), lambda bi, i: (bi, jnp.minimum((i + first + 1) * per, s // HALO - 1), 0)),
            pl.BlockSpec((1, O_HALO, n), lambda bi, i: (bi, jnp.maximum(lat(i) * per_o - 1, 0), 0)),
            pl.BlockSpec((1, tm, n), lambda bi, i: (bi, lat(i), 0)),
            pl.BlockSpec((1, tm, n), lambda bi, i: (bi, jnp.minimum(i + first, n_ctx_tiles - 1), 0)),
            pl.BlockSpec((1, O_HALO, n),
                         lambda bi, i: (bi, jnp.minimum((lat(i) + 1) * per_o, o_x.shape[1] // O_HALO - 1), 0)),
            _resident(w_out.shape),
            pl.BlockSpec((1, 6, d), lambda bi, i: (jnp.where(i + first < n_ctx_tiles, b, bi), 0, 0)),
            _resident((1, d)),
            pl.BlockSpec(wup.shape, lambda bi, i: (0, 0), pipeline_mode=pl.Buffered(1)),
            _resident(cw.shape), _resident(cb.shape),
            pl.BlockSpec(wdn.shape, lambda bi, i: (0, 0), pipeline_mode=pl.Buffered(1)),
            _resident((1, d)),
        ],
        out_specs=pl.BlockSpec((1, tm, d), lambda bi, i: (bi, i, 0)),
        out_shape=jax.ShapeDtypeStruct((b, s - first * tm, d), F32),
        scratch_shapes=[pltpu.VMEM((tm + 2 * HALO, 2 * chunk), F32), pltpu.VMEM((tm, f), BF16)],
        compiler_params=_cparams(2),
        name="mix_ffn",
    )(xs, xs, xs, o_x, o_x, o_c, o_x, w_out, mods, g, wup, cw, cb, wdn, final_g)


def _rope_table(t, n_ctx, rot_dim, lane_lo):
    tt = jnp.arange(t)
    row = (tt // GRID_W).astype(F32)
    col = (tt % GRID_W).astype(F32)
    half = rot_dim // 2
    inv = ROPE_BASE ** (-jnp.arange(0, half, 2, dtype=F32) / half)
    ar, ac = row[:, None] * inv, col[:, None] * inv
    ang = jnp.concatenate([ar, ar, ac, ac], axis=-1)
    quarter = rot_dim // 4
    sign = jnp.where((jnp.arange(rot_dim) % (2 * quarter)) < quarter, -1.0, 1.0).astype(F32)
    cos, sin = jnp.cos(ang), jnp.sin(ang) * sign
    reps = (LANES - lane_lo) // rot_dim if lane_lo == 0 else 1
    cos_l = jnp.ones((t, LANES), F32).at[:, lane_lo:lane_lo + reps * rot_dim].set(jnp.tile(cos, (1, reps)))
    sin_l = jnp.zeros((t, LANES), F32).at[:, lane_lo:lane_lo + reps * rot_dim].set(jnp.tile(sin, (1, reps)))
    cos_s = jnp.concatenate([jnp.ones((n_ctx, LANES), F32), cos_l], axis=0)
    sin_s = jnp.concatenate([jnp.zeros((n_ctx, LANES), F32), sin_l], axis=0)
    return cos_s, sin_s


def _to_slots(w, n_heads, width, halves):
    d = w.shape[0]
    wh = w.reshape(d, n_heads, width)
    lo = jnp.pad(wh, ((0, 0), (0, 0), (0, LANES - width)))
    if not any(halves):
        return lo.reshape(d, n_heads * LANES)
    hi = jnp.pad(wh, ((0, 0), (0, 0), (LANES // 2, LANES // 2 - width)))
    in_hi = jnp.asarray(halves, jnp.int32).reshape(1, n_heads, 1) == 1
    return jnp.where(in_hi, hi, lo).reshape(d, n_heads * LANES)


def _vt_weights(wv, groups, width, tm):
    d = wv.shape[0]
    rows = width + O_HALO
    wt = jnp.zeros((groups, rows, d), wv.dtype).at[:, :width, :].set(wv.T.reshape(groups, width, d))
    ones = jnp.zeros((groups, rows, tm), F32).at[:, width, :].set(1.0)
    return wt.reshape(groups * rows, d), ones.reshape(groups * rows, tm)


def _dup(g):
    return jnp.concatenate([g, g]).reshape(1, LANES).astype(F32)


def kernel(x, c, ctx, c_ctx, ada_w, ada_b, norm1_g, norm2_g, ffn_w_up, ffn_conv_w, ffn_conv_b, ffn_w_down,
           gqa_w_in, gqa_q_norm_g, gqa_k_norm_g, gqa_w_out,
           mla_w_in, mla_q_norm_g, mla_kv_norm_g, mla_w_uq, mla_w_ukv, mla_w_out,
           diff_w_in, diff_lambda, diff_subln_g, diff_w_out,
           na_w_in, na_rpb, na_w_out, final_norm_g):
    b, t, d = x.shape
    n_ctx = ctx.shape[1]
    depth = ada_w.shape[0]
    assert n_ctx % TOKEN_TILE == 0 and t % TOKEN_TILE == 0 and b + 1 <= 8
    nct = n_ctx // TOKEN_TILE
    f = ffn_w_down.shape[1]
    chunk = 2 * LANES
    assert f % chunk == 0

    xs = jnp.concatenate([ctx, x], axis=1)
    cin = jnp.zeros((8, d), F32).at[:b].set(c).at[b].set(c_ctx)
    mods_all = _ada_all(cin, ada_w, ada_b).reshape(depth, 8, 6, d)

    tm = TOKEN_TILE
    ones_l = jnp.ones((1, LANES), F32)
    dummy_tab = jnp.zeros((xs.shape[1], LANES), F32)
    dummy_ones = jnp.zeros((8, LANES), F32)
    cos64, sin64 = _rope_table(t, n_ctx, GQA_HEAD_DIM, 0)
    cos_mla, sin_mla = _rope_table(t, n_ctx, MLA_ROPE_DIM, MLA_NOPE_DIM)
    no_lam, no_g = jnp.zeros((4, DIFF_HEAD_DIM), F32), jnp.ones((1, LANES), F32)
    pair_rows = MLA_V_DIM + O_HALO
    diff_rows = 2 * DIFF_HEAD_DIM + O_HALO

    for i in range(depth):
        m, j = i % N_MIXERS, i // N_MIXERS
        mods = mods_all[i]
        g1 = norm1_g[i].reshape(1, d)
        if m == 0:
            w = gqa_w_in[j].astype(BF16)
            nqc, nkc = GQA_HEADS * GQA_HEAD_DIM, GQA_KV_HEADS * GQA_HEAD_DIM
            wk = w[:, nqc:nqc + nkc].reshape(d, GQA_KV_HEADS, 1, GQA_HEAD_DIM)
            wk = jnp.broadcast_to(wk, (d, GQA_KV_HEADS, 2, GQA_HEAD_DIM)).reshape(d, GQA_KV_HEADS * LANES)
            wvt, vones = _vt_weights(w[:, nqc + nkc:], GQA_KV_HEADS, GQA_HEAD_DIM, tm)
            qc = GQA_HEAD_DIM ** -0.5 * LOG2E
            q, k, vt = _project(xs, mods, g1, w[:, :nqc], wk, wvt, _dup(gqa_q_norm_g[j]), _dup(gqa_k_norm_g[j]),
                                cos64 * qc, sin64 * qc, cos64, sin64, vones, qk_norm=True, rope=True,
                                head_dim=GQA_HEAD_DIM, k_norm_dim=2 * GQA_HEAD_DIM, v_transposed=True, q_rotated=True, n_ctx_tiles=nct,
                                name="proj_gqa")
            o_x, o_c = _flash(q, k, vt, no_lam, no_g, mode="pair", kw=LANES, k_cols=((0, LANES), (0, LANES)),
                       v_block_rows=pair_rows, v_rows=((0, pair_rows), (0, pair_rows)), vw=GQA_HEAD_DIM,
                       k_index=lambda p: p // 2, v_index=lambda p: p // 2, n_ctx=n_ctx, diff_scale=1.0,
                       name="attn_gqa")
            w_out = gqa_w_out[j]
        elif m == 1:
            w = mla_w_in[j].astype(BF16)
            nlat = MLA_Q_LORA + MLA_KV_LORA
            kr_slot = jnp.zeros((d, LANES), BF16).at[:, MLA_NOPE_DIM:MLA_NOPE_DIM + MLA_ROPE_DIM].set(w[:, nlat:])
            win = jnp.concatenate([w[:, :nlat], kr_slot], axis=1)
            qk_dim = MLA_NOPE_DIM + MLA_ROPE_DIM
            wuq = _to_slots(mla_w_uq[j].astype(BF16), MLA_HEADS, qk_dim, [0] * MLA_HEADS)
            wukv = mla_w_ukv[j].astype(BF16).reshape(MLA_KV_LORA, MLA_HEADS, MLA_NOPE_DIM + MLA_V_DIM)
            wuk = _to_slots(wukv[:, :, :MLA_NOPE_DIM].reshape(MLA_KV_LORA, -1), MLA_HEADS, MLA_NOPE_DIM,
                            [0] * MLA_HEADS)
            wuvt, vones = _vt_weights(wukv[:, :, MLA_NOPE_DIM:].reshape(MLA_KV_LORA, MLA_HEADS * MLA_V_DIM),
                                      MLA_HEADS, MLA_V_DIM, tm)
            qc = qk_dim ** -0.5 * LOG2E
            q, k, vt = _project_mla(xs, mods, g1, win, mla_q_norm_g[j].reshape(1, -1),
                                    mla_kv_norm_g[j].reshape(1, -1), wuq, wuk, wuvt,
                                    cos_mla * qc, sin_mla * qc, cos_mla, sin_mla, vones, n_ctx_tiles=nct)
            o_x, o_c = _flash(q, k, vt, no_lam, no_g, mode="pair", kw=2 * LANES, k_cols=((0, LANES), (LANES, 2 * LANES)),
                       v_block_rows=2 * pair_rows, v_rows=((0, pair_rows), (pair_rows, 2 * pair_rows)),
                       vw=MLA_V_DIM, k_index=lambda p: p, v_index=lambda p: p, n_ctx=n_ctx, diff_scale=1.0,
                       name="attn_mla")
            w_out = mla_w_out[j]
        elif m == 2:
            w = diff_w_in[j].astype(BF16)
            wvt, vones = _vt_weights(w[:, 2 * d:], DIFF_HEADS, 2 * DIFF_HEAD_DIM, tm)
            lambda_init = 0.8 - 0.6 * math.exp(-0.3 * i)
            qc = DIFF_HEAD_DIM ** -0.5 * LOG2E
            q, k, vt = _project(xs, mods, g1, w[:, :d], w[:, d:2 * d], wvt, ones_l, ones_l,
                                cos64 * qc, sin64 * qc, cos64, sin64, vones, qk_norm=False, rope=True,
                                head_dim=DIFF_HEAD_DIM, k_norm_dim=DIFF_HEAD_DIM, v_transposed=True, q_rotated=True, n_ctx_tiles=nct,
                                name="proj_diff")
            o_x, o_c = _flash(q, k, vt, diff_lambda[j], diff_subln_g[j].reshape(1, LANES), mode="diff", kw=LANES,
                       k_cols=((0, LANES), (0, LANES)), v_block_rows=diff_rows,
                       v_rows=((0, diff_rows), (0, diff_rows)), vw=2 * DIFF_HEAD_DIM,
                       k_index=lambda p: p, v_index=lambda p: p, n_ctx=n_ctx, diff_scale=1.0 - lambda_init,
                       name="attn_diff")
            w_out = diff_w_out[j]
        else:
            w = na_w_in[j].astype(BF16)
            na_scale = NA_HEAD_DIM ** -0.5
            assert math.frexp(na_scale)[0] == 0.5
            q, k, v = _project(xs, mods, g1, w[:, :d] * jnp.asarray(na_scale, BF16), w[:, d:2 * d], w[:, 2 * d:],
                               ones_l, ones_l, dummy_tab, dummy_tab, dummy_tab, dummy_tab, dummy_ones,
                               qk_norm=False, rope=False, head_dim=NA_HEAD_DIM, k_norm_dim=NA_HEAD_DIM,
                               v_transposed=False, q_rotated=False, n_ctx_tiles=nct,
                               name="proj_na")
            idx_all, var, wstart = _na_tables(t)
            bias = _na_bias(idx_all, na_rpb[j])
            o_x = o_c = _na_attention(q, k, v, bias, jnp.asarray(var), jnp.asarray(wstart), n_ctx=n_ctx)
            w_out = na_w_out[j]
        last = i == depth - 1

        xs = _mix_ffn(xs, o_x, o_c, w_out.astype(BF16), mods, norm2_g[i].reshape(1, d),
                      ffn_w_up[i].astype(BF16), ffn_conv_w[i], ffn_conv_b[i].reshape(1, 2 * f),
                      ffn_w_down[i].astype(BF16), final_norm_g.reshape(1, d), n_ctx_tiles=nct,
                      x_offset=0 if m == 3 else nct, latent_only=last, chunk=chunk, final_norm=last)
    return xs
```

```python
import functools
import math

import numpy as np
import jax
import jax.numpy as jnp
from jax import lax
from jax.experimental import pallas as pl
from jax.experimental.pallas import tpu as pltpu

F32 = jnp.float32
BF16 = jnp.bfloat16

LANES = 128
TOKEN_TILE = 256
HALO = 8
KV_CHUNK = 768
SUB_KEYS = 256
FLASH_TQ = 512
LOOP_CHUNKS = 4
VMEM_LIMIT = 56 * 1024 * 1024

GRID_W = 64
ROPE_BASE = 10000.0
EPS = 1e-6
NEG_INF = -1e30
LOG2E = 1.4426950408889634

GQA_HEADS, GQA_KV_HEADS, GQA_HEAD_DIM = 16, 4, 64
MLA_HEADS, MLA_NOPE_DIM, MLA_ROPE_DIM, MLA_V_DIM = 16, 64, 32, 64
MLA_Q_LORA, MLA_KV_LORA = 384, 256
DIFF_HEADS, DIFF_HEAD_DIM = 8, 64
NA_HEADS, NA_HEAD_DIM, NA_ROWS, NA_COLS = 16, 64, 8, 16
NA_QBLOCK = 128
NA_BIAS_TILE = 2048
NA_WIN_ROWS = 10
N_MIXERS = 4

NT_DIMS = (((1,), (1,)), ((), ()))


def _cparams(n_axes):
    return pltpu.CompilerParams(dimension_semantics=("arbitrary",) * n_axes, vmem_limit_bytes=VMEM_LIMIT)


def _resident(shape):
    nd = len(shape)
    return pl.BlockSpec(shape, lambda *_: (0,) * nd)


def _ada_kernel(c_ref, w_ref, b_ref, o_ref):
    c = c_ref[...]
    s = c * (1.0 / (1.0 + jnp.exp(-c)))
    o_ref[0] = jnp.dot(s.astype(BF16), w_ref[0].astype(BF16), preferred_element_type=F32) + b_ref[0]


def _ada_all(cin, ada_w, ada_b):
    depth, d, n = ada_w.shape
    tn = n // 4
    return pl.pallas_call(
        _ada_kernel,
        grid=(depth, n // tn),
        in_specs=[
            pl.BlockSpec((8, d), lambda l, j: (0, 0)),
            pl.BlockSpec((1, d, tn), lambda l, j: (l, 0, j)),
            pl.BlockSpec((1, 1, tn), lambda l, j: (l, 0, j)),
        ],
        out_specs=pl.BlockSpec((1, 8, tn), lambda l, j: (l, 0, j)),
        out_shape=jax.ShapeDtypeStruct((depth, 8, n), F32),
        compiler_params=_cparams(2),
        name="ada",
    )(cin, ada_w, ada_b.reshape(depth, 1, n))


def _norm_mod(x, g, shift, scale):
    ms = jnp.mean(x * x, axis=-1, keepdims=True)
    return (x * lax.rsqrt(ms + EPS) * g) * (1.0 + scale) + shift


def _rope_slot(y, cos, sin_signed, lo_mask, quarter):
    fwd = pltpu.roll(y, LANES - quarter, axis=1)
    bwd = pltpu.roll(y, quarter, axis=1)
    return y * cos + jnp.where(lo_mask, fwd, bwd) * sin_signed


def _lo_mask(shape, quarter):
    lane = lax.broadcasted_iota(jnp.int32, shape, 1)
    return (lane % (2 * quarter)) < quarter


def _q_spec(n, tm, n_tiles, n_ctx_tiles, rotate):
    if not rotate:
        return pl.BlockSpec((1, tm, n), lambda bi, i: (bi, i, 0))
    n_lat = n_tiles - n_ctx_tiles
    return pl.BlockSpec((1, tm, n), lambda bi, i: (bi, jnp.where(i < n_ctx_tiles, i + n_lat, i - n_ctx_tiles), 0))


def _vt_spec(rows, tm):
    per = KV_CHUNK // tm
    return pl.BlockSpec((1, 1, rows, tm), lambda bi, i: (bi, i // per, 0, i % per))


def _proj_kernel(x_ref, mod_ref, g_ref, wq_ref, wk_ref, wv_ref, qg_ref, kg_ref, cosq_ref, sinq_ref, cosk_ref, sink_ref,
                 vones_ref, q_ref, k_ref, v_ref, *, qk_norm, rope, head_dim, k_norm_dim, v_transposed):
    h = _norm_mod(x_ref[0], g_ref[...], mod_ref[0, 0:1, :], mod_ref[0, 1:2, :]).astype(BF16)
    tm = h.shape[0]
    lo = _lo_mask((tm, LANES), head_dim // 4) if rope else None
    low_half = lax.broadcasted_iota(jnp.int32, (tm, LANES), 1) < LANES // 2

    def chunks(w_ref):
        n = w_ref.shape[1]
        for c0 in range(0, n, 2 * LANES):
            c1 = min(c0 + 2 * LANES, n)
            y2 = jnp.dot(h, w_ref[:, c0:c1], preferred_element_type=F32)
            for s0 in range(0, c1 - c0, LANES):
                yield (c0 + s0) // LANES, y2[:, s0:s0 + LANES]

    for c, y in chunks(wq_ref):
        if qk_norm:
            sq = y * y
            ms_lo = jnp.sum(jnp.where(low_half, sq, 0.0), axis=-1, keepdims=True) * (1.0 / head_dim)
            ms_hi = jnp.sum(jnp.where(low_half, 0.0, sq), axis=-1, keepdims=True) * (1.0 / head_dim)
            y = y * jnp.where(low_half, lax.rsqrt(ms_lo + EPS), lax.rsqrt(ms_hi + EPS)) * qg_ref[...]
        if rope:
            y = _rope_slot(y, cosq_ref[...], sinq_ref[...], lo, head_dim // 4)
        q_ref[0, :, 2 * c * LANES:(2 * c + 1) * LANES] = jnp.where(low_half, y, 0.0).astype(BF16)
        q_ref[0, :, (2 * c + 1) * LANES:(2 * c + 2) * LANES] = jnp.where(low_half, 0.0, y).astype(BF16)
    for c, y in chunks(wk_ref):
        if qk_norm:
            ms = jnp.sum(y * y, axis=-1, keepdims=True) * (1.0 / k_norm_dim)
            y = y * lax.rsqrt(ms + EPS) * kg_ref[...]
        if rope:
            y = _rope_slot(y, cosk_ref[...], sink_ref[...], lo, head_dim // 4)
        k_ref[0, :, c * LANES:(c + 1) * LANES] = y.astype(BF16)
    if v_transposed:
        vt = lax.dot_general(wv_ref[...], h, NT_DIMS, preferred_element_type=F32) + vones_ref[...]
        v_ref[0, 0] = vt.astype(BF16)
    else:
        v_ref[0] = jnp.dot(h, wv_ref[...], preferred_element_type=F32).astype(BF16)


def _project(xs, mods, g, wq, wk, wv, qg, kg, cosq, sinq, cosk, sink, vones, *, qk_norm, rope, head_dim, k_norm_dim,
             v_transposed, q_rotated, n_ctx_tiles, name):
    b, s, d = xs.shape
    tm = TOKEN_TILE
    nq, nk = 2 * wq.shape[1], wk.shape[1]
    tok = lambda n: pl.BlockSpec((1, tm, n), lambda bi, i: (bi, i, 0))
    tab = pl.BlockSpec((tm, LANES), lambda bi, i: (i, 0))
    if v_transposed:
        v_spec = _vt_spec(wv.shape[0], tm)
        v_shape = jax.ShapeDtypeStruct((b, s // KV_CHUNK, wv.shape[0], KV_CHUNK), BF16)
    else:
        v_spec, v_shape = tok(wv.shape[1]), jax.ShapeDtypeStruct((b, s, wv.shape[1]), BF16)
    return pl.pallas_call(
        functools.partial(_proj_kernel, qk_norm=qk_norm, rope=rope, head_dim=head_dim, k_norm_dim=k_norm_dim,
                          v_transposed=v_transposed),
        grid=(b, s // tm),
        in_specs=[
            tok(d),
            pl.BlockSpec((1, 6, d), lambda bi, i: (jnp.where(i < n_ctx_tiles, b, bi), 0, 0)),
            _resident((1, d)),
            _resident(wq.shape), _resident(wk.shape), _resident(wv.shape),
            _resident((1, LANES)), _resident((1, LANES)),
            tab, tab, tab, tab,
            _resident(vones.shape),
        ],
        out_specs=[_q_spec(nq, tm, s // tm, n_ctx_tiles, q_rotated), tok(nk), v_spec],
        out_shape=[jax.ShapeDtypeStruct((b, s, nq), BF16), jax.ShapeDtypeStruct((b, s, nk), BF16), v_shape],
        compiler_params=_cparams(2),
        name=name,
    )(xs, mods, g, wq, wk, wv, qg, kg, cosq, sinq, cosk, sink, vones)


def _mla_proj_kernel(x_ref, mod_ref, g_ref, win_ref, qn_ref, kvn_ref, wuq_ref, wuk_ref, wuvt_ref,
                     cosq_ref, sinq_ref, cosk_ref, sink_ref, vones_ref, q_ref, k_ref, v_ref):
    h = _norm_mod(x_ref[0], g_ref[...], mod_ref[0, 0:1, :], mod_ref[0, 1:2, :]).astype(BF16)
    tm = h.shape[0]
    t1 = jnp.dot(h, win_ref[...], preferred_element_type=F32)
    cq = t1[:, :MLA_Q_LORA]
    ckv = t1[:, MLA_Q_LORA:MLA_Q_LORA + MLA_KV_LORA]
    kr = t1[:, MLA_Q_LORA + MLA_KV_LORA:]
    cqn = (cq * lax.rsqrt(jnp.mean(cq * cq, axis=-1, keepdims=True) + EPS) * qn_ref[...]).astype(BF16)
    ckvn = (ckv * lax.rsqrt(jnp.mean(ckv * ckv, axis=-1, keepdims=True) + EPS) * kvn_ref[...]).astype(BF16)
    quarter = MLA_ROPE_DIM // 4
    lo = _lo_mask((tm, LANES), quarter)
    kr = _rope_slot(kr, cosk_ref[...], sink_ref[...], lo, quarter)
    n = wuq_ref.shape[1]
    for c0 in range(0, n, 2 * LANES):
        q2 = jnp.dot(cqn, wuq_ref[:, c0:c0 + 2 * LANES], preferred_element_type=F32)
        k2 = jnp.dot(ckvn, wuk_ref[:, c0:c0 + 2 * LANES], preferred_element_type=F32)
        for s0 in (0, LANES):
            q_ref[0, :, c0 + s0:c0 + s0 + LANES] = _rope_slot(
                q2[:, s0:s0 + LANES], cosq_ref[...], sinq_ref[...], lo, quarter).astype(BF16)
            k_ref[0, :, c0 + s0:c0 + s0 + LANES] = (k2[:, s0:s0 + LANES] + kr).astype(BF16)
    vt = lax.dot_general(wuvt_ref[...], ckvn, NT_DIMS, preferred_element_type=F32) + vones_ref[...]
    v_ref[0, 0] = vt.astype(BF16)


def _project_mla(xs, mods, g, win, qn, kvn, wuq, wuk, wuvt, cosq, sinq, cosk, sink, vones, *, n_ctx_tiles):
    b, s, d = xs.shape
    tm = TOKEN_TILE
    nq, nk = wuq.shape[1], wuk.shape[1]
    tok = lambda n: pl.BlockSpec((1, tm, n), lambda bi, i: (bi, i, 0))
    tab = pl.BlockSpec((tm, LANES), lambda bi, i: (i, 0))
    return pl.pallas_call(
        _mla_proj_kernel,
        grid=(b, s // tm),
        in_specs=[
            tok(d),
            pl.BlockSpec((1, 6, d), lambda bi, i: (jnp.where(i < n_ctx_tiles, b, bi), 0, 0)),
            _resident((1, d)),
            _resident(win.shape), _resident(qn.shape), _resident(kvn.shape),
            _resident(wuq.shape), _resident(wuk.shape), _resident(wuvt.shape),
            tab, tab, tab, tab,
            _resident(vones.shape),
        ],
        out_specs=[_q_spec(nq, tm, s // tm, n_ctx_tiles, True), tok(nk), _vt_spec(wuvt.shape[0], tm)],
        out_shape=[jax.ShapeDtypeStruct((b, s, nq), BF16), jax.ShapeDtypeStruct((b, s, nk), BF16),
                   jax.ShapeDtypeStruct((b, s // KV_CHUNK, wuvt.shape[0], KV_CHUNK), BF16)],
        compiler_params=_cparams(2),
        name="proj_mla",
    )(xs, mods, g, win, qn, kvn, wuq, wuk, wuvt, cosq, sinq, cosk, sink, vones)


def _flash_kernel(q_ref, k_ref, vt_ref, lam_ref, subg_ref, o_ref, *scratch, mode, k_cols, v_rows, vw, diff_scale,
                  ctx_keys):
    tq = q_ref.shape[1]
    n_chunks = vt_ref.shape[1]
    qs = [q_ref[0, :, a * LANES:(a + 1) * LANES] for a in (0, 1)]
    n_rows = v_rows[0][1] - v_rows[0][0]

    def keys(key_rows):
        k0 = k_ref[0, key_rows, k_cols[0][0]:k_cols[0][1]]
        k1 = k0 if k_cols[1] == k_cols[0] else k_ref[0, key_rows, k_cols[1][0]:k_cols[1][1]]
        return k0, k1

    def values(chunk_idx, key_lanes):
        v0 = vt_ref[0, chunk_idx, v_rows[0][0]:v_rows[0][1], key_lanes]
        v1 = v0 if v_rows[1] == v_rows[0] else vt_ref[0, chunk_idx, v_rows[1][0]:v_rows[1][1], key_lanes]
        return v0, v1

    def update(st, cmax, vt_a, carry_a):
        m, acc = carry_a
        m_new = jnp.maximum(m, cmax)
        p = jnp.exp2(st - m_new).astype(BF16)
        return m_new, jnp.exp2(m - m_new) * acc + jnp.dot(vt_a, p, preferred_element_type=F32)

    init = tuple((jnp.full((1, tq), -jnp.inf, F32), jnp.zeros((n_rows, tq), F32)) for _ in (0, 1))

    def finish(carry):
        (_, acc_a), (_, acc_b) = carry
        oa = acc_a[0:vw] / acc_a[vw:vw + 1]
        ob = acc_b[0:vw] / acc_b[vw:vw + 1]
        if mode == "pair":
            out = jnp.concatenate([oa, ob], axis=0).T
        else:
            lam = lam_ref[...]
            lam_full = (jnp.exp(jnp.sum(lam[0:1] * lam[1:2], axis=-1, keepdims=True))
                        - jnp.exp(jnp.sum(lam[2:3] * lam[3:4], axis=-1, keepdims=True))
                        + (1.0 - diff_scale))
            o = (oa - lam_full * ob).T
            out = o * lax.rsqrt(jnp.mean(o * o, axis=-1, keepdims=True) + EPS) * subg_ref[...] * diff_scale
        o_ref[0] = out.astype(BF16)

    if ctx_keys:
        ks, vts = keys(slice(0, ctx_keys)), values(0, slice(0, ctx_keys))
        out = []
        for a in (0, 1):
            st = lax.dot_general(ks[a], qs[a], NT_DIMS, preferred_element_type=F32)
            out.append(update(st, jnp.max(st, axis=0, keepdims=True), vts[a], init[a]))
        finish(tuple(out))
        return

    (st_scr,) = scratch

    def chunk_keys(chunk_idx):
        start = chunk_idx * KV_CHUNK
        return keys(pl.ds(start if isinstance(start, int) else pl.multiple_of(start, KV_CHUNK), KV_CHUNK))

    def score(slot, a, ks):
        st = lax.dot_general(ks[a], qs[a], NT_DIMS, preferred_element_type=F32)
        st_scr[slot, a] = st
        return jnp.max(st, axis=0, keepdims=True)

    def scores(slot, chunk_idx):
        ks = chunk_keys(chunk_idx)
        return tuple(score(slot, a, ks) for a in (0, 1))

    def run(first, count, cmax, carry):
        for u in range(count):
            nxt, slot = first + u + 1, u % 2
            has_next = isinstance(nxt, jax.Array) or nxt < n_chunks
            m_new = [jnp.maximum(carry[a][0], cmax[a]) for a in (0, 1)]
            acc = [jnp.exp2(carry[a][0] - m_new[a]) * carry[a][1] for a in (0, 1)]
            cmax_next = [None, None]
            for j in range(KV_CHUNK // SUB_KEYS):
                rows = slice(j * SUB_KEYS, (j + 1) * SUB_KEYS)
                if has_next:
                    start = nxt * KV_CHUNK + j * SUB_KEYS
                    ks = keys(pl.ds(start if isinstance(start, int) else pl.multiple_of(start, SUB_KEYS), SUB_KEYS))
                    for a in (0, 1):
                        st = lax.dot_general(ks[a], qs[a], NT_DIMS, preferred_element_type=F32)
                        st_scr[1 - slot, a, rows, :] = st
                        mx = jnp.max(st, axis=0, keepdims=True)
                        cmax_next[a] = mx if cmax_next[a] is None else jnp.maximum(cmax_next[a], mx)
                vts = values(first + u, rows)
                for a in (0, 1):
                    p = jnp.exp2(st_scr[slot, a, rows, :] - m_new[a]).astype(BF16)
                    acc[a] = acc[a] + jnp.dot(vts[a], p, preferred_element_type=F32)
            carry = tuple((m_new[a], acc[a]) for a in (0, 1))
            cmax = tuple(cmax_next)
        return cmax, carry

    n_loop = (n_chunks - 1) // LOOP_CHUNKS
    cmax, carry = lax.fori_loop(0, n_loop, lambda i, st: run(i * LOOP_CHUNKS, LOOP_CHUNKS, *st),
                                (scores(0, 0), init))
    _, carry = run(n_loop * LOOP_CHUNKS, n_chunks - n_loop * LOOP_CHUNKS, cmax, carry)
    finish(carry)


def _flash(q, k, vt, lam, subg, *, mode, kw, k_cols, v_block_rows, v_rows, vw, k_index, v_index, n_ctx, diff_scale,
           name):
    b, s, nq = q.shape
    t = s - n_ctx
    n_pairs = nq // (2 * LANES)
    n_chunks = s // KV_CHUNK
    assert s % KV_CHUNK == 0 and n_ctx <= KV_CHUNK and t % FLASH_TQ == 0 and t % n_ctx == 0
    assert vt.shape[1] == n_chunks
    common = dict(mode=mode, k_cols=k_cols, v_rows=v_rows, vw=vw, diff_scale=diff_scale)
    o_x = pl.pallas_call(
        functools.partial(_flash_kernel, ctx_keys=0, **common),
        grid=(b, n_pairs, t // FLASH_TQ),
        in_specs=[
            pl.BlockSpec((1, FLASH_TQ, 2 * LANES), lambda bi, p, i: (bi, i, p)),
            pl.BlockSpec((1, s, kw), lambda bi, p, i: (bi, 0, k_index(p))),
            pl.BlockSpec((1, n_chunks, v_block_rows, KV_CHUNK), lambda bi, p, i: (bi, 0, v_index(p), 0)),
            _resident(lam.shape), _resident(subg.shape),
        ],
        out_specs=pl.BlockSpec((1, FLASH_TQ, LANES), lambda bi, p, i: (bi, i, p)),
        out_shape=jax.ShapeDtypeStruct((b, t, n_pairs * LANES), BF16),
        scratch_shapes=[pltpu.VMEM((2, 2, KV_CHUNK, FLASH_TQ), F32)],
        compiler_params=_cparams(3),
        name=name,
    )(q, k, vt, lam, subg)
    o_c = pl.pallas_call(
        functools.partial(_flash_kernel, ctx_keys=n_ctx, **common),
        grid=(b, n_pairs),
        in_specs=[
            pl.BlockSpec((1, n_ctx, 2 * LANES), lambda bi, p: (bi, t // n_ctx, p)),
            pl.BlockSpec((1, n_ctx, kw), lambda bi, p: (bi, 0, k_index(p))),
            pl.BlockSpec((1, 1, v_block_rows, KV_CHUNK), lambda bi, p: (bi, 0, v_index(p), 0)),
            _resident(lam.shape), _resident(subg.shape),
        ],
        out_specs=pl.BlockSpec((1, n_ctx, LANES), lambda bi, p: (bi, 0, p)),
        out_shape=jax.ShapeDtypeStruct((b, n_ctx, n_pairs * LANES), BF16),
        compiler_params=_cparams(2),
        name=name + "_ctx",
    )(q, k, vt, lam, subg)
    return o_x, o_c


def _na_tables(t):
    rows = t // GRID_W
    win_rows = min(NA_ROWS, rows)
    band_rows = min(win_rows + 1, rows)
    assert rows >= NA_WIN_ROWS and NA_QBLOCK % GRID_W == 0
    qrows = NA_QBLOCK // GRID_W
    nb = t // NA_QBLOCK
    nwin = NA_WIN_ROWS * GRID_W
    q_off, k_off = np.arange(NA_QBLOCK), np.arange(nwin)
    q_dr, q_col = q_off // GRID_W, q_off % GRID_W
    k_dr, k_col = k_off // GRID_W, k_off % GRID_W
    col_start = np.clip(q_col - NA_COLS // 2, 0, GRID_W - NA_COLS)
    col_in = (k_col[None, :] >= col_start[:, None]) & (k_col[None, :] < col_start[:, None] + NA_COLS)
    dc_idx = np.clip(k_col[None, :] - q_col[:, None] + NA_COLS - 1, 0, 2 * NA_COLS - 2)
    patterns, var, wstart = {}, [], []
    for j in range(nb):
        r = j * qrows + q_dr
        row_start = np.clip(r - win_rows // 2, 0, rows - win_rows)
        b0 = min(int(row_start[0]), rows - band_rows)
        w0 = min(b0 - b0 % 2, rows - NA_WIN_ROWS)
        assert w0 % qrows == 0 and w0 <= b0 and b0 + band_rows <= w0 + NA_WIN_ROWS
        k_row = w0 + k_dr
        in_band = (k_row >= b0) & (k_row < b0 + band_rows)
        in_win = (col_in & in_band[None, :] & (k_row[None, :] >= row_start[:, None])
                  & (k_row[None, :] < row_start[:, None] + win_rows))
        dr_idx = np.clip(k_row[None, :] - r[:, None] + NA_ROWS - 1, 0, 2 * NA_ROWS - 2)
        idx = np.where(in_win, dr_idx * (2 * NA_COLS - 1) + dc_idx, -1).astype(np.int32)
        key = idx.tobytes()
        if key not in patterns:
            patterns[key] = (len(patterns), idx)
        var.append(patterns[key][0])
        wstart.append(w0 * GRID_W)
    idx_all = np.stack([p[1] for p in sorted(patterns.values(), key=lambda p: p[0])])
    return idx_all.reshape(len(patterns), 1, -1), np.asarray(var, np.int32), np.asarray(wstart, np.int32)


def _na_bias_kernel(idx_ref, rpb_ref, o_ref):
    idx = idx_ref[0]
    r = rpb_ref[...]
    onehot = (lax.broadcasted_iota(jnp.int32, (r.shape[1], idx.shape[1]), 0) == idx).astype(BF16)
    b1 = r.astype(BF16)
    r2 = r - b1.astype(F32)
    b2 = r2.astype(BF16)
    b3 = (r2 - b2.astype(F32)).astype(BF16)
    heads = r.shape[0]
    parts = jnp.dot(jnp.concatenate([b1, b2, b3], axis=0), onehot, preferred_element_type=F32)
    out = parts[:heads] + parts[heads:2 * heads] + parts[2 * heads:]
    o_ref[0] = jnp.where(idx >= 0, out, NEG_INF)


def _na_bias(idx_all, rpb):
    nvar, _, npos = idx_all.shape
    heads = rpb.shape[0]
    nr = (2 * NA_ROWS - 1) * (2 * NA_COLS - 1)
    nr_pad = pl.cdiv(nr, LANES) * LANES
    rflat = jnp.pad(rpb.reshape(heads, nr), ((0, 0), (0, nr_pad - nr)))
    tn = NA_BIAS_TILE
    assert npos % tn == 0
    out = pl.pallas_call(
        _na_bias_kernel,
        grid=(nvar, npos // tn),
        in_specs=[pl.BlockSpec((1, 1, tn), lambda v, j: (v, 0, j)), _resident(rflat.shape)],
        out_specs=pl.BlockSpec((1, heads, tn), lambda v, j: (v, 0, j)),
        out_shape=jax.ShapeDtypeStruct((nvar, heads, npos), F32),
        compiler_params=_cparams(2),
        name="na_bias",
    )(jnp.asarray(idx_all), rflat)
    return out.reshape(nvar, heads, NA_QBLOCK, npos // NA_QBLOCK)


def _halves(lo_src, hi_src):
    lane = lax.broadcasted_iota(jnp.int32, lo_src.shape, 1)
    return jnp.where(lane < LANES // 2, lo_src, hi_src)


def _na_kernel(var_ref, ws_ref, q_ref, k_ref, v_ref, b0_ref, b1_ref, o_ref, *, n_ctx, n_ctx_tiles):
    qi = pl.program_id(2)
    nwin = NA_WIN_ROWS * GRID_W

    @pl.when(qi < n_ctx_tiles)
    def _():
        kc, vc = k_ref[0, 0:n_ctx, :], v_ref[0, 0:n_ctx, :]
        outs = []
        for a in (0, 1):
            s_c = lax.dot_general(q_ref[0, :, a * LANES:(a + 1) * LANES], kc, NT_DIMS,
                                  preferred_element_type=F32)
            p_c = jnp.exp(s_c - jnp.max(s_c, axis=-1, keepdims=True))
            acc = jnp.dot(p_c.astype(BF16), vc, preferred_element_type=F32)
            outs.append(acc / jnp.sum(p_c, axis=-1, keepdims=True))
        o_ref[0] = _halves(outs[0], outs[1]).astype(BF16)

    @pl.when(qi >= n_ctx_tiles)
    def _():
        kc, vc = k_ref[0, 0:n_ctx, :], v_ref[0, 0:n_ctx, :]
        work = []
        for sub, bias_ref in ((0, b0_ref), (1, b1_ref)):
            j = (qi - n_ctx_tiles) * (TOKEN_TILE // NA_QBLOCK) + sub
            start = pl.multiple_of(n_ctx + ws_ref[j], NA_QBLOCK)
            kw, vw = k_ref[0, pl.ds(start, nwin), :], v_ref[0, pl.ds(start, nwin), :]
            for a in (0, 1):
                q = q_ref[0, sub * NA_QBLOCK:(sub + 1) * NA_QBLOCK, a * LANES:(a + 1) * LANES]
                s_c = lax.dot_general(q, kc, NT_DIMS, preferred_element_type=F32)
                s_w = lax.dot_general(q, kw, NT_DIMS, preferred_element_type=F32) + bias_ref[0, a]
                work.append((s_c, s_w, vw))
        outs = []
        for s_c, s_w, vw in work:
            m = jnp.maximum(jnp.max(s_c, axis=-1, keepdims=True), jnp.max(s_w, axis=-1, keepdims=True))
            p_c, p_w = jnp.exp(s_c - m), jnp.exp(s_w - m)
            l = jnp.sum(p_c, axis=-1, keepdims=True) + jnp.sum(p_w, axis=-1, keepdims=True)
            acc = (jnp.dot(p_c.astype(BF16), vc, preferred_element_type=F32)
                   + jnp.dot(p_w.astype(BF16), vw, preferred_element_type=F32))
            outs.append(acc / l)
        for sub in (0, 1):
            o_ref[0, sub * NA_QBLOCK:(sub + 1) * NA_QBLOCK, :] = _halves(outs[2 * sub], outs[2 * sub + 1]).astype(BF16)


def _na_attention(q, k, v, bias, var, wstart, *, n_ctx):
    b, s, nq = q.shape
    tq = TOKEN_TILE
    n_pairs = nq // (2 * LANES)
    n_ctx_tiles = n_ctx // tq
    per_tile = tq // NA_QBLOCK
    nwin = NA_WIN_ROWS * GRID_W

    def bias_spec(sub):
        def index(bi, p, i, var_ref, ws_ref):
            j = jnp.maximum(i - n_ctx_tiles, 0) * per_tile + sub
            return (var_ref[j], p, 0, 0)
        return pl.BlockSpec((1, 2, NA_QBLOCK, nwin), index)

    grid_spec = pltpu.PrefetchScalarGridSpec(
        num_scalar_prefetch=2,
        grid=(b, n_pairs, s // tq),
        in_specs=[
            pl.BlockSpec((1, tq, 2 * LANES), lambda bi, p, i, *_: (bi, i, p)),
            pl.BlockSpec((1, s, LANES), lambda bi, p, i, *_: (bi, 0, p)),
            pl.BlockSpec((1, s, LANES), lambda bi, p, i, *_: (bi, 0, p)),
            bias_spec(0), bias_spec(1),
        ],
        out_specs=pl.BlockSpec((1, tq, LANES), lambda bi, p, i, *_: (bi, i, p)),
    )
    return pl.pallas_call(
        functools.partial(_na_kernel, n_ctx=n_ctx, n_ctx_tiles=n_ctx_tiles),
        grid_spec=grid_spec,
        out_shape=jax.ShapeDtypeStruct((b, s, n_pairs * LANES), BF16),
        compiler_params=_cparams(3),
        name="na_attn",
    )(var, wstart, q, k, v, bias, bias)


O_HALO = 16


def _mix_ffn_kernel(xp_ref, x_ref, xn_ref, op_ref, ox_ref, oc_ref, on_ref, wo_ref, mod_ref, g_ref, wup_ref, cw_ref,
                    cb_ref, wdn_ref, fg_ref, y_ref, u_scr, a_scr, *, n_ctx_tiles, n_tiles, first_tile, chunk,
                    final_norm):
    t = pl.program_id(1) + first_tile
    tm = x_ref.shape[1]
    has_prev = jnp.logical_and(t != 0, t != n_ctx_tiles).astype(F32)
    has_next = jnp.logical_and(t != n_ctx_tiles - 1, t != n_tiles - 1).astype(F32)
    o = jnp.where(t < n_ctx_tiles, oc_ref[0], ox_ref[0])
    o_cat = jnp.concatenate([op_ref[0], o, on_ref[0]], axis=0)
    x_cat = jnp.concatenate([xp_ref[0], x_ref[0], xn_ref[0]], axis=0)
    x1 = x_cat + mod_ref[0, 2:3, :] * jnp.dot(o_cat, wo_ref[...], preferred_element_type=F32)
    g, shift, scale = g_ref[...], mod_ref[0, 3:4, :], mod_ref[0, 4:5, :]
    x = x1[O_HALO:O_HALO + tm]
    h = jnp.concatenate([_norm_mod(x1[O_HALO - HALO:O_HALO], g, shift, scale) * has_prev,
                         _norm_mod(x, g, shift, scale),
                         _norm_mod(x1[O_HALO + tm:O_HALO + tm + HALO], g, shift, scale) * has_next],
                        axis=0).astype(BF16)
    f = wdn_ref.shape[0]

    def val_gate(ref, j):
        return jnp.concatenate([ref[:, j * chunk:(j + 1) * chunk], ref[:, f + j * chunk:f + (j + 1) * chunk]], axis=1)

    for j in range(f // chunk):
        u_scr[...] = jnp.dot(h, val_gate(wup_ref, j), preferred_element_type=F32)
        cw = val_gate(cw_ref, j)
        u = (u_scr[HALO - 1:HALO - 1 + tm, :] * cw[0:1] + u_scr[HALO:HALO + tm, :] * cw[1:2]
             + u_scr[HALO + 1:HALO + 1 + tm, :] * cw[2:3] + val_gate(cb_ref, j))
        val, gate = u[:, :chunk], u[:, chunk:]
        a_scr[:, j * chunk:(j + 1) * chunk] = (gate * (1.0 / (1.0 + jnp.exp(-gate))) * val).astype(BF16)
    y = x + mod_ref[0, 5:6, :] * jnp.dot(a_scr[...], wdn_ref[...], preferred_element_type=F32)
    if final_norm:
        y = y * lax.rsqrt(jnp.mean(y * y, axis=-1, keepdims=True) + EPS) * fg_ref[...]
    y_ref[0] = y


def _mix_ffn(xs, o_x, o_c, w_out, mods, g, wup, cw, cb, wdn, final_g, *, n_ctx_tiles, x_offset, latent_only, chunk,
             final_norm):
    b, s, d = xs.shape
    tm = TOKEN_TILE
    n_tiles = s // tm
    n = o_x.shape[2]
    f = wdn.shape[0]
    first = n_ctx_tiles if latent_only else 0
    assert n_ctx_tiles == 1
    per_o = tm // O_HALO
    lat = lambda i: jnp.maximum(i + first, n_ctx_tiles) - x_offset
    return pl.pallas_call(
        functools.partial(_mix_ffn_kernel, n_ctx_tiles=n_ctx_tiles, n_tiles=n_tiles, first_tile=first, chunk=chunk,
                          final_norm=final_norm),
        grid=(b, n_tiles - first),
        in_specs=[
            pl.BlockSpec((1, O_HALO, d), lambda bi, i: (bi, jnp.maximum((i + first) * per_o - 1, 0), 0)),
            pl.BlockSpec((1, tm, d), lambda bi, i: (bi, i + first, 0)),
            pl.BlockSpec((1, O_HALO, d), lambda bi, i: (bi, jnp.minimum((i + first + 1) * per_o, s // O_HALO - 1), 0)),
            pl.BlockSpec((1, O_HALO, n), lambda bi, i: (bi, jnp.maximum(lat(i) * per_o - 1, 0), 0)),
            pl.BlockSpec((1, tm, n), lambda bi, i: (bi, lat(i), 0)),
            pl.BlockSpec((1, tm, n), lambda bi, i: (bi, jnp.minimum(i + first, n_ctx_tiles - 1), 0)),
            pl.BlockSpec((1, O_HALO, n),
                         lambda bi, i: (bi, jnp.minimum((lat(i) + 1) * per_o, o_x.shape[1] // O_HALO - 1), 0)),
            _resident(w_out.shape),
            pl.BlockSpec((1, 6, d), lambda bi, i: (jnp.where(i + first < n_ctx_tiles, b, bi), 0, 0)),
            _resident((1, d)),
            pl.BlockSpec(wup.shape, lambda bi, i: (0, 0), pipeline_mode=pl.Buffered(1)),
            _resident(cw.shape), _resident(cb.shape),
            pl.BlockSpec(wdn.shape, lambda bi, i: (0, 0), pipeline_mode=pl.Buffered(1)),
            _resident((1, d)),
        ],
        out_specs=pl.BlockSpec((1, tm, d), lambda bi, i: (bi, i, 0)),
        out_shape=jax.ShapeDtypeStruct((b, s - first * tm, d), F32),
        scratch_shapes=[pltpu.VMEM((tm + 2 * HALO, 2 * chunk), F32), pltpu.VMEM((tm, f), BF16)],
        compiler_params=_cparams(2),
        name="mix_ffn",
    )(xs, xs, xs, o_x, o_x, o_c, o_x, w_out, mods, g, wup, cw, cb, wdn, final_g)


def _rope_table(t, n_ctx, rot_dim, lane_lo):
    tt = jnp.arange(t)
    row = (tt // GRID_W).astype(F32)
    col = (tt % GRID_W).astype(F32)
    half = rot_dim // 2
    inv = ROPE_BASE ** (-jnp.arange(0, half, 2, dtype=F32) / half)
    ar, ac = row[:, None] * inv, col[:, None] * inv
    ang = jnp.concatenate([ar, ar, ac, ac], axis=-1)
    quarter = rot_dim // 4
    sign = jnp.where((jnp.arange(rot_dim) % (2 * quarter)) < quarter, -1.0, 1.0).astype(F32)
    cos, sin = jnp.cos(ang), jnp.sin(ang) * sign
    reps = (LANES - lane_lo) // rot_dim if lane_lo == 0 else 1
    cos_l = jnp.ones((t, LANES), F32).at[:, lane_lo:lane_lo + reps * rot_dim].set(jnp.tile(cos, (1, reps)))
    sin_l = jnp.zeros((t, LANES), F32).at[:, lane_lo:lane_lo + reps * rot_dim].set(jnp.tile(sin, (1, reps)))
    cos_s = jnp.concatenate([jnp.ones((n_ctx, LANES), F32), cos_l], axis=0)
    sin_s = jnp.concatenate([jnp.zeros((n_ctx, LANES), F32), sin_l], axis=0)
    return cos_s, sin_s


def _to_slots(w, n_heads, width, halves):
    d = w.shape[0]
    wh = w.reshape(d, n_heads, width)
    lo = jnp.pad(wh, ((0, 0), (0, 0), (0, LANES - width)))
    if not any(halves):
        return lo.reshape(d, n_heads * LANES)
    hi = jnp.pad(wh, ((0, 0), (0, 0), (LANES // 2, LANES // 2 - width)))
    in_hi = jnp.asarray(halves, jnp.int32).reshape(1, n_heads, 1) == 1
    return jnp.where(in_hi, hi, lo).reshape(d, n_heads * LANES)


def _vt_weights(wv, groups, width, tm):
    d = wv.shape[0]
    rows = width + O_HALO
    wt = jnp.zeros((groups, rows, d), wv.dtype).at[:, :width, :].set(wv.T.reshape(groups, width, d))
    ones = jnp.zeros((groups, rows, tm), F32).at[:, width, :].set(1.0)
    return wt.reshape(groups * rows, d), ones.reshape(groups * rows, tm)


def _dup(g):
    return jnp.concatenate([g, g]).reshape(1, LANES).astype(F32)


def kernel(x, c, ctx, c_ctx, ada_w, ada_b, norm1_g, norm2_g, ffn_w_up, ffn_conv_w, ffn_conv_b, ffn_w_down,
           gqa_w_in, gqa_q_norm_g, gqa_k_norm_g, gqa_w_out,
           mla_w_in, mla_q_norm_g, mla_kv_norm_g, mla_w_uq, mla_w_ukv, mla_w_out,
           diff_w_in, diff_lambda, diff_subln_g, diff_w_out,
           na_w_in, na_rpb, na_w_out, final_norm_g):
    b, t, d = x.shape
    n_ctx = ctx.shape[1]
    depth = ada_w.shape[0]
    assert n_ctx % TOKEN_TILE == 0 and t % TOKEN_TILE == 0 and b + 1 <= 8
    nct = n_ctx // TOKEN_TILE
    f = ffn_w_down.shape[1]
    chunk = 2 * LANES
    assert f % chunk == 0

    xs = jnp.concatenate([ctx, x], axis=1)
    cin = jnp.zeros((8, d), F32).at[:b].set(c).at[b].set(c_ctx)
    mods_all = _ada_all(cin, ada_w, ada_b).reshape(depth, 8, 6, d)

    tm = TOKEN_TILE
    ones_l = jnp.ones((1, LANES), F32)
    dummy_tab = jnp.zeros((xs.shape[1], LANES), F32)
    dummy_ones = jnp.zeros((8, LANES), F32)
    cos64, sin64 = _rope_table(t, n_ctx, GQA_HEAD_DIM, 0)
    cos_mla, sin_mla = _rope_table(t, n_ctx, MLA_ROPE_DIM, MLA_NOPE_DIM)
    no_lam, no_g = jnp.zeros((4, DIFF_HEAD_DIM), F32), jnp.ones((1, LANES), F32)
    pair_rows = MLA_V_DIM + O_HALO
    diff_rows = 2 * DIFF_HEAD_DIM + O_HALO

    for i in range(depth):
        m, j = i % N_MIXERS, i // N_MIXERS
        mods = mods_all[i]
        g1 = norm1_g[i].reshape(1, d)
        if m == 0:
            w = gqa_w_in[j].astype(BF16)
            nqc, nkc = GQA_HEADS * GQA_HEAD_DIM, GQA_KV_HEADS * GQA_HEAD_DIM
            wk = w[:, nqc:nqc + nkc].reshape(d, GQA_KV_HEADS, 1, GQA_HEAD_DIM)
            wk = jnp.broadcast_to(wk, (d, GQA_KV_HEADS, 2, GQA_HEAD_DIM)).reshape(d, GQA_KV_HEADS * LANES)
            wvt, vones = _vt_weights(w[:, nqc + nkc:], GQA_KV_HEADS, GQA_HEAD_DIM, tm)
            qc = GQA_HEAD_DIM ** -0.5 * LOG2E
            q, k, vt = _project(xs, mods, g1, w[:, :nqc], wk, wvt, _dup(gqa_q_norm_g[j]), _dup(gqa_k_norm_g[j]),
                                cos64 * qc, sin64 * qc, cos64, sin64, vones, qk_norm=True, rope=True,
                                head_dim=GQA_HEAD_DIM, k_norm_dim=2 * GQA_HEAD_DIM, v_transposed=True, q_rotated=True, n_ctx_tiles=nct,
                                name="proj_gqa")
            o_x, o_c = _flash(q, k, vt, no_lam, no_g, mode="pair", kw=LANES, k_cols=((0, LANES), (0, LANES)),
                       v_block_rows=pair_rows, v_rows=((0, pair_rows), (0, pair_rows)), vw=GQA_HEAD_DIM,
                       k_index=lambda p: p // 2, v_index=lambda p: p // 2, n_ctx=n_ctx, diff_scale=1.0,
                       name="attn_gqa")
            w_out = gqa_w_out[j]
        elif m == 1:
            w = mla_w_in[j].astype(BF16)
            nlat = MLA_Q_LORA + MLA_KV_LORA
            kr_slot = jnp.zeros((d, LANES), BF16).at[:, MLA_NOPE_DIM:MLA_NOPE_DIM + MLA_ROPE_DIM].set(w[:, nlat:])
            win = jnp.concatenate([w[:, :nlat], kr_slot], axis=1)
            qk_dim = MLA_NOPE_DIM + MLA_ROPE_DIM
            wuq = _to_slots(mla_w_uq[j].astype(BF16), MLA_HEADS, qk_dim, [0] * MLA_HEADS)
            wukv = mla_w_ukv[j].astype(BF16).reshape(MLA_KV_LORA, MLA_HEADS, MLA_NOPE_DIM + MLA_V_DIM)
            wuk = _to_slots(wukv[:, :, :MLA_NOPE_DIM].reshape(MLA_KV_LORA, -1), MLA_HEADS, MLA_NOPE_DIM,
                            [0] * MLA_HEADS)
            wuvt, vones = _vt_weights(wukv[:, :, MLA_NOPE_DIM:].reshape(MLA_KV_LORA, MLA_HEADS * MLA_V_DIM),
                                      MLA_HEADS, MLA_V_DIM, tm)
            qc = qk_dim ** -0.5 * LOG2E
            q, k, vt = _project_mla(xs, mods, g1, win, mla_q_norm_g[j].reshape(1, -1),
                                    mla_kv_norm_g[j].reshape(1, -1), wuq, wuk, wuvt,
                                    cos_mla * qc, sin_mla * qc, cos_mla, sin_mla, vones, n_ctx_tiles=nct)
            o_x, o_c = _flash(q, k, vt, no_lam, no_g, mode="pair", kw=2 * LANES, k_cols=((0, LANES), (LANES, 2 * LANES)),
                       v_block_rows=2 * pair_rows, v_rows=((0, pair_rows), (pair_rows, 2 * pair_rows)),
                       vw=MLA_V_DIM, k_index=lambda p: p, v_index=lambda p: p, n_ctx=n_ctx, diff_scale=1.0,
                       name="attn_mla")
            w_out = mla_w_out[j]
        elif m == 2:
            w = diff_w_in[j].astype(BF16)
            wvt, vones = _vt_weights(w[:, 2 * d:], DIFF_HEADS, 2 * DIFF_HEAD_DIM, tm)
            lambda_init = 0.8 - 0.6 * math.exp(-0.3 * i)
            qc = DIFF_HEAD_DIM ** -0.5 * LOG2E
            q, k, vt = _project(xs, mods, g1, w[:, :d], w[:, d:2 * d], wvt, ones_l, ones_l,
                                cos64 * qc, sin64 * qc, cos64, sin64, vones, qk_norm=False, rope=True,
                                head_dim=DIFF_HEAD_DIM, k_norm_dim=DIFF_HEAD_DIM, v_transposed=True, q_rotated=True, n_ctx_tiles=nct,
                                name="proj_diff")
            o_x, o_c = _flash(q, k, vt, diff_lambda[j], diff_subln_g[j].reshape(1, LANES), mode="diff", kw=LANES,
                       k_cols=((0, LANES), (0, LANES)), v_block_rows=diff_rows,
                       v_rows=((0, diff_rows), (0, diff_rows)), vw=2 * DIFF_HEAD_DIM,
                       k_index=lambda p: p, v_index=lambda p: p, n_ctx=n_ctx, diff_scale=1.0 - lambda_init,
                       name="attn_diff")
            w_out = diff_w_out[j]
        else:
            w = na_w_in[j].astype(BF16)
            na_scale = NA_HEAD_DIM ** -0.5
            assert math.frexp(na_scale)[0] == 0.5
            q, k, v = _project(xs, mods, g1, w[:, :d] * jnp.asarray(na_scale, BF16), w[:, d:2 * d], w[:, 2 * d:],
                               ones_l, ones_l, dummy_tab, dummy_tab, dummy_tab, dummy_tab, dummy_ones,
                               qk_norm=False, rope=False, head_dim=NA_HEAD_DIM, k_norm_dim=NA_HEAD_DIM,
                               v_transposed=False, q_rotated=False, n_ctx_tiles=nct,
                               name="proj_na")
            idx_all, var, wstart = _na_tables(t)
            bias = _na_bias(idx_all, na_rpb[j])
            o_x = o_c = _na_attention(q, k, v, bias, jnp.asarray(var), jnp.asarray(wstart), n_ctx=n_ctx)
            w_out = na_w_out[j]
        last = i == depth - 1

        xs = _mix_ffn(xs, o_x, o_c, w_out.astype(BF16), mods, norm2_g[i].reshape(1, d),
                      ffn_w_up[i].astype(BF16), ffn_conv_w[i], ffn_conv_b[i].reshape(1, 2 * f),
                      ffn_w_down[i].astype(BF16), final_norm_g.reshape(1, d), n_ctx_tiles=nct,
                      x_offset=0 if m == 3 else nct, latent_only=last, chunk=chunk, final_norm=last)
    return xs
```
